```python
import jax, jax.numpy as jnp
from jax import lax
import numpy as np

D_MODEL = 1024
BATCH = 16
SEQ = 256
DEPTH = 2
DEC_BATCH = 8
DEC_SEQ = 2048
PAST_LEN = 512

GRID_W = 64
N_MIXERS = 2
N_A_LAYERS = (DEPTH + 1) // 2
N_B_LAYERS = DEPTH // 2
HG_HEADS = 8
HG_DK = 128
HG_DV = D_MODEL // HG_HEADS
HG_CHUNK = 32
CONV_WIDTH = 3
N_EXPERTS = 32
TOP_K = 4
D_FF = D_MODEL
SWIGLU_LIMIT = 7.0
SWIGLU_ALPHA = 1.702
MOE_BLOCK = 128
EPS = 1e-6

kernel_name = "hybrid_hgrn2_shortconv_moe_diffusion_step"


def rms_norm(x, g):
    xf = x.astype(jnp.float32)
    y = xf * lax.rsqrt(jnp.mean(xf * xf, axis=-1, keepdims=True) + EPS)
    return (y * g.astype(jnp.float32)).astype(x.dtype)


def gla_chunked(q, k, v, logf, s0):
    bsz, t_len, heads, dk = q.shape
    n_chunks = t_len // HG_CHUNK

    def to_chunks(a):
        return a.reshape(bsz, n_chunks, HG_CHUNK, heads, a.shape[-1]).transpose(1, 0, 3, 2, 4)

    qc, kc, vc, gc = to_chunks(q), to_chunks(k), to_chunks(v), to_chunks(logf)
    b = jnp.cumsum(gc, axis=3)
    b_last = b[:, :, :, -1:, :]
    q_dec = qc * jnp.exp(b)
    k_inv = kc * jnp.exp(-b)
    k_state = kc * jnp.exp(b_last - b)
    causal = jnp.tril(jnp.ones((HG_CHUNK, HG_CHUNK), dtype=bool))
    att = jnp.where(causal, jnp.einsum('nbhtk,nbhsk->nbhts', q_dec, k_inv), 0.0)
    o_intra = jnp.einsum('nbhts,nbhsv->nbhtv', att, vc)

    def step(s, xs):
        qd, ks, vv, bl = xs
        o_inter = jnp.einsum('bhtk,bhkv->bhtv', qd, s)
        s = jnp.exp(bl)[:, :, 0, :, None] * s + jnp.einsum('bhsk,bhsv->bhkv', ks, vv)
        return s, o_inter

    s_final, o_inter = lax.scan(step, s0.astype(jnp.float32), (q_dec, k_state, vc, b_last))
    o = (o_intra + o_inter).transpose(1, 0, 3, 2, 4).reshape(bsz, t_len, heads, vc.shape[-1])
    return o, s_final


def hgrn2_mixer(h, w_in, lb_f, lb_b, gnorm, w_out, s0_f, s0_b):
    bsz, t_len, _ = h.shape
    q, fz_f, fz_b, inp, og = jnp.split(h @ w_in, 5, axis=-1)
    q = jax.nn.silu(q.astype(jnp.float32)).reshape(bsz, t_len, HG_HEADS, HG_DK)
    inp = inp.astype(jnp.float32).reshape(bsz, t_len, HG_HEADS, HG_DV)

    def gates(fz, lb):
        lb = lb.astype(jnp.float32).reshape(HG_HEADS, HG_DK)
        f = lb + (1.0 - lb) * jax.nn.sigmoid(fz.astype(jnp.float32).reshape(bsz, t_len, HG_HEADS, HG_DK))
        return 1.0 - f, jnp.log(f)

    k_f, lf_f = gates(fz_f, lb_f)
    o_f, s_f = gla_chunked(q, k_f, inp, lf_f, s0_f)
    k_b, lf_b = gates(fz_b, lb_b)
    flip = lambda a: jnp.flip(a, axis=1)
    o_b, s_b = gla_chunked(flip(q), flip(k_b), flip(inp), flip(lf_b), s0_b)
    o = o_f + flip(o_b)
    o = o * lax.rsqrt(jnp.mean(o * o, axis=-1, keepdims=True) + EPS) * gnorm.astype(jnp.float32)
    o = o.reshape(bsz, t_len, HG_HEADS * HG_DV) * jax.nn.sigmoid(og.astype(jnp.float32))
    return o.astype(h.dtype) @ w_out, s_f, s_b


def shortconv_mixer(h, w_in, conv_w, w_out, grid):
    bsz, t_len, d = h.shape
    bg, cg, hx = jnp.split(h @ w_in, 3, axis=-1)
    u = cg * hx
    if grid:
        rows = t_len // GRID_W
        u = u.reshape(bsz * rows, GRID_W, d)
    up = jnp.pad(u, ((0, 0), (1, 1), (0, 0)))
    v = conv_w[0] * up[:, :-2] + conv_w[1] * up[:, 1:-1] + conv_w[2] * up[:, 2:]
    v = v.reshape(bsz, t_len, d)
    return (bg * v) @ w_out


def moe_ffn(h, router_w, router_b, w_gu, b_gu, w_down, b_down):
    bsz, t_len, d = h.shape
    x = h.reshape(-1, d)
    n_tok = x.shape[0]
    n_slot = n_tok * TOP_K
    logits = (x @ router_w + router_b).astype(jnp.float32)
    top_v, top_e = lax.top_k(logits, TOP_K)
    gates = jax.nn.softmax(top_v, axis=-1)
    flat_e = top_e.reshape(-1)
    order = jnp.argsort(flat_e)
    sorted_e = flat_e[order]
    sorted_tok = order // TOP_K
    sorted_gate = gates.reshape(-1)[order]
    counts = jnp.bincount(flat_e, length=N_EXPERTS)
    padded = (counts + MOE_BLOCK - 1) // MOE_BLOCK * MOE_BLOCK
    start = jnp.cumsum(counts) - counts
    pstart = jnp.cumsum(padded) - padded
    dest = pstart[sorted_e] + jnp.arange(n_slot, dtype=jnp.int32) - start[sorted_e]
    n_blocks = -(-n_slot // MOE_BLOCK) + N_EXPERTS
    slot_tok = jnp.full((n_blocks * MOE_BLOCK,), n_tok, jnp.int32).at[dest].set(sorted_tok.astype(jnp.int32))
    x_pad = jnp.concatenate([x, jnp.zeros((1, d), x.dtype)], axis=0)
    xb = x_pad[slot_tok].reshape(n_blocks, MOE_BLOCK, d)
    block_e = jnp.minimum(
        jnp.searchsorted(jnp.cumsum(padded), jnp.arange(n_blocks) * MOE_BLOCK, side='right'),
        N_EXPERTS - 1)

    def expert(args):
        xe, e = args
        gate, up = jnp.split(xe @ w_gu[e] + b_gu[e], 2, axis=-1)
        gate = jnp.minimum(gate, SWIGLU_LIMIT)
        up = jnp.clip(up, -SWIGLU_LIMIT, SWIGLU_LIMIT)
        act = (up + 1.0) * gate * jax.nn.sigmoid(SWIGLU_ALPHA * gate)
        return act @ w_down[e] + b_down[e]

    yb = lax.map(expert, (xb, block_e)).reshape(-1, d)
    y = jax.ops.segment_sum(yb[dest] * sorted_gate[:, None].astype(yb.dtype), sorted_tok,
                            num_segments=n_tok)
    return y.reshape(bsz, t_len, d)


def run_trunk(x, cond, hg_state0, grid, w_mod, b_mod, norm_mix, norm_ffn, hg_w_in, hg_lb_logits,
              hg_gnorm, hg_w_out, cv_w_in, cv_w, cv_w_out, router_w, router_b, moe_w_gu, moe_b_gu,
              moe_w_down, moe_b_down, final_norm):
    lb_all = jnp.cumsum(jax.nn.softmax(hg_lb_logits.astype(jnp.float32), axis=1), axis=1)
    s_cond = jax.nn.silu(cond)
    finals = []
    for l in range(DEPTH):
        mod = (s_cond @ w_mod[l] + b_mod[l])[:, None, :]
        sh1, sc1, g1, sh2, sc2, g2 = jnp.split(mod, 6, axis=-1)
        hm = rms_norm(x, norm_mix[l]) * (1.0 + sc1) + sh1
        if l % N_MIXERS == 0:
            a = l // N_MIXERS
            y, s_f, s_b = hgrn2_mixer(hm, hg_w_in[a], lb_all[0, a], lb_all[1, a], hg_gnorm[a],
                                      hg_w_out[a], hg_state0[:, a, 0], hg_state0[:, a, 1])
            finals.append(jnp.stack([s_f, s_b], axis=1))
        else:
            bi = l // N_MIXERS
            y = shortconv_mixer(hm, cv_w_in[bi], cv_w[bi], cv_w_out[bi], grid)
        x = x + g1 * y
        hf = rms_norm(x, norm_ffn[l]) * (1.0 + sc2) + sh2
        x = x + g2 * moe_ffn(hf, router_w[l], router_b[l], moe_w_gu[l], moe_b_gu[l],
                             moe_w_down[l], moe_b_down[l])
    return rms_norm(x, final_norm), jnp.stack(finals, axis=1)


def setup_inputs(seed: int = 0) -> dict:
    key = jax.random.key(seed)
    ks = jax.random.split(key, 24)
    D = D_MODEL

    def nrm(k, shape, scale):
        return jax.random.normal(k, shape, jnp.float32) * scale

    return {
        "x_prompt": nrm(ks[0], (BATCH, SEQ, D), 1.0),
        "x_sample": nrm(ks[1], (DEC_BATCH, DEC_SEQ, D), 1.0),
        "state_hgrn": nrm(ks[2], (DEC_BATCH, N_A_LAYERS, 2, HG_HEADS, HG_DK, HG_DV), 0.5),
        "c": nrm(ks[3], (DEC_BATCH, D), 1.0),
        "c_ctx": nrm(ks[4], (D,), 1.0),
        "w_mod": nrm(ks[5], (DEPTH, D, 6 * D), 0.5 * D ** -0.5),
        "b_mod": nrm(ks[6], (DEPTH, 6 * D), 0.02),
        "norm_mix": 1.0 + nrm(ks[7], (DEPTH, D), 0.02),
        "norm_ffn": 1.0 + nrm(ks[8], (DEPTH, D), 0.02),
        "hg_w_in": nrm(ks[9], (N_A_LAYERS, D, 5 * D), D ** -0.5),
        "hg_lb_logits": nrm(ks[10], (2, N_A_LAYERS + 1, HG_HEADS * HG_DK), 0.2),
        "hg_gnorm": 1.0 + nrm(ks[11], (N_A_LAYERS, HG_HEADS, HG_DV), 0.02),
        "hg_w_out": nrm(ks[12], (N_A_LAYERS, D, D), D ** -0.5),
        "cv_w_in": nrm(ks[13], (N_B_LAYERS, D, 3 * D), D ** -0.5),
        "cv_w": nrm(ks[14], (N_B_LAYERS, CONV_WIDTH, D), CONV_WIDTH ** -0.5),
        "cv_w_out": nrm(ks[15], (N_B_LAYERS, D, D), D ** -0.5),
        "router_w": nrm(ks[16], (DEPTH, D, N_EXPERTS), D ** -0.5),
        "router_b": nrm(ks[17], (DEPTH, N_EXPERTS), 0.01),
        "moe_w_gu": nrm(ks[18], (DEPTH, N_EXPERTS, D, 2 * D_FF), D ** -0.5),
        "moe_b_gu": nrm(ks[19], (DEPTH, N_EXPERTS, 2 * D_FF), 0.01),
        "moe_w_down": nrm(ks[20], (DEPTH, N_EXPERTS, D_FF, D), D_FF ** -0.5),
        "moe_b_down": nrm(ks[21], (DEPTH, N_EXPERTS, D), 0.01),
        "final_norm": 1.0 + nrm(ks[22], (D,), 0.02),
    }


def reference(x_prompt, x_sample, state_hgrn, c, c_ctx, w_mod, b_mod, norm_mix, norm_ffn, hg_w_in,
              hg_lb_logits, hg_gnorm, hg_w_out, cv_w_in, cv_w, cv_w_out, router_w, router_b,
              moe_w_gu, moe_b_gu, moe_w_down, moe_b_down, final_norm):
    weights = (w_mod, b_mod, norm_mix, norm_ffn, hg_w_in, hg_lb_logits, hg_gnorm, hg_w_out,
               cv_w_in, cv_w, cv_w_out, router_w, router_b, moe_w_gu, moe_b_gu, moe_w_down,
               moe_b_down, final_norm)
    zero_state = jnp.zeros((x_prompt.shape[0], N_A_LAYERS, 2, HG_HEADS, HG_DK, HG_DV), jnp.float32)
    y_prompt, new_state_hgrn = run_trunk(x_prompt, c_ctx[None, :], zero_state, False, *weights)
    y_sample, _ = run_trunk(x_sample, c, state_hgrn, True, *weights)
    return (y_prompt, y_sample, new_state_hgrn)
```

```python
import functools

import numpy as np
import jax
import jax.numpy as jnp
from jax import lax
from jax.experimental import pallas as pl
from jax.experimental.pallas import tpu as pltpu

F32 = jnp.float32
BF16 = jnp.bfloat16

D_MODEL = 1024
HEADS = 8
HEAD_DIM = 128
CHUNK = 32
GRID_W = 64
N_EXPERTS = 32
TOP_K = 4
D_FF = 1024
SWIGLU_LIMIT = 7.0
SWIGLU_ALPHA = 1.702
EPS = 1e-6

LANES = 128
SCAN_ROWS = 2048
CUMSUM_ROWS = 256
TOKEN_TILE = 512
EXPERT_TILE = 512
ROW_TILE = 256
VMEM_LIMIT = 56 * 1024 * 1024


def _cparams(*sem):
    return pltpu.CompilerParams(dimension_semantics=sem, vmem_limit_bytes=VMEM_LIMIT)


def _sigmoid(x):
    return 1.0 / (1.0 + jnp.exp(-x))


def _rms(x, w):
    return x * lax.rsqrt(jnp.mean(x * x, axis=-1, keepdims=True) + EPS) * w


def _dot(a, b):
    return jnp.dot(a, b, preferred_element_type=F32)


def _dot_nt(a, b):
    return lax.dot_general(a, b, (((1,), (1,)), ((), ())), preferred_element_type=F32)


def _dot_tn(a, b):
    return lax.dot_general(a, b, (((0,), (0,)), ((), ())), preferred_element_type=F32)


def _mod_kernel(cond_ref, w_ref, b_ref, o_ref):
    c = cond_ref[...]
    s = (c * _sigmoid(c)).astype(BF16)
    o_ref[...] = _dot(s, w_ref[...]) + b_ref[...]


def _modulation(cond16, w_mod_bf, b_mod):
    depth = w_mod_bf.shape[0]
    n_out = w_mod_bf.shape[2]
    tn = 1024
    return pl.pallas_call(
        _mod_kernel,
        grid=(depth, n_out // tn),
        in_specs=[
            pl.BlockSpec((16, D_MODEL), lambda l, j: (0, 0)),
            pl.BlockSpec((None, D_MODEL, tn), lambda l, j: (l, 0, j)),
            pl.BlockSpec((None, 1, tn), lambda l, j: (l, 0, j)),
        ],
        out_specs=pl.BlockSpec((None, 16, tn), lambda l, j: (l, 0, j)),
        out_shape=jax.ShapeDtypeStruct((depth, 16, n_out), F32),
        compiler_params=_cparams("arbitrary", "arbitrary"),
        name="modulation",
    )(cond16, w_mod_bf, b_mod.reshape(depth, 1, n_out))


def _norm_mod_kernel(row_ref, x_ref, mod_ref, nw_ref, o_ref):
    del row_ref
    y = _rms(x_ref[...], nw_ref[...])
    o_ref[...] = (y * (1.0 + mod_ref[1:2, :]) + mod_ref[0:1, :]).astype(o_ref.dtype)


def _norm_mod(x, tile_row, mod_l, norm_w):
    n = x.shape[0]
    tm = TOKEN_TILE
    return pl.pallas_call(
        _norm_mod_kernel,
        grid_spec=pltpu.PrefetchScalarGridSpec(
            num_scalar_prefetch=1,
            grid=(n // tm,),
            in_specs=[
                pl.BlockSpec((tm, D_MODEL), lambda i, row: (i, 0)),
                pl.BlockSpec((None, 6, D_MODEL), lambda i, row: (row[i], 0, 0)),
                pl.BlockSpec((1, D_MODEL), lambda i, row: (0, 0)),
            ],
            out_specs=pl.BlockSpec((tm, D_MODEL), lambda i, row: (i, 0)),
        ),
        out_shape=jax.ShapeDtypeStruct((n, D_MODEL), BF16),
        compiler_params=_cparams("arbitrary"),
        name="norm_mod",
    )(tile_row, x, mod_l, norm_w.reshape(1, D_MODEL))


def _hgrn_kernel(hm_ref, wq_ref, wff_ref, wfb_ref, wv_ref, wg_ref, lbf_ref, lbb_ref, gn_ref,
                 s0_ref, o_ref, so_ref,
                 q_s, kf_s, bf_s, kb_s, bb_s, v_s, of_s, ob_s, stf_s, stb_s,
                 *, seq_len, zero_init):
    rows = hm_ref.shape[0]
    n_seq = rows // seq_len
    n_chunks = seq_len // CHUNK
    hm = hm_ref[...]

    zq = _dot(hm, wq_ref[...])
    q_s[...] = zq * _sigmoid(zq)
    v_s[...] = _dot(hm, wv_ref[...])

    r = lax.broadcasted_iota(jnp.int32, (CUMSUM_ROWS, CUMSUM_ROWS), 0)
    c = lax.broadcasted_iota(jnp.int32, (CUMSUM_ROWS, CUMSUM_ROWS), 1)
    same = (r // CHUNK) == (c // CHUNK)
    prefix = jnp.where(same & (c <= r), 1.0, 0.0).astype(BF16)
    suffix = jnp.where(same & (c >= r), 1.0, 0.0).astype(BF16)

    def gates(w_ref, lb_ref, k_s, b_s, tri):
        lb = lb_ref[...]
        f = lb + (1.0 - lb) * _sigmoid(_dot(hm, w_ref[...]))
        k_s[...] = 1.0 - f
        lf = jnp.log(f)
        hi = lf.astype(BF16)
        r1 = lf - hi.astype(F32)
        mid = r1.astype(BF16)
        lo = (r1 - mid.astype(F32)).astype(BF16)
        for p in range(rows // CUMSUM_ROWS):
            sl = slice(p * CUMSUM_ROWS, (p + 1) * CUMSUM_ROWS)
            b_s[sl, :] = _dot(tri, hi[sl]) + _dot(tri, mid[sl]) + _dot(tri, lo[sl])

    gates(wff_ref, lbf_ref, kf_s, bf_s, prefix)
    gates(wfb_ref, lbb_ref, kb_s, bb_s, suffix)

    tr = lax.broadcasted_iota(jnp.int32, (CHUNK, CHUNK), 0)
    tc = lax.broadcasted_iota(jnp.int32, (CHUNK, CHUNK), 1)
    causal = tc <= tr
    anti = tc >= tr

    def one_dir(r0, k_s, b_s, st_s, o_s, mask, last_row):
        sl = pl.ds(r0, CHUNK)
        b = b_s[sl, :]
        bl = b[last_row:last_row + 1, :]
        q = q_s[sl, :]
        k = k_s[sl, :]
        vb = v_s[sl, :].astype(BF16)
        qd = (q * jnp.exp(b)).astype(BF16)
        ki = (k * jnp.exp(-b)).astype(BF16)
        ks = (k * jnp.exp(bl - b)).astype(BF16)
        att = jnp.where(mask, _dot_nt(qd, ki), 0.0).astype(BF16)
        st = st_s[...]
        o_s[sl, :] = _dot(att, vb) + _dot_nt(qd, st.astype(BF16))
        st_s[...] = st * jnp.exp(bl) + _dot_tn(vb, ks)

    def seq_body(j, carry):
        base = j * seq_len
        if zero_init:
            stf_s[...] = jnp.zeros_like(stf_s)
            stb_s[...] = jnp.zeros_like(stb_s)
        else:
            stf_s[...] = s0_ref[j, 0].T
            stb_s[...] = s0_ref[j, 1].T

        def chunk_body(ci, carry2):
            rf = pl.multiple_of(base + ci * CHUNK, CHUNK)
            rb = pl.multiple_of(base + (n_chunks - 1 - ci) * CHUNK, CHUNK)
            one_dir(rf, kf_s, bf_s, stf_s, of_s, causal, CHUNK - 1)
            one_dir(rb, kb_s, bb_s, stb_s, ob_s, anti, 0)
            return carry2

        lax.fori_loop(0, n_chunks, chunk_body, 0)
        so_ref[j, 0] = stf_s[...].T
        so_ref[j, 1] = stb_s[...].T
        return carry

    lax.fori_loop(0, n_seq, seq_body, 0)

    o = of_s[...] + ob_s[...]
    o = _rms(o, gn_ref[...])
    o_ref[...] = (o * _sigmoid(_dot(hm, wg_ref[...]))).astype(o_ref.dtype)


def _hgrn_scan(hm, w_in_bf, lb_f, lb_b, gnorm, s0, seq_len, zero_init):
    n = hm.shape[0]
    n_seq_total = n // seq_len
    rows = SCAN_ROWS
    seq_per_step = rows // seq_len

    def w_spec(seg):
        return pl.BlockSpec((D_MODEL, HEAD_DIM), lambda sb, h: (0, seg * HEADS + h))

    vec_spec = pl.BlockSpec((1, HEAD_DIM), lambda sb, h: (0, h))
    st_spec = pl.BlockSpec((seq_per_step, 2, None, HEAD_DIM, HEAD_DIM), lambda sb, h: (sb, 0, h, 0, 0))
    s0_spec = pl.BlockSpec((seq_per_step, 2, None, HEAD_DIM, HEAD_DIM),
                           (lambda sb, h: (0, 0, h, 0, 0)) if zero_init else (lambda sb, h: (sb, 0, h, 0, 0)))
    scratch = [pltpu.VMEM((rows, HEAD_DIM), F32) for _ in range(8)]
    scratch += [pltpu.VMEM((HEAD_DIM, HEAD_DIM), F32) for _ in range(2)]
    return pl.pallas_call(
        functools.partial(_hgrn_kernel, seq_len=seq_len, zero_init=zero_init),
        grid=(n // rows, HEADS),
        in_specs=[
            pl.BlockSpec((rows, D_MODEL), lambda sb, h: (sb, 0)),
            w_spec(0), w_spec(1), w_spec(2), w_spec(3), w_spec(4),
            vec_spec, vec_spec, vec_spec,
            s0_spec,
        ],
        out_specs=[
            pl.BlockSpec((rows, HEAD_DIM), lambda sb, h: (sb, h)),
            st_spec,
        ],
        out_shape=[
            jax.ShapeDtypeStruct((n, D_MODEL), BF16),
            jax.ShapeDtypeStruct((n_seq_total, 2, HEADS, HEAD_DIM, HEAD_DIM), F32),
        ],
        scratch_shapes=scratch,
        compiler_params=_cparams("arbitrary", "arbitrary"),
        name="hgrn_scan",
    )(hm, w_in_bf, w_in_bf, w_in_bf, w_in_bf, w_in_bf,
      lb_f.reshape(1, D_MODEL), lb_b.reshape(1, D_MODEL), gnorm.reshape(1, D_MODEL), s0)


def _proj_res_kernel(row_ref, x_ref, o_ref, w_ref, mod_ref, xo_ref):
    del row_ref
    xo_ref[...] = x_ref[...] + mod_ref[2:3, :] * _dot(o_ref[...], w_ref[...])


def _proj_residual(x, o, w_out_bf, tile_row, mod_l):
    n = x.shape[0]
    tm = TOKEN_TILE
    return pl.pallas_call(
        _proj_res_kernel,
        grid_spec=pltpu.PrefetchScalarGridSpec(
            num_scalar_prefetch=1,
            grid=(n // tm,),
            in_specs=[
                pl.BlockSpec((tm, D_MODEL), lambda i, row: (i, 0)),
                pl.BlockSpec((tm, D_MODEL), lambda i, row: (i, 0)),
                pl.BlockSpec((D_MODEL, D_MODEL), lambda i, row: (0, 0)),
                pl.BlockSpec((None, 6, D_MODEL), lambda i, row: (row[i], 0, 0)),
            ],
            out_specs=pl.BlockSpec((tm, D_MODEL), lambda i, row: (i, 0)),
        ),
        out_shape=jax.ShapeDtypeStruct((n, D_MODEL), F32),
        compiler_params=_cparams("arbitrary"),
        name="proj_residual",
    )(tile_row, x, o, w_out_bf, mod_l)


def _conv_kernel(row_ref, width_ref, x_ref, mod_ref, nw_ref, win_ref, cw_ref, wout_ref, xo_ref):
    del row_ref
    i = pl.program_id(0)
    x = x_ref[...]
    hm = (_rms(x, nw_ref[...]) * (1.0 + mod_ref[1:2, :]) + mod_ref[0:1, :]).astype(BF16)
    z = _dot(hm, win_ref[...])
    bg = z[:, :D_MODEL]
    u = z[:, D_MODEL:2 * D_MODEL] * z[:, 2 * D_MODEL:]
    tm = x.shape[0]
    pos = lax.broadcasted_iota(jnp.int32, (tm, 1), 0) & (width_ref[i] - 1)
    prev = jnp.where(pos == 0, 0.0, pltpu.roll(u, 1, axis=0))
    nxt = jnp.where(pos == width_ref[i] - 1, 0.0, pltpu.roll(u, tm - 1, axis=0))
    v = cw_ref[0:1, :] * prev + cw_ref[1:2, :] * u + cw_ref[2:3, :] * nxt
    y = _dot((bg * v).astype(BF16), wout_ref[...])
    xo_ref[...] = x + mod_ref[2:3, :] * y


def _conv_mixer(x, tile_row, tile_width, mod_l, norm_w, w_in_bf, conv_w, w_out_bf):
    n = x.shape[0]
    tm = TOKEN_TILE
    return pl.pallas_call(
        _conv_kernel,
        grid_spec=pltpu.PrefetchScalarGridSpec(
            num_scalar_prefetch=2,
            grid=(n // tm,),
            in_specs=[
                pl.BlockSpec((tm, D_MODEL), lambda i, row, w: (i, 0)),
                pl.BlockSpec((None, 6, D_MODEL), lambda i, row, w: (row[i], 0, 0)),
                pl.BlockSpec((1, D_MODEL), lambda i, row, w: (0, 0)),
                pl.BlockSpec((D_MODEL, 3 * D_MODEL), lambda i, row, w: (0, 0)),
                pl.BlockSpec((3, D_MODEL), lambda i, row, w: (0, 0)),
                pl.BlockSpec((D_MODEL, D_MODEL), lambda i, row, w: (0, 0)),
            ],
            out_specs=pl.BlockSpec((tm, D_MODEL), lambda i, row, w: (i, 0)),
        ),
        out_shape=jax.ShapeDtypeStruct((n, D_MODEL), F32),
        compiler_params=_cparams("arbitrary"),
        name="conv_mixer",
    )(tile_row, tile_width, x, mod_l, norm_w.reshape(1, D_MODEL), w_in_bf, conv_w, w_out_bf)


def _router_kernel(row_ref, x_ref, mod_ref, nw_ref, rw_ref, rb_ref,
                   hf_ref, e_ref, g_ref, rank_ref, cnt_ref, carry_s):
    del row_ref
    i = pl.program_id(0)

    @pl.when(i == 0)
    def _():
        carry_s[...] = jnp.zeros_like(carry_s)

    hf = _rms(x_ref[...], nw_ref[...]) * (1.0 + mod_ref[4:5, :]) + mod_ref[3:4, :]
    hf_ref[...] = hf
    logits = jnp.dot(hf, rw_ref[...], preferred_element_type=F32,
                     precision=lax.Precision.HIGHEST) + rb_ref[...]
    tm = hf.shape[0]
    lane = lax.broadcasted_iota(jnp.int32, (tm, LANES), 1)
    lane_f = lane.astype(F32)
    work = logits
    vals, idxs, hots = [], [], []
    for _ in range(TOP_K):
        m = jnp.max(work, axis=-1, keepdims=True)
        idx_f = jnp.min(jnp.where(work == m, lane_f, float(LANES)), axis=-1, keepdims=True)
        hot = lane_f == idx_f
        vals.append(m)
        idxs.append(idx_f.astype(jnp.int32))
        hots.append(hot)
        work = jnp.where(hot, -jnp.inf, work)
    ex = [jnp.exp(v - vals[0]) for v in vals]
    inv = 1.0 / (ex[0] + ex[1] + ex[2] + ex[3])

    multi = jnp.where(hots[0] | hots[1] | hots[2] | hots[3], 1.0, 0.0)
    tr = lax.broadcasted_iota(jnp.int32, (tm, tm), 0)
    tc = lax.broadcasted_iota(jnp.int32, (tm, tm), 1)
    strict = jnp.where(tc < tr, 1.0, 0.0).astype(BF16)
    before = _dot(strict, multi.astype(BF16)) + carry_s[...]
    carry_s[...] = carry_s[...] + jnp.sum(multi, axis=0, keepdims=True)
    cnt_ref[...] = carry_s[...].astype(jnp.int32)

    e_out = jnp.zeros((tm, LANES), jnp.int32)
    g_out = jnp.zeros((tm, LANES), F32)
    r_out = jnp.zeros((tm, LANES), jnp.int32)
    for k in range(TOP_K):
        rank = jnp.sum(jnp.where(hots[k], before, 0.0), axis=-1, keepdims=True).astype(jnp.int32)
        e_out = jnp.where(lane == k, idxs[k], e_out)
        g_out = jnp.where(lane == k, ex[k] * inv, g_out)
        r_out = jnp.where(lane == k, rank, r_out)
    e_ref[...] = e_out
    g_ref[...] = g_out
    rank_ref[...] = r_out


def _router(x, tile_row, mod_l, norm_w, router_w_pad, router_b_pad):
    n = x.shape[0]
    tm = TOKEN_TILE
    tok_spec = pl.BlockSpec((tm, LANES), lambda i, row: (i, 0))
    return pl.pallas_call(
        _router_kernel,
        grid_spec=pltpu.PrefetchScalarGridSpec(
            num_scalar_prefetch=1,
            grid=(n // tm,),
            in_specs=[
                pl.BlockSpec((tm, D_MODEL), lambda i, row: (i, 0)),
                pl.BlockSpec((None, 6, D_MODEL), lambda i, row: (row[i], 0, 0)),
                pl.BlockSpec((1, D_MODEL), lambda i, row: (0, 0)),
                pl.BlockSpec((D_MODEL, LANES), lambda i, row: (0, 0)),
                pl.BlockSpec((1, LANES), lambda i, row: (0, 0)),
            ],
            out_specs=[
                pl.BlockSpec((tm, D_MODEL), lambda i, row: (i, 0)),
                tok_spec, tok_spec, tok_spec,
                pl.BlockSpec((1, LANES), lambda i, row: (0, 0)),
            ],
            scratch_shapes=[pltpu.VMEM((1, LANES), F32)],
        ),
        out_shape=[
            jax.ShapeDtypeStruct((n, D_MODEL), F32),
            jax.ShapeDtypeStruct((n, LANES), jnp.int32),
            jax.ShapeDtypeStruct((n, LANES), F32),
            jax.ShapeDtypeStruct((n, LANES), jnp.int32),
            jax.ShapeDtypeStruct((1, LANES), jnp.int32),
        ],
        compiler_params=_cparams("arbitrary"),
        name="router",
    )(tile_row, x, mod_l, norm_w.reshape(1, D_MODEL), router_w_pad, router_b_pad)


def _dispatch_kernel(dest_ref, hf_ref, xb_in_ref, xb_ref, sem):
    del xb_in_ref
    tm = hf_ref.shape[0]

    def issue(r, carry):
        for k in range(TOP_K):
            d = dest_ref[0, r * TOP_K + k]
            pltpu.make_async_copy(hf_ref.at[pl.ds(r, 1), :], xb_ref.at[pl.ds(d, 1), :], sem).start()
        return carry

    lax.fori_loop(0, tm, issue, 0)
    for _ in range(TOP_K):
        pltpu.make_async_copy(hf_ref, xb_ref.at[pl.ds(0, tm), :], sem).wait()


def _dispatch(hf, dest_tiles, xb_zero):
    n = hf.shape[0]
    tm = ROW_TILE
    return pl.pallas_call(
        _dispatch_kernel,
        grid=(n // tm,),
        in_specs=[
            pl.BlockSpec((None, 1, tm * TOP_K), lambda i: (i, 0, 0), memory_space=pltpu.SMEM),
            pl.BlockSpec((tm, D_MODEL), lambda i: (i, 0)),
            pl.BlockSpec(memory_space=pl.ANY),
        ],
        out_specs=pl.BlockSpec(memory_space=pl.ANY),
        out_shape=jax.ShapeDtypeStruct(xb_zero.shape, xb_zero.dtype),
        scratch_shapes=[pltpu.SemaphoreType.DMA],
        input_output_aliases={2: 0},
        compiler_params=_cparams("arbitrary"),
        name="dispatch",
    )(dest_tiles, hf, xb_zero)


def _expert_kernel(be_ref, nu_ref, xb_ref, wgu_ref, bgu_ref, wd_ref, bd_ref, yb_ref, wgu_s, wd_s):
    i = pl.program_id(0)

    @pl.when(i < nu_ref[0])
    def _():
        changed = jnp.logical_or(i == 0, be_ref[i] != be_ref[jnp.maximum(i - 1, 0)])

        @pl.when(changed)
        def _():
            wgu_s[...] = wgu_ref[...].astype(BF16)
            wd_s[...] = wd_ref[...].astype(BF16)

        gu = _dot(xb_ref[...].astype(BF16), wgu_s[...]) + bgu_ref[...]
        gate = jnp.minimum(gu[:, :D_FF], SWIGLU_LIMIT)
        up = jnp.clip(gu[:, D_FF:], -SWIGLU_LIMIT, SWIGLU_LIMIT)
        act = (up + 1.0) * gate * _sigmoid(SWIGLU_ALPHA * gate)
        yb_ref[...] = _dot(act.astype(BF16), wd_s[...]) + bd_ref[...]

    @pl.when(i >= nu_ref[0])
    def _():
        yb_ref[...] = jnp.zeros_like(yb_ref)


def _experts(xb, block_e, n_used, w_gu, b_gu, w_down, b_down):
    te = EXPERT_TILE
    n_blocks = xb.shape[0] // te

    def x_map(i, be, nu):
        return (jnp.minimum(i, nu[0] - 1), 0)

    return pl.pallas_call(
        _expert_kernel,
        grid_spec=pltpu.PrefetchScalarGridSpec(
            num_scalar_prefetch=2,
            grid=(n_blocks,),
            in_specs=[
                pl.BlockSpec((te, D_MODEL), x_map),
                pl.BlockSpec((None, D_MODEL, 2 * D_FF), lambda i, be, nu: (be[i], 0, 0)),
                pl.BlockSpec((None, 1, 2 * D_FF), lambda i, be, nu: (be[i], 0, 0)),
                pl.BlockSpec((None, D_FF, D_MODEL), lambda i, be, nu: (be[i], 0, 0)),
                pl.BlockSpec((None, 1, D_MODEL), lambda i, be, nu: (be[i], 0, 0)),
            ],
            out_specs=pl.BlockSpec((te, D_MODEL), lambda i, be, nu: (i, 0)),
            scratch_shapes=[pltpu.VMEM((D_MODEL, 2 * D_FF), BF16), pltpu.VMEM((D_FF, D_MODEL), BF16)],
        ),
        out_shape=jax.ShapeDtypeStruct(xb.shape, F32),
        compiler_params=_cparams("arbitrary"),
        name="experts",
    )(block_e, n_used, xb, w_gu, b_gu.reshape(N_EXPERTS, 1, 2 * D_FF), w_down,
      b_down.reshape(N_EXPERTS, 1, D_MODEL))


def _combine_kernel(row_ref, dest_ref, x_ref, g_ref, mod_ref, fw_ref, yb_ref, xo_ref, rows_s, sem,
                    *, final_norm):
    del row_ref
    tm = x_ref.shape[0]

    def issue(r, carry):
        for k in range(TOP_K):
            d = dest_ref[0, r * TOP_K + k]
            pltpu.make_async_copy(yb_ref.at[pl.ds(d, 1), :], rows_s.at[k, pl.ds(r, 1), :], sem).start()
        return carry

    lax.fori_loop(0, tm, issue, 0)
    for k in range(TOP_K):
        pltpu.make_async_copy(yb_ref.at[pl.ds(0, tm), :], rows_s.at[k], sem).wait()
    g = g_ref[...]
    y = g[:, 0:1] * rows_s[0]
    for k in range(1, TOP_K):
        y = y + g[:, k:k + 1] * rows_s[k]
    x = x_ref[...] + mod_ref[5:6, :] * y
    if final_norm:
        x = _rms(x, fw_ref[...])
    xo_ref[...] = x


def _combine(x, gates, dest_tiles, yb, tile_row, mod_l, final_w, final_norm):
    n = x.shape[0]
    tm = ROW_TILE
    return pl.pallas_call(
        functools.partial(_combine_kernel, final_norm=final_norm),
        grid_spec=pltpu.PrefetchScalarGridSpec(
            num_scalar_prefetch=1,
            grid=(n // tm,),
            in_specs=[
                pl.BlockSpec((None, 1, tm * TOP_K), lambda i, row: (i, 0, 0), memory_space=pltpu.SMEM),
                pl.BlockSpec((tm, D_MODEL), lambda i, row: (i, 0)),
                pl.BlockSpec((tm, LANES), lambda i, row: (i, 0)),
                pl.BlockSpec((None, 6, D_MODEL), lambda i, row: (row[i], 0, 0)),
                pl.BlockSpec((1, D_MODEL), lambda i, row: (0, 0)),
                pl.BlockSpec(memory_space=pl.ANY),
            ],
            out_specs=pl.BlockSpec((tm, D_MODEL), lambda i, row: (i, 0)),
            scratch_shapes=[pltpu.VMEM((TOP_K, tm, D_MODEL), F32), pltpu.SemaphoreType.DMA],
        ),
        out_shape=jax.ShapeDtypeStruct((n, D_MODEL), F32),
        compiler_params=_cparams("arbitrary"),
        name="combine",
    )(tile_row, dest_tiles, x, gates, mod_l, final_w.reshape(1, D_MODEL), yb)


def _moe(x, tile_row, tile_row_small, mod_l, norm_w, router_w, router_b, w_gu, b_gu, w_down, b_down,
         final_w, final_norm):
    n = x.shape[0]
    rw = jnp.zeros((D_MODEL, LANES), F32).at[:, :N_EXPERTS].set(router_w)
    rb = jnp.full((1, LANES), -jnp.inf, F32).at[0, :N_EXPERTS].set(router_b)
    hf, top_e, gates, rank, counts = _router(x, tile_row, mod_l, norm_w, rw, rb)

    te = EXPERT_TILE
    n_blocks = (n * TOP_K) // te + N_EXPERTS
    counts = counts[0, :N_EXPERTS]
    blocks_e = (counts + te - 1) // te
    block_end = jnp.cumsum(blocks_e)
    pstart = (block_end - blocks_e) * te
    n_used = block_end[-1]
    dest = pstart[top_e[:, :TOP_K]] + rank[:, :TOP_K]
    block_ids = jnp.arange(n_blocks, dtype=jnp.int32)
    block_e = jnp.minimum(jnp.searchsorted(block_end, block_ids, side='right'), N_EXPERTS - 1)
    last_e = block_e[jnp.maximum(n_used - 1, 0)]
    block_e = jnp.where(block_ids < n_used, block_e, last_e).astype(jnp.int32)
    dest_tiles = dest.astype(jnp.int32).reshape(n // ROW_TILE, 1, ROW_TILE * TOP_K)

    xb = _dispatch(hf, dest_tiles, jnp.zeros((n_blocks * te, D_MODEL), F32))
    yb = _experts(xb, block_e, n_used.astype(jnp.int32).reshape(1), w_gu, b_gu, w_down, b_down)
    return _combine(x, gates, dest_tiles, yb, tile_row_small, mod_l, final_w, final_norm)


def _tile_rows(n_prompt_tok, n_sample_seq, sample_len, tile):
    starts = np.arange(0, n_prompt_tok + n_sample_seq * sample_len, tile)
    row = np.where(starts < n_prompt_tok, 0, 1 + (starts - n_prompt_tok) // sample_len)
    return jnp.asarray(row, dtype=jnp.int32)


def kernel(x_prompt, x_sample, state_hgrn, c, c_ctx, w_mod, b_mod, norm_mix, norm_ffn, hg_w_in,
           hg_lb_logits, hg_gnorm, hg_w_out, cv_w_in, cv_w, cv_w_out, router_w, router_b,
           moe_w_gu, moe_b_gu, moe_w_down, moe_b_down, final_norm):
    bp, tp, d = x_prompt.shape
    bs, ts, _ = x_sample.shape
    depth = w_mod.shape[0]
    n_prompt = bp * tp
    n = n_prompt + bs * ts
    assert d == D_MODEL and depth == 2 and 1 + bs <= 16
    assert n_prompt % SCAN_ROWS == 0 and ts == SCAN_ROWS and SCAN_ROWS % tp == 0

    x = jnp.concatenate([x_prompt.reshape(n_prompt, d), x_sample.reshape(bs * ts, d)], axis=0)
    cond16 = jnp.zeros((16, d), F32).at[0].set(c_ctx).at[1:1 + bs].set(c)
    mod = _modulation(cond16, w_mod.astype(BF16), b_mod).reshape(depth, 16, 6, d)

    tile_row = _tile_rows(n_prompt, bs, ts, TOKEN_TILE)
    tile_row_small = _tile_rows(n_prompt, bs, ts, ROW_TILE)
    starts = np.arange(0, n, TOKEN_TILE)
    tile_width = jnp.asarray(np.where(starts < n_prompt, tp, GRID_W), dtype=jnp.int32)

    lb_all = jnp.cumsum(jax.nn.softmax(hg_lb_logits.astype(F32), axis=1), axis=1)

    hm = _norm_mod(x, tile_row, mod[0], norm_mix[0])
    w_in_bf = hg_w_in[0].astype(BF16)
    gn = hg_gnorm[0].reshape(-1)
    zero_state = jnp.zeros((SCAN_ROWS // tp, 2, HEADS, HEAD_DIM, HEAD_DIM), F32)
    o_p, s_new = _hgrn_scan(hm[:n_prompt], w_in_bf, lb_all[0, 0], lb_all[1, 0], gn, zero_state,
                            tp, True)
    o_s, _ = _hgrn_scan(hm[n_prompt:], w_in_bf, lb_all[0, 0], lb_all[1, 0], gn, state_hgrn[:, 0],
                        ts, False)
    o = jnp.concatenate([o_p, o_s], axis=0)
    x = _proj_residual(x, o, hg_w_out[0].astype(BF16), tile_row, mod[0])
    x = _moe(x, tile_row, tile_row_small, mod[0], norm_ffn[0], router_w[0], router_b[0],
             moe_w_gu[0], moe_b_gu[0], moe_w_down[0], moe_b_down[0], final_norm, False)

    x = _conv_mixer(x, tile_row, tile_width, mod[1], norm_mix[1], cv_w_in[0].astype(BF16), cv_w[0],
                    cv_w_out[0].astype(BF16))
    x = _moe(x, tile_row, tile_row_small, mod[1], norm_ffn[1], router_w[1], router_b[1],
             moe_w_gu[1], moe_b_gu[1], moe_w_down[1], moe_b_down[1], final_norm, True)

    y_prompt = x[:n_prompt].reshape(bp, tp, d)
    y_sample = x[n_prompt:].reshape(bs, ts, d)
    return (y_prompt, y_sample, s_new.reshape(bp, 1, 2, HEADS, HEAD_DIM, HEAD_DIM))
```

```python
import functools

import numpy as np
import jax
import jax.numpy as jnp
from jax import lax
from jax.experimental import pallas as pl
from jax.experimental.pallas import tpu as pltpu

F32 = jnp.float32
BF16 = jnp.bfloat16

D_MODEL = 1024
HEADS = 8
HEAD_DIM = 128
CHUNK = 32
GRID_W = 64
N_EXPERTS = 32
TOP_K = 4
D_FF = 1024
SWIGLU_LIMIT = 7.0
SWIGLU_ALPHA = 1.702
EPS = 1e-6

LANES = 128
SCAN_ROWS = 2048
SCAN_PIECE = 256
TOKEN_TILE = 512
EXPERT_TILE = 512
ROW_TILE = 256
VMEM_LIMIT = 56 * 1024 * 1024


def _cparams(*sem):
    return pltpu.CompilerParams(dimension_semantics=sem, vmem_limit_bytes=VMEM_LIMIT)


def _sigmoid(x):
    return 1.0 / (1.0 + jnp.exp(-x))


def _rms(x, w):
    return x * lax.rsqrt(jnp.mean(x * x, axis=-1, keepdims=True) + EPS) * w


def _dot(a, b):
    return jnp.dot(a, b, preferred_element_type=F32)


def _dot_nt(a, b):
    return lax.dot_general(a, b, (((1,), (1,)), ((), ())), preferred_element_type=F32)


def _dot_tn(a, b):
    return lax.dot_general(a, b, (((0,), (0,)), ((), ())), preferred_element_type=F32)


def _mod_kernel(cond_ref, w_ref, b_ref, o_ref):
    c = cond_ref[...]
    s = (c * _sigmoid(c)).astype(BF16)
    o_ref[...] = _dot(s, w_ref[...]) + b_ref[...]


def _modulation(cond16, w_mod_bf, b_mod):
    depth = w_mod_bf.shape[0]
    n_out = w_mod_bf.shape[2]
    tn = 1024
    return pl.pallas_call(
        _mod_kernel,
        grid=(depth, n_out // tn),
        in_specs=[
            pl.BlockSpec((16, D_MODEL), lambda l, j: (0, 0)),
            pl.BlockSpec((None, D_MODEL, tn), lambda l, j: (l, 0, j)),
            pl.BlockSpec((None, 1, tn), lambda l, j: (l, 0, j)),
        ],
        out_specs=pl.BlockSpec((None, 16, tn), lambda l, j: (l, 0, j)),
        out_shape=jax.ShapeDtypeStruct((depth, 16, n_out), F32),
        compiler_params=_cparams("arbitrary", "arbitrary"),
        name="modulation",
    )(cond16, w_mod_bf, b_mod.reshape(depth, 1, n_out))


def _norm_mod_kernel(row_ref, x_ref, mod_ref, nw_ref, o_ref):
    del row_ref
    y = _rms(x_ref[...], nw_ref[...])
    o_ref[...] = (y * (1.0 + mod_ref[1:2, :]) + mod_ref[0:1, :]).astype(o_ref.dtype)


def _norm_mod(x, tile_row, mod_l, norm_w):
    n = x.shape[0]
    tm = TOKEN_TILE
    return pl.pallas_call(
        _norm_mod_kernel,
        grid_spec=pltpu.PrefetchScalarGridSpec(
            num_scalar_prefetch=1,
            grid=(n // tm,),
            in_specs=[
                pl.BlockSpec((tm, D_MODEL), lambda i, row: (i, 0)),
                pl.BlockSpec((None, 6, D_MODEL), lambda i, row: (row[i], 0, 0)),
                pl.BlockSpec((1, D_MODEL), lambda i, row: (0, 0)),
            ],
            out_specs=pl.BlockSpec((tm, D_MODEL), lambda i, row: (i, 0)),
        ),
        out_shape=jax.ShapeDtypeStruct((n, D_MODEL), BF16),
        compiler_params=_cparams("arbitrary"),
        name="norm_mod",
    )(tile_row, x, mod_l, norm_w.reshape(1, D_MODEL))


def _hgrn_kernel(hm_ref, wq_ref, wff_ref, wfb_ref, wv_ref, wg_ref, lbf_ref, lbb_ref, gn_ref,
                 s0_ref, o_ref, so_ref, q_s, vt_s, kf_s, lf_s, kb_s, lb_s, of_s, ob_s, st_s,
                 *, seq_len, zero_init):
    rows = hm_ref.shape[0]
    piece = SCAN_PIECE
    n_pieces = rows // piece
    cpp = piece // CHUNK
    per_piece_seq = seq_len == piece
    assert per_piece_seq or seq_len == rows
    hm = hm_ref[...]

    zq = _dot(hm, wq_ref[...])
    q_s[...] = zq * _sigmoid(zq)
    v = _dot(hm, wv_ref[...])
    for p in range(n_pieces):
        vt_s[p] = v[p * piece:(p + 1) * piece, :].T.astype(BF16)
    for w_ref, lbr, k_s, l_s in ((wff_ref, lbf_ref, kf_s, lf_s), (wfb_ref, lbb_ref, kb_s, lb_s)):
        lb = lbr[...]
        f = lb + (1.0 - lb) * _sigmoid(_dot(hm, w_ref[...]))
        k_s[...] = 1.0 - f
        l_s[...] = jnp.log(f)

    def load_state(j, d):
        for h in range(2):
            if zero_init:
                st_s[d, h] = jnp.zeros((HEAD_DIM, HEAD_DIM), F32)
            else:
                st_s[d, h] = s0_ref[j, d, h].T

    def store_state(j, d):
        for h in range(2):
            so_ref[j, d, h] = st_s[d, h].T

    def piece_body(i, carry):
        r = lax.broadcasted_iota(jnp.int32, (piece, piece), 0)
        c = lax.broadcasted_iota(jnp.int32, (piece, piece), 1)
        same = (r // CHUNK) == (c // CHUNK)
        row_chunk = lax.broadcasted_iota(jnp.int32, (piece, 1), 0) // CHUNK
        col_chunk = lax.broadcasted_iota(jnp.int32, (1, piece), 1) // CHUNK
        for d in range(2):
            k_s, l_s, o_s = ((kf_s, lf_s, of_s), (kb_s, lb_s, ob_s))[d]
            p = i if d == 0 else n_pieces - 1 - i
            if per_piece_seq:
                load_state(p, d)
            sl = pl.ds(pl.multiple_of(p * piece, piece), piece)
            keep = same & ((c <= r) if d == 0 else (c >= r))
            tri = jnp.where(keep, 1.0, 0.0).astype(BF16)
            lf = l_s[sl, :]
            hi = lf.astype(BF16)
            r1 = lf - hi.astype(F32)
            mid = r1.astype(BF16)
            lo = (r1 - mid.astype(F32)).astype(BF16)
            b = _dot(tri, hi) + _dot(tri, mid) + _dot(tri, lo)
            b3 = b.reshape(cpp, CHUNK, 2 * HEAD_DIM)
            edge = CHUNK - 1 if d == 0 else 0
            bl3 = b3[:, edge:edge + 1, :]
            q = q_s[sl, :]
            k = k_s[sl, :]
            qd = (q * jnp.exp(b)).astype(BF16)
            ki = (k * jnp.exp(-b)).astype(BF16)
            ks = (k.reshape(cpp, CHUNK, 2 * HEAD_DIM) * jnp.exp(bl3 - b3)).reshape(piece, 2 * HEAD_DIM)
            ks = ks.astype(BF16)
            dec = jnp.exp(bl3)
            vt = vt_s[p]
            for h in range(2):
                hs = slice(h * HEAD_DIM, (h + 1) * HEAD_DIM)
                att = jnp.where(keep, _dot_nt(qd[:, hs], ki[:, hs]), 0.0).astype(BF16)
                vt_h = vt[hs, :]
                v_exp = jnp.concatenate(
                    [jnp.where(col_chunk == ci, vt_h, jnp.zeros_like(vt_h)) for ci in range(cpp)], axis=0)
                incr = _dot(v_exp, ks[:, hs])
                st = st_s[d, h]
                before = [None] * cpp
                for ci in (range(cpp) if d == 0 else range(cpp - 1, -1, -1)):
                    before[ci] = st.astype(BF16)
                    st = st * dec[ci, :, hs] + incr[ci * HEAD_DIM:(ci + 1) * HEAD_DIM, :]
                st_s[d, h] = st
                qd_h = qd[:, hs]
                q_exp = [jnp.where(row_chunk == ci, qd_h, jnp.zeros_like(qd_h)) for ci in range(cpp)]
                lhs = jnp.concatenate([att] + q_exp, axis=1)
                rhs_t = jnp.concatenate([vt_h] + before, axis=1)
                o_s[sl, hs] = _dot_nt(lhs, rhs_t)
            if per_piece_seq:
                store_state(p, d)
        return carry

    if not per_piece_seq:
        load_state(0, 0)
        load_state(0, 1)
    lax.fori_loop(0, n_pieces, piece_body, 0)
    if not per_piece_seq:
        store_state(0, 0)
        store_state(0, 1)

    o = of_s[...] + ob_s[...]
    gn = gn_ref[...]
    o = jnp.concatenate(
        [_rms(o[:, h * HEAD_DIM:(h + 1) * HEAD_DIM], gn[:, h * HEAD_DIM:(h + 1) * HEAD_DIM])
         for h in range(2)], axis=1)
    o_ref[...] = (o * _sigmoid(_dot(hm, wg_ref[...]))).astype(o_ref.dtype)


def _hgrn_scan(hm, row0, n_rows, w_in_bf, lb_f, lb_b, gnorm, s0, seq_len, zero_init):
    n_seq_total = n_rows // seq_len
    rows = SCAN_ROWS
    seq_per_step = rows // seq_len
    blk0 = row0 // rows
    pair = 2 * HEAD_DIM
    n_pairs = HEADS // 2

    def w_spec(seg):
        return pl.BlockSpec((D_MODEL, pair), lambda sb, hp: (0, seg * n_pairs + hp))

    vec_spec = pl.BlockSpec((1, pair), lambda sb, hp: (0, hp))
    st_block = (seq_per_step, 2, 2, HEAD_DIM, HEAD_DIM)
    st_spec = pl.BlockSpec(st_block, lambda sb, hp: (sb, 0, hp, 0, 0))
    s0_spec = pl.BlockSpec(st_block, (lambda sb, hp: (0, 0, hp, 0, 0)) if zero_init
                           else (lambda sb, hp: (sb, 0, hp, 0, 0)))
    scratch = [
        pltpu.VMEM((rows, pair), F32),
        pltpu.VMEM((rows // SCAN_PIECE, pair, SCAN_PIECE), BF16),
        pltpu.VMEM((rows, pair), F32),
        pltpu.VMEM((rows, pair), F32),
        pltpu.VMEM((rows, pair), F32),
        pltpu.VMEM((rows, pair), F32),
        pltpu.VMEM((rows, pair), F32),
        pltpu.VMEM((rows, pair), F32),
        pltpu.VMEM((2, 2, HEAD_DIM, HEAD_DIM), F32),
    ]
    in_specs = [
        pl.BlockSpec((rows, D_MODEL), lambda sb, hp: (blk0 + sb, 0)),
        w_spec(0), w_spec(1), w_spec(2), w_spec(3), w_spec(4),
        vec_spec, vec_spec, vec_spec,
        s0_spec,
    ]
    args = [hm, w_in_bf, w_in_bf, w_in_bf, w_in_bf, w_in_bf,
            lb_f.reshape(1, D_MODEL), lb_b.reshape(1, D_MODEL), gnorm.reshape(1, D_MODEL), s0]
    return pl.pallas_call(
        functools.partial(_hgrn_kernel, seq_len=seq_len, zero_init=zero_init),
        grid=(n_rows // rows, n_pairs),
        in_specs=in_specs,
        out_specs=[
            pl.BlockSpec((rows, pair), lambda sb, hp: (sb, hp)),
            st_spec,
        ],
        out_shape=[
            jax.ShapeDtypeStruct((n_rows, D_MODEL), BF16),
            jax.ShapeDtypeStruct((n_seq_total, 2, HEADS, HEAD_DIM, HEAD_DIM), F32),
        ],
        scratch_shapes=scratch,
        compiler_params=_cparams("arbitrary", "arbitrary"),
        name="hgrn_scan",
    )(*args)


def _proj_res_kernel(row_ref, x_ref, op_ref, os_ref, w_ref, mod_ref, xo_ref, *, prompt_tiles):
    del row_ref
    o = jnp.where(pl.program_id(0) < prompt_tiles, op_ref[...], os_ref[...])
    xo_ref[...] = x_ref[...] + mod_ref[2:3, :] * _dot(o, w_ref[...])


def _proj_residual(x, o_prompt, o_sample, w_out_bf, tile_row, mod_l):
    n = x.shape[0]
    tm = TOKEN_TILE
    pt = o_prompt.shape[0] // tm
    return pl.pallas_call(
        functools.partial(_proj_res_kernel, prompt_tiles=pt),
        grid_spec=pltpu.PrefetchScalarGridSpec(
            num_scalar_prefetch=1,
            grid=(n // tm,),
            in_specs=[
                pl.BlockSpec((tm, D_MODEL), lambda i, row: (i, 0)),
                pl.BlockSpec((tm, D_MODEL), lambda i, row: (jnp.minimum(i, pt - 1), 0)),
                pl.BlockSpec((tm, D_MODEL), lambda i, row: (jnp.maximum(i - pt, 0), 0)),
                pl.BlockSpec((D_MODEL, D_MODEL), lambda i, row: (0, 0)),
                pl.BlockSpec((None, 6, D_MODEL), lambda i, row: (row[i], 0, 0)),
            ],
            out_specs=pl.BlockSpec((tm, D_MODEL), lambda i, row: (i, 0)),
        ),
        out_shape=jax.ShapeDtypeStruct((n, D_MODEL), F32),
        compiler_params=_cparams("arbitrary"),
        name="proj_residual",
    )(tile_row, x, o_prompt, o_sample, w_out_bf, mod_l)


def _conv_kernel(row_ref, width_ref, x_ref, mod_ref, nw_ref, win_ref, cw_ref, wout_ref, xo_ref):
    del row_ref
    i = pl.program_id(0)
    x = x_ref[...]
    hm = (_rms(x, nw_ref[...]) * (1.0 + mod_ref[1:2, :]) + mod_ref[0:1, :]).astype(BF16)
    z = _dot(hm, win_ref[...])
    bg = z[:, :D_MODEL]
    u = z[:, D_MODEL:2 * D_MODEL] * z[:, 2 * D_MODEL:]
    tm = x.shape[0]
    pos = lax.broadcasted_iota(jnp.int32, (tm, 1), 0) & (width_ref[i] - 1)
    prev = jnp.where(pos == 0, 0.0, pltpu.roll(u, 1, axis=0))
    nxt = jnp.where(pos == width_ref[i] - 1, 0.0, pltpu.roll(u, tm - 1, axis=0))
    v = cw_ref[0:1, :] * prev + cw_ref[1:2, :] * u + cw_ref[2:3, :] * nxt
    y = _dot((bg * v).astype(BF16), wout_ref[...])
    xo_ref[...] = x + mod_ref[2:3, :] * y


def _conv_mixer(x, tile_row, tile_width, mod_l, norm_w, w_in_bf, conv_w, w_out_bf):
    n = x.shape[0]
    tm = TOKEN_TILE
    return pl.pallas_call(
        _conv_kernel,
        grid_spec=pltpu.PrefetchScalarGridSpec(
            num_scalar_prefetch=2,
            grid=(n // tm,),
            in_specs=[
                pl.BlockSpec((tm, D_MODEL), lambda i, row, w: (i, 0)),
                pl.BlockSpec((None, 6, D_MODEL), lambda i, row, w: (row[i], 0, 0)),
                pl.BlockSpec((1, D_MODEL), lambda i, row, w: (0, 0)),
                pl.BlockSpec((D_MODEL, 3 * D_MODEL), lambda i, row, w: (0, 0)),
                pl.BlockSpec((3, D_MODEL), lambda i, row, w: (0, 0)),
                pl.BlockSpec((D_MODEL, D_MODEL), lambda i, row, w: (0, 0)),
            ],
            out_specs=pl.BlockSpec((tm, D_MODEL), lambda i, row, w: (i, 0)),
        ),
        out_shape=jax.ShapeDtypeStruct((n, D_MODEL), F32),
        compiler_params=_cparams("arbitrary"),
        name="conv_mixer",
    )(tile_row, tile_width, x, mod_l, norm_w.reshape(1, D_MODEL), w_in_bf, conv_w, w_out_bf)


def _router_kernel(row_ref, x_ref, mod_ref, nw_ref, rw_ref, rb_ref,
                   hf_ref, e_ref, g_ref, rank_ref, cnt_ref, carry_s):
    del row_ref
    i = pl.program_id(0)

    @pl.when(i == 0)
    def _():
        carry_s[...] = jnp.zeros_like(carry_s)

    hf = _rms(x_ref[...], nw_ref[...]) * (1.0 + mod_ref[4:5, :]) + mod_ref[3:4, :]
    hf_ref[...] = hf
    logits = jnp.dot(hf, rw_ref[...], preferred_element_type=F32,
                     precision=lax.Precision.HIGHEST) + rb_ref[...]
    tm = hf.shape[0]
    lane = lax.broadcasted_iota(jnp.int32, (tm, LANES), 1)
    lane_f = lane.astype(F32)
    work = logits
    vals, idxs, hots = [], [], []
    for _ in range(TOP_K):
        m = jnp.max(work, axis=-1, keepdims=True)
        idx_f = jnp.min(jnp.where(work == m, lane_f, float(LANES)), axis=-1, keepdims=True)
        hot = lane_f == idx_f
        vals.append(m)
        idxs.append(idx_f.astype(jnp.int32))
        hots.append(hot)
        work = jnp.where(hot, -jnp.inf, work)
    ex = [jnp.exp(v - vals[0]) for v in vals]
    inv = 1.0 / (ex[0] + ex[1] + ex[2] + ex[3])

    multi = jnp.where(hots[0] | hots[1] | hots[2] | hots[3], 1.0, 0.0)
    tr = lax.broadcasted_iota(jnp.int32, (tm, tm), 0)
    tc = lax.broadcasted_iota(jnp.int32, (tm, tm), 1)
    strict = jnp.where(tc < tr, 1.0, 0.0).astype(BF16)
    before = _dot(strict, multi.astype(BF16)) + carry_s[...]
    carry_s[...] = carry_s[...] + jnp.sum(multi, axis=0, keepdims=True)
    cnt_ref[...] = carry_s[...].astype(jnp.int32)

    e_out = jnp.zeros((tm, LANES), jnp.int32)
    g_out = jnp.zeros((tm, LANES), F32)
    r_out = jnp.zeros((tm, LANES), jnp.int32)
    for k in range(TOP_K):
        rank = jnp.sum(jnp.where(hots[k], before, 0.0), axis=-1, keepdims=True).astype(jnp.int32)
        e_out = jnp.where(lane == k, idxs[k], e_out)
        g_out = jnp.where(lane == k, ex[k] * inv, g_out)
        r_out = jnp.where(lane == k, rank, r_out)
    e_ref[...] = e_out
    g_ref[...] = g_out
    rank_ref[...] = r_out


def _router(x, tile_row, mod_l, norm_w, router_w_pad, router_b_pad):
    n = x.shape[0]
    tm = TOKEN_TILE
    tok_spec = pl.BlockSpec((tm, LANES), lambda i, row: (i, 0))
    return pl.pallas_call(
        _router_kernel,
        grid_spec=pltpu.PrefetchScalarGridSpec(
            num_scalar_prefetch=1,
            grid=(n // tm,),
            in_specs=[
                pl.BlockSpec((tm, D_MODEL), lambda i, row: (i, 0)),
                pl.BlockSpec((None, 6, D_MODEL), lambda i, row: (row[i], 0, 0)),
                pl.BlockSpec((1, D_MODEL), lambda i, row: (0, 0)),
                pl.BlockSpec((D_MODEL, LANES), lambda i, row: (0, 0)),
                pl.BlockSpec((1, LANES), lambda i, row: (0, 0)),
            ],
            out_specs=[
                pl.BlockSpec((tm, D_MODEL), lambda i, row: (i, 0)),
                tok_spec, tok_spec, tok_spec,
                pl.BlockSpec((1, LANES), lambda i, row: (0, 0)),
            ],
            scratch_shapes=[pltpu.VMEM((1, LANES), F32)],
        ),
        out_shape=[
            jax.ShapeDtypeStruct((n, D_MODEL), F32),
            jax.ShapeDtypeStruct((n, LANES), jnp.int32),
            jax.ShapeDtypeStruct((n, LANES), F32),
            jax.ShapeDtypeStruct((n, LANES), jnp.int32),
            jax.ShapeDtypeStruct((1, LANES), jnp.int32),
        ],
        compiler_params=_cparams("arbitrary"),
        name="router",
    )(tile_row, x, mod_l, norm_w.reshape(1, D_MODEL), router_w_pad, router_b_pad)


def _dispatch_kernel(dest_ref, hf_ref, xb_in_ref, xb_ref, sem):
    del xb_in_ref
    tm = hf_ref.shape[0]

    def issue(r, carry):
        for k in range(TOP_K):
            d = dest_ref[0, r * TOP_K + k]
            pltpu.make_async_copy(hf_ref.at[pl.ds(r, 1), :], xb_ref.at[pl.ds(d, 1), :], sem).start()
        return carry

    lax.fori_loop(0, tm, issue, 0)
    for _ in range(TOP_K):
        pltpu.make_async_copy(hf_ref, xb_ref.at[pl.ds(0, tm), :], sem).wait()


def _dispatch(hf, dest_tiles, xb_zero):
    n = hf.shape[0]
    tm = ROW_TILE
    return pl.pallas_call(
        _dispatch_kernel,
        grid=(n // tm,),
        in_specs=[
            pl.BlockSpec((None, 1, tm * TOP_K), lambda i: (i, 0, 0), memory_space=pltpu.SMEM),
            pl.BlockSpec((tm, D_MODEL), lambda i: (i, 0)),
            pl.BlockSpec(memory_space=pl.ANY),
        ],
        out_specs=pl.BlockSpec(memory_space=pl.ANY),
        out_shape=jax.ShapeDtypeStruct(xb_zero.shape, xb_zero.dtype),
        scratch_shapes=[pltpu.SemaphoreType.DMA],
        input_output_aliases={2: 0},
        compiler_params=_cparams("arbitrary"),
        name="dispatch",
    )(dest_tiles, hf, xb_zero)


def _expert_kernel(be_ref, nu_ref, xb_ref, wgu_ref, bgu_ref, wd_ref, bd_ref, yb_ref, wgu_s, wd_s):
    i = pl.program_id(0)

    @pl.when(i < nu_ref[0])
    def _():
        changed = jnp.logical_or(i == 0, be_ref[i] != be_ref[jnp.maximum(i - 1, 0)])

        @pl.when(changed)
        def _():
            wgu_s[...] = wgu_ref[...].astype(BF16)
            wd_s[...] = wd_ref[...].astype(BF16)

        gu = _dot(xb_ref[...].astype(BF16), wgu_s[...]) + bgu_ref[...]
        gate = jnp.minimum(gu[:, :D_FF], SWIGLU_LIMIT)
        up = jnp.clip(gu[:, D_FF:], -SWIGLU_LIMIT, SWIGLU_LIMIT)
        act = (up + 1.0) * gate * _sigmoid(SWIGLU_ALPHA * gate)
        yb_ref[...] = _dot(act.astype(BF16), wd_s[...]) + bd_ref[...]

    @pl.when(i >= nu_ref[0])
    def _():
        yb_ref[...] = jnp.zeros_like(yb_ref)


def _experts(xb, block_e, n_used, layer, w_gu, b_gu, w_down, b_down):
    te = EXPERT_TILE
    n_blocks = xb.shape[0] // te
    depth = w_gu.shape[0]

    def x_map(i, be, nu):
        return (jnp.minimum(i, nu[0] - 1), 0)

    return pl.pallas_call(
        _expert_kernel,
        grid_spec=pltpu.PrefetchScalarGridSpec(
            num_scalar_prefetch=2,
            grid=(n_blocks,),
            in_specs=[
                pl.BlockSpec((te, D_MODEL), x_map),
                pl.BlockSpec((None, None, D_MODEL, 2 * D_FF), lambda i, be, nu: (layer, be[i], 0, 0)),
                pl.BlockSpec((None, None, 1, 2 * D_FF), lambda i, be, nu: (layer, be[i], 0, 0)),
                pl.BlockSpec((None, None, D_FF, D_MODEL), lambda i, be, nu: (layer, be[i], 0, 0)),
                pl.BlockSpec((None, None, 1, D_MODEL), lambda i, be, nu: (layer, be[i], 0, 0)),
            ],
            out_specs=pl.BlockSpec((te, D_MODEL), lambda i, be, nu: (i, 0)),
            scratch_shapes=[pltpu.VMEM((D_MODEL, 2 * D_FF), BF16), pltpu.VMEM((D_FF, D_MODEL), BF16)],
        ),
        out_shape=jax.ShapeDtypeStruct(xb.shape, F32),
        compiler_params=_cparams("arbitrary"),
        name="experts",
    )(block_e, n_used, xb, w_gu, b_gu.reshape(depth, N_EXPERTS, 1, 2 * D_FF), w_down,
      b_down.reshape(depth, N_EXPERTS, 1, D_MODEL))


def _combine_kernel(row_ref, dest_ref, x_ref, g_ref, mod_ref, fw_ref, yb_ref, xo_ref, rows_s, sem,
                    *, final_norm):
    del row_ref
    tm = x_ref.shape[0]

    def issue(r, carry):
        for k in range(TOP_K):
            d = dest_ref[0, r * TOP_K + k]
            pltpu.make_async_copy(yb_ref.at[pl.ds(d, 1), :], rows_s.at[k, pl.ds(r, 1), :], sem).start()
        return carry

    lax.fori_loop(0, tm, issue, 0)
    for k in range(TOP_K):
        pltpu.make_async_copy(yb_ref.at[pl.ds(0, tm), :], rows_s.at[k], sem).wait()
    g = g_ref[...]
    y = g[:, 0:1] * rows_s[0]
    for k in range(1, TOP_K):
        y = y + g[:, k:k + 1] * rows_s[k]
    x = x_ref[...] + mod_ref[5:6, :] * y
    if final_norm:
        x = _rms(x, fw_ref[...])
    xo_ref[...] = x


def _combine(x, gates, dest_tiles, yb, tile_row, mod_l, final_w, final_norm):
    n = x.shape[0]
    tm = ROW_TILE
    return pl.pallas_call(
        functools.partial(_combine_kernel, final_norm=final_norm),
        grid_spec=pltpu.PrefetchScalarGridSpec(
            num_scalar_prefetch=1,
            grid=(n // tm,),
            in_specs=[
                pl.BlockSpec((None, 1, tm * TOP_K), lambda i, row: (i, 0, 0), memory_space=pltpu.SMEM),
                pl.BlockSpec((tm, D_MODEL), lambda i, row: (i, 0)),
                pl.BlockSpec((tm, LANES), lambda i, row: (i, 0)),
                pl.BlockSpec((None, 6, D_MODEL), lambda i, row: (row[i], 0, 0)),
                pl.BlockSpec((1, D_MODEL), lambda i, row: (0, 0)),
                pl.BlockSpec(memory_space=pl.ANY),
            ],
            out_specs=pl.BlockSpec((tm, D_MODEL), lambda i, row: (i, 0)),
            scratch_shapes=[pltpu.VMEM((TOP_K, tm, D_MODEL), F32), pltpu.SemaphoreType.DMA],
        ),
        out_shape=jax.ShapeDtypeStruct((n, D_MODEL), F32),
        compiler_params=_cparams("arbitrary"),
        name="combine",
    )(tile_row, dest_tiles, x, gates, mod_l, final_w.reshape(1, D_MODEL), yb)


def _moe(x, tile_row, tile_row_small, mod_l, norm_w, router_w, router_b, layer, w_gu, b_gu, w_down,
         b_down, final_w, final_norm):
    n = x.shape[0]
    rw = jnp.zeros((D_MODEL, LANES), F32).at[:, :N_EXPERTS].set(router_w)
    rb = jnp.full((1, LANES), -jnp.inf, F32).at[0, :N_EXPERTS].set(router_b)
    hf, top_e, gates, rank, counts = _router(x, tile_row, mod_l, norm_w, rw, rb)

    te = EXPERT_TILE
    n_blocks = (n * TOP_K) // te + N_EXPERTS
    counts = counts[0, :N_EXPERTS]
    blocks_e = (counts + te - 1) // te
    block_end = jnp.cumsum(blocks_e)
    pstart = (block_end - blocks_e) * te
    n_used = block_end[-1]
    dest = pstart[top_e[:, :TOP_K]] + rank[:, :TOP_K]
    block_ids = jnp.arange(n_blocks, dtype=jnp.int32)
    clamped = jnp.minimum(block_ids, n_used - 1)
    block_e = jnp.sum((clamped[:, None] >= block_end[None, :]).astype(jnp.int32), axis=1)
    block_e = jnp.minimum(block_e, N_EXPERTS - 1).astype(jnp.int32)
    dest_tiles = dest.astype(jnp.int32).reshape(n // ROW_TILE, 1, ROW_TILE * TOP_K)

    xb = _dispatch(hf, dest_tiles, jnp.zeros((n_blocks * te, D_MODEL), F32))
    yb = _experts(xb, block_e, n_used.astype(jnp.int32).reshape(1), layer, w_gu, b_gu, w_down, b_down)
    return _combine(x, gates, dest_tiles, yb, tile_row_small, mod_l, final_w, final_norm)


def _tile_rows(n_prompt_tok, n_sample_seq, sample_len, tile):
    starts = np.arange(0, n_prompt_tok + n_sample_seq * sample_len, tile)
    row = np.where(starts < n_prompt_tok, 0, 1 + (starts - n_prompt_tok) // sample_len)
    return jnp.asarray(row, dtype=jnp.int32)


def kernel(x_prompt, x_sample, state_hgrn, c, c_ctx, w_mod, b_mod, norm_mix, norm_ffn, hg_w_in,
           hg_lb_logits, hg_gnorm, hg_w_out, cv_w_in, cv_w, cv_w_out, router_w, router_b,
           moe_w_gu, moe_b_gu, moe_w_down, moe_b_down, final_norm):
    bp, tp, d = x_prompt.shape
    bs, ts, _ = x_sample.shape
    depth = w_mod.shape[0]
    n_prompt = bp * tp
    n = n_prompt + bs * ts
    assert d == D_MODEL and depth == 2 and 1 + bs <= 16
    assert n_prompt % SCAN_ROWS == 0 and ts == SCAN_ROWS and tp == SCAN_PIECE

    x = jnp.concatenate([x_prompt.reshape(n_prompt, d), x_sample.reshape(bs * ts, d)], axis=0)
    cond16 = jnp.zeros((16, d), F32).at[0].set(c_ctx).at[1:1 + bs].set(c)
    mod = _modulation(cond16, w_mod.astype(BF16), b_mod).reshape(depth, 16, 6, d)

    tile_row = _tile_rows(n_prompt, bs, ts, TOKEN_TILE)
    tile_row_small = _tile_rows(n_prompt, bs, ts, ROW_TILE)
    starts = np.arange(0, n, TOKEN_TILE)
    tile_width = jnp.asarray(np.where(starts < n_prompt, tp, GRID_W), dtype=jnp.int32)

    lb_all = jnp.cumsum(jax.nn.softmax(hg_lb_logits.astype(F32), axis=1), axis=1)

    hm = _norm_mod(x, tile_row, mod[0], norm_mix[0])
    w_in_bf = hg_w_in[0].astype(BF16)
    gn = hg_gnorm[0].reshape(-1)
    zero_state = jnp.zeros((SCAN_ROWS // tp, 2, HEADS, HEAD_DIM, HEAD_DIM), F32)
    o_p, s_new = _hgrn_scan(hm, 0, n_prompt, w_in_bf, lb_all[0, 0], lb_all[1, 0], gn, zero_state,
                            tp, True)
    o_s, _ = _hgrn_scan(hm, n_prompt, bs * ts, w_in_bf, lb_all[0, 0], lb_all[1, 0], gn,
                        state_hgrn[:, 0], ts, False)
    x = _proj_residual(x, o_p, o_s, hg_w_out[0].astype(BF16), tile_row, mod[0])
    x = _moe(x, tile_row, tile_row_small, mod[0], norm_ffn[0], router_w[0], router_b[0],
             0, moe_w_gu, moe_b_gu, moe_w_down, moe_b_down, final_norm, False)

    x = _conv_mixer(x, tile_row, tile_width, mod[1], norm_mix[1], cv_w_in[0].astype(BF16), cv_w[0],
                    cv_w_out[0].astype(BF16))
    x = _moe(x, tile_row, tile_row_small, mod[1], norm_ffn[1], router_w[1], router_b[1],
             1, moe_w_gu, moe_b_gu, moe_w_down, moe_b_down, final_norm, True)

    y_prompt = x[:n_prompt].reshape(bp, tp, d)
    y_sample = x[n_prompt:].reshape(bs, ts, d)
    return (y_prompt, y_sample, s_new.reshape(bp, 1, 2, HEADS, HEAD_DIM, HEAD_DIM))
```

```python
import functools

import numpy as np
import jax
import jax.numpy as jnp
from jax import lax
from jax.experimental import pallas as pl
from jax.experimental.pallas import tpu as pltpu

F32 = jnp.float32
BF16 = jnp.bfloat16

D_MODEL = 1024
HEADS = 8
HEAD_DIM = 128
CHUNK = 32
GRID_W = 64
N_EXPERTS = 32
TOP_K = 4
D_FF = 1024
SWIGLU_LIMIT = 7.0
SWIGLU_ALPHA = 1.702
EPS = 1e-6

LANES = 128
SCAN_ROWS = 2048
SCAN_PIECE = 256
TOKEN_TILE = 512
EXPERT_TILE = 512
ROUTE_TILE = 256
SEG_ALIGN = 8
SLOTS = ROUTE_TILE * TOP_K + N_EXPERTS * SEG_ALIGN
VMEM_LIMIT = 56 * 1024 * 1024


def _cparams(*sem):
    return pltpu.CompilerParams(dimension_semantics=sem, vmem_limit_bytes=VMEM_LIMIT)


def _sigmoid(x):
    return 1.0 / (1.0 + jnp.exp(-x))


def _rms(x, w):
    return x * lax.rsqrt(jnp.mean(x * x, axis=-1, keepdims=True) + EPS) * w


def _dot(a, b):
    return jnp.dot(a, b, preferred_element_type=F32)


def _dot_nt(a, b):
    return lax.dot_general(a, b, (((1,), (1,)), ((), ())), preferred_element_type=F32)


def _dot_tn(a, b):
    return lax.dot_general(a, b, (((0,), (0,)), ((), ())), preferred_element_type=F32)


def _mod_kernel(cond_ref, w_ref, b_ref, o_ref):
    c = cond_ref[...]
    s = (c * _sigmoid(c)).astype(BF16)
    o_ref[...] = _dot(s, w_ref[...]) + b_ref[...]


def _modulation(cond16, w_mod_bf, b_mod):
    depth = w_mod_bf.shape[0]
    n_out = w_mod_bf.shape[2]
    tn = 1024
    return pl.pallas_call(
        _mod_kernel,
        grid=(depth, n_out // tn),
        in_specs=[
            pl.BlockSpec((16, D_MODEL), lambda l, j: (0, 0)),
            pl.BlockSpec((None, D_MODEL, tn), lambda l, j: (l, 0, j)),
            pl.BlockSpec((None, 1, tn), lambda l, j: (l, 0, j)),
        ],
        out_specs=pl.BlockSpec((None, 16, tn), lambda l, j: (l, 0, j)),
        out_shape=jax.ShapeDtypeStruct((depth, 16, n_out), F32),
        compiler_params=_cparams("arbitrary", "arbitrary"),
        name="modulation",
    )(cond16, w_mod_bf, b_mod.reshape(depth, 1, n_out))


def _norm_mod_kernel(row_ref, x_ref, mod_ref, nw_ref, o_ref):
    del row_ref
    y = _rms(x_ref[...], nw_ref[...])
    o_ref[...] = (y * (1.0 + mod_ref[1:2, :]) + mod_ref[0:1, :]).astype(o_ref.dtype)


def _norm_mod(x, tile_row, mod_l, norm_w):
    n = x.shape[0]
    tm = TOKEN_TILE
    return pl.pallas_call(
        _norm_mod_kernel,
        grid_spec=pltpu.PrefetchScalarGridSpec(
            num_scalar_prefetch=1,
            grid=(n // tm,),
            in_specs=[
                pl.BlockSpec((tm, D_MODEL), lambda i, row: (i, 0)),
                pl.BlockSpec((None, 6, D_MODEL), lambda i, row: (row[i], 0, 0)),
                pl.BlockSpec((1, D_MODEL), lambda i, row: (0, 0)),
            ],
            out_specs=pl.BlockSpec((tm, D_MODEL), lambda i, row: (i, 0)),
        ),
        out_shape=jax.ShapeDtypeStruct((n, D_MODEL), BF16),
        compiler_params=_cparams("arbitrary"),
        name="norm_mod",
    )(tile_row, x, mod_l, norm_w.reshape(1, D_MODEL))


def _hgrn_kernel(hm_ref, wq_ref, wff_ref, wfb_ref, wv_ref, wg_ref, lbf_ref, lbb_ref, gn_ref,
                 s0_ref, o_ref, so_ref, q_s, vt_s, kf_s, lf_s, kb_s, lb_s, of_s, ob_s, st_s,
                 *, seq_len, zero_init):
    rows = hm_ref.shape[0]
    piece = SCAN_PIECE
    n_pieces = rows // piece
    cpp = piece // CHUNK
    per_piece_seq = seq_len == piece
    assert per_piece_seq or seq_len == rows
    hm = hm_ref[...]

    zq = _dot(hm, wq_ref[...])
    q_s[...] = zq * _sigmoid(zq)
    v = _dot(hm, wv_ref[...])
    for p in range(n_pieces):
        vt_s[p] = v[p * piece:(p + 1) * piece, :].T.astype(BF16)
    for w_ref, lbr, k_s, l_s in ((wff_ref, lbf_ref, kf_s, lf_s), (wfb_ref, lbb_ref, kb_s, lb_s)):
        lb = lbr[...]
        f = lb + (1.0 - lb) * _sigmoid(_dot(hm, w_ref[...]))
        k_s[...] = 1.0 - f
        l_s[...] = jnp.log(f)

    def load_state(j, d):
        for h in range(2):
            if zero_init:
                st_s[d, h] = jnp.zeros((HEAD_DIM, HEAD_DIM), F32)
            else:
                st_s[d, h] = s0_ref[j, d, h].T

    def store_state(j, d):
        for h in range(2):
            so_ref[j, d, h] = st_s[d, h].T

    def piece_body(i, carry):
        r = lax.broadcasted_iota(jnp.int32, (piece, piece), 0)
        c = lax.broadcasted_iota(jnp.int32, (piece, piece), 1)
        same = (r // CHUNK) == (c // CHUNK)
        row_chunk = lax.broadcasted_iota(jnp.int32, (piece, 1), 0) // CHUNK
        col_chunk = lax.broadcasted_iota(jnp.int32, (1, piece), 1) // CHUNK
        for d in range(2):
            k_s, l_s, o_s = ((kf_s, lf_s, of_s), (kb_s, lb_s, ob_s))[d]
            p = i if d == 0 else n_pieces - 1 - i
            if per_piece_seq:
                load_state(p, d)
            sl = pl.ds(pl.multiple_of(p * piece, piece), piece)
            keep = same & ((c <= r) if d == 0 else (c >= r))
            tri = jnp.where(keep, 1.0, 0.0).astype(BF16)
            lf = l_s[sl, :]
            hi = lf.astype(BF16)
            r1 = lf - hi.astype(F32)
            mid = r1.astype(BF16)
            lo = (r1 - mid.astype(F32)).astype(BF16)
            b = _dot(tri, hi) + _dot(tri, mid) + _dot(tri, lo)
            b3 = b.reshape(cpp, CHUNK, 2 * HEAD_DIM)
            edge = CHUNK - 1 if d == 0 else 0
            bl3 = b3[:, edge:edge + 1, :]
            q = q_s[sl, :]
            k = k_s[sl, :]
            qd = (q * jnp.exp(b)).astype(BF16)
            ki = (k * jnp.exp(-b)).astype(BF16)
            ks = (k.reshape(cpp, CHUNK, 2 * HEAD_DIM) * jnp.exp(bl3 - b3)).reshape(piece, 2 * HEAD_DIM)
            ks = ks.astype(BF16)
            dec = jnp.exp(bl3)
            vt = vt_s[p]
            for h in range(2):
                hs = slice(h * HEAD_DIM, (h + 1) * HEAD_DIM)
                att = jnp.where(keep, _dot_nt(qd[:, hs], ki[:, hs]), 0.0).astype(BF16)
                vt_h = vt[hs, :]
                v_exp = jnp.concatenate(
                    [jnp.where(col_chunk == ci, vt_h, jnp.zeros_like(vt_h)) for ci in range(cpp)], axis=0)
                incr = _dot(v_exp, ks[:, hs])
                st = st_s[d, h]
                before = [None] * cpp
                for ci in (range(cpp) if d == 0 else range(cpp - 1, -1, -1)):
                    before[ci] = st.astype(BF16)
                    st = st * dec[ci, :, hs] + incr[ci * HEAD_DIM:(ci + 1) * HEAD_DIM, :]
                st_s[d, h] = st
                qd_h = qd[:, hs]
                q_exp = [jnp.where(row_chunk == ci, qd_h, jnp.zeros_like(qd_h)) for ci in range(cpp)]
                lhs = jnp.concatenate([att] + q_exp, axis=1)
                rhs_t = jnp.concatenate([vt_h] + before, axis=1)
                o_s[sl, hs] = _dot_nt(lhs, rhs_t)
            if per_piece_seq:
                store_state(p, d)
        return carry

    if not per_piece_seq:
        load_state(0, 0)
        load_state(0, 1)
    lax.fori_loop(0, n_pieces, piece_body, 0)
    if not per_piece_seq:
        store_state(0, 0)
        store_state(0, 1)

    o = of_s[...] + ob_s[...]
    gn = gn_ref[...]
    o = jnp.concatenate(
        [_rms(o[:, h * HEAD_DIM:(h + 1) * HEAD_DIM], gn[:, h * HEAD_DIM:(h + 1) * HEAD_DIM])
         for h in range(2)], axis=1)
    o_ref[...] = (o * _sigmoid(_dot(hm, wg_ref[...]))).astype(o_ref.dtype)


def _hgrn_scan(hm, row0, n_rows, w_in_bf, lb_f, lb_b, gnorm, s0, seq_len, zero_init):
    n_seq_total = n_rows // seq_len
    rows = SCAN_ROWS
    seq_per_step = rows // seq_len
    blk0 = row0 // rows
    pair = 2 * HEAD_DIM
    n_pairs = HEADS // 2

    def w_spec(seg):
        return pl.BlockSpec((D_MODEL, pair), lambda sb, hp: (0, seg * n_pairs + hp))

    vec_spec = pl.BlockSpec((1, pair), lambda sb, hp: (0, hp))
    st_block = (seq_per_step, 2, 2, HEAD_DIM, HEAD_DIM)
    st_spec = pl.BlockSpec(st_block, lambda sb, hp: (sb, 0, hp, 0, 0))
    s0_spec = pl.BlockSpec(st_block, (lambda sb, hp: (0, 0, hp, 0, 0)) if zero_init
                           else (lambda sb, hp: (sb, 0, hp, 0, 0)))
    scratch = [
        pltpu.VMEM((rows, pair), F32),
        pltpu.VMEM((rows // SCAN_PIECE, pair, SCAN_PIECE), BF16),
        pltpu.VMEM((rows, pair), F32),
        pltpu.VMEM((rows, pair), F32),
        pltpu.VMEM((rows, pair), F32),
        pltpu.VMEM((rows, pair), F32),
        pltpu.VMEM((rows, pair), F32),
        pltpu.VMEM((rows, pair), F32),
        pltpu.VMEM((2, 2, HEAD_DIM, HEAD_DIM), F32),
    ]
    in_specs = [
        pl.BlockSpec((rows, D_MODEL), lambda sb, hp: (blk0 + sb, 0)),
        w_spec(0), w_spec(1), w_spec(2), w_spec(3), w_spec(4),
        vec_spec, vec_spec, vec_spec,
        s0_spec,
    ]
    args = [hm, w_in_bf, w_in_bf, w_in_bf, w_in_bf, w_in_bf,
            lb_f.reshape(1, D_MODEL), lb_b.reshape(1, D_MODEL), gnorm.reshape(1, D_MODEL), s0]
    return pl.pallas_call(
        functools.partial(_hgrn_kernel, seq_len=seq_len, zero_init=zero_init),
        grid=(n_rows // rows, n_pairs),
        in_specs=in_specs,
        out_specs=[
            pl.BlockSpec((rows, pair), lambda sb, hp: (sb, hp)),
            st_spec,
        ],
        out_shape=[
            jax.ShapeDtypeStruct((n_rows, D_MODEL), BF16),
            jax.ShapeDtypeStruct((n_seq_total, 2, HEADS, HEAD_DIM, HEAD_DIM), F32),
        ],
        scratch_shapes=scratch,
        compiler_params=_cparams("arbitrary", "arbitrary"),
        name="hgrn_scan",
    )(*args)


def _proj_res_kernel(row_ref, x_ref, op_ref, os_ref, w_ref, mod_ref, xo_ref, *, prompt_tiles):
    del row_ref
    o = jnp.where(pl.program_id(0) < prompt_tiles, op_ref[...], os_ref[...])
    xo_ref[...] = x_ref[...] + mod_ref[2:3, :] * _dot(o, w_ref[...])


def _proj_residual(x, o_prompt, o_sample, w_out_bf, tile_row, mod_l):
    n = x.shape[0]
    tm = TOKEN_TILE
    pt = o_prompt.shape[0] // tm
    return pl.pallas_call(
        functools.partial(_proj_res_kernel, prompt_tiles=pt),
        grid_spec=pltpu.PrefetchScalarGridSpec(
            num_scalar_prefetch=1,
            grid=(n // tm,),
            in_specs=[
                pl.BlockSpec((tm, D_MODEL), lambda i, row: (i, 0)),
                pl.BlockSpec((tm, D_MODEL), lambda i, row: (jnp.minimum(i, pt - 1), 0)),
                pl.BlockSpec((tm, D_MODEL), lambda i, row: (jnp.maximum(i - pt, 0), 0)),
                pl.BlockSpec((D_MODEL, D_MODEL), lambda i, row: (0, 0)),
                pl.BlockSpec((None, 6, D_MODEL), lambda i, row: (row[i], 0, 0)),
            ],
            out_specs=pl.BlockSpec((tm, D_MODEL), lambda i, row: (i, 0)),
        ),
        out_shape=jax.ShapeDtypeStruct((n, D_MODEL), F32),
        compiler_params=_cparams("arbitrary"),
        name="proj_residual",
    )(tile_row, x, o_prompt, o_sample, w_out_bf, mod_l)


def _conv_kernel(row_ref, width_ref, x_ref, mod_ref, nw_ref, win_ref, cw_ref, wout_ref, xo_ref):
    del row_ref
    i = pl.program_id(0)
    x = x_ref[...]
    hm = (_rms(x, nw_ref[...]) * (1.0 + mod_ref[1:2, :]) + mod_ref[0:1, :]).astype(BF16)
    z = _dot(hm, win_ref[...])
    bg = z[:, :D_MODEL]
    u = z[:, D_MODEL:2 * D_MODEL] * z[:, 2 * D_MODEL:]
    tm = x.shape[0]
    pos = lax.broadcasted_iota(jnp.int32, (tm, 1), 0) & (width_ref[i] - 1)
    prev = jnp.where(pos == 0, 0.0, pltpu.roll(u, 1, axis=0))
    nxt = jnp.where(pos == width_ref[i] - 1, 0.0, pltpu.roll(u, tm - 1, axis=0))
    v = cw_ref[0:1, :] * prev + cw_ref[1:2, :] * u + cw_ref[2:3, :] * nxt
    y = _dot((bg * v).astype(BF16), wout_ref[...])
    xo_ref[...] = x + mod_ref[2:3, :] * y


def _conv_mixer(x, tile_row, tile_width, mod_l, norm_w, w_in_bf, conv_w, w_out_bf):
    n = x.shape[0]
    tm = TOKEN_TILE
    return pl.pallas_call(
        _conv_kernel,
        grid_spec=pltpu.PrefetchScalarGridSpec(
            num_scalar_prefetch=2,
            grid=(n // tm,),
            in_specs=[
                pl.BlockSpec((tm, D_MODEL), lambda i, row, w: (i, 0)),
                pl.BlockSpec((None, 6, D_MODEL), lambda i, row, w: (row[i], 0, 0)),
                pl.BlockSpec((1, D_MODEL), lambda i, row, w: (0, 0)),
                pl.BlockSpec((D_MODEL, 3 * D_MODEL), lambda i, row, w: (0, 0)),
                pl.BlockSpec((3, D_MODEL), lambda i, row, w: (0, 0)),
                pl.BlockSpec((D_MODEL, D_MODEL), lambda i, row, w: (0, 0)),
            ],
            out_specs=pl.BlockSpec((tm, D_MODEL), lambda i, row, w: (i, 0)),
        ),
        out_shape=jax.ShapeDtypeStruct((n, D_MODEL), F32),
        compiler_params=_cparams("arbitrary"),
        name="conv_mixer",
    )(tile_row, tile_width, x, mod_l, norm_w.reshape(1, D_MODEL), w_in_bf, conv_w, w_out_bf)


def _router_kernel(row_ref, x_ref, mod_ref, nw_ref, rw_ref, rb_ref, hf_ref, pg_ref, pgt_ref, cnt_ref):
    del row_ref
    hf = _rms(x_ref[...], nw_ref[...]) * (1.0 + mod_ref[4:5, :]) + mod_ref[3:4, :]
    hf_ref[...] = hf.astype(BF16)
    logits = jnp.dot(hf, rw_ref[...], preferred_element_type=F32,
                     precision=lax.Precision.HIGHEST) + rb_ref[...]
    tm = hf.shape[0]
    lane = lax.broadcasted_iota(jnp.int32, (tm, LANES), 1)
    lane_f = lane.astype(F32)
    work = logits
    vals, hots = [], []
    for _ in range(TOP_K):
        m = jnp.max(work, axis=-1, keepdims=True)
        idx_f = jnp.min(jnp.where(work == m, lane_f, float(LANES)), axis=-1, keepdims=True)
        hot = lane_f == idx_f
        vals.append(m)
        hots.append(hot)
        work = jnp.where(hot, -jnp.inf, work)
    ex = [jnp.exp(v - vals[0]) for v in vals]
    inv = 1.0 / (ex[0] + ex[1] + ex[2] + ex[3])

    multi = jnp.where(hots[0] | hots[1] | hots[2] | hots[3], 1.0, 0.0)
    tr = lax.broadcasted_iota(jnp.int32, (tm, tm), 0)
    tc = lax.broadcasted_iota(jnp.int32, (tm, tm), 1)
    earlier = jnp.where(tc < tr, 1.0, 0.0).astype(BF16)
    before = _dot(earlier, multi.astype(BF16))
    cnt = jnp.sum(multi, axis=0, keepdims=True)
    er = lax.broadcasted_iota(jnp.int32, (LANES, LANES), 0)
    ec = lax.broadcasted_iota(jnp.int32, (LANES, LANES), 1)
    lower = jnp.where(er < ec, 1.0, 0.0).astype(BF16)
    cnt_al = jnp.floor((cnt + (SEG_ALIGN - 1)) * (1.0 / SEG_ALIGN)) * SEG_ALIGN
    estart = _dot(jnp.broadcast_to(cnt_al, (8, LANES)).astype(BF16), lower)[0:1, :]
    pos_all = before + estart

    pg = jnp.zeros((tm, LANES), F32)
    for k in range(TOP_K):
        pos = jnp.sum(jnp.where(hots[k], pos_all, 0.0), axis=-1, keepdims=True)
        pg = jnp.where(lane == k, pos, pg)
        pg = jnp.where(lane == TOP_K + k, ex[k] * inv, pg)
    pg_ref[...] = pg
    pgt_ref[...] = pg.T[0:8, :]
    cnt_ref[...] = cnt.astype(jnp.int32)


def _router(x, tile_row, mod_l, norm_w, router_w_pad, router_b_pad):
    n = x.shape[0]
    tm = ROUTE_TILE
    nt = n // tm
    return pl.pallas_call(
        _router_kernel,
        grid_spec=pltpu.PrefetchScalarGridSpec(
            num_scalar_prefetch=1,
            grid=(nt,),
            in_specs=[
                pl.BlockSpec((tm, D_MODEL), lambda i, row: (i, 0)),
                pl.BlockSpec((None, 6, D_MODEL), lambda i, row: (row[i], 0, 0)),
                pl.BlockSpec((1, D_MODEL), lambda i, row: (0, 0)),
                pl.BlockSpec((D_MODEL, LANES), lambda i, row: (0, 0)),
                pl.BlockSpec((1, LANES), lambda i, row: (0, 0)),
            ],
            out_specs=[
                pl.BlockSpec((tm, D_MODEL), lambda i, row: (i, 0)),
                pl.BlockSpec((tm, LANES), lambda i, row: (i, 0)),
                pl.BlockSpec((None, 8, tm), lambda i, row: (i, 0, 0)),
                pl.BlockSpec((None, 1, LANES), lambda i, row: (i, 0, 0)),
            ],
        ),
        out_shape=[
            jax.ShapeDtypeStruct((n, D_MODEL), BF16),
            jax.ShapeDtypeStruct((n, LANES), F32),
            jax.ShapeDtypeStruct((nt, 8, tm), F32),
            jax.ShapeDtypeStruct((nt, 1, LANES), jnp.int32),
        ],
        compiler_params=_cparams("arbitrary"),
        name="router",
    )(tile_row, x, mod_l, norm_w.reshape(1, D_MODEL), router_w_pad, router_b_pad)


SEG_STRIDE = 4 * N_EXPERTS


def _segment_table(tbl_ref, t, e):
    base = t * SEG_STRIDE
    al = lambda v: pl.multiple_of(v, SEG_ALIGN)
    return (al(tbl_ref[base + e]), al(tbl_ref[base + N_EXPERTS + e]),
            al(tbl_ref[base + 2 * N_EXPERTS + e]))


def _tile_rows_total(tbl_ref, t):
    return pl.multiple_of(tbl_ref[t * SEG_STRIDE + 3 * N_EXPERTS], SEG_ALIGN)


def _dispatch_kernel(tbl_ref, pad_ref, nu_ref, hf_ref, pgt_ref, xb_ref, stage_s, zero_s, sem, zsem,
                     *, n_blocks):
    t = pl.program_id(0)
    nt = pl.num_programs(0)
    slot = t % 2
    half = zero_s.shape[0]

    def zero_rows(start, rows):
        start = pl.multiple_of(start, SEG_ALIGN)
        rows = pl.multiple_of(rows, SEG_ALIGN)

        @pl.when(rows > 0)
        def _():
            pltpu.make_async_copy(zero_s.at[pl.ds(0, rows), :], xb_ref.at[pl.ds(start, rows), :],
                                  zsem).start()

    @pl.when(t == 0)
    def _():
        zero_s[...] = jnp.zeros_like(zero_s)

        def pad_body(e, carry):
            start = pad_ref[e]
            rows = pad_ref[N_EXPERTS + e]
            first = jnp.minimum(rows, half)
            zero_rows(start, first)
            zero_rows(start + half, rows - first)
            return carry

        lax.fori_loop(0, N_EXPERTS, pad_body, 0)

        def tail_body(b, carry):
            for part in range(EXPERT_TILE // half):
                zero_rows(b * EXPERT_TILE + part * half, half)
            return carry

        lax.fori_loop(nu_ref[0], n_blocks, tail_body, 0)

    def wait_tile(tile, sl):
        rows = _tile_rows_total(tbl_ref, tile)
        pltpu.make_async_copy(stage_s.at[sl, pl.ds(0, rows), :], xb_ref.at[pl.ds(0, rows), :],
                              sem.at[sl]).wait()

    @pl.when(t >= 2)
    def _():
        wait_tile(t - 2, slot)

    s_iota = lax.broadcasted_iota(jnp.int32, (SLOTS, ROUTE_TILE), 0).astype(F32)
    pgt = pgt_ref[...]
    hit = s_iota == pgt[0:1, :]
    for k in range(1, TOP_K):
        hit = hit | (s_iota == pgt[k:k + 1, :])
    perm = jnp.where(hit, 1.0, 0.0).astype(BF16)
    stage_s[slot] = _dot(perm, hf_ref[...])

    def seg_body(e, carry):
        rows, src, dst = _segment_table(tbl_ref, t, e)

        @pl.when(rows > 0)
        def _():
            pltpu.make_async_copy(stage_s.at[slot, pl.ds(src, rows), :],
                                  xb_ref.at[pl.ds(dst, rows), :], sem.at[slot]).start()
        return carry

    lax.fori_loop(0, N_EXPERTS, seg_body, 0)

    @pl.when(t == nt - 1)
    def _():
        wait_tile(t, slot)

        @pl.when(nt > 1)
        def _():
            wait_tile(t - 1, 1 - slot)

        zeroed = pl.multiple_of(pad_ref[2 * N_EXPERTS], SEG_ALIGN)

        @pl.when(zeroed > 0)
        def _():
            pltpu.make_async_copy(xb_ref.at[pl.ds(0, zeroed), :], xb_ref.at[pl.ds(0, zeroed), :],
                                  zsem).wait()


def _dispatch(hf, pgt, seg_tbl, pad_tbl, n_used, n_blocks):
    n = hf.shape[0]
    tm = ROUTE_TILE
    return pl.pallas_call(
        functools.partial(_dispatch_kernel, n_blocks=n_blocks),
        grid_spec=pltpu.PrefetchScalarGridSpec(
            num_scalar_prefetch=3,
            grid=(n // tm,),
            in_specs=[
                pl.BlockSpec((tm, D_MODEL), lambda i, a, b, c: (i, 0)),
                pl.BlockSpec((None, 8, tm), lambda i, a, b, c: (i, 0, 0)),
            ],
            out_specs=pl.BlockSpec(memory_space=pl.ANY),
            scratch_shapes=[
                pltpu.VMEM((2, SLOTS, D_MODEL), F32),
                pltpu.VMEM((EXPERT_TILE // 2, D_MODEL), F32),
                pltpu.SemaphoreType.DMA((2,)),
                pltpu.SemaphoreType.DMA,
            ],
        ),
        out_shape=jax.ShapeDtypeStruct((n_blocks * EXPERT_TILE, D_MODEL), F32),
        compiler_params=_cparams("arbitrary"),
        name="dispatch",
    )(seg_tbl, pad_tbl, n_used, hf, pgt)


def _expert_kernel(be_ref, nu_ref, xb_ref, wgu_ref, bgu_ref, wd_ref, bd_ref, yb_ref, wgu_s, wd_s):
    i = pl.program_id(0)

    @pl.when(i < nu_ref[0])
    def _():
        changed = jnp.logical_or(i == 0, be_ref[i] != be_ref[jnp.maximum(i - 1, 0)])

        @pl.when(changed)
        def _():
            wgu_s[...] = wgu_ref[...].astype(BF16)
            wd_s[...] = wd_ref[...].astype(BF16)

        gu = _dot(xb_ref[...].astype(BF16), wgu_s[...]) + bgu_ref[...]
        gate = jnp.minimum(gu[:, :D_FF], SWIGLU_LIMIT)
        up = jnp.clip(gu[:, D_FF:], -SWIGLU_LIMIT, SWIGLU_LIMIT)
        act = (up + 1.0) * gate * _sigmoid(SWIGLU_ALPHA * gate)
        yb_ref[...] = _dot(act.astype(BF16), wd_s[...]) + bd_ref[...]

    @pl.when(i >= nu_ref[0])
    def _():
        yb_ref[...] = jnp.zeros_like(yb_ref)


def _experts(xb, block_e, n_used, layer, w_gu, b_gu, w_down, b_down):
    te = EXPERT_TILE
    n_blocks = xb.shape[0] // te
    depth = w_gu.shape[0]

    def x_map(i, be, nu):
        return (jnp.maximum(jnp.minimum(i, nu[0] - 1), 0), 0)

    return pl.pallas_call(
        _expert_kernel,
        grid_spec=pltpu.PrefetchScalarGridSpec(
            num_scalar_prefetch=2,
            grid=(n_blocks,),
            in_specs=[
                pl.BlockSpec((te, D_MODEL), x_map),
                pl.BlockSpec((None, None, D_MODEL, 2 * D_FF), lambda i, be, nu: (layer, be[i], 0, 0)),
                pl.BlockSpec((None, None, 1, 2 * D_FF), lambda i, be, nu: (layer, be[i], 0, 0)),
                pl.BlockSpec((None, None, D_FF, D_MODEL), lambda i, be, nu: (layer, be[i], 0, 0)),
                pl.BlockSpec((None, None, 1, D_MODEL), lambda i, be, nu: (layer, be[i], 0, 0)),
            ],
            out_specs=pl.BlockSpec((te, D_MODEL), lambda i, be, nu: (i, 0)),
            scratch_shapes=[pltpu.VMEM((D_MODEL, 2 * D_FF), BF16), pltpu.VMEM((D_FF, D_MODEL), BF16)],
        ),
        out_shape=jax.ShapeDtypeStruct(xb.shape, F32),
        compiler_params=_cparams("arbitrary"),
        name="experts",
    )(block_e, n_used, xb, w_gu, b_gu.reshape(depth, N_EXPERTS, 1, 2 * D_FF), w_down,
      b_down.reshape(depth, N_EXPERTS, 1, D_MODEL))


def _combine_kernel(row_ref, tbl_ref, x_ref, pg_ref, mod_ref, fw_ref, yb_ref, xo_ref, stage_s, sem,
                    *, final_norm):
    del row_ref
    t = pl.program_id(0)
    nt = pl.num_programs(0)
    slot = t % 2

    def fetch(tile, into):
        def seg_body(e, carry):
            rows, dst, src = _segment_table(tbl_ref, tile, e)

            @pl.when(rows > 0)
            def _():
                pltpu.make_async_copy(yb_ref.at[pl.ds(src, rows), :],
                                      stage_s.at[into, pl.ds(dst, rows), :], sem.at[into]).start()
            return carry

        lax.fori_loop(0, N_EXPERTS, seg_body, 0)

    @pl.when(t == 0)
    def _():
        stage_s[...] = jnp.zeros_like(stage_s)
        fetch(0, 0)

    @pl.when(t + 1 < nt)
    def _():
        fetch(t + 1, 1 - slot)

    fetched = _tile_rows_total(tbl_ref, t)
    pltpu.make_async_copy(yb_ref.at[pl.ds(0, fetched), :], stage_s.at[slot, pl.ds(0, fetched), :],
                          sem.at[slot]).wait()
    rows_sorted = stage_s[slot].astype(BF16)
    pg = pg_ref[...]
    s_iota = lax.broadcasted_iota(jnp.int32, (ROUTE_TILE, SLOTS), 1).astype(F32)
    sel = jnp.zeros((ROUTE_TILE, SLOTS), F32)
    for k in range(TOP_K):
        sel = jnp.where(s_iota == pg[:, k:k + 1], pg[:, TOP_K + k:TOP_K + k + 1], sel)
    sel_hi = sel.astype(BF16)
    sel_lo = (sel - sel_hi.astype(F32)).astype(BF16)
    y = _dot(sel_hi, rows_sorted) + _dot(sel_lo, rows_sorted)
    x = x_ref[...] + mod_ref[5:6, :] * y
    if final_norm:
        x = _rms(x, fw_ref[...])
    xo_ref[...] = x


def _combine(x, pg, seg_tbl, yb, tile_row, mod_l, final_w, final_norm):
    n = x.shape[0]
    tm = ROUTE_TILE
    return pl.pallas_call(
        functools.partial(_combine_kernel, final_norm=final_norm),
        grid_spec=pltpu.PrefetchScalarGridSpec(
            num_scalar_prefetch=2,
            grid=(n // tm,),
            in_specs=[
                pl.BlockSpec((tm, D_MODEL), lambda i, row, tbl: (i, 0)),
                pl.BlockSpec((tm, LANES), lambda i, row, tbl: (i, 0)),
                pl.BlockSpec((None, 6, D_MODEL), lambda i, row, tbl: (row[i], 0, 0)),
                pl.BlockSpec((1, D_MODEL), lambda i, row, tbl: (0, 0)),
                pl.BlockSpec(memory_space=pl.ANY),
            ],
            out_specs=pl.BlockSpec((tm, D_MODEL), lambda i, row, tbl: (i, 0)),
            scratch_shapes=[pltpu.VMEM((2, SLOTS, D_MODEL), F32), pltpu.SemaphoreType.DMA((2,))],
        ),
        out_shape=jax.ShapeDtypeStruct((n, D_MODEL), F32),
        compiler_params=_cparams("arbitrary"),
        name="combine",
    )(tile_row, seg_tbl, x, pg, mod_l, final_w.reshape(1, D_MODEL), yb)


def _moe(x, tile_row, mod_l, norm_w, router_w, router_b, layer, w_gu, b_gu, w_down, b_down,
         final_w, final_norm):
    n = x.shape[0]
    rw = jnp.zeros((D_MODEL, LANES), F32).at[:, :N_EXPERTS].set(router_w)
    rb = jnp.full((1, LANES), -jnp.inf, F32).at[0, :N_EXPERTS].set(router_b)
    hf, pg, pgt, cnt = _router(x, tile_row, mod_l, norm_w, rw, rb)

    te = EXPERT_TILE
    nt = n // ROUTE_TILE
    n_blocks = (n * TOP_K + nt * N_EXPERTS * (SEG_ALIGN - 1)) // te + 1 + N_EXPERTS
    cnt = cnt[:, 0, :N_EXPERTS]
    cnt = (cnt + SEG_ALIGN - 1) // SEG_ALIGN * SEG_ALIGN
    total = jnp.sum(cnt, axis=0)
    blocks_e = (total + te - 1) // te
    block_end = jnp.cumsum(blocks_e)
    pstart = (block_end - blocks_e) * te
    n_used = block_end[-1]
    first_row = pstart[None, :] + jnp.cumsum(cnt, axis=0) - cnt
    tile_off = jnp.cumsum(cnt, axis=1) - cnt
    tile_total = jnp.broadcast_to(jnp.sum(cnt, axis=1, keepdims=True), cnt.shape)
    seg_tbl = jnp.concatenate([cnt, tile_off, first_row, tile_total], axis=1)
    seg_tbl = seg_tbl.reshape(-1).astype(jnp.int32)
    pad_rows = blocks_e * te - total
    zeroed = jnp.sum(pad_rows) + (n_blocks - n_used) * te
    pad_tbl = jnp.concatenate([pstart + total, pad_rows, zeroed[None]]).astype(jnp.int32)
    block_ids = jnp.arange(n_blocks, dtype=jnp.int32)
    clamped = jnp.minimum(block_ids, n_used - 1)
    block_e = jnp.sum((clamped[:, None] >= block_end[None, :]).astype(jnp.int32), axis=1)
    block_e = jnp.minimum(block_e, N_EXPERTS - 1).astype(jnp.int32)
    n_used = n_used.astype(jnp.int32).reshape(1)

    xb = _dispatch(hf, pgt, seg_tbl, pad_tbl, n_used, n_blocks)
    yb = _experts(xb, block_e, n_used, layer, w_gu, b_gu, w_down, b_down)
    return _combine(x, pg, seg_tbl, yb, tile_row, mod_l, final_w, final_norm)


def _tile_rows(n_prompt_tok, n_sample_seq, sample_len, tile):
    starts = np.arange(0, n_prompt_tok + n_sample_seq * sample_len, tile)
    row = np.where(starts < n_prompt_tok, 0, 1 + (starts - n_prompt_tok) // sample_len)
    return jnp.asarray(row, dtype=jnp.int32)


def kernel(x_prompt, x_sample, state_hgrn, c, c_ctx, w_mod, b_mod, norm_mix, norm_ffn, hg_w_in,
           hg_lb_logits, hg_gnorm, hg_w_out, cv_w_in, cv_w, cv_w_out, router_w, router_b,
           moe_w_gu, moe_b_gu, moe_w_down, moe_b_down, final_norm):
    bp, tp, d = x_prompt.shape
    bs, ts, _ = x_sample.shape
    depth = w_mod.shape[0]
    n_prompt = bp * tp
    n = n_prompt + bs * ts
    assert d == D_MODEL and depth == 2 and 1 + bs <= 16
    assert n_prompt % SCAN_ROWS == 0 and ts == SCAN_ROWS and tp == SCAN_PIECE

    x = jnp.concatenate([x_prompt.reshape(n_prompt, d), x_sample.reshape(bs * ts, d)], axis=0)
    cond16 = jnp.zeros((16, d), F32).at[0].set(c_ctx).at[1:1 + bs].set(c)
    mod = _modulation(cond16, w_mod.astype(BF16), b_mod).reshape(depth, 16, 6, d)

    tile_row = _tile_rows(n_prompt, bs, ts, TOKEN_TILE)
    tile_row_route = _tile_rows(n_prompt, bs, ts, ROUTE_TILE)
    starts = np.arange(0, n, TOKEN_TILE)
    tile_width = jnp.asarray(np.where(starts < n_prompt, tp, GRID_W), dtype=jnp.int32)

    lb_all = jnp.cumsum(jax.nn.softmax(hg_lb_logits.astype(F32), axis=1), axis=1)

    hm = _norm_mod(x, tile_row, mod[0], norm_mix[0])
    w_in_bf = hg_w_in[0].astype(BF16)
    gn = hg_gnorm[0].reshape(-1)
    zero_state = jnp.zeros((SCAN_ROWS // tp, 2, HEADS, HEAD_DIM, HEAD_DIM), F32)
    o_p, s_new = _hgrn_scan(hm, 0, n_prompt, w_in_bf, lb_all[0, 0], lb_all[1, 0], gn, zero_state,
                            tp, True)
    o_s, _ = _hgrn_scan(hm, n_prompt, bs * ts, w_in_bf, lb_all[0, 0], lb_all[1, 0], gn,
                        state_hgrn[:, 0], ts, False)
    x = _proj_residual(x, o_p, o_s, hg_w_out[0].astype(BF16), tile_row, mod[0])
    x = _moe(x, tile_row_route, mod[0], norm_ffn[0], router_w[0], router_b[0],
             0, moe_w_gu, moe_b_gu, moe_w_down, moe_b_down, final_norm, False)

    x = _conv_mixer(x, tile_row, tile_width, mod[1], norm_mix[1], cv_w_in[0].astype(BF16), cv_w[0],
                    cv_w_out[0].astype(BF16))
    x = _moe(x, tile_row_route, mod[1], norm_ffn[1], router_w[1], router_b[1],
             1, moe_w_gu, moe_b_gu, moe_w_down, moe_b_down, final_norm, True)

    y_prompt = x[:n_prompt].reshape(bp, tp, d)
    y_sample = x[n_prompt:].reshape(bs, ts, d)
    return (y_prompt, y_sample, s_new.reshape(bp, 1, 2, HEADS, HEAD_DIM, HEAD_DIM))
```

```python
import functools

import numpy as np
import jax
import jax.numpy as jnp
from jax import lax
from jax.experimental import pallas as pl
from jax.experimental.pallas import tpu as pltpu

F32 = jnp.float32
BF16 = jnp.bfloat16

D_MODEL = 1024
HEADS = 8
HEAD_DIM = 128
CHUNK = 32
GRID_W = 64
N_EXPERTS = 32
TOP_K = 4
D_FF = 1024
SWIGLU_LIMIT = 7.0
SWIGLU_ALPHA = 1.702
EPS = 1e-6

LANES = 128
SCAN_ROWS = 2048
SCAN_PIECE = 256
TOKEN_TILE = 512
EXPERT_TILE = 512
ROUTE_TILE = 256
SEG_ALIGN = 8
SLOTS = ROUTE_TILE * TOP_K + N_EXPERTS * SEG_ALIGN
VMEM_LIMIT = 56 * 1024 * 1024


def _cparams(*sem):
    return pltpu.CompilerParams(dimension_semantics=sem, vmem_limit_bytes=VMEM_LIMIT)


def _sigmoid(x):
    return 1.0 / (1.0 + jnp.exp(-x))


def _rms(x, w):
    return x * lax.rsqrt(jnp.mean(x * x, axis=-1, keepdims=True) + EPS) * w


def _dot(a, b):
    return jnp.dot(a, b, preferred_element_type=F32)


def _dot_nt(a, b):
    return lax.dot_general(a, b, (((1,), (1,)), ((), ())), preferred_element_type=F32)


def _dot_tn(a, b):
    return lax.dot_general(a, b, (((0,), (0,)), ((), ())), preferred_element_type=F32)


def _mod_kernel(cond_ref, w_ref, b_ref, o_ref):
    c = cond_ref[...]
    s = (c * _sigmoid(c)).astype(BF16)
    o_ref[...] = _dot(s, w_ref[...]) + b_ref[...]


def _modulation(cond16, w_mod_bf, b_mod):
    depth = w_mod_bf.shape[0]
    n_out = w_mod_bf.shape[2]
    tn = 1024
    return pl.pallas_call(
        _mod_kernel,
        grid=(depth, n_out // tn),
        in_specs=[
            pl.BlockSpec((16, D_MODEL), lambda l, j: (0, 0)),
            pl.BlockSpec((None, D_MODEL, tn), lambda l, j: (l, 0, j)),
            pl.BlockSpec((None, 1, tn), lambda l, j: (l, 0, j)),
        ],
        out_specs=pl.BlockSpec((None, 16, tn), lambda l, j: (l, 0, j)),
        out_shape=jax.ShapeDtypeStruct((depth, 16, n_out), F32),
        compiler_params=_cparams("arbitrary", "arbitrary"),
        name="modulation",
    )(cond16, w_mod_bf, b_mod.reshape(depth, 1, n_out))


def _two_group_specs(tm, prompt_tiles):
    return [
        pl.BlockSpec((tm, D_MODEL), lambda i, *_: (jnp.minimum(i, prompt_tiles - 1), 0)),
        pl.BlockSpec((tm, D_MODEL), lambda i, *_: (jnp.maximum(i - prompt_tiles, 0), 0)),
    ]


def _pick_group(a_ref, b_ref, prompt_tiles):
    return jnp.where(pl.program_id(0) < prompt_tiles, a_ref[...], b_ref[...])


def _norm_mod_kernel(row_ref, xp_ref, xs_ref, mod_ref, nw_ref, o_ref, *, prompt_tiles):
    del row_ref
    y = _rms(_pick_group(xp_ref, xs_ref, prompt_tiles), nw_ref[...])
    o_ref[...] = (y * (1.0 + mod_ref[1:2, :]) + mod_ref[0:1, :]).astype(o_ref.dtype)


def _norm_mod(x_p, x_s, tile_row, mod_l, norm_w):
    n = x_p.shape[0] + x_s.shape[0]
    tm = TOKEN_TILE
    pt = x_p.shape[0] // tm
    return pl.pallas_call(
        functools.partial(_norm_mod_kernel, prompt_tiles=pt),
        grid_spec=pltpu.PrefetchScalarGridSpec(
            num_scalar_prefetch=1,
            grid=(n // tm,),
            in_specs=_two_group_specs(tm, pt) + [
                pl.BlockSpec((None, 6, D_MODEL), lambda i, row: (row[i], 0, 0)),
                pl.BlockSpec((1, D_MODEL), lambda i, row: (0, 0)),
            ],
            out_specs=pl.BlockSpec((tm, D_MODEL), lambda i, row: (i, 0)),
        ),
        out_shape=jax.ShapeDtypeStruct((n, D_MODEL), BF16),
        compiler_params=_cparams("arbitrary"),
        name="norm_mod",
    )(tile_row, x_p, x_s, mod_l, norm_w.reshape(1, D_MODEL))


def _hgrn_kernel(hm_ref, wq_ref, wff_ref, wfb_ref, wv_ref, wg_ref, lbf_ref, lbb_ref, gn_ref,
                 s0_ref, o_ref, so_ref, q_s, vt_s, kf_s, lf_s, kb_s, lb_s, of_s, ob_s, st_s,
                 *, seq_len, zero_init):
    rows = hm_ref.shape[0]
    piece = SCAN_PIECE
    n_pieces = rows // piece
    cpp = piece // CHUNK
    per_piece_seq = seq_len == piece
    assert per_piece_seq or seq_len == rows
    hm = hm_ref[...]

    zq = _dot(hm, wq_ref[...])
    q_s[...] = zq * _sigmoid(zq)
    v = _dot(hm, wv_ref[...])
    for p in range(n_pieces):
        vt_s[p] = v[p * piece:(p + 1) * piece, :].T.astype(BF16)
    for w_ref, lbr, k_s, l_s in ((wff_ref, lbf_ref, kf_s, lf_s), (wfb_ref, lbb_ref, kb_s, lb_s)):
        lb = lbr[...]
        f = lb + (1.0 - lb) * _sigmoid(_dot(hm, w_ref[...]))
        k_s[...] = 1.0 - f
        l_s[...] = jnp.log(f)

    def load_state(j, d):
        for h in range(2):
            if zero_init:
                st_s[d, h] = jnp.zeros((HEAD_DIM, HEAD_DIM), F32)
            else:
                st_s[d, h] = s0_ref[j, d, h].T

    def store_state(j, d):
        for h in range(2):
            so_ref[j, d, h] = st_s[d, h].T

    def piece_body(i, carry):
        r = lax.broadcasted_iota(jnp.int32, (piece, piece), 0)
        c = lax.broadcasted_iota(jnp.int32, (piece, piece), 1)
        same = (r // CHUNK) == (c // CHUNK)
        row_chunk = lax.broadcasted_iota(jnp.int32, (piece, 1), 0) // CHUNK
        col_chunk = lax.broadcasted_iota(jnp.int32, (1, piece), 1) // CHUNK
        for d in range(2):
            k_s, l_s, o_s = ((kf_s, lf_s, of_s), (kb_s, lb_s, ob_s))[d]
            p = i if d == 0 else n_pieces - 1 - i
            if per_piece_seq:
                load_state(p, d)
            sl = pl.ds(pl.multiple_of(p * piece, piece), piece)
            keep = same & ((c <= r) if d == 0 else (c >= r))
            tri = jnp.where(keep, 1.0, 0.0).astype(BF16)
            lf = l_s[sl, :]
            hi = lf.astype(BF16)
            r1 = lf - hi.astype(F32)
            mid = r1.astype(BF16)
            lo = (r1 - mid.astype(F32)).astype(BF16)
            b = _dot(tri, hi) + _dot(tri, mid) + _dot(tri, lo)
            b3 = b.reshape(cpp, CHUNK, 2 * HEAD_DIM)
            edge = CHUNK - 1 if d == 0 else 0
            bl3 = b3[:, edge:edge + 1, :]
            q = q_s[sl, :]
            k = k_s[sl, :]
            qd = (q * jnp.exp(b)).astype(BF16)
            ki = (k * jnp.exp(-b)).astype(BF16)
            ks = (k.reshape(cpp, CHUNK, 2 * HEAD_DIM) * jnp.exp(bl3 - b3)).reshape(piece, 2 * HEAD_DIM)
            ks = ks.astype(BF16)
            dec = jnp.exp(bl3)
            vt = vt_s[p]
            for h in range(2):
                hs = slice(h * HEAD_DIM, (h + 1) * HEAD_DIM)
                att = jnp.where(keep, _dot_nt(qd[:, hs], ki[:, hs]), 0.0).astype(BF16)
                vt_h = vt[hs, :]
                v_exp = jnp.concatenate(
                    [jnp.where(col_chunk == ci, vt_h, jnp.zeros_like(vt_h)) for ci in range(cpp)], axis=0)
                incr = _dot(v_exp, ks[:, hs])
                st = st_s[d, h]
                before = [None] * cpp
                for ci in (range(cpp) if d == 0 else range(cpp - 1, -1, -1)):
                    before[ci] = st.astype(BF16)
                    st = st * dec[ci, :, hs] + incr[ci * HEAD_DIM:(ci + 1) * HEAD_DIM, :]
                st_s[d, h] = st
                qd_h = qd[:, hs]
                q_exp = [jnp.where(row_chunk == ci, qd_h, jnp.zeros_like(qd_h)) for ci in range(cpp)]
                lhs = jnp.concatenate([att] + q_exp, axis=1)
                rhs_t = jnp.concatenate([vt_h] + before, axis=1)
                o_s[sl, hs] = _dot_nt(lhs, rhs_t)
            if per_piece_seq:
                store_state(p, d)
        return carry

    if not per_piece_seq:
        load_state(0, 0)
        load_state(0, 1)
    lax.fori_loop(0, n_pieces, piece_body, 0)
    if not per_piece_seq:
        store_state(0, 0)
        store_state(0, 1)

    o = of_s[...] + ob_s[...]
    gn = gn_ref[...]
    o = jnp.concatenate(
        [_rms(o[:, h * HEAD_DIM:(h + 1) * HEAD_DIM], gn[:, h * HEAD_DIM:(h + 1) * HEAD_DIM])
         for h in range(2)], axis=1)
    o_ref[...] = (o * _sigmoid(_dot(hm, wg_ref[...]))).astype(o_ref.dtype)


def _hgrn_scan(hm, row0, n_rows, w_in_bf, lb_f, lb_b, gnorm, s0, seq_len, zero_init):
    n_seq_total = n_rows // seq_len
    rows = SCAN_ROWS
    seq_per_step = rows // seq_len
    blk0 = row0 // rows
    pair = 2 * HEAD_DIM
    n_pairs = HEADS // 2

    def w_spec(seg):
        return pl.BlockSpec((D_MODEL, pair), lambda sb, hp: (0, seg * n_pairs + hp))

    vec_spec = pl.BlockSpec((1, pair), lambda sb, hp: (0, hp))
    st_block = (seq_per_step, 2, 2, HEAD_DIM, HEAD_DIM)
    st_spec = pl.BlockSpec(st_block, lambda sb, hp: (sb, 0, hp, 0, 0))
    s0_spec = pl.BlockSpec(st_block, (lambda sb, hp: (0, 0, hp, 0, 0)) if zero_init
                           else (lambda sb, hp: (sb, 0, hp, 0, 0)))
    scratch = [
        pltpu.VMEM((rows, pair), F32),
        pltpu.VMEM((rows // SCAN_PIECE, pair, SCAN_PIECE), BF16),
        pltpu.VMEM((rows, pair), F32),
        pltpu.VMEM((rows, pair), F32),
        pltpu.VMEM((rows, pair), F32),
        pltpu.VMEM((rows, pair), F32),
        pltpu.VMEM((rows, pair), F32),
        pltpu.VMEM((rows, pair), F32),
        pltpu.VMEM((2, 2, HEAD_DIM, HEAD_DIM), F32),
    ]
    in_specs = [
        pl.BlockSpec((rows, D_MODEL), lambda sb, hp: (blk0 + sb, 0)),
        w_spec(0), w_spec(1), w_spec(2), w_spec(3), w_spec(4),
        vec_spec, vec_spec, vec_spec,
        s0_spec,
    ]
    args = [hm, w_in_bf, w_in_bf, w_in_bf, w_in_bf, w_in_bf,
            lb_f.reshape(1, D_MODEL), lb_b.reshape(1, D_MODEL), gnorm.reshape(1, D_MODEL), s0]
    return pl.pallas_call(
        functools.partial(_hgrn_kernel, seq_len=seq_len, zero_init=zero_init),
        grid=(n_rows // rows, n_pairs),
        in_specs=in_specs,
        out_specs=[
            pl.BlockSpec((rows, pair), lambda sb, hp: (sb, hp)),
            st_spec,
        ],
        out_shape=[
            jax.ShapeDtypeStruct((n_rows, D_MODEL), BF16),
            jax.ShapeDtypeStruct((n_seq_total, 2, HEADS, HEAD_DIM, HEAD_DIM), F32),
        ],
        scratch_shapes=scratch,
        compiler_params=_cparams("arbitrary", "arbitrary"),
        name="hgrn_scan",
    )(*args)


def _proj_res_kernel(row_ref, xp_ref, xs_ref, op_ref, os_ref, w_ref, mod_ref, xo_ref, *, prompt_tiles):
    del row_ref
    x = _pick_group(xp_ref, xs_ref, prompt_tiles)
    o = _pick_group(op_ref, os_ref, prompt_tiles)
    xo_ref[...] = x + mod_ref[2:3, :] * _dot(o, w_ref[...])


def _proj_residual(x_p, x_s, o_prompt, o_sample, w_out_bf, tile_row, mod_l):
    n = x_p.shape[0] + x_s.shape[0]
    tm = TOKEN_TILE
    pt = o_prompt.shape[0] // tm
    return pl.pallas_call(
        functools.partial(_proj_res_kernel, prompt_tiles=pt),
        grid_spec=pltpu.PrefetchScalarGridSpec(
            num_scalar_prefetch=1,
            grid=(n // tm,),
            in_specs=_two_group_specs(tm, pt) + _two_group_specs(tm, pt) + [
                pl.BlockSpec((D_MODEL, D_MODEL), lambda i, row: (0, 0)),
                pl.BlockSpec((None, 6, D_MODEL), lambda i, row: (row[i], 0, 0)),
            ],
            out_specs=pl.BlockSpec((tm, D_MODEL), lambda i, row: (i, 0)),
        ),
        out_shape=jax.ShapeDtypeStruct((n, D_MODEL), F32),
        compiler_params=_cparams("arbitrary"),
        name="proj_residual",
    )(tile_row, x_p, x_s, o_prompt, o_sample, w_out_bf, mod_l)


def _conv_kernel(row_ref, width_ref, x_ref, mod_ref, nw_ref, win_ref, cw_ref, wout_ref, xo_ref):
    del row_ref
    i = pl.program_id(0)
    x = x_ref[...]
    hm = (_rms(x, nw_ref[...]) * (1.0 + mod_ref[1:2, :]) + mod_ref[0:1, :]).astype(BF16)
    z = _dot(hm, win_ref[...])
    bg = z[:, :D_MODEL]
    u = z[:, D_MODEL:2 * D_MODEL] * z[:, 2 * D_MODEL:]
    tm = x.shape[0]
    pos = lax.broadcasted_iota(jnp.int32, (tm, 1), 0) & (width_ref[i] - 1)
    prev = jnp.where(pos == 0, 0.0, pltpu.roll(u, 1, axis=0))
    nxt = jnp.where(pos == width_ref[i] - 1, 0.0, pltpu.roll(u, tm - 1, axis=0))
    v = cw_ref[0:1, :] * prev + cw_ref[1:2, :] * u + cw_ref[2:3, :] * nxt
    y = _dot((bg * v).astype(BF16), wout_ref[...])
    xo_ref[...] = x + mod_ref[2:3, :] * y


def _conv_mixer(x, tile_row, tile_width, mod_l, norm_w, w_in_bf, conv_w, w_out_bf):
    n = x.shape[0]
    tm = TOKEN_TILE
    return pl.pallas_call(
        _conv_kernel,
        grid_spec=pltpu.PrefetchScalarGridSpec(
            num_scalar_prefetch=2,
            grid=(n // tm,),
            in_specs=[
                pl.BlockSpec((tm, D_MODEL), lambda i, row, w: (i, 0)),
                pl.BlockSpec((None, 6, D_MODEL), lambda i, row, w: (row[i], 0, 0)),
                pl.BlockSpec((1, D_MODEL), lambda i, row, w: (0, 0)),
                pl.BlockSpec((D_MODEL, 3 * D_MODEL), lambda i, row, w: (0, 0)),
                pl.BlockSpec((3, D_MODEL), lambda i, row, w: (0, 0)),
                pl.BlockSpec((D_MODEL, D_MODEL), lambda i, row, w: (0, 0)),
            ],
            out_specs=pl.BlockSpec((tm, D_MODEL), lambda i, row, w: (i, 0)),
        ),
        out_shape=jax.ShapeDtypeStruct((n, D_MODEL), F32),
        compiler_params=_cparams("arbitrary"),
        name="conv_mixer",
    )(tile_row, tile_width, x, mod_l, norm_w.reshape(1, D_MODEL), w_in_bf, conv_w, w_out_bf)


def _router_kernel(row_ref, x_ref, mod_ref, nw_ref, rwh_ref, rwl_ref, rb_ref,
                   hf_ref, pg_ref, pgt_ref, cnt_ref):
    del row_ref
    hf = _rms(x_ref[...], nw_ref[...]) * (1.0 + mod_ref[4:5, :]) + mod_ref[3:4, :]
    hf_hi = hf.astype(BF16)
    hf_ref[...] = hf_hi
    hf_lo = (hf - hf_hi.astype(F32)).astype(BF16)
    logits = (_dot(hf_hi, rwh_ref[...]) + (_dot(hf_hi, rwl_ref[...]) + _dot(hf_lo, rwh_ref[...]))
              + rb_ref[...])
    tm = hf.shape[0]
    lane = lax.broadcasted_iota(jnp.int32, (tm, LANES), 1)
    lane_f = lane.astype(F32)
    work = logits
    vals, hots = [], []
    for _ in range(TOP_K):
        m = jnp.max(work, axis=-1, keepdims=True)
        idx_f = jnp.min(jnp.where(work == m, lane_f, float(LANES)), axis=-1, keepdims=True)
        hot = lane_f == idx_f
        vals.append(m)
        hots.append(hot)
        work = jnp.where(hot, -jnp.inf, work)
    ex = [jnp.exp(v - vals[0]) for v in vals]
    inv = 1.0 / (ex[0] + ex[1] + ex[2] + ex[3])

    multi = jnp.where(hots[0] | hots[1] | hots[2] | hots[3], 1.0, 0.0)
    tr = lax.broadcasted_iota(jnp.int32, (tm, tm), 0)
    tc = lax.broadcasted_iota(jnp.int32, (tm, tm), 1)
    earlier = jnp.where(tc < tr, 1.0, 0.0).astype(BF16)
    before = _dot(earlier, multi.astype(BF16))
    cnt = jnp.sum(multi, axis=0, keepdims=True)
    er = lax.broadcasted_iota(jnp.int32, (LANES, LANES), 0)
    ec = lax.broadcasted_iota(jnp.int32, (LANES, LANES), 1)
    lower = jnp.where(er < ec, 1.0, 0.0).astype(BF16)
    cnt_al = jnp.floor((cnt + (SEG_ALIGN - 1)) * (1.0 / SEG_ALIGN)) * SEG_ALIGN
    estart = _dot(jnp.broadcast_to(cnt_al, (8, LANES)).astype(BF16), lower)[0:1, :]
    pos_all = before + estart

    pg = jnp.zeros((tm, LANES), F32)
    for k in range(TOP_K):
        pos = jnp.sum(jnp.where(hots[k], pos_all, 0.0), axis=-1, keepdims=True)
        pg = jnp.where(lane == k, pos, pg)
        pg = jnp.where(lane == TOP_K + k, ex[k] * inv, pg)
    pg_ref[...] = pg
    pgt_ref[...] = pg.T[0:8, :]
    cnt_ref[...] = cnt.astype(jnp.int32)


def _router(x, tile_row, mod_l, norm_w, router_w_pad, router_b_pad):
    n = x.shape[0]
    tm = ROUTE_TILE
    nt = n // tm
    rw_hi = router_w_pad.astype(BF16)
    rw_lo = (router_w_pad - rw_hi.astype(F32)).astype(BF16)
    return pl.pallas_call(
        _router_kernel,
        grid_spec=pltpu.PrefetchScalarGridSpec(
            num_scalar_prefetch=1,
            grid=(nt,),
            in_specs=[
                pl.BlockSpec((tm, D_MODEL), lambda i, row: (i, 0)),
                pl.BlockSpec((None, 6, D_MODEL), lambda i, row: (row[i], 0, 0)),
                pl.BlockSpec((1, D_MODEL), lambda i, row: (0, 0)),
                pl.BlockSpec((D_MODEL, LANES), lambda i, row: (0, 0)),
                pl.BlockSpec((D_MODEL, LANES), lambda i, row: (0, 0)),
                pl.BlockSpec((1, LANES), lambda i, row: (0, 0)),
            ],
            out_specs=[
                pl.BlockSpec((tm, D_MODEL), lambda i, row: (i, 0)),
                pl.BlockSpec((tm, LANES), lambda i, row: (i, 0)),
                pl.BlockSpec((None, 8, tm), lambda i, row: (i, 0, 0)),
                pl.BlockSpec((None, 1, LANES), lambda i, row: (i, 0, 0)),
            ],
        ),
        out_shape=[
            jax.ShapeDtypeStruct((n, D_MODEL), BF16),
            jax.ShapeDtypeStruct((n, LANES), F32),
            jax.ShapeDtypeStruct((nt, 8, tm), F32),
            jax.ShapeDtypeStruct((nt, 1, LANES), jnp.int32),
        ],
        compiler_params=_cparams("arbitrary"),
        name="router",
    )(tile_row, x, mod_l, norm_w.reshape(1, D_MODEL), rw_hi, rw_lo, router_b_pad)


SEG_STRIDE = 4 * N_EXPERTS


def _segment_table(tbl_ref, t, e):
    base = t * SEG_STRIDE
    al = lambda v: pl.multiple_of(v, SEG_ALIGN)
    return (al(tbl_ref[base + e]), al(tbl_ref[base + N_EXPERTS + e]),
            al(tbl_ref[base + 2 * N_EXPERTS + e]))


def _tile_rows_total(tbl_ref, t):
    return pl.multiple_of(tbl_ref[t * SEG_STRIDE + 3 * N_EXPERTS], SEG_ALIGN)


def _dispatch_kernel(tbl_ref, pad_ref, nu_ref, hf_ref, pgt_ref, xb_ref, stage_s, zero_s, sem, zsem,
                     *, n_blocks):
    t = pl.program_id(0)
    nt = pl.num_programs(0)
    slot = t % 2
    half = zero_s.shape[0]

    def zero_rows(start, rows):
        start = pl.multiple_of(start, SEG_ALIGN)
        rows = pl.multiple_of(rows, SEG_ALIGN)

        @pl.when(rows > 0)
        def _():
            pltpu.make_async_copy(zero_s.at[pl.ds(0, rows), :], xb_ref.at[pl.ds(start, rows), :],
                                  zsem).start()

    @pl.when(t == 0)
    def _():
        zero_s[...] = jnp.zeros_like(zero_s)

        def pad_body(e, carry):
            start = pad_ref[e]
            rows = pad_ref[N_EXPERTS + e]
            first = jnp.minimum(rows, half)
            zero_rows(start, first)
            zero_rows(start + half, rows - first)
            return carry

        lax.fori_loop(0, N_EXPERTS, pad_body, 0)

        def tail_body(b, carry):
            for part in range(EXPERT_TILE // half):
                zero_rows(b * EXPERT_TILE + part * half, half)
            return carry

        lax.fori_loop(nu_ref[0], n_blocks, tail_body, 0)

    def wait_tile(tile, sl):
        rows = _tile_rows_total(tbl_ref, tile)
        pltpu.make_async_copy(stage_s.at[sl, pl.ds(0, rows), :], xb_ref.at[pl.ds(0, rows), :],
                              sem.at[sl]).wait()

    @pl.when(t >= 2)
    def _():
        wait_tile(t - 2, slot)

    s_iota = lax.broadcasted_iota(jnp.int32, (SLOTS, ROUTE_TILE), 0).astype(F32)
    pgt = pgt_ref[...]
    hit = s_iota == pgt[0:1, :]
    for k in range(1, TOP_K):
        hit = hit | (s_iota == pgt[k:k + 1, :])
    perm = jnp.where(hit, 1.0, 0.0).astype(BF16)
    stage_s[slot] = _dot(perm, hf_ref[...])

    def seg_body(e, carry):
        rows, src, dst = _segment_table(tbl_ref, t, e)

        @pl.when(rows > 0)
        def _():
            pltpu.make_async_copy(stage_s.at[slot, pl.ds(src, rows), :],
                                  xb_ref.at[pl.ds(dst, rows), :], sem.at[slot]).start()
        return carry

    lax.fori_loop(0, N_EXPERTS, seg_body, 0)

    @pl.when(t == nt - 1)
    def _():
        wait_tile(t, slot)

        @pl.when(nt > 1)
        def _():
            wait_tile(t - 1, 1 - slot)

        zeroed = pl.multiple_of(pad_ref[2 * N_EXPERTS], SEG_ALIGN)

        @pl.when(zeroed > 0)
        def _():
            pltpu.make_async_copy(xb_ref.at[pl.ds(0, zeroed), :], xb_ref.at[pl.ds(0, zeroed), :],
                                  zsem).wait()


def _dispatch(hf, pgt, seg_tbl, pad_tbl, n_used, n_blocks):
    n = hf.shape[0]
    tm = ROUTE_TILE
    return pl.pallas_call(
        functools.partial(_dispatch_kernel, n_blocks=n_blocks),
        grid_spec=pltpu.PrefetchScalarGridSpec(
            num_scalar_prefetch=3,
            grid=(n // tm,),
            in_specs=[
                pl.BlockSpec((tm, D_MODEL), lambda i, a, b, c: (i, 0)),
                pl.BlockSpec((None, 8, tm), lambda i, a, b, c: (i, 0, 0)),
            ],
            out_specs=pl.BlockSpec(memory_space=pl.ANY),
            scratch_shapes=[
                pltpu.VMEM((2, SLOTS, D_MODEL), F32),
                pltpu.VMEM((EXPERT_TILE // 2, D_MODEL), F32),
                pltpu.SemaphoreType.DMA((2,)),
                pltpu.SemaphoreType.DMA,
            ],
        ),
        out_shape=jax.ShapeDtypeStruct((n_blocks * EXPERT_TILE, D_MODEL), F32),
        compiler_params=_cparams("arbitrary"),
        name="dispatch",
    )(seg_tbl, pad_tbl, n_used, hf, pgt)


def _expert_kernel(be_ref, nu_ref, xb_ref, wgu_ref, bgu_ref, wd_ref, bd_ref, yb_ref, wgu_s, wd_s):
    i = pl.program_id(0)

    @pl.when(i < nu_ref[0])
    def _():
        changed = jnp.logical_or(i == 0, be_ref[i] != be_ref[jnp.maximum(i - 1, 0)])

        @pl.when(changed)
        def _():
            wgu_s[...] = wgu_ref[...].astype(BF16)
            wd_s[...] = wd_ref[...].astype(BF16)

        gu = _dot(xb_ref[...].astype(BF16), wgu_s[...]) + bgu_ref[...]
        gate = jnp.minimum(gu[:, :D_FF], SWIGLU_LIMIT)
        up = jnp.clip(gu[:, D_FF:], -SWIGLU_LIMIT, SWIGLU_LIMIT)
        act = (up + 1.0) * gate * _sigmoid(SWIGLU_ALPHA * gate)
        yb_ref[...] = _dot(act.astype(BF16), wd_s[...]) + bd_ref[...]

    @pl.when(i >= nu_ref[0])
    def _():
        yb_ref[...] = jnp.zeros_like(yb_ref)


def _experts(xb, block_e, n_used, layer, w_gu, b_gu, w_down, b_down):
    te = EXPERT_TILE
    n_blocks = xb.shape[0] // te
    depth = w_gu.shape[0]

    def x_map(i, be, nu):
        return (jnp.maximum(jnp.minimum(i, nu[0] - 1), 0), 0)

    return pl.pallas_call(
        _expert_kernel,
        grid_spec=pltpu.PrefetchScalarGridSpec(
            num_scalar_prefetch=2,
            grid=(n_blocks,),
            in_specs=[
                pl.BlockSpec((te, D_MODEL), x_map),
                pl.BlockSpec((None, None, D_MODEL, 2 * D_FF), lambda i, be, nu: (layer, be[i], 0, 0)),
                pl.BlockSpec((None, None, 1, 2 * D_FF), lambda i, be, nu: (layer, be[i], 0, 0)),
                pl.BlockSpec((None, None, D_FF, D_MODEL), lambda i, be, nu: (layer, be[i], 0, 0)),
                pl.BlockSpec((None, None, 1, D_MODEL), lambda i, be, nu: (layer, be[i], 0, 0)),
            ],
            out_specs=pl.BlockSpec((te, D_MODEL), lambda i, be, nu: (i, 0)),
            scratch_shapes=[pltpu.VMEM((D_MODEL, 2 * D_FF), BF16), pltpu.VMEM((D_FF, D_MODEL), BF16)],
        ),
        out_shape=jax.ShapeDtypeStruct(xb.shape, F32),
        compiler_params=_cparams("arbitrary"),
        name="experts",
    )(block_e, n_used, xb, w_gu, b_gu.reshape(depth, N_EXPERTS, 1, 2 * D_FF), w_down,
      b_down.reshape(depth, N_EXPERTS, 1, D_MODEL))


def _combine_kernel(row_ref, tbl_ref, x_ref, pg_ref, mod_ref, fw_ref, yb_ref, *rest, split_tiles):
    del row_ref
    stage_s, sem = rest[-2:]
    t = pl.program_id(0)
    nt = pl.num_programs(0)
    slot = t % 2

    def fetch(tile, into):
        def seg_body(e, carry):
            rows, dst, src = _segment_table(tbl_ref, tile, e)

            @pl.when(rows > 0)
            def _():
                pltpu.make_async_copy(yb_ref.at[pl.ds(src, rows), :],
                                      stage_s.at[into, pl.ds(dst, rows), :], sem.at[into]).start()
            return carry

        lax.fori_loop(0, N_EXPERTS, seg_body, 0)

    @pl.when(t == 0)
    def _():
        stage_s[...] = jnp.zeros_like(stage_s)
        fetch(0, 0)

    @pl.when(t + 1 < nt)
    def _():
        fetch(t + 1, 1 - slot)

    fetched = _tile_rows_total(tbl_ref, t)
    pltpu.make_async_copy(yb_ref.at[pl.ds(0, fetched), :], stage_s.at[slot, pl.ds(0, fetched), :],
                          sem.at[slot]).wait()
    rows_sorted = stage_s[slot].astype(BF16)
    pg = pg_ref[...]
    s_iota = lax.broadcasted_iota(jnp.int32, (ROUTE_TILE, SLOTS), 1).astype(F32)
    sel = jnp.zeros((ROUTE_TILE, SLOTS), F32)
    for k in range(TOP_K):
        sel = jnp.where(s_iota == pg[:, k:k + 1], pg[:, TOP_K + k:TOP_K + k + 1], sel)
    sel_hi = sel.astype(BF16)
    sel_lo = (sel - sel_hi.astype(F32)).astype(BF16)
    y = _dot(sel_hi, rows_sorted) + _dot(sel_lo, rows_sorted)
    x = x_ref[...] + mod_ref[5:6, :] * y
    if split_tiles is None:
        rest[0][...] = x
    else:
        x = _rms(x, fw_ref[...])

        @pl.when(t < split_tiles)
        def _():
            rest[0][...] = x

        @pl.when(t >= split_tiles)
        def _():
            rest[1][...] = x


def _combine(x, pg, seg_tbl, yb, tile_row, mod_l, final_w, n_prompt=None):
    n = x.shape[0]
    tm = ROUTE_TILE
    if n_prompt is None:
        split = None
        out_specs = pl.BlockSpec((tm, D_MODEL), lambda i, row, tbl: (i, 0))
        out_shape = jax.ShapeDtypeStruct((n, D_MODEL), F32)
    else:
        split = n_prompt // tm
        out_specs = [
            pl.BlockSpec((tm, D_MODEL), lambda i, row, tbl: (jnp.minimum(i, split - 1), 0)),
            pl.BlockSpec((tm, D_MODEL), lambda i, row, tbl: (jnp.maximum(i - split, 0), 0)),
        ]
        out_shape = [jax.ShapeDtypeStruct((n_prompt, D_MODEL), F32),
                     jax.ShapeDtypeStruct((n - n_prompt, D_MODEL), F32)]
    return pl.pallas_call(
        functools.partial(_combine_kernel, split_tiles=split),
        grid_spec=pltpu.PrefetchScalarGridSpec(
            num_scalar_prefetch=2,
            grid=(n // tm,),
            in_specs=[
                pl.BlockSpec((tm, D_MODEL), lambda i, row, tbl: (i, 0)),
                pl.BlockSpec((tm, LANES), lambda i, row, tbl: (i, 0)),
                pl.BlockSpec((None, 6, D_MODEL), lambda i, row, tbl: (row[i], 0, 0)),
                pl.BlockSpec((1, D_MODEL), lambda i, row, tbl: (0, 0)),
                pl.BlockSpec(memory_space=pl.ANY),
            ],
            out_specs=out_specs,
            scratch_shapes=[pltpu.VMEM((2, SLOTS, D_MODEL), F32), pltpu.SemaphoreType.DMA((2,))],
        ),
        out_shape=out_shape,
        compiler_params=_cparams("arbitrary"),
        name="combine",
    )(tile_row, seg_tbl, x, pg, mod_l, final_w.reshape(1, D_MODEL), yb)


def _moe(x, tile_row, mod_l, norm_w, router_w, router_b, layer, w_gu, b_gu, w_down, b_down,
         final_w, n_prompt=None):
    n = x.shape[0]
    rw = jnp.zeros((D_MODEL, LANES), F32).at[:, :N_EXPERTS].set(router_w)
    rb = jnp.full((1, LANES), -jnp.inf, F32).at[0, :N_EXPERTS].set(router_b)
    hf, pg, pgt, cnt = _router(x, tile_row, mod_l, norm_w, rw, rb)

    te = EXPERT_TILE
    nt = n // ROUTE_TILE
    n_blocks = (n * TOP_K + nt * N_EXPERTS * (SEG_ALIGN - 1)) // te + 1 + N_EXPERTS
    cnt = cnt[:, 0, :N_EXPERTS]
    cnt = (cnt + SEG_ALIGN - 1) // SEG_ALIGN * SEG_ALIGN
    total = jnp.sum(cnt, axis=0)
    blocks_e = (total + te - 1) // te
    block_end = jnp.cumsum(blocks_e)
    pstart = (block_end - blocks_e) * te
    n_used = block_end[-1]
    first_row = pstart[None, :] + jnp.cumsum(cnt, axis=0) - cnt
    tile_off = jnp.cumsum(cnt, axis=1) - cnt
    tile_total = jnp.broadcast_to(jnp.sum(cnt, axis=1, keepdims=True), cnt.shape)
    seg_tbl = jnp.concatenate([cnt, tile_off, first_row, tile_total], axis=1)
    seg_tbl = seg_tbl.reshape(-1).astype(jnp.int32)
    pad_rows = blocks_e * te - total
    zeroed = jnp.sum(pad_rows) + (n_blocks - n_used) * te
    pad_tbl = jnp.concatenate([pstart + total, pad_rows, zeroed[None]]).astype(jnp.int32)
    block_ids = jnp.arange(n_blocks, dtype=jnp.int32)
    clamped = jnp.minimum(block_ids, n_used - 1)
    block_e = jnp.sum((clamped[:, None] >= block_end[None, :]).astype(jnp.int32), axis=1)
    block_e = jnp.minimum(block_e, N_EXPERTS - 1).astype(jnp.int32)
    n_used = n_used.astype(jnp.int32).reshape(1)

    xb = _dispatch(hf, pgt, seg_tbl, pad_tbl, n_used, n_blocks)
    yb = _experts(xb, block_e, n_used, layer, w_gu, b_gu, w_down, b_down)
    return _combine(x, pg, seg_tbl, yb, tile_row, mod_l, final_w, n_prompt)


def _tile_rows(n_prompt_tok, n_sample_seq, sample_len, tile):
    starts = np.arange(0, n_prompt_tok + n_sample_seq * sample_len, tile)
    row = np.where(starts < n_prompt_tok, 0, 1 + (starts - n_prompt_tok) // sample_len)
    return jnp.asarray(row, dtype=jnp.int32)


def kernel(x_prompt, x_sample, state_hgrn, c, c_ctx, w_mod, b_mod, norm_mix, norm_ffn, hg_w_in,
           hg_lb_logits, hg_gnorm, hg_w_out, cv_w_in, cv_w, cv_w_out, router_w, router_b,
           moe_w_gu, moe_b_gu, moe_w_down, moe_b_down, final_norm):
    bp, tp, d = x_prompt.shape
    bs, ts, _ = x_sample.shape
    depth = w_mod.shape[0]
    n_prompt = bp * tp
    n = n_prompt + bs * ts
    assert d == D_MODEL and depth == 2 and 1 + bs <= 16
    assert n_prompt % SCAN_ROWS == 0 and ts == SCAN_ROWS and tp == SCAN_PIECE

    x_p = x_prompt.reshape(n_prompt, d)
    x_s = x_sample.reshape(bs * ts, d)
    cond16 = jnp.zeros((16, d), F32).at[0].set(c_ctx).at[1:1 + bs].set(c)
    mod = _modulation(cond16, w_mod.astype(BF16), b_mod).reshape(depth, 16, 6, d)

    tile_row = _tile_rows(n_prompt, bs, ts, TOKEN_TILE)
    tile_row_route = _tile_rows(n_prompt, bs, ts, ROUTE_TILE)
    starts = np.arange(0, n, TOKEN_TILE)
    tile_width = jnp.asarray(np.where(starts < n_prompt, tp, GRID_W), dtype=jnp.int32)

    lb_all = jnp.cumsum(jax.nn.softmax(hg_lb_logits.astype(F32), axis=1), axis=1)

    hm = _norm_mod(x_p, x_s, tile_row, mod[0], norm_mix[0])
    w_in_bf = hg_w_in[0].astype(BF16)
    gn = hg_gnorm[0].reshape(-1)
    zero_state = jnp.zeros((SCAN_ROWS // tp, 2, HEADS, HEAD_DIM, HEAD_DIM), F32)
    o_p, s_new = _hgrn_scan(hm, 0, n_prompt, w_in_bf, lb_all[0, 0], lb_all[1, 0], gn, zero_state,
                            tp, True)
    o_s, _ = _hgrn_scan(hm, n_prompt, bs * ts, w_in_bf, lb_all[0, 0], lb_all[1, 0], gn,
                        state_hgrn[:, 0], ts, False)
    x = _proj_residual(x_p, x_s, o_p, o_s, hg_w_out[0].astype(BF16), tile_row, mod[0])
    x = _moe(x, tile_row_route, mod[0], norm_ffn[0], router_w[0], router_b[0],
             0, moe_w_gu, moe_b_gu, moe_w_down, moe_b_down, final_norm)

    x = _conv_mixer(x, tile_row, tile_width, mod[1], norm_mix[1], cv_w_in[0].astype(BF16), cv_w[0],
                    cv_w_out[0].astype(BF16))
    y_p, y_s = _moe(x, tile_row_route, mod[1], norm_ffn[1], router_w[1], router_b[1],
                    1, moe_w_gu, moe_b_gu, moe_w_down, moe_b_down, final_norm, n_prompt)

    y_prompt = y_p.reshape(bp, tp, d)
    y_sample = y_s.reshape(bs, ts, d)
    return (y_prompt, y_sample, s_new.reshape(bp, 1, 2, HEADS, HEAD_DIM, HEAD_DIM))
```

```python
import functools

import numpy as np
import jax
import jax.numpy as jnp
from jax import lax
from jax.experimental import pallas as pl
from jax.experimental.pallas import tpu as pltpu

F32 = jnp.float32
BF16 = jnp.bfloat16

D_MODEL = 1024
HEADS = 8
HEAD_DIM = 128
CHUNK = 32
GRID_W = 64
N_EXPERTS = 32
TOP_K = 4
D_FF = 1024
SWIGLU_LIMIT = 7.0
SWIGLU_ALPHA = 1.702
EPS = 1e-6

LANES = 128
SCAN_ROWS = 2048
SCAN_PIECE = 256
TOKEN_TILE = 512
EXPERT_TILE = 512
ROUTE_TILE = 256
SEG_ALIGN = 8
SLOTS = ROUTE_TILE * TOP_K + N_EXPERTS * SEG_ALIGN
VMEM_LIMIT = 56 * 1024 * 1024


def _cparams(*sem):
    return pltpu.CompilerParams(dimension_semantics=sem, vmem_limit_bytes=VMEM_LIMIT)


def _sigmoid(x):
    return 1.0 / (1.0 + jnp.exp(-x))


def _rms(x, w):
    return x * lax.rsqrt(jnp.mean(x * x, axis=-1, keepdims=True) + EPS) * w


def _dot(a, b):
    return jnp.dot(a, b, preferred_element_type=F32)


def _dot_nt(a, b):
    return lax.dot_general(a, b, (((1,), (1,)), ((), ())), preferred_element_type=F32)


def _dot_tn(a, b):
    return lax.dot_general(a, b, (((0,), (0,)), ((), ())), preferred_element_type=F32)


PACKED = D_MODEL // 2
_HIGH_HALF = 0xFFFF0000


def _pack_pairs(x):
    c = x.shape[1] // 2
    lo = lax.bitcast_convert_type(x[:, :c], jnp.uint32) >> 16
    hi = lax.bitcast_convert_type(x[:, c:], jnp.uint32) & jnp.uint32(_HIGH_HALF)
    return hi | lo


def _unpack_pairs(u):
    lo = lax.bitcast_convert_type(u << 16, F32)
    hi = lax.bitcast_convert_type(u & jnp.uint32(_HIGH_HALF), F32)
    return jnp.concatenate([lo, hi], axis=1).astype(BF16)


def _mod_kernel(cond_ref, w_ref, b_ref, o_ref):
    c = cond_ref[...]
    s = (c * _sigmoid(c)).astype(BF16)
    o_ref[...] = _dot(s, w_ref[...]) + b_ref[...]


def _modulation(cond16, w_mod_bf, b_mod):
    depth = w_mod_bf.shape[0]
    n_out = w_mod_bf.shape[2]
    tn = 1024
    return pl.pallas_call(
        _mod_kernel,
        grid=(depth, n_out // tn),
        in_specs=[
            pl.BlockSpec((16, D_MODEL), lambda l, j: (0, 0)),
            pl.BlockSpec((None, D_MODEL, tn), lambda l, j: (l, 0, j)),
            pl.BlockSpec((None, 1, tn), lambda l, j: (l, 0, j)),
        ],
        out_specs=pl.BlockSpec((None, 16, tn), lambda l, j: (l, 0, j)),
        out_shape=jax.ShapeDtypeStruct((depth, 16, n_out), F32),
        compiler_params=_cparams("arbitrary", "arbitrary"),
        name="modulation",
    )(cond16, w_mod_bf, b_mod.reshape(depth, 1, n_out))


def _two_group_specs(tm, prompt_tiles):
    return [
        pl.BlockSpec((tm, D_MODEL), lambda i, *_: (jnp.minimum(i, prompt_tiles - 1), 0)),
        pl.BlockSpec((tm, D_MODEL), lambda i, *_: (jnp.maximum(i - prompt_tiles, 0), 0)),
    ]


def _pick_group(a_ref, b_ref, prompt_tiles):
    return jnp.where(pl.program_id(0) < prompt_tiles, a_ref[...], b_ref[...])


def _norm_mod_kernel(row_ref, xp_ref, xs_ref, mod_ref, nw_ref, o_ref, *, prompt_tiles):
    del row_ref
    y = _rms(_pick_group(xp_ref, xs_ref, prompt_tiles), nw_ref[...])
    o_ref[...] = (y * (1.0 + mod_ref[1:2, :]) + mod_ref[0:1, :]).astype(o_ref.dtype)


def _norm_mod(x_p, x_s, tile_row, mod_l, norm_w):
    n = x_p.shape[0] + x_s.shape[0]
    tm = TOKEN_TILE
    pt = x_p.shape[0] // tm
    return pl.pallas_call(
        functools.partial(_norm_mod_kernel, prompt_tiles=pt),
        grid_spec=pltpu.PrefetchScalarGridSpec(
            num_scalar_prefetch=1,
            grid=(n // tm,),
            in_specs=_two_group_specs(tm, pt) + [
                pl.BlockSpec((None, 6, D_MODEL), lambda i, row: (row[i], 0, 0)),
                pl.BlockSpec((1, D_MODEL), lambda i, row: (0, 0)),
            ],
            out_specs=pl.BlockSpec((tm, D_MODEL), lambda i, row: (i, 0)),
        ),
        out_shape=jax.ShapeDtypeStruct((n, D_MODEL), BF16),
        compiler_params=_cparams("arbitrary"),
        name="norm_mod",
    )(tile_row, x_p, x_s, mod_l, norm_w.reshape(1, D_MODEL))


def _hgrn_kernel(hm_ref, wq_ref, wff_ref, wfb_ref, wv_ref, wg_ref, lbf_ref, lbb_ref, gn_ref,
                 s0_ref, o_ref, so_ref, q_s, vt_s, kf_s, lf_s, kb_s, lb_s, of_s, ob_s, st_s,
                 *, seq_len, zero_init):
    rows = hm_ref.shape[0]
    piece = SCAN_PIECE
    n_pieces = rows // piece
    cpp = piece // CHUNK
    per_piece_seq = seq_len == piece
    assert per_piece_seq or seq_len == rows
    hm = hm_ref[...]

    zq = _dot(hm, wq_ref[...])
    q_s[...] = zq * _sigmoid(zq)
    v = _dot(hm, wv_ref[...])
    for p in range(n_pieces):
        vt_s[p] = v[p * piece:(p + 1) * piece, :].T.astype(BF16)
    for w_ref, lbr, k_s, l_s in ((wff_ref, lbf_ref, kf_s, lf_s), (wfb_ref, lbb_ref, kb_s, lb_s)):
        lb = lbr[...]
        f = lb + (1.0 - lb) * _sigmoid(_dot(hm, w_ref[...]))
        k_s[...] = 1.0 - f
        l_s[...] = jnp.log(f)

    def load_state(j, d):
        for h in range(2):
            if zero_init:
                st_s[d, h] = jnp.zeros((HEAD_DIM, HEAD_DIM), F32)
            else:
                st_s[d, h] = s0_ref[j, d, h].T

    def store_state(j, d):
        for h in range(2):
            so_ref[j, d, h] = st_s[d, h].T

    def piece_body(i, carry):
        r = lax.broadcasted_iota(jnp.int32, (piece, piece), 0)
        c = lax.broadcasted_iota(jnp.int32, (piece, piece), 1)
        same = (r // CHUNK) == (c // CHUNK)
        row_chunk = lax.broadcasted_iota(jnp.int32, (piece, 1), 0) // CHUNK
        col_chunk = lax.broadcasted_iota(jnp.int32, (1, piece), 1) // CHUNK
        for d in range(2):
            k_s, l_s, o_s = ((kf_s, lf_s, of_s), (kb_s, lb_s, ob_s))[d]
            p = i if d == 0 else n_pieces - 1 - i
            if per_piece_seq:
                load_state(p, d)
            sl = pl.ds(pl.multiple_of(p * piece, piece), piece)
            keep = same & ((c <= r) if d == 0 else (c >= r))
            tri = jnp.where(keep, 1.0, 0.0).astype(BF16)
            lf = l_s[sl, :]
            hi = lf.astype(BF16)
            r1 = lf - hi.astype(F32)
            mid = r1.astype(BF16)
            lo = (r1 - mid.astype(F32)).astype(BF16)
            b = _dot(tri, hi) + _dot(tri, mid) + _dot(tri, lo)
            b3 = b.reshape(cpp, CHUNK, 2 * HEAD_DIM)
            edge = CHUNK - 1 if d == 0 else 0
            bl3 = b3[:, edge:edge + 1, :]
            q = q_s[sl, :]
            k = k_s[sl, :]
            qd = (q * jnp.exp(b)).astype(BF16)
            ki = (k * jnp.exp(-b)).astype(BF16)
            ks = (k.reshape(cpp, CHUNK, 2 * HEAD_DIM) * jnp.exp(bl3 - b3)).reshape(piece, 2 * HEAD_DIM)
            ks = ks.astype(BF16)
            dec = jnp.exp(bl3)
            vt = vt_s[p]
            for h in range(2):
                hs = slice(h * HEAD_DIM, (h + 1) * HEAD_DIM)
                att = jnp.where(keep, _dot_nt(qd[:, hs], ki[:, hs]), 0.0).astype(BF16)
                vt_h = vt[hs, :]
                v_exp = jnp.concatenate(
                    [jnp.where(col_chunk == ci, vt_h, jnp.zeros_like(vt_h)) for ci in range(cpp)], axis=0)
                incr = _dot(v_exp, ks[:, hs])
                st = st_s[d, h]
                before = [None] * cpp
                for ci in (range(cpp) if d == 0 else range(cpp - 1, -1, -1)):
                    before[ci] = st.astype(BF16)
                    st = st * dec[ci, :, hs] + incr[ci * HEAD_DIM:(ci + 1) * HEAD_DIM, :]
                st_s[d, h] = st
                qd_h = qd[:, hs]
                q_exp = [jnp.where(row_chunk == ci, qd_h, jnp.zeros_like(qd_h)) for ci in range(cpp)]
                lhs = jnp.concatenate([att] + q_exp, axis=1)
                rhs_t = jnp.concatenate([vt_h] + before, axis=1)
                o_s[sl, hs] = _dot_nt(lhs, rhs_t)
            if per_piece_seq:
                store_state(p, d)
        return carry

    if not per_piece_seq:
        load_state(0, 0)
        load_state(0, 1)
    lax.fori_loop(0, n_pieces, piece_body, 0)
    if not per_piece_seq:
        store_state(0, 0)
        store_state(0, 1)

    o = of_s[...] + ob_s[...]
    gn = gn_ref[...]
    o = jnp.concatenate(
        [_rms(o[:, h * HEAD_DIM:(h + 1) * HEAD_DIM], gn[:, h * HEAD_DIM:(h + 1) * HEAD_DIM])
         for h in range(2)], axis=1)
    o_ref[...] = (o * _sigmoid(_dot(hm, wg_ref[...]))).astype(o_ref.dtype)


def _hgrn_scan(hm, row0, n_rows, w_in_bf, lb_f, lb_b, gnorm, s0, seq_len, zero_init):
    n_seq_total = n_rows // seq_len
    rows = SCAN_ROWS
    seq_per_step = rows // seq_len
    blk0 = row0 // rows
    pair = 2 * HEAD_DIM
    n_pairs = HEADS // 2

    def w_spec(seg):
        return pl.BlockSpec((D_MODEL, pair), lambda sb, hp: (0, seg * n_pairs + hp))

    vec_spec = pl.BlockSpec((1, pair), lambda sb, hp: (0, hp))
    st_block = (seq_per_step, 2, 2, HEAD_DIM, HEAD_DIM)
    st_spec = pl.BlockSpec(st_block, lambda sb, hp: (sb, 0, hp, 0, 0))
    s0_spec = pl.BlockSpec(st_block, (lambda sb, hp: (0, 0, hp, 0, 0)) if zero_init
                           else (lambda sb, hp: (sb, 0, hp, 0, 0)))
    scratch = [
        pltpu.VMEM((rows, pair), F32),
        pltpu.VMEM((rows // SCAN_PIECE, pair, SCAN_PIECE), BF16),
        pltpu.VMEM((rows, pair), F32),
        pltpu.VMEM((rows, pair), F32),
        pltpu.VMEM((rows, pair), F32),
        pltpu.VMEM((rows, pair), F32),
        pltpu.VMEM((rows, pair), F32),
        pltpu.VMEM((rows, pair), F32),
        pltpu.VMEM((2, 2, HEAD_DIM, HEAD_DIM), F32),
    ]
    in_specs = [
        pl.BlockSpec((rows, D_MODEL), lambda sb, hp: (blk0 + sb, 0)),
        w_spec(0), w_spec(1), w_spec(2), w_spec(3), w_spec(4),
        vec_spec, vec_spec, vec_spec,
        s0_spec,
    ]
    args = [hm, w_in_bf, w_in_bf, w_in_bf, w_in_bf, w_in_bf,
            lb_f.reshape(1, D_MODEL), lb_b.reshape(1, D_MODEL), gnorm.reshape(1, D_MODEL), s0]
    return pl.pallas_call(
        functools.partial(_hgrn_kernel, seq_len=seq_len, zero_init=zero_init),
        grid=(n_rows // rows, n_pairs),
        in_specs=in_specs,
        out_specs=[
            pl.BlockSpec((rows, pair), lambda sb, hp: (sb, hp)),
            st_spec,
        ],
        out_shape=[
            jax.ShapeDtypeStruct((n_rows, D_MODEL), BF16),
            jax.ShapeDtypeStruct((n_seq_total, 2, HEADS, HEAD_DIM, HEAD_DIM), F32),
        ],
        scratch_shapes=scratch,
        compiler_params=_cparams("arbitrary", "arbitrary"),
        name="hgrn_scan",
    )(*args)


def _proj_res_kernel(row_ref, xp_ref, xs_ref, op_ref, os_ref, w_ref, mod_ref, xo_ref, *, prompt_tiles):
    del row_ref
    x = _pick_group(xp_ref, xs_ref, prompt_tiles)
    o = _pick_group(op_ref, os_ref, prompt_tiles)
    xo_ref[...] = x + mod_ref[2:3, :] * _dot(o, w_ref[...])


def _proj_residual(x_p, x_s, o_prompt, o_sample, w_out_bf, tile_row, mod_l):
    n = x_p.shape[0] + x_s.shape[0]
    tm = TOKEN_TILE
    pt = o_prompt.shape[0] // tm
    return pl.pallas_call(
        functools.partial(_proj_res_kernel, prompt_tiles=pt),
        grid_spec=pltpu.PrefetchScalarGridSpec(
            num_scalar_prefetch=1,
            grid=(n // tm,),
            in_specs=_two_group_specs(tm, pt) + _two_group_specs(tm, pt) + [
                pl.BlockSpec((D_MODEL, D_MODEL), lambda i, row: (0, 0)),
                pl.BlockSpec((None, 6, D_MODEL), lambda i, row: (row[i], 0, 0)),
            ],
            out_specs=pl.BlockSpec((tm, D_MODEL), lambda i, row: (i, 0)),
        ),
        out_shape=jax.ShapeDtypeStruct((n, D_MODEL), F32),
        compiler_params=_cparams("arbitrary"),
        name="proj_residual",
    )(tile_row, x_p, x_s, o_prompt, o_sample, w_out_bf, mod_l)


def _conv_kernel(row_ref, width_ref, x_ref, mod_ref, nw_ref, win_ref, cw_ref, wout_ref, xo_ref):
    del row_ref
    i = pl.program_id(0)
    x = x_ref[...]
    hm = (_rms(x, nw_ref[...]) * (1.0 + mod_ref[1:2, :]) + mod_ref[0:1, :]).astype(BF16)
    z = _dot(hm, win_ref[...])
    bg = z[:, :D_MODEL]
    u = z[:, D_MODEL:2 * D_MODEL] * z[:, 2 * D_MODEL:]
    tm = x.shape[0]
    pos = lax.broadcasted_iota(jnp.int32, (tm, 1), 0) & (width_ref[i] - 1)
    prev = jnp.where(pos == 0, 0.0, pltpu.roll(u, 1, axis=0))
    nxt = jnp.where(pos == width_ref[i] - 1, 0.0, pltpu.roll(u, tm - 1, axis=0))
    v = cw_ref[0:1, :] * prev + cw_ref[1:2, :] * u + cw_ref[2:3, :] * nxt
    y = _dot((bg * v).astype(BF16), wout_ref[...])
    xo_ref[...] = x + mod_ref[2:3, :] * y


def _conv_mixer(x, tile_row, tile_width, mod_l, norm_w, w_in_bf, conv_w, w_out_bf):
    n = x.shape[0]
    tm = TOKEN_TILE
    return pl.pallas_call(
        _conv_kernel,
        grid_spec=pltpu.PrefetchScalarGridSpec(
            num_scalar_prefetch=2,
            grid=(n // tm,),
            in_specs=[
                pl.BlockSpec((tm, D_MODEL), lambda i, row, w: (i, 0)),
                pl.BlockSpec((None, 6, D_MODEL), lambda i, row, w: (row[i], 0, 0)),
                pl.BlockSpec((1, D_MODEL), lambda i, row, w: (0, 0)),
                pl.BlockSpec((D_MODEL, 3 * D_MODEL), lambda i, row, w: (0, 0)),
                pl.BlockSpec((3, D_MODEL), lambda i, row, w: (0, 0)),
                pl.BlockSpec((D_MODEL, D_MODEL), lambda i, row, w: (0, 0)),
            ],
            out_specs=pl.BlockSpec((tm, D_MODEL), lambda i, row, w: (i, 0)),
        ),
        out_shape=jax.ShapeDtypeStruct((n, D_MODEL), F32),
        compiler_params=_cparams("arbitrary"),
        name="conv_mixer",
    )(tile_row, tile_width, x, mod_l, norm_w.reshape(1, D_MODEL), w_in_bf, conv_w, w_out_bf)


def _router_kernel(row_ref, x_ref, mod_ref, nw_ref, rwh_ref, rwl_ref, rb_ref,
                   hf_ref, pg_ref, pgt_ref, cnt_ref):
    del row_ref
    hf = _rms(x_ref[...], nw_ref[...]) * (1.0 + mod_ref[4:5, :]) + mod_ref[3:4, :]
    hf_hi = hf.astype(BF16)
    hf_ref[...] = hf_hi
    hf_lo = (hf - hf_hi.astype(F32)).astype(BF16)
    logits = (_dot(hf_hi, rwh_ref[...]) + (_dot(hf_hi, rwl_ref[...]) + _dot(hf_lo, rwh_ref[...]))
              + rb_ref[...])
    tm = hf.shape[0]
    lane = lax.broadcasted_iota(jnp.int32, (tm, LANES), 1)
    lane_f = lane.astype(F32)
    work = logits
    vals, hots = [], []
    for _ in range(TOP_K):
        m = jnp.max(work, axis=-1, keepdims=True)
        idx_f = jnp.min(jnp.where(work == m, lane_f, float(LANES)), axis=-1, keepdims=True)
        hot = lane_f == idx_f
        vals.append(m)
        hots.append(hot)
        work = jnp.where(hot, -jnp.inf, work)
    ex = [jnp.exp(v - vals[0]) for v in vals]
    inv = 1.0 / (ex[0] + ex[1] + ex[2] + ex[3])

    multi = jnp.where(hots[0] | hots[1] | hots[2] | hots[3], 1.0, 0.0)
    tr = lax.broadcasted_iota(jnp.int32, (tm, tm), 0)
    tc = lax.broadcasted_iota(jnp.int32, (tm, tm), 1)
    earlier = jnp.where(tc < tr, 1.0, 0.0).astype(BF16)
    before = _dot(earlier, multi.astype(BF16))
    cnt = jnp.sum(multi, axis=0, keepdims=True)
    er = lax.broadcasted_iota(jnp.int32, (LANES, LANES), 0)
    ec = lax.broadcasted_iota(jnp.int32, (LANES, LANES), 1)
    lower = jnp.where(er < ec, 1.0, 0.0).astype(BF16)
    cnt_al = jnp.floor((cnt + (SEG_ALIGN - 1)) * (1.0 / SEG_ALIGN)) * SEG_ALIGN
    estart = _dot(jnp.broadcast_to(cnt_al, (8, LANES)).astype(BF16), lower)[0:1, :]
    pos_all = before + estart

    pg = jnp.zeros((tm, LANES), F32)
    for k in range(TOP_K):
        pos = jnp.sum(jnp.where(hots[k], pos_all, 0.0), axis=-1, keepdims=True)
        pg = jnp.where(lane == k, pos, pg)
        pg = jnp.where(lane == TOP_K + k, ex[k] * inv, pg)
    pg_ref[...] = pg
    pgt_ref[...] = pg.T[0:8, :]
    cnt_ref[...] = cnt.astype(jnp.int32)


def _router(x, tile_row, mod_l, norm_w, router_w_pad, router_b_pad):
    n = x.shape[0]
    tm = ROUTE_TILE
    nt = n // tm
    rw_hi = router_w_pad.astype(BF16)
    rw_lo = (router_w_pad - rw_hi.astype(F32)).astype(BF16)
    return pl.pallas_call(
        _router_kernel,
        grid_spec=pltpu.PrefetchScalarGridSpec(
            num_scalar_prefetch=1,
            grid=(nt,),
            in_specs=[
                pl.BlockSpec((tm, D_MODEL), lambda i, row: (i, 0)),
                pl.BlockSpec((None, 6, D_MODEL), lambda i, row: (row[i], 0, 0)),
                pl.BlockSpec((1, D_MODEL), lambda i, row: (0, 0)),
                pl.BlockSpec((D_MODEL, LANES), lambda i, row: (0, 0)),
                pl.BlockSpec((D_MODEL, LANES), lambda i, row: (0, 0)),
                pl.BlockSpec((1, LANES), lambda i, row: (0, 0)),
            ],
            out_specs=[
                pl.BlockSpec((tm, D_MODEL), lambda i, row: (i, 0)),
                pl.BlockSpec((tm, LANES), lambda i, row: (i, 0)),
                pl.BlockSpec((None, 8, tm), lambda i, row: (i, 0, 0)),
                pl.BlockSpec((None, 1, LANES), lambda i, row: (i, 0, 0)),
            ],
        ),
        out_shape=[
            jax.ShapeDtypeStruct((n, D_MODEL), BF16),
            jax.ShapeDtypeStruct((n, LANES), F32),
            jax.ShapeDtypeStruct((nt, 8, tm), F32),
            jax.ShapeDtypeStruct((nt, 1, LANES), jnp.int32),
        ],
        compiler_params=_cparams("arbitrary"),
        name="router",
    )(tile_row, x, mod_l, norm_w.reshape(1, D_MODEL), rw_hi, rw_lo, router_b_pad)


SEG_STRIDE = 4 * N_EXPERTS


def _segment_table(tbl_ref, t, e):
    base = t * SEG_STRIDE
    al = lambda v: pl.multiple_of(v, SEG_ALIGN)
    return (al(tbl_ref[base + e]), al(tbl_ref[base + N_EXPERTS + e]),
            al(tbl_ref[base + 2 * N_EXPERTS + e]))


def _tile_rows_total(tbl_ref, t):
    return pl.multiple_of(tbl_ref[t * SEG_STRIDE + 3 * N_EXPERTS], SEG_ALIGN)


def _dispatch_kernel(tbl_ref, pad_ref, nu_ref, hf_ref, pgt_ref, xb_ref, stage_s, zero_s, sem, zsem,
                     *, n_blocks):
    t = pl.program_id(0)
    nt = pl.num_programs(0)
    slot = t % 2
    half = zero_s.shape[0]

    def zero_rows(start, rows):
        start = pl.multiple_of(start, SEG_ALIGN)
        rows = pl.multiple_of(rows, SEG_ALIGN)

        @pl.when(rows > 0)
        def _():
            pltpu.make_async_copy(zero_s.at[pl.ds(0, rows), :], xb_ref.at[pl.ds(start, rows), :],
                                  zsem).start()

    @pl.when(t == 0)
    def _():
        zero_s[...] = jnp.zeros_like(zero_s)

        def pad_body(e, carry):
            start = pad_ref[e]
            rows = pad_ref[N_EXPERTS + e]
            first = jnp.minimum(rows, half)
            zero_rows(start, first)
            zero_rows(start + half, rows - first)
            return carry

        lax.fori_loop(0, N_EXPERTS, pad_body, 0)

        def tail_body(b, carry):
            for part in range(EXPERT_TILE // half):
                zero_rows(b * EXPERT_TILE + part * half, half)
            return carry

        lax.fori_loop(nu_ref[0], n_blocks, tail_body, 0)

    def wait_tile(tile, sl):
        rows = _tile_rows_total(tbl_ref, tile)
        pltpu.make_async_copy(stage_s.at[sl, pl.ds(0, rows), :], xb_ref.at[pl.ds(0, rows), :],
                              sem.at[sl]).wait()

    @pl.when(t >= 2)
    def _():
        wait_tile(t - 2, slot)

    s_iota = lax.broadcasted_iota(jnp.int32, (SLOTS, ROUTE_TILE), 0).astype(F32)
    pgt = pgt_ref[...]
    hit = s_iota == pgt[0:1, :]
    for k in range(1, TOP_K):
        hit = hit | (s_iota == pgt[k:k + 1, :])
    perm = jnp.where(hit, 1.0, 0.0).astype(BF16)
    stage_s[slot] = _pack_pairs(_dot(perm, hf_ref[...]))

    def seg_body(e, carry):
        rows, src, dst = _segment_table(tbl_ref, t, e)

        @pl.when(rows > 0)
        def _():
            pltpu.make_async_copy(stage_s.at[slot, pl.ds(src, rows), :],
                                  xb_ref.at[pl.ds(dst, rows), :], sem.at[slot]).start()
        return carry

    lax.fori_loop(0, N_EXPERTS, seg_body, 0)

    @pl.when(t == nt - 1)
    def _():
        wait_tile(t, slot)

        @pl.when(nt > 1)
        def _():
            wait_tile(t - 1, 1 - slot)

        zeroed = pl.multiple_of(pad_ref[2 * N_EXPERTS], SEG_ALIGN)

        @pl.when(zeroed > 0)
        def _():
            pltpu.make_async_copy(xb_ref.at[pl.ds(0, zeroed), :], xb_ref.at[pl.ds(0, zeroed), :],
                                  zsem).wait()


def _dispatch(hf, pgt, seg_tbl, pad_tbl, n_used, n_blocks):
    n = hf.shape[0]
    tm = ROUTE_TILE
    return pl.pallas_call(
        functools.partial(_dispatch_kernel, n_blocks=n_blocks),
        grid_spec=pltpu.PrefetchScalarGridSpec(
            num_scalar_prefetch=3,
            grid=(n // tm,),
            in_specs=[
                pl.BlockSpec((tm, D_MODEL), lambda i, a, b, c: (i, 0)),
                pl.BlockSpec((None, 8, tm), lambda i, a, b, c: (i, 0, 0)),
            ],
            out_specs=pl.BlockSpec(memory_space=pl.ANY),
            scratch_shapes=[
                pltpu.VMEM((2, SLOTS, PACKED), jnp.uint32),
                pltpu.VMEM((EXPERT_TILE // 2, PACKED), jnp.uint32),
                pltpu.SemaphoreType.DMA((2,)),
                pltpu.SemaphoreType.DMA,
            ],
        ),
        out_shape=jax.ShapeDtypeStruct((n_blocks * EXPERT_TILE, PACKED), jnp.uint32),
        compiler_params=_cparams("arbitrary"),
        name="dispatch",
    )(seg_tbl, pad_tbl, n_used, hf, pgt)


def _expert_kernel(be_ref, nu_ref, xb_ref, wgu_ref, bgu_ref, wd_ref, bd_ref, yb_ref, wgu_s, wd_s):
    i = pl.program_id(0)

    @pl.when(i < nu_ref[0])
    def _():
        changed = jnp.logical_or(i == 0, be_ref[i] != be_ref[jnp.maximum(i - 1, 0)])

        @pl.when(changed)
        def _():
            wgu_s[...] = wgu_ref[...].astype(BF16)
            wd_s[...] = wd_ref[...].astype(BF16)

        gu = _dot(_unpack_pairs(xb_ref[...]), wgu_s[...]) + bgu_ref[...]
        gate = jnp.minimum(gu[:, :D_FF], SWIGLU_LIMIT)
        up = jnp.clip(gu[:, D_FF:], -SWIGLU_LIMIT, SWIGLU_LIMIT)
        act = (up + 1.0) * gate * _sigmoid(SWIGLU_ALPHA * gate)
        y = _dot(act.astype(BF16), wd_s[...]) + bd_ref[...]
        yb_ref[...] = _pack_pairs(y.astype(BF16).astype(F32))

    @pl.when(i >= nu_ref[0])
    def _():
        yb_ref[...] = jnp.zeros_like(yb_ref)


def _experts(xb, block_e, n_used, layer, w_gu, b_gu, w_down, b_down):
    te = EXPERT_TILE
    n_blocks = xb.shape[0] // te
    depth = w_gu.shape[0]

    def x_map(i, be, nu):
        return (jnp.maximum(jnp.minimum(i, nu[0] - 1), 0), 0)

    return pl.pallas_call(
        _expert_kernel,
        grid_spec=pltpu.PrefetchScalarGridSpec(
            num_scalar_prefetch=2,
            grid=(n_blocks,),
            in_specs=[
                pl.BlockSpec((te, PACKED), x_map),
                pl.BlockSpec((None, None, D_MODEL, 2 * D_FF), lambda i, be, nu: (layer, be[i], 0, 0)),
                pl.BlockSpec((None, None, 1, 2 * D_FF), lambda i, be, nu: (layer, be[i], 0, 0)),
                pl.BlockSpec((None, None, D_FF, D_MODEL), lambda i, be, nu: (layer, be[i], 0, 0)),
                pl.BlockSpec((None, None, 1, D_MODEL), lambda i, be, nu: (layer, be[i], 0, 0)),
            ],
            out_specs=pl.BlockSpec((te, PACKED), lambda i, be, nu: (i, 0)),
            scratch_shapes=[pltpu.VMEM((D_MODEL, 2 * D_FF), BF16), pltpu.VMEM((D_FF, D_MODEL), BF16)],
        ),
        out_shape=jax.ShapeDtypeStruct(xb.shape, jnp.uint32),
        compiler_params=_cparams("arbitrary"),
        name="experts",
    )(block_e, n_used, xb, w_gu, b_gu.reshape(depth, N_EXPERTS, 1, 2 * D_FF), w_down,
      b_down.reshape(depth, N_EXPERTS, 1, D_MODEL))


def _combine_kernel(row_ref, tbl_ref, x_ref, pg_ref, mod_ref, fw_ref, yb_ref, *rest, split_tiles):
    del row_ref
    stage_s, sem = rest[-2:]
    t = pl.program_id(0)
    nt = pl.num_programs(0)
    slot = t % 2

    def fetch(tile, into):
        def seg_body(e, carry):
            rows, dst, src = _segment_table(tbl_ref, tile, e)

            @pl.when(rows > 0)
            def _():
                pltpu.make_async_copy(yb_ref.at[pl.ds(src, rows), :],
                                      stage_s.at[into, pl.ds(dst, rows), :], sem.at[into]).start()
            return carry

        lax.fori_loop(0, N_EXPERTS, seg_body, 0)

    @pl.when(t == 0)
    def _():
        stage_s[...] = jnp.zeros_like(stage_s)
        fetch(0, 0)

    @pl.when(t + 1 < nt)
    def _():
        fetch(t + 1, 1 - slot)

    fetched = _tile_rows_total(tbl_ref, t)
    pltpu.make_async_copy(yb_ref.at[pl.ds(0, fetched), :], stage_s.at[slot, pl.ds(0, fetched), :],
                          sem.at[slot]).wait()
    rows_sorted = _unpack_pairs(stage_s[slot])
    pg = pg_ref[...]
    s_iota = lax.broadcasted_iota(jnp.int32, (ROUTE_TILE, SLOTS), 1).astype(F32)
    sel = jnp.zeros((ROUTE_TILE, SLOTS), F32)
    for k in range(TOP_K):
        sel = jnp.where(s_iota == pg[:, k:k + 1], pg[:, TOP_K + k:TOP_K + k + 1], sel)
    sel_hi = sel.astype(BF16)
    sel_lo = (sel - sel_hi.astype(F32)).astype(BF16)
    y = _dot(sel_hi, rows_sorted) + _dot(sel_lo, rows_sorted)
    x = x_ref[...] + mod_ref[5:6, :] * y
    if split_tiles is None:
        rest[0][...] = x
    else:
        x = _rms(x, fw_ref[...])

        @pl.when(t < split_tiles)
        def _():
            rest[0][...] = x

        @pl.when(t >= split_tiles)
        def _():
            rest[1][...] = x


def _combine(x, pg, seg_tbl, yb, tile_row, mod_l, final_w, n_prompt=None):
    n = x.shape[0]
    tm = ROUTE_TILE
    if n_prompt is None:
        split = None
        out_specs = pl.BlockSpec((tm, D_MODEL), lambda i, row, tbl: (i, 0))
        out_shape = jax.ShapeDtypeStruct((n, D_MODEL), F32)
    else:
        split = n_prompt // tm
        out_specs = [
            pl.BlockSpec((tm, D_MODEL), lambda i, row, tbl: (jnp.minimum(i, split - 1), 0)),
            pl.BlockSpec((tm, D_MODEL), lambda i, row, tbl: (jnp.maximum(i - split, 0), 0)),
        ]
        out_shape = [jax.ShapeDtypeStruct((n_prompt, D_MODEL), F32),
                     jax.ShapeDtypeStruct((n - n_prompt, D_MODEL), F32)]
    return pl.pallas_call(
        functools.partial(_combine_kernel, split_tiles=split),
        grid_spec=pltpu.PrefetchScalarGridSpec(
            num_scalar_prefetch=2,
            grid=(n // tm,),
            in_specs=[
                pl.BlockSpec((tm, D_MODEL), lambda i, row, tbl: (i, 0)),
                pl.BlockSpec((tm, LANES), lambda i, row, tbl: (i, 0)),
                pl.BlockSpec((None, 6, D_MODEL), lambda i, row, tbl: (row[i], 0, 0)),
                pl.BlockSpec((1, D_MODEL), lambda i, row, tbl: (0, 0)),
                pl.BlockSpec(memory_space=pl.ANY),
            ],
            out_specs=out_specs,
            scratch_shapes=[pltpu.VMEM((2, SLOTS, PACKED), jnp.uint32), pltpu.SemaphoreType.DMA((2,))],
        ),
        out_shape=out_shape,
        compiler_params=_cparams("arbitrary"),
        name="combine",
    )(tile_row, seg_tbl, x, pg, mod_l, final_w.reshape(1, D_MODEL), yb)


def _moe(x, tile_row, mod_l, norm_w, router_w, router_b, layer, w_gu, b_gu, w_down, b_down,
         final_w, n_prompt=None):
    n = x.shape[0]
    rw = jnp.zeros((D_MODEL, LANES), F32).at[:, :N_EXPERTS].set(router_w)
    rb = jnp.full((1, LANES), -jnp.inf, F32).at[0, :N_EXPERTS].set(router_b)
    hf, pg, pgt, cnt = _router(x, tile_row, mod_l, norm_w, rw, rb)

    te = EXPERT_TILE
    nt = n // ROUTE_TILE
    n_blocks = (n * TOP_K + nt * N_EXPERTS * (SEG_ALIGN - 1)) // te + 1 + N_EXPERTS
    cnt = cnt[:, 0, :N_EXPERTS]
    cnt = (cnt + SEG_ALIGN - 1) // SEG_ALIGN * SEG_ALIGN
    total = jnp.sum(cnt, axis=0)
    blocks_e = (total + te - 1) // te
    block_end = jnp.cumsum(blocks_e)
    pstart = (block_end - blocks_e) * te
    n_used = block_end[-1]
    first_row = pstart[None, :] + jnp.cumsum(cnt, axis=0) - cnt
    tile_off = jnp.cumsum(cnt, axis=1) - cnt
    tile_total = jnp.broadcast_to(jnp.sum(cnt, axis=1, keepdims=True), cnt.shape)
    seg_tbl = jnp.concatenate([cnt, tile_off, first_row, tile_total], axis=1)
    seg_tbl = seg_tbl.reshape(-1).astype(jnp.int32)
    pad_rows = blocks_e * te - total
    zeroed = jnp.sum(pad_rows) + (n_blocks - n_used) * te
    pad_tbl = jnp.concatenate([pstart + total, pad_rows, zeroed[None]]).astype(jnp.int32)
    block_ids = jnp.arange(n_blocks, dtype=jnp.int32)
    clamped = jnp.minimum(block_ids, n_used - 1)
    block_e = jnp.sum((clamped[:, None] >= block_end[None, :]).astype(jnp.int32), axis=1)
    block_e = jnp.minimum(block_e, N_EXPERTS - 1).astype(jnp.int32)
    n_used = n_used.astype(jnp.int32).reshape(1)

    xb = _dispatch(hf, pgt, seg_tbl, pad_tbl, n_used, n_blocks)
    yb = _experts(xb, block_e, n_used, layer, w_gu, b_gu, w_down, b_down)
    return _combine(x, pg, seg_tbl, yb, tile_row, mod_l, final_w, n_prompt)


def _tile_rows(n_prompt_tok, n_sample_seq, sample_len, tile):
    starts = np.arange(0, n_prompt_tok + n_sample_seq * sample_len, tile)
    row = np.where(starts < n_prompt_tok, 0, 1 + (starts - n_prompt_tok) // sample_len)
    return jnp.asarray(row, dtype=jnp.int32)


def kernel(x_prompt, x_sample, state_hgrn, c, c_ctx, w_mod, b_mod, norm_mix, norm_ffn, hg_w_in,
           hg_lb_logits, hg_gnorm, hg_w_out, cv_w_in, cv_w, cv_w_out, router_w, router_b,
           moe_w_gu, moe_b_gu, moe_w_down, moe_b_down, final_norm):
    bp, tp, d = x_prompt.shape
    bs, ts, _ = x_sample.shape
    depth = w_mod.shape[0]
    n_prompt = bp * tp
    n = n_prompt + bs * ts
    assert d == D_MODEL and depth == 2 and 1 + bs <= 16
    assert n_prompt % SCAN_ROWS == 0 and ts == SCAN_ROWS and tp == SCAN_PIECE

    x_p = x_prompt.reshape(n_prompt, d)
    x_s = x_sample.reshape(bs * ts, d)
    cond16 = jnp.zeros((16, d), F32).at[0].set(c_ctx).at[1:1 + bs].set(c)
    mod = _modulation(cond16, w_mod.astype(BF16), b_mod).reshape(depth, 16, 6, d)

    tile_row = _tile_rows(n_prompt, bs, ts, TOKEN_TILE)
    tile_row_route = _tile_rows(n_prompt, bs, ts, ROUTE_TILE)
    starts = np.arange(0, n, TOKEN_TILE)
    tile_width = jnp.asarray(np.where(starts < n_prompt, tp, GRID_W), dtype=jnp.int32)

    lb_all = jnp.cumsum(jax.nn.softmax(hg_lb_logits.astype(F32), axis=1), axis=1)

    hm = _norm_mod(x_p, x_s, tile_row, mod[0], norm_mix[0])
    w_in_bf = hg_w_in[0].astype(BF16)
    gn = hg_gnorm[0].reshape(-1)
    zero_state = jnp.zeros((SCAN_ROWS // tp, 2, HEADS, HEAD_DIM, HEAD_DIM), F32)
    o_p, s_new = _hgrn_scan(hm, 0, n_prompt, w_in_bf, lb_all[0, 0], lb_all[1, 0], gn, zero_state,
                            tp, True)
    o_s, _ = _hgrn_scan(hm, n_prompt, bs * ts, w_in_bf, lb_all[0, 0], lb_all[1, 0], gn,
                        state_hgrn[:, 0], ts, False)
    x = _proj_residual(x_p, x_s, o_p, o_s, hg_w_out[0].astype(BF16), tile_row, mod[0])
    x = _moe(x, tile_row_route, mod[0], norm_ffn[0], router_w[0], router_b[0],
             0, moe_w_gu, moe_b_gu, moe_w_down, moe_b_down, final_norm)

    x = _conv_mixer(x, tile_row, tile_width, mod[1], norm_mix[1], cv_w_in[0].astype(BF16), cv_w[0],
                    cv_w_out[0].astype(BF16))
    y_p, y_s = _moe(x, tile_row_route, mod[1], norm_ffn[1], router_w[1], router_b[1],
                    1, moe_w_gu, moe_b_gu, moe_w_down, moe_b_down, final_norm, n_prompt)

    y_prompt = y_p.reshape(bp, tp, d)
    y_sample = y_s.reshape(bs, ts, d)
    return (y_prompt, y_sample, s_new.reshape(bp, 1, 2, HEADS, HEAD_DIM, HEAD_DIM))
```

```python
import functools

import numpy as np
import jax
import jax.numpy as jnp
from jax import lax
from jax.experimental import pallas as pl
from jax.experimental.pallas import tpu as pltpu

F32 = jnp.float32
BF16 = jnp.bfloat16

D_MODEL = 1024
HEADS = 8
HEAD_DIM = 128
CHUNK = 32
GRID_W = 64
N_EXPERTS = 32
TOP_K = 4
D_FF = 1024
SWIGLU_LIMIT = 7.0
SWIGLU_ALPHA = 1.702
EPS = 1e-6

LANES = 128
SCAN_ROWS = 2048
SCAN_PIECE = 256
TOKEN_TILE = 512
EXPERT_TILE = 512
ROUTE_TILE = 256
SEG_ALIGN = 8
SLOTS = ROUTE_TILE * TOP_K + N_EXPERTS * SEG_ALIGN
VMEM_LIMIT = 56 * 1024 * 1024


def _cparams(*sem):
    return pltpu.CompilerParams(dimension_semantics=sem, vmem_limit_bytes=VMEM_LIMIT)


def _sigmoid(x):
    return 1.0 / (1.0 + jnp.exp(-x))


def _rms(x, w):
    return x * lax.rsqrt(jnp.mean(x * x, axis=-1, keepdims=True) + EPS) * w


def _dot(a, b):
    return jnp.dot(a, b, preferred_element_type=F32)


def _dot_nt(a, b):
    return lax.dot_general(a, b, (((1,), (1,)), ((), ())), preferred_element_type=F32)


def _dot_tn(a, b):
    return lax.dot_general(a, b, (((0,), (0,)), ((), ())), preferred_element_type=F32)


PACKED = D_MODEL // 2
_HIGH_HALF = 0xFFFF0000


def _pack_pairs(x):
    c = x.shape[1] // 2
    lo = lax.bitcast_convert_type(x[:, :c], jnp.uint32) >> 16
    hi = lax.bitcast_convert_type(x[:, c:], jnp.uint32) & jnp.uint32(_HIGH_HALF)
    return hi | lo


def _unpack_pairs(u):
    lo = lax.bitcast_convert_type(u << 16, F32)
    hi = lax.bitcast_convert_type(u & jnp.uint32(_HIGH_HALF), F32)
    return jnp.concatenate([lo, hi], axis=1).astype(BF16)


def _mod_kernel(cond_ref, w_ref, b_ref, o_ref):
    c = cond_ref[...]
    s = (c * _sigmoid(c)).astype(BF16)
    o_ref[...] = _dot(s, w_ref[...]) + b_ref[...]


def _modulation(cond16, w_mod_bf, b_mod):
    depth = w_mod_bf.shape[0]
    n_out = w_mod_bf.shape[2]
    tn = 1024
    return pl.pallas_call(
        _mod_kernel,
        grid=(depth, n_out // tn),
        in_specs=[
            pl.BlockSpec((16, D_MODEL), lambda l, j: (0, 0)),
            pl.BlockSpec((None, D_MODEL, tn), lambda l, j: (l, 0, j)),
            pl.BlockSpec((None, 1, tn), lambda l, j: (l, 0, j)),
        ],
        out_specs=pl.BlockSpec((None, 16, tn), lambda l, j: (l, 0, j)),
        out_shape=jax.ShapeDtypeStruct((depth, 16, n_out), F32),
        compiler_params=_cparams("arbitrary", "arbitrary"),
        name="modulation",
    )(cond16, w_mod_bf, b_mod.reshape(depth, 1, n_out))


def _two_group_specs(tm, prompt_tiles):
    return [
        pl.BlockSpec((tm, D_MODEL), lambda i, *_: (jnp.minimum(i, prompt_tiles - 1), 0)),
        pl.BlockSpec((tm, D_MODEL), lambda i, *_: (jnp.maximum(i - prompt_tiles, 0), 0)),
    ]


def _pick_group(a_ref, b_ref, prompt_tiles):
    return jnp.where(pl.program_id(0) < prompt_tiles, a_ref[...], b_ref[...])


def _norm_mod_kernel(row_ref, xp_ref, xs_ref, mod_ref, nw_ref, o_ref, *, prompt_tiles):
    del row_ref
    y = _rms(_pick_group(xp_ref, xs_ref, prompt_tiles), nw_ref[...])
    o_ref[...] = (y * (1.0 + mod_ref[1:2, :]) + mod_ref[0:1, :]).astype(o_ref.dtype)


def _norm_mod(x_p, x_s, tile_row, mod_l, norm_w):
    n = x_p.shape[0] + x_s.shape[0]
    tm = TOKEN_TILE
    pt = x_p.shape[0] // tm
    return pl.pallas_call(
        functools.partial(_norm_mod_kernel, prompt_tiles=pt),
        grid_spec=pltpu.PrefetchScalarGridSpec(
            num_scalar_prefetch=1,
            grid=(n // tm,),
            in_specs=_two_group_specs(tm, pt) + [
                pl.BlockSpec((None, 6, D_MODEL), lambda i, row: (row[i], 0, 0)),
                pl.BlockSpec((1, D_MODEL), lambda i, row: (0, 0)),
            ],
            out_specs=pl.BlockSpec((tm, D_MODEL), lambda i, row: (i, 0)),
        ),
        out_shape=jax.ShapeDtypeStruct((n, D_MODEL), BF16),
        compiler_params=_cparams("arbitrary"),
        name="norm_mod",
    )(tile_row, x_p, x_s, mod_l, norm_w.reshape(1, D_MODEL))


def _hgrn_kernel(hm_ref, wq_ref, wff_ref, wfb_ref, wv_ref, wg_ref, lbf_ref, lbb_ref, gn_ref,
                 s0_ref, o_ref, so_ref, q_s, vt_s, kf_s, lf_s, kb_s, lb_s, of_s, ob_s, st_s,
                 *, seq_len, zero_init):
    rows = hm_ref.shape[0]
    piece = SCAN_PIECE
    n_pieces = rows // piece
    cpp = piece // CHUNK
    per_piece_seq = seq_len == piece
    assert per_piece_seq or seq_len == rows
    hm = hm_ref[...]

    zq = _dot(hm, wq_ref[...])
    q_s[...] = zq * _sigmoid(zq)
    v = _dot(hm, wv_ref[...])
    for p in range(n_pieces):
        vt_s[p] = v[p * piece:(p + 1) * piece, :].T.astype(BF16)
    for w_ref, lbr, k_s, l_s in ((wff_ref, lbf_ref, kf_s, lf_s), (wfb_ref, lbb_ref, kb_s, lb_s)):
        lb = lbr[...]
        f = lb + (1.0 - lb) * _sigmoid(_dot(hm, w_ref[...]))
        k_s[...] = 1.0 - f
        l_s[...] = jnp.log(f)

    def load_state(j, d):
        for h in range(2):
            if zero_init:
                st_s[d, h] = jnp.zeros((HEAD_DIM, HEAD_DIM), F32)
            else:
                st_s[d, h] = s0_ref[j, d, h].T

    def store_state(j, d):
        for h in range(2):
            so_ref[j, d, h] = st_s[d, h].T

    def piece_body(i, carry):
        r = lax.broadcasted_iota(jnp.int32, (piece, piece), 0)
        c = lax.broadcasted_iota(jnp.int32, (piece, piece), 1)
        same = (r // CHUNK) == (c // CHUNK)
        row_chunk = lax.broadcasted_iota(jnp.int32, (piece, 1), 0) // CHUNK
        col_chunk = lax.broadcasted_iota(jnp.int32, (1, piece), 1) // CHUNK
        for d in range(2):
            k_s, l_s, o_s = ((kf_s, lf_s, of_s), (kb_s, lb_s, ob_s))[d]
            p = i if d == 0 else n_pieces - 1 - i
            if per_piece_seq:
                load_state(p, d)
            sl = pl.ds(pl.multiple_of(p * piece, piece), piece)
            keep = same & ((c <= r) if d == 0 else (c >= r))
            tri = jnp.where(keep, 1.0, 0.0).astype(BF16)
            lf = l_s[sl, :]
            hi = lf.astype(BF16)
            r1 = lf - hi.astype(F32)
            mid = r1.astype(BF16)
            lo = (r1 - mid.astype(F32)).astype(BF16)
            b = _dot(tri, hi) + _dot(tri, mid) + _dot(tri, lo)
            b3 = b.reshape(cpp, CHUNK, 2 * HEAD_DIM)
            edge = CHUNK - 1 if d == 0 else 0
            bl3 = b3[:, edge:edge + 1, :]
            q = q_s[sl, :]
            k = k_s[sl, :]
            qd = (q * jnp.exp(b)).astype(BF16)
            ki = (k * jnp.exp(-b)).astype(BF16)
            ks = (k.reshape(cpp, CHUNK, 2 * HEAD_DIM) * jnp.exp(bl3 - b3)).reshape(piece, 2 * HEAD_DIM)
            ks = ks.astype(BF16)
            dec = jnp.exp(bl3)
            vt = vt_s[p]
            for h in range(2):
                hs = slice(h * HEAD_DIM, (h + 1) * HEAD_DIM)
                att = jnp.where(keep, _dot_nt(qd[:, hs], ki[:, hs]), 0.0).astype(BF16)
                vt_h = vt[hs, :]
                v_exp = jnp.concatenate(
                    [jnp.where(col_chunk == ci, vt_h, jnp.zeros_like(vt_h)) for ci in range(cpp)], axis=0)
                incr = _dot(v_exp, ks[:, hs])
                st = st_s[d, h]
                before = [None] * cpp
                for ci in (range(cpp) if d == 0 else range(cpp - 1, -1, -1)):
                    before[ci] = st.astype(BF16)
                    st = st * dec[ci, :, hs] + incr[ci * HEAD_DIM:(ci + 1) * HEAD_DIM, :]
                st_s[d, h] = st
                qd_h = qd[:, hs]
                q_exp = [jnp.where(row_chunk == ci, qd_h, jnp.zeros_like(qd_h)) for ci in range(cpp)]
                lhs = jnp.concatenate([att] + q_exp, axis=1)
                rhs_t = jnp.concatenate([vt_h] + before, axis=1)
                o_s[sl, hs] = _dot_nt(lhs, rhs_t)
            if per_piece_seq:
                store_state(p, d)
        return carry

    if not per_piece_seq:
        load_state(0, 0)
        load_state(0, 1)
    lax.fori_loop(0, n_pieces, piece_body, 0)
    if not per_piece_seq:
        store_state(0, 0)
        store_state(0, 1)

    o = of_s[...] + ob_s[...]
    gn = gn_ref[...]
    o = jnp.concatenate(
        [_rms(o[:, h * HEAD_DIM:(h + 1) * HEAD_DIM], gn[:, h * HEAD_DIM:(h + 1) * HEAD_DIM])
         for h in range(2)], axis=1)
    o_ref[...] = (o * _sigmoid(_dot(hm, wg_ref[...]))).astype(o_ref.dtype)


def _hgrn_scan(hm, row0, n_rows, w_in_bf, lb_f, lb_b, gnorm, s0, seq_len, zero_init):
    n_seq_total = n_rows // seq_len
    rows = SCAN_ROWS
    seq_per_step = rows // seq_len
    blk0 = row0 // rows
    pair = 2 * HEAD_DIM
    n_pairs = HEADS // 2

    def w_spec(seg):
        return pl.BlockSpec((D_MODEL, pair), lambda sb, hp: (0, seg * n_pairs + hp))

    vec_spec = pl.BlockSpec((1, pair), lambda sb, hp: (0, hp))
    st_block = (seq_per_step, 2, 2, HEAD_DIM, HEAD_DIM)
    st_spec = pl.BlockSpec(st_block, lambda sb, hp: (sb, 0, hp, 0, 0))
    s0_spec = pl.BlockSpec(st_block, (lambda sb, hp: (0, 0, hp, 0, 0)) if zero_init
                           else (lambda sb, hp: (sb, 0, hp, 0, 0)))
    scratch = [
        pltpu.VMEM((rows, pair), F32),
        pltpu.VMEM((rows // SCAN_PIECE, pair, SCAN_PIECE), BF16),
        pltpu.VMEM((rows, pair), F32),
        pltpu.VMEM((rows, pair), F32),
        pltpu.VMEM((rows, pair), F32),
        pltpu.VMEM((rows, pair), F32),
        pltpu.VMEM((rows, pair), F32),
        pltpu.VMEM((rows, pair), F32),
        pltpu.VMEM((2, 2, HEAD_DIM, HEAD_DIM), F32),
    ]
    in_specs = [
        pl.BlockSpec((rows, D_MODEL), lambda sb, hp: (blk0 + sb, 0)),
        w_spec(0), w_spec(1), w_spec(2), w_spec(3), w_spec(4),
        vec_spec, vec_spec, vec_spec,
        s0_spec,
    ]
    args = [hm, w_in_bf, w_in_bf, w_in_bf, w_in_bf, w_in_bf,
            lb_f.reshape(1, D_MODEL), lb_b.reshape(1, D_MODEL), gnorm.reshape(1, D_MODEL), s0]
    return pl.pallas_call(
        functools.partial(_hgrn_kernel, seq_len=seq_len, zero_init=zero_init),
        grid=(n_rows // rows, n_pairs),
        in_specs=in_specs,
        out_specs=[
            pl.BlockSpec((rows, pair), lambda sb, hp: (sb, hp)),
            st_spec,
        ],
        out_shape=[
            jax.ShapeDtypeStruct((n_rows, D_MODEL), BF16),
            jax.ShapeDtypeStruct((n_seq_total, 2, HEADS, HEAD_DIM, HEAD_DIM), F32),
        ],
        scratch_shapes=scratch,
        compiler_params=_cparams("arbitrary", "arbitrary"),
        name="hgrn_scan",
    )(*args)


def _proj_res_kernel(row_ref, xp_ref, xs_ref, op_ref, os_ref, w_ref, mod_ref, xo_ref, *, prompt_tiles):
    del row_ref
    x = _pick_group(xp_ref, xs_ref, prompt_tiles)
    o = _pick_group(op_ref, os_ref, prompt_tiles)
    xo_ref[...] = x + mod_ref[2:3, :] * _dot(o, w_ref[...])


def _proj_residual(x_p, x_s, o_prompt, o_sample, w_out_bf, tile_row, mod_l):
    n = x_p.shape[0] + x_s.shape[0]
    tm = TOKEN_TILE
    pt = o_prompt.shape[0] // tm
    return pl.pallas_call(
        functools.partial(_proj_res_kernel, prompt_tiles=pt),
        grid_spec=pltpu.PrefetchScalarGridSpec(
            num_scalar_prefetch=1,
            grid=(n // tm,),
            in_specs=_two_group_specs(tm, pt) + _two_group_specs(tm, pt) + [
                pl.BlockSpec((D_MODEL, D_MODEL), lambda i, row: (0, 0)),
                pl.BlockSpec((None, 6, D_MODEL), lambda i, row: (row[i], 0, 0)),
            ],
            out_specs=pl.BlockSpec((tm, D_MODEL), lambda i, row: (i, 0)),
        ),
        out_shape=jax.ShapeDtypeStruct((n, D_MODEL), F32),
        compiler_params=_cparams("arbitrary"),
        name="proj_residual",
    )(tile_row, x_p, x_s, o_prompt, o_sample, w_out_bf, mod_l)


def _conv_kernel(row_ref, width_ref, x_ref, mod_ref, nw_ref, win_ref, cw_ref, wout_ref, xo_ref):
    del row_ref
    i = pl.program_id(0)
    x = x_ref[...]
    hm = (_rms(x, nw_ref[...]) * (1.0 + mod_ref[1:2, :]) + mod_ref[0:1, :]).astype(BF16)
    z = _dot(hm, win_ref[...])
    bg = z[:, :D_MODEL]
    u = z[:, D_MODEL:2 * D_MODEL] * z[:, 2 * D_MODEL:]
    tm = x.shape[0]
    pos = lax.broadcasted_iota(jnp.int32, (tm, 1), 0) & (width_ref[i] - 1)
    prev = jnp.where(pos == 0, 0.0, pltpu.roll(u, 1, axis=0))
    nxt = jnp.where(pos == width_ref[i] - 1, 0.0, pltpu.roll(u, tm - 1, axis=0))
    v = cw_ref[0:1, :] * prev + cw_ref[1:2, :] * u + cw_ref[2:3, :] * nxt
    y = _dot((bg * v).astype(BF16), wout_ref[...])
    xo_ref[...] = x + mod_ref[2:3, :] * y


def _conv_mixer(x, tile_row, tile_width, mod_l, norm_w, w_in_bf, conv_w, w_out_bf):
    n = x.shape[0]
    tm = TOKEN_TILE
    return pl.pallas_call(
        _conv_kernel,
        grid_spec=pltpu.PrefetchScalarGridSpec(
            num_scalar_prefetch=2,
            grid=(n // tm,),
            in_specs=[
                pl.BlockSpec((tm, D_MODEL), lambda i, row, w: (i, 0)),
                pl.BlockSpec((None, 6, D_MODEL), lambda i, row, w: (row[i], 0, 0)),
                pl.BlockSpec((1, D_MODEL), lambda i, row, w: (0, 0)),
                pl.BlockSpec((D_MODEL, 3 * D_MODEL), lambda i, row, w: (0, 0)),
                pl.BlockSpec((3, D_MODEL), lambda i, row, w: (0, 0)),
                pl.BlockSpec((D_MODEL, D_MODEL), lambda i, row, w: (0, 0)),
            ],
            out_specs=pl.BlockSpec((tm, D_MODEL), lambda i, row, w: (i, 0)),
        ),
        out_shape=jax.ShapeDtypeStruct((n, D_MODEL), F32),
        compiler_params=_cparams("arbitrary"),
        name="conv_mixer",
    )(tile_row, tile_width, x, mod_l, norm_w.reshape(1, D_MODEL), w_in_bf, conv_w, w_out_bf)


def _router_kernel(row_ref, x_ref, mod_ref, nw_ref, rwh_ref, rwl_ref, rb_ref,
                   hf_ref, pg_ref, pgt_ref, cnt_ref):
    del row_ref
    hf = _rms(x_ref[...], nw_ref[...]) * (1.0 + mod_ref[4:5, :]) + mod_ref[3:4, :]
    hf_hi = hf.astype(BF16)
    hf_ref[...] = hf_hi
    hf_lo = (hf - hf_hi.astype(F32)).astype(BF16)
    logits = (_dot(hf_hi, rwh_ref[...]) + (_dot(hf_hi, rwl_ref[...]) + _dot(hf_lo, rwh_ref[...]))
              + rb_ref[...])
    tm = hf.shape[0]
    lane = lax.broadcasted_iota(jnp.int32, (tm, LANES), 1)
    lane_f = lane.astype(F32)
    work = logits
    vals, hots = [], []
    for _ in range(TOP_K):
        m = jnp.max(work, axis=-1, keepdims=True)
        idx_f = jnp.min(jnp.where(work == m, lane_f, float(LANES)), axis=-1, keepdims=True)
        hot = lane_f == idx_f
        vals.append(m)
        hots.append(hot)
        work = jnp.where(hot, -jnp.inf, work)
    ex = [jnp.exp(v - vals[0]) for v in vals]
    inv = 1.0 / (ex[0] + ex[1] + ex[2] + ex[3])

    multi = jnp.where(hots[0] | hots[1] | hots[2] | hots[3], 1.0, 0.0)
    tr = lax.broadcasted_iota(jnp.int32, (tm, tm), 0)
    tc = lax.broadcasted_iota(jnp.int32, (tm, tm), 1)
    earlier = jnp.where(tc < tr, 1.0, 0.0).astype(BF16)
    before = _dot(earlier, multi.astype(BF16))
    cnt = jnp.sum(multi, axis=0, keepdims=True)
    er = lax.broadcasted_iota(jnp.int32, (LANES, LANES), 0)
    ec = lax.broadcasted_iota(jnp.int32, (LANES, LANES), 1)
    lower = jnp.where(er < ec, 1.0, 0.0).astype(BF16)
    cnt_al = jnp.floor((cnt + (SEG_ALIGN - 1)) * (1.0 / SEG_ALIGN)) * SEG_ALIGN
    estart = _dot(jnp.broadcast_to(cnt_al, (8, LANES)).astype(BF16), lower)[0:1, :]
    pos_all = before + estart

    pg = jnp.zeros((tm, LANES), F32)
    for k in range(TOP_K):
        pos = jnp.sum(jnp.where(hots[k], pos_all, 0.0), axis=-1, keepdims=True)
        pg = jnp.where(lane == k, pos, pg)
        pg = jnp.where(lane == TOP_K + k, ex[k] * inv, pg)
    pg_ref[...] = pg
    pgt_ref[...] = pg.T[0:8, :]
    cnt_ref[...] = cnt.astype(jnp.int32)


def _router(x, tile_row, mod_l, norm_w, router_w_pad, router_b_pad):
    n = x.shape[0]
    tm = ROUTE_TILE
    nt = n // tm
    rw_hi = router_w_pad.astype(BF16)
    rw_lo = (router_w_pad - rw_hi.astype(F32)).astype(BF16)
    return pl.pallas_call(
        _router_kernel,
        grid_spec=pltpu.PrefetchScalarGridSpec(
            num_scalar_prefetch=1,
            grid=(nt,),
            in_specs=[
                pl.BlockSpec((tm, D_MODEL), lambda i, row: (i, 0)),
                pl.BlockSpec((None, 6, D_MODEL), lambda i, row: (row[i], 0, 0)),
                pl.BlockSpec((1, D_MODEL), lambda i, row: (0, 0)),
                pl.BlockSpec((D_MODEL, LANES), lambda i, row: (0, 0)),
                pl.BlockSpec((D_MODEL, LANES), lambda i, row: (0, 0)),
                pl.BlockSpec((1, LANES), lambda i, row: (0, 0)),
            ],
            out_specs=[
                pl.BlockSpec((tm, D_MODEL), lambda i, row: (i, 0)),
                pl.BlockSpec((tm, LANES), lambda i, row: (i, 0)),
                pl.BlockSpec((None, 8, tm), lambda i, row: (i, 0, 0)),
                pl.BlockSpec((None, 1, LANES), lambda i, row: (i, 0, 0)),
            ],
        ),
        out_shape=[
            jax.ShapeDtypeStruct((n, D_MODEL), BF16),
            jax.ShapeDtypeStruct((n, LANES), F32),
            jax.ShapeDtypeStruct((nt, 8, tm), F32),
            jax.ShapeDtypeStruct((nt, 1, LANES), jnp.int32),
        ],
        compiler_params=_cparams("arbitrary"),
        name="router",
    )(tile_row, x, mod_l, norm_w.reshape(1, D_MODEL), rw_hi, rw_lo, router_b_pad)


SEG_STRIDE = 4 * N_EXPERTS


def _segment_table(tbl_ref, t, e):
    base = t * SEG_STRIDE
    al = lambda v: pl.multiple_of(v, SEG_ALIGN)
    return (al(tbl_ref[base + e]), al(tbl_ref[base + N_EXPERTS + e]),
            al(tbl_ref[base + 2 * N_EXPERTS + e]))


def _tile_rows_total(tbl_ref, t):
    return pl.multiple_of(tbl_ref[t * SEG_STRIDE + 3 * N_EXPERTS], SEG_ALIGN)


def _dispatch_kernel(tbl_ref, pad_ref, nu_ref, hf_ref, pgt_ref, xb_ref, stage_s, zero_s, sem, zsem,
                     *, n_blocks):
    t = pl.program_id(0)
    nt = pl.num_programs(0)
    slot = t % 2
    half = zero_s.shape[0]

    def zero_rows(start, rows):
        start = pl.multiple_of(start, SEG_ALIGN)
        rows = pl.multiple_of(rows, SEG_ALIGN)

        @pl.when(rows > 0)
        def _():
            pltpu.make_async_copy(zero_s.at[pl.ds(0, rows), :], xb_ref.at[pl.ds(start, rows), :],
                                  zsem).start()

    @pl.when(t == 0)
    def _():
        zero_s[...] = jnp.zeros_like(zero_s)

        def pad_body(e, carry):
            start = pad_ref[e]
            rows = pad_ref[N_EXPERTS + e]
            first = jnp.minimum(rows, half)
            zero_rows(start, first)
            zero_rows(start + half, rows - first)
            return carry

        lax.fori_loop(0, N_EXPERTS, pad_body, 0)

        def tail_body(b, carry):
            for part in range(EXPERT_TILE // half):
                zero_rows(b * EXPERT_TILE + part * half, half)
            return carry

        lax.fori_loop(nu_ref[0], n_blocks, tail_body, 0)

    def wait_tile(tile, sl):
        rows = _tile_rows_total(tbl_ref, tile)
        pltpu.make_async_copy(stage_s.at[sl, pl.ds(0, rows), :], xb_ref.at[pl.ds(0, rows), :],
                              sem.at[sl]).wait()

    @pl.when(t >= 2)
    def _():
        wait_tile(t - 2, slot)

    s_iota = lax.broadcasted_iota(jnp.int32, (SLOTS, ROUTE_TILE), 0).astype(F32)
    pgt = pgt_ref[...]
    hit = s_iota == pgt[0:1, :]
    for k in range(1, TOP_K):
        hit = hit | (s_iota == pgt[k:k + 1, :])
    perm = jnp.where(hit, 1.0, 0.0).astype(BF16)
    stage_s[slot] = _pack_pairs(_dot(perm, hf_ref[...]))

    def seg_body(e, carry):
        rows, src, dst = _segment_table(tbl_ref, t, e)

        @pl.when(rows > 0)
        def _():
            pltpu.make_async_copy(stage_s.at[slot, pl.ds(src, rows), :],
                                  xb_ref.at[pl.ds(dst, rows), :], sem.at[slot]).start()
        return carry

    lax.fori_loop(0, N_EXPERTS, seg_body, 0)

    @pl.when(t == nt - 1)
    def _():
        wait_tile(t, slot)

        @pl.when(nt > 1)
        def _():
            wait_tile(t - 1, 1 - slot)

        zeroed = pl.multiple_of(pad_ref[2 * N_EXPERTS], SEG_ALIGN)

        @pl.when(zeroed > 0)
        def _():
            pltpu.make_async_copy(xb_ref.at[pl.ds(0, zeroed), :], xb_ref.at[pl.ds(0, zeroed), :],
                                  zsem).wait()


def _dispatch(hf, pgt, seg_tbl, pad_tbl, n_used, n_blocks):
    n = hf.shape[0]
    tm = ROUTE_TILE
    return pl.pallas_call(
        functools.partial(_dispatch_kernel, n_blocks=n_blocks),
        grid_spec=pltpu.PrefetchScalarGridSpec(
            num_scalar_prefetch=3,
            grid=(n // tm,),
            in_specs=[
                pl.BlockSpec((tm, D_MODEL), lambda i, a, b, c: (i, 0)),
                pl.BlockSpec((None, 8, tm), lambda i, a, b, c: (i, 0, 0)),
            ],
            out_specs=pl.BlockSpec(memory_space=pl.ANY),
            scratch_shapes=[
                pltpu.VMEM((2, SLOTS, PACKED), jnp.uint32),
                pltpu.VMEM((EXPERT_TILE // 2, PACKED), jnp.uint32),
                pltpu.SemaphoreType.DMA((2,)),
                pltpu.SemaphoreType.DMA,
            ],
        ),
        out_shape=jax.ShapeDtypeStruct((n_blocks * EXPERT_TILE, PACKED), jnp.uint32),
        compiler_params=_cparams("arbitrary"),
        name="dispatch",
    )(seg_tbl, pad_tbl, n_used, hf, pgt)


def _expert_kernel(be_ref, nu_ref, nxt_ref, slot_ref, xb_ref, wgu_hbm, bgu_ref, wd_hbm, bd_ref, yb_ref,
                   wgu_f, wd_f, wgu_s, wd_s, sem, *, layer):
    i = pl.program_id(0)

    def fetch(e, slot):
        return (pltpu.make_async_copy(wgu_hbm.at[layer, e], wgu_f.at[slot], sem.at[slot]),
                pltpu.make_async_copy(wd_hbm.at[layer, e], wd_f.at[slot], sem.at[slot]))

    @pl.when(i < nu_ref[0])
    def _():
        e = be_ref[i]
        slot = slot_ref[i]
        first_block = jnp.logical_or(i == 0, e != be_ref[jnp.maximum(i - 1, 0)])

        @pl.when(first_block)
        def _():
            @pl.when(i == 0)
            def _():
                for cp in fetch(e, slot):
                    cp.start()

            for cp in fetch(e, slot):
                cp.wait()
            wgu_s[...] = wgu_f[slot].astype(BF16)
            wd_s[...] = wd_f[slot].astype(BF16)

            @pl.when(nxt_ref[i] >= 0)
            def _():
                for cp in fetch(nxt_ref[i], 1 - slot):
                    cp.start()

        gu = _dot(_unpack_pairs(xb_ref[...]), wgu_s[...]) + bgu_ref[...]
        gate = jnp.minimum(gu[:, :D_FF], SWIGLU_LIMIT)
        up = jnp.clip(gu[:, D_FF:], -SWIGLU_LIMIT, SWIGLU_LIMIT)
        act = (up + 1.0) * gate * _sigmoid(SWIGLU_ALPHA * gate)
        y = _dot(act.astype(BF16), wd_s[...]) + bd_ref[...]
        yb_ref[...] = _pack_pairs(y.astype(BF16).astype(F32))

    @pl.when(i >= nu_ref[0])
    def _():
        yb_ref[...] = jnp.zeros_like(yb_ref)


def _experts(xb, block_e, n_used, next_e, slot, layer, w_gu, b_gu, w_down, b_down):
    te = EXPERT_TILE
    n_blocks = xb.shape[0] // te
    depth = w_gu.shape[0]

    def x_map(i, be, nu, nx, sl):
        return (jnp.maximum(jnp.minimum(i, nu[0] - 1), 0), 0)

    def bias_spec(width):
        return pl.BlockSpec((None, None, 1, width), lambda i, be, nu, nx, sl: (layer, be[i], 0, 0))

    return pl.pallas_call(
        functools.partial(_expert_kernel, layer=layer),
        grid_spec=pltpu.PrefetchScalarGridSpec(
            num_scalar_prefetch=4,
            grid=(n_blocks,),
            in_specs=[
                pl.BlockSpec((te, PACKED), x_map),
                pl.BlockSpec(memory_space=pl.ANY),
                bias_spec(2 * D_FF),
                pl.BlockSpec(memory_space=pl.ANY),
                bias_spec(D_MODEL),
            ],
            out_specs=pl.BlockSpec((te, PACKED), lambda i, be, nu, nx, sl: (i, 0)),
            scratch_shapes=[
                pltpu.VMEM((2, D_MODEL, 2 * D_FF), F32),
                pltpu.VMEM((2, D_FF, D_MODEL), F32),
                pltpu.VMEM((D_MODEL, 2 * D_FF), BF16),
                pltpu.VMEM((D_FF, D_MODEL), BF16),
                pltpu.SemaphoreType.DMA((2,)),
            ],
        ),
        out_shape=jax.ShapeDtypeStruct(xb.shape, jnp.uint32),
        compiler_params=_cparams("arbitrary"),
        name="experts",
    )(block_e, n_used, next_e, slot, xb, w_gu, b_gu.reshape(depth, N_EXPERTS, 1, 2 * D_FF), w_down,
      b_down.reshape(depth, N_EXPERTS, 1, D_MODEL))


def _combine_kernel(row_ref, tbl_ref, x_ref, pg_ref, mod_ref, fw_ref, yb_ref, *rest, split_tiles):
    del row_ref
    stage_s, sem = rest[-2:]
    t = pl.program_id(0)
    nt = pl.num_programs(0)
    slot = t % 2

    def fetch(tile, into):
        def seg_body(e, carry):
            rows, dst, src = _segment_table(tbl_ref, tile, e)

            @pl.when(rows > 0)
            def _():
                pltpu.make_async_copy(yb_ref.at[pl.ds(src, rows), :],
                                      stage_s.at[into, pl.ds(dst, rows), :], sem.at[into]).start()
            return carry

        lax.fori_loop(0, N_EXPERTS, seg_body, 0)

    @pl.when(t == 0)
    def _():
        stage_s[...] = jnp.zeros_like(stage_s)
        fetch(0, 0)

    @pl.when(t + 1 < nt)
    def _():
        fetch(t + 1, 1 - slot)

    fetched = _tile_rows_total(tbl_ref, t)
    pltpu.make_async_copy(yb_ref.at[pl.ds(0, fetched), :], stage_s.at[slot, pl.ds(0, fetched), :],
                          sem.at[slot]).wait()
    rows_sorted = _unpack_pairs(stage_s[slot])
    pg = pg_ref[...]
    s_iota = lax.broadcasted_iota(jnp.int32, (ROUTE_TILE, SLOTS), 1).astype(F32)
    sel = jnp.zeros((ROUTE_TILE, SLOTS), F32)
    for k in range(TOP_K):
        sel = jnp.where(s_iota == pg[:, k:k + 1], pg[:, TOP_K + k:TOP_K + k + 1], sel)
    sel_hi = sel.astype(BF16)
    sel_lo = (sel - sel_hi.astype(F32)).astype(BF16)
    y = _dot(sel_hi, rows_sorted) + _dot(sel_lo, rows_sorted)
    x = x_ref[...] + mod_ref[5:6, :] * y
    if split_tiles is None:
        rest[0][...] = x
    else:
        x = _rms(x, fw_ref[...])

        @pl.when(t < split_tiles)
        def _():
            rest[0][...] = x

        @pl.when(t >= split_tiles)
        def _():
            rest[1][...] = x


def _combine(x, pg, seg_tbl, yb, tile_row, mod_l, final_w, n_prompt=None):
    n = x.shape[0]
    tm = ROUTE_TILE
    if n_prompt is None:
        split = None
        out_specs = pl.BlockSpec((tm, D_MODEL), lambda i, row, tbl: (i, 0))
        out_shape = jax.ShapeDtypeStruct((n, D_MODEL), F32)
    else:
        split = n_prompt // tm
        out_specs = [
            pl.BlockSpec((tm, D_MODEL), lambda i, row, tbl: (jnp.minimum(i, split - 1), 0)),
            pl.BlockSpec((tm, D_MODEL), lambda i, row, tbl: (jnp.maximum(i - split, 0), 0)),
        ]
        out_shape = [jax.ShapeDtypeStruct((n_prompt, D_MODEL), F32),
                     jax.ShapeDtypeStruct((n - n_prompt, D_MODEL), F32)]
    return pl.pallas_call(
        functools.partial(_combine_kernel, split_tiles=split),
        grid_spec=pltpu.PrefetchScalarGridSpec(
            num_scalar_prefetch=2,
            grid=(n // tm,),
            in_specs=[
                pl.BlockSpec((tm, D_MODEL), lambda i, row, tbl: (i, 0)),
                pl.BlockSpec((tm, LANES), lambda i, row, tbl: (i, 0)),
                pl.BlockSpec((None, 6, D_MODEL), lambda i, row, tbl: (row[i], 0, 0)),
                pl.BlockSpec((1, D_MODEL), lambda i, row, tbl: (0, 0)),
                pl.BlockSpec(memory_space=pl.ANY),
            ],
            out_specs=out_specs,
            scratch_shapes=[pltpu.VMEM((2, SLOTS, PACKED), jnp.uint32), pltpu.SemaphoreType.DMA((2,))],
        ),
        out_shape=out_shape,
        compiler_params=_cparams("arbitrary"),
        name="combine",
    )(tile_row, seg_tbl, x, pg, mod_l, final_w.reshape(1, D_MODEL), yb)


def _moe(x, tile_row, mod_l, norm_w, router_w, router_b, layer, w_gu, b_gu, w_down, b_down,
         final_w, n_prompt=None):
    n = x.shape[0]
    rw = jnp.zeros((D_MODEL, LANES), F32).at[:, :N_EXPERTS].set(router_w)
    rb = jnp.full((1, LANES), -jnp.inf, F32).at[0, :N_EXPERTS].set(router_b)
    hf, pg, pgt, cnt = _router(x, tile_row, mod_l, norm_w, rw, rb)

    te = EXPERT_TILE
    nt = n // ROUTE_TILE
    n_blocks = (n * TOP_K + nt * N_EXPERTS * (SEG_ALIGN - 1)) // te + 1 + N_EXPERTS
    cnt = cnt[:, 0, :N_EXPERTS]
    cnt = (cnt + SEG_ALIGN - 1) // SEG_ALIGN * SEG_ALIGN
    total = jnp.sum(cnt, axis=0)
    blocks_e = (total + te - 1) // te
    block_end = jnp.cumsum(blocks_e)
    pstart = (block_end - blocks_e) * te
    n_used = block_end[-1]
    first_row = pstart[None, :] + jnp.cumsum(cnt, axis=0) - cnt
    tile_off = jnp.cumsum(cnt, axis=1) - cnt
    tile_total = jnp.broadcast_to(jnp.sum(cnt, axis=1, keepdims=True), cnt.shape)
    seg_tbl = jnp.concatenate([cnt, tile_off, first_row, tile_total], axis=1)
    seg_tbl = seg_tbl.reshape(-1).astype(jnp.int32)
    pad_rows = blocks_e * te - total
    zeroed = jnp.sum(pad_rows) + (n_blocks - n_used) * te
    pad_tbl = jnp.concatenate([pstart + total, pad_rows, zeroed[None]]).astype(jnp.int32)
    block_ids = jnp.arange(n_blocks, dtype=jnp.int32)
    clamped = jnp.minimum(block_ids, n_used - 1)
    block_e = jnp.sum((clamped[:, None] >= block_end[None, :]).astype(jnp.int32), axis=1)
    block_e = jnp.minimum(block_e, N_EXPERTS - 1).astype(jnp.int32)
    n_used = n_used.astype(jnp.int32).reshape(1)
    owns = blocks_e > 0
    ids = jnp.arange(N_EXPERTS, dtype=jnp.int32)
    later = (ids[None, :] > ids[:, None]) & owns[None, :]
    next_owner = jnp.min(jnp.where(later, ids[None, :], N_EXPERTS), axis=1)
    next_owner = jnp.where(next_owner < N_EXPERTS, next_owner, -1).astype(jnp.int32)
    slot_e = ((jnp.cumsum(owns.astype(jnp.int32)) - 1) % 2).astype(jnp.int32)

    xb = _dispatch(hf, pgt, seg_tbl, pad_tbl, n_used, n_blocks)
    yb = _experts(xb, block_e, n_used, next_owner[block_e], slot_e[block_e], layer,
                  w_gu, b_gu, w_down, b_down)
    return _combine(x, pg, seg_tbl, yb, tile_row, mod_l, final_w, n_prompt)


def _tile_rows(n_prompt_tok, n_sample_seq, sample_len, tile):
    starts = np.arange(0, n_prompt_tok + n_sample_seq * sample_len, tile)
    row = np.where(starts < n_prompt_tok, 0, 1 + (starts - n_prompt_tok) // sample_len)
    return jnp.asarray(row, dtype=jnp.int32)


def kernel(x_prompt, x_sample, state_hgrn, c, c_ctx, w_mod, b_mod, norm_mix, norm_ffn, hg_w_in,
           hg_lb_logits, hg_gnorm, hg_w_out, cv_w_in, cv_w, cv_w_out, router_w, router_b,
           moe_w_gu, moe_b_gu, moe_w_down, moe_b_down, final_norm):
    bp, tp, d = x_prompt.shape
    bs, ts, _ = x_sample.shape
    depth = w_mod.shape[0]
    n_prompt = bp * tp
    n = n_prompt + bs * ts
    assert d == D_MODEL and depth == 2 and 1 + bs <= 16
    assert n_prompt % SCAN_ROWS == 0 and ts == SCAN_ROWS and tp == SCAN_PIECE

    x_p = x_prompt.reshape(n_prompt, d)
    x_s = x_sample.reshape(bs * ts, d)
    cond16 = jnp.zeros((16, d), F32).at[0].set(c_ctx).at[1:1 + bs].set(c)
    mod = _modulation(cond16, w_mod.astype(BF16), b_mod).reshape(depth, 16, 6, d)

    tile_row = _tile_rows(n_prompt, bs, ts, TOKEN_TILE)
    tile_row_route = _tile_rows(n_prompt, bs, ts, ROUTE_TILE)
    starts = np.arange(0, n, TOKEN_TILE)
    tile_width = jnp.asarray(np.where(starts < n_prompt, tp, GRID_W), dtype=jnp.int32)

    lb_all = jnp.cumsum(jax.nn.softmax(hg_lb_logits.astype(F32), axis=1), axis=1)

    hm = _norm_mod(x_p, x_s, tile_row, mod[0], norm_mix[0])
    w_in_bf = hg_w_in[0].astype(BF16)
    gn = hg_gnorm[0].reshape(-1)
    zero_state = jnp.zeros((SCAN_ROWS // tp, 2, HEADS, HEAD_DIM, HEAD_DIM), F32)
    o_p, s_new = _hgrn_scan(hm, 0, n_prompt, w_in_bf, lb_all[0, 0], lb_all[1, 0], gn, zero_state,
                            tp, True)
    o_s, _ = _hgrn_scan(hm, n_prompt, bs * ts, w_in_bf, lb_all[0, 0], lb_all[1, 0], gn,
                        state_hgrn[:, 0], ts, False)
    x = _proj_residual(x_p, x_s, o_p, o_s, hg_w_out[0].astype(BF16), tile_row, mod[0])
    x = _moe(x, tile_row_route, mod[0], norm_ffn[0], router_w[0], router_b[0],
             0, moe_w_gu, moe_b_gu, moe_w_down, moe_b_down, final_norm)

    x = _conv_mixer(x, tile_row, tile_width, mod[1], norm_mix[1], cv_w_in[0].astype(BF16), cv_w[0],
                    cv_w_out[0].astype(BF16))
    y_p, y_s = _moe(x, tile_row_route, mod[1], norm_ffn[1], router_w[1], router_b[1],
                    1, moe_w_gu, moe_b_gu, moe_w_down, moe_b_down, final_norm, n_prompt)

    y_prompt = y_p.reshape(bp, tp, d)
    y_sample = y_s.reshape(bs, ts, d)
    return (y_prompt, y_sample, s_new.reshape(bp, 1, 2, HEADS, HEAD_DIM, HEAD_DIM))
```

```python
import functools

import numpy as np
import jax
import jax.numpy as jnp
from jax import lax
from jax.experimental import pallas as pl
from jax.experimental.pallas import tpu as pltpu

F32 = jnp.float32
BF16 = jnp.bfloat16

D_MODEL = 1024
HEADS = 8
HEAD_DIM = 128
CHUNK = 32
GRID_W = 64
N_EXPERTS = 32
TOP_K = 4
D_FF = 1024
SWIGLU_LIMIT = 7.0
SWIGLU_ALPHA = 1.702
EPS = 1e-6

LANES = 128
SCAN_ROWS = 2048
SCAN_PIECE = 256
TOKEN_TILE = 512
EXPERT_TILE = 512
ROUTE_TILE = 256
SEG_ALIGN = 8
SLOTS = ROUTE_TILE * TOP_K + N_EXPERTS * SEG_ALIGN
VMEM_LIMIT = 56 * 1024 * 1024


def _cparams(*sem):
    return pltpu.CompilerParams(dimension_semantics=sem, vmem_limit_bytes=VMEM_LIMIT)


def _sigmoid(x):
    return 1.0 / (1.0 + jnp.exp(-x))


def _rms(x, w):
    return x * lax.rsqrt(jnp.mean(x * x, axis=-1, keepdims=True) + EPS) * w


def _dot(a, b):
    return jnp.dot(a, b, preferred_element_type=F32)


def _dot_nt(a, b):
    return lax.dot_general(a, b, (((1,), (1,)), ((), ())), preferred_element_type=F32)


def _dot_tn(a, b):
    return lax.dot_general(a, b, (((0,), (0,)), ((), ())), preferred_element_type=F32)


PACKED = D_MODEL // 2
_HIGH_HALF = 0xFFFF0000


def _pack_pairs(x):
    c = x.shape[1] // 2
    lo = lax.bitcast_convert_type(x[:, :c], jnp.uint32) >> 16
    hi = lax.bitcast_convert_type(x[:, c:], jnp.uint32) & jnp.uint32(_HIGH_HALF)
    return hi | lo


def _unpack_pairs(u):
    lo = lax.bitcast_convert_type(u << 16, F32)
    hi = lax.bitcast_convert_type(u & jnp.uint32(_HIGH_HALF), F32)
    return jnp.concatenate([lo, hi], axis=1).astype(BF16)


def _mod_kernel(cond_ref, w_ref, b_ref, o_ref):
    c = cond_ref[...]
    s = (c * _sigmoid(c)).astype(BF16)
    o_ref[...] = _dot(s, w_ref[...].astype(BF16)) + b_ref[...]


def _modulation(cond16, w_mod, b_mod):
    depth = w_mod.shape[0]
    n_out = w_mod.shape[2]
    tn = 1024
    return pl.pallas_call(
        _mod_kernel,
        grid=(depth, n_out // tn),
        in_specs=[
            pl.BlockSpec((16, D_MODEL), lambda l, j: (0, 0)),
            pl.BlockSpec((None, D_MODEL, tn), lambda l, j: (l, 0, j)),
            pl.BlockSpec((None, 1, tn), lambda l, j: (l, 0, j)),
        ],
        out_specs=pl.BlockSpec((None, 16, tn), lambda l, j: (l, 0, j)),
        out_shape=jax.ShapeDtypeStruct((depth, 16, n_out), F32),
        compiler_params=_cparams("arbitrary", "arbitrary"),
        name="modulation",
    )(cond16, w_mod, b_mod.reshape(depth, 1, n_out))


def _two_group_specs(tm, prompt_tiles):
    return [
        pl.BlockSpec((tm, D_MODEL), lambda i, *_: (jnp.minimum(i, prompt_tiles - 1), 0)),
        pl.BlockSpec((tm, D_MODEL), lambda i, *_: (jnp.maximum(i - prompt_tiles, 0), 0)),
    ]


def _pick_group(a_ref, b_ref, prompt_tiles):
    return jnp.where(pl.program_id(0) < prompt_tiles, a_ref[...], b_ref[...])


def _hgrn_kernel(x_ref, mod_ref, nw_ref, wq_ref, wff_ref, wfb_ref, wv_ref, wg_ref, lbf_ref, lbb_ref,
                 gn_ref, s0_ref, o_ref, so_ref, hm_s, q_s, vt_s, kf_s, lf_s, kb_s, lb_s, of_s, ob_s, st_s,
                 *, seq_len, zero_init):
    rows = x_ref.shape[0]
    piece = SCAN_PIECE
    n_pieces = rows // piece
    cpp = piece // CHUNK
    per_piece_seq = seq_len == piece
    assert per_piece_seq or seq_len == rows

    @pl.when(pl.program_id(1) == 0)
    def _():
        y = _rms(x_ref[...], nw_ref[...])
        hm_s[...] = (y * (1.0 + mod_ref[1:2, :]) + mod_ref[0:1, :]).astype(BF16)

    hm = hm_s[...]

    zq = _dot(hm, wq_ref[...])
    q_s[...] = zq * _sigmoid(zq)
    v = _dot(hm, wv_ref[...])
    for p in range(n_pieces):
        vt_s[p] = v[p * piece:(p + 1) * piece, :].T.astype(BF16)
    for w_ref, lbr, k_s, l_s in ((wff_ref, lbf_ref, kf_s, lf_s), (wfb_ref, lbb_ref, kb_s, lb_s)):
        lb = lbr[...]
        f = lb + (1.0 - lb) * _sigmoid(_dot(hm, w_ref[...]))
        k_s[...] = 1.0 - f
        l_s[...] = jnp.log(f)

    def load_state(j, d):
        for h in range(2):
            if zero_init:
                st_s[d, h] = jnp.zeros((HEAD_DIM, HEAD_DIM), F32)
            else:
                st_s[d, h] = s0_ref[j, d, h].T

    def store_state(j, d):
        for h in range(2):
            so_ref[j, d, h] = st_s[d, h].T

    def piece_body(i, carry):
        r = lax.broadcasted_iota(jnp.int32, (piece, piece), 0)
        c = lax.broadcasted_iota(jnp.int32, (piece, piece), 1)
        same = (r // CHUNK) == (c // CHUNK)
        row_chunk = lax.broadcasted_iota(jnp.int32, (piece, 1), 0) // CHUNK
        col_chunk = lax.broadcasted_iota(jnp.int32, (1, piece), 1) // CHUNK
        for d in range(2):
            k_s, l_s, o_s = ((kf_s, lf_s, of_s), (kb_s, lb_s, ob_s))[d]
            p = i if d == 0 else n_pieces - 1 - i
            if per_piece_seq:
                load_state(p, d)
            sl = pl.ds(pl.multiple_of(p * piece, piece), piece)
            keep = same & ((c <= r) if d == 0 else (c >= r))
            tri = jnp.where(keep, 1.0, 0.0).astype(BF16)
            lf = l_s[sl, :]
            hi = lf.astype(BF16)
            mid = (lf - hi.astype(F32)).astype(BF16)
            b = _dot(tri, hi) + _dot(tri, mid)
            b3 = b.reshape(cpp, CHUNK, 2 * HEAD_DIM)
            edge = CHUNK - 1 if d == 0 else 0
            bl3 = b3[:, edge:edge + 1, :]
            q = q_s[sl, :]
            k = k_s[sl, :]
            qd = (q * jnp.exp(b)).astype(BF16)
            ki = (k * jnp.exp(-b)).astype(BF16)
            ks = (k.reshape(cpp, CHUNK, 2 * HEAD_DIM) * jnp.exp(bl3 - b3)).reshape(piece, 2 * HEAD_DIM)
            ks = ks.astype(BF16)
            dec = jnp.exp(bl3)
            vt = vt_s[p]
            for h in range(2):
                hs = slice(h * HEAD_DIM, (h + 1) * HEAD_DIM)
                att = jnp.where(keep, _dot_nt(qd[:, hs], ki[:, hs]), 0.0).astype(BF16)
                vt_h = vt[hs, :]
                v_exp = jnp.concatenate(
                    [jnp.where(col_chunk == ci, vt_h, jnp.zeros_like(vt_h)) for ci in range(cpp)], axis=0)
                incr = _dot(v_exp, ks[:, hs])
                st = st_s[d, h]
                before = [None] * cpp
                for ci in (range(cpp) if d == 0 else range(cpp - 1, -1, -1)):
                    before[ci] = st.astype(BF16)
                    st = st * dec[ci, :, hs] + incr[ci * HEAD_DIM:(ci + 1) * HEAD_DIM, :]
                st_s[d, h] = st
                qd_h = qd[:, hs]
                q_exp = [jnp.where(row_chunk == ci, qd_h, jnp.zeros_like(qd_h)) for ci in range(cpp)]
                lhs = jnp.concatenate([att] + q_exp, axis=1)
                rhs_t = jnp.concatenate([vt_h] + before, axis=1)
                o_s[sl, hs] = _dot_nt(lhs, rhs_t)
            if per_piece_seq:
                store_state(p, d)
        return carry

    if not per_piece_seq:
        load_state(0, 0)
        load_state(0, 1)
    lax.fori_loop(0, n_pieces, piece_body, 0)
    if not per_piece_seq:
        store_state(0, 0)
        store_state(0, 1)

    o = of_s[...] + ob_s[...]
    gn = gn_ref[...]
    o = jnp.concatenate(
        [_rms(o[:, h * HEAD_DIM:(h + 1) * HEAD_DIM], gn[:, h * HEAD_DIM:(h + 1) * HEAD_DIM])
         for h in range(2)], axis=1)
    o_ref[...] = (o * _sigmoid(_dot(hm, wg_ref[...]))).astype(o_ref.dtype)


def _hgrn_scan(x, mod_l, mod_row0, mod_row_step, norm_w, w_in_bf, lb_f, lb_b, gnorm, s0, seq_len,
               zero_init):
    n_rows = x.shape[0]
    n_seq_total = n_rows // seq_len
    rows = SCAN_ROWS
    seq_per_step = rows // seq_len
    pair = 2 * HEAD_DIM
    n_pairs = HEADS // 2

    def w_spec(seg):
        return pl.BlockSpec((D_MODEL, pair), lambda sb, hp: (0, seg * n_pairs + hp))

    vec_spec = pl.BlockSpec((1, pair), lambda sb, hp: (0, hp))
    st_block = (seq_per_step, 2, 2, HEAD_DIM, HEAD_DIM)
    st_spec = pl.BlockSpec(st_block, lambda sb, hp: (sb, 0, hp, 0, 0))
    s0_spec = pl.BlockSpec(st_block, (lambda sb, hp: (0, 0, hp, 0, 0)) if zero_init
                           else (lambda sb, hp: (sb, 0, hp, 0, 0)))
    scratch = [
        pltpu.VMEM((rows, D_MODEL), BF16),
        pltpu.VMEM((rows, pair), F32),
        pltpu.VMEM((rows // SCAN_PIECE, pair, SCAN_PIECE), BF16),
        pltpu.VMEM((rows, pair), F32),
        pltpu.VMEM((rows, pair), F32),
        pltpu.VMEM((rows, pair), F32),
        pltpu.VMEM((rows, pair), F32),
        pltpu.VMEM((rows, pair), F32),
        pltpu.VMEM((rows, pair), F32),
        pltpu.VMEM((2, 2, HEAD_DIM, HEAD_DIM), F32),
    ]
    in_specs = [
        pl.BlockSpec((rows, D_MODEL), lambda sb, hp: (sb, 0), pipeline_mode=pl.Buffered(1)),
        pl.BlockSpec((None, 6, D_MODEL), lambda sb, hp: (mod_row0 + sb * mod_row_step, 0, 0)),
        pl.BlockSpec((1, D_MODEL), lambda sb, hp: (0, 0)),
        w_spec(0), w_spec(1), w_spec(2), w_spec(3), w_spec(4),
        vec_spec, vec_spec, vec_spec,
        s0_spec,
    ]
    args = [x, mod_l, norm_w.reshape(1, D_MODEL), w_in_bf, w_in_bf, w_in_bf, w_in_bf, w_in_bf,
            lb_f.reshape(1, D_MODEL), lb_b.reshape(1, D_MODEL), gnorm.reshape(1, D_MODEL), s0]
    return pl.pallas_call(
        functools.partial(_hgrn_kernel, seq_len=seq_len, zero_init=zero_init),
        grid=(n_rows // rows, n_pairs),
        in_specs=in_specs,
        out_specs=[
            pl.BlockSpec((rows, pair), lambda sb, hp: (sb, hp)),
            st_spec,
        ],
        out_shape=[
            jax.ShapeDtypeStruct((n_rows, D_MODEL), BF16),
            jax.ShapeDtypeStruct((n_seq_total, 2, HEADS, HEAD_DIM, HEAD_DIM), F32),
        ],
        scratch_shapes=scratch,
        compiler_params=_cparams("arbitrary", "arbitrary"),
        name="hgrn_scan",
    )(*args)


def _proj_res_kernel(row_ref, xp_ref, xs_ref, op_ref, os_ref, w_ref, mod_ref, xo_ref, *, prompt_tiles):
    del row_ref
    x = _pick_group(xp_ref, xs_ref, prompt_tiles)
    o = _pick_group(op_ref, os_ref, prompt_tiles)
    xo_ref[...] = x + mod_ref[2:3, :] * _dot(o, w_ref[...])


def _proj_residual(x_p, x_s, o_prompt, o_sample, w_out_bf, tile_row, mod_l):
    n = x_p.shape[0] + x_s.shape[0]
    tm = TOKEN_TILE
    pt = o_prompt.shape[0] // tm
    return pl.pallas_call(
        functools.partial(_proj_res_kernel, prompt_tiles=pt),
        grid_spec=pltpu.PrefetchScalarGridSpec(
            num_scalar_prefetch=1,
            grid=(n // tm,),
            in_specs=_two_group_specs(tm, pt) + _two_group_specs(tm, pt) + [
                pl.BlockSpec((D_MODEL, D_MODEL), lambda i, row: (0, 0)),
                pl.BlockSpec((None, 6, D_MODEL), lambda i, row: (row[i], 0, 0)),
            ],
            out_specs=pl.BlockSpec((tm, D_MODEL), lambda i, row: (i, 0)),
        ),
        out_shape=jax.ShapeDtypeStruct((n, D_MODEL), F32),
        compiler_params=_cparams("arbitrary"),
        name="proj_residual",
    )(tile_row, x_p, x_s, o_prompt, o_sample, w_out_bf, mod_l)


def _conv_kernel(row_ref, width_ref, x_ref, mod_ref, nw_ref, win_ref, cw_ref, wout_ref, xo_ref):
    del row_ref
    i = pl.program_id(0)
    x = x_ref[...]
    hm = (_rms(x, nw_ref[...]) * (1.0 + mod_ref[1:2, :]) + mod_ref[0:1, :]).astype(BF16)
    z = _dot(hm, win_ref[...])
    bg = z[:, :D_MODEL]
    u = z[:, D_MODEL:2 * D_MODEL] * z[:, 2 * D_MODEL:]
    tm = x.shape[0]
    pos = lax.broadcasted_iota(jnp.int32, (tm, 1), 0) & (width_ref[i] - 1)
    prev = jnp.where(pos == 0, 0.0, pltpu.roll(u, 1, axis=0))
    nxt = jnp.where(pos == width_ref[i] - 1, 0.0, pltpu.roll(u, tm - 1, axis=0))
    v = cw_ref[0:1, :] * prev + cw_ref[1:2, :] * u + cw_ref[2:3, :] * nxt
    y = _dot((bg * v).astype(BF16), wout_ref[...])
    xo_ref[...] = x + mod_ref[2:3, :] * y


def _conv_mixer(x, tile_row, tile_width, mod_l, norm_w, w_in_bf, conv_w, w_out_bf):
    n = x.shape[0]
    tm = TOKEN_TILE
    return pl.pallas_call(
        _conv_kernel,
        grid_spec=pltpu.PrefetchScalarGridSpec(
            num_scalar_prefetch=2,
            grid=(n // tm,),
            in_specs=[
                pl.BlockSpec((tm, D_MODEL), lambda i, row, w: (i, 0)),
                pl.BlockSpec((None, 6, D_MODEL), lambda i, row, w: (row[i], 0, 0)),
                pl.BlockSpec((1, D_MODEL), lambda i, row, w: (0, 0)),
                pl.BlockSpec((D_MODEL, 3 * D_MODEL), lambda i, row, w: (0, 0)),
                pl.BlockSpec((3, D_MODEL), lambda i, row, w: (0, 0)),
                pl.BlockSpec((D_MODEL, D_MODEL), lambda i, row, w: (0, 0)),
            ],
            out_specs=pl.BlockSpec((tm, D_MODEL), lambda i, row, w: (i, 0)),
        ),
        out_shape=jax.ShapeDtypeStruct((n, D_MODEL), F32),
        compiler_params=_cparams("arbitrary"),
        name="conv_mixer",
    )(tile_row, tile_width, x, mod_l, norm_w.reshape(1, D_MODEL), w_in_bf, conv_w, w_out_bf)


def _router_kernel(row_ref, x_ref, mod_ref, nw_ref, rwh_ref, rwl_ref, rb_ref,
                   hf_ref, pg_ref, pgt_ref, cnt_ref):
    del row_ref
    hf = _rms(x_ref[...], nw_ref[...]) * (1.0 + mod_ref[4:5, :]) + mod_ref[3:4, :]
    hf_hi = hf.astype(BF16)
    hf_ref[...] = hf_hi
    hf_lo = (hf - hf_hi.astype(F32)).astype(BF16)
    logits = (_dot(hf_hi, rwh_ref[...]) + (_dot(hf_hi, rwl_ref[...]) + _dot(hf_lo, rwh_ref[...]))
              + rb_ref[...])
    tm = hf.shape[0]
    lane = lax.broadcasted_iota(jnp.int32, (tm, LANES), 1)
    lane_f = lane.astype(F32)
    work = logits
    vals, hots = [], []
    for _ in range(TOP_K):
        m = jnp.max(work, axis=-1, keepdims=True)
        idx_f = jnp.min(jnp.where(work == m, lane_f, float(LANES)), axis=-1, keepdims=True)
        hot = lane_f == idx_f
        vals.append(m)
        hots.append(hot)
        work = jnp.where(hot, -jnp.inf, work)
    ex = [jnp.exp(v - vals[0]) for v in vals]
    inv = 1.0 / (ex[0] + ex[1] + ex[2] + ex[3])

    multi = jnp.where(hots[0] | hots[1] | hots[2] | hots[3], 1.0, 0.0)
    tr = lax.broadcasted_iota(jnp.int32, (tm, tm), 0)
    tc = lax.broadcasted_iota(jnp.int32, (tm, tm), 1)
    earlier = jnp.where(tc < tr, 1.0, 0.0).astype(BF16)
    before = _dot(earlier, multi.astype(BF16))
    cnt = jnp.sum(multi, axis=0, keepdims=True)
    er = lax.broadcasted_iota(jnp.int32, (LANES, LANES), 0)
    ec = lax.broadcasted_iota(jnp.int32, (LANES, LANES), 1)
    lower = jnp.where(er < ec, 1.0, 0.0).astype(BF16)
    cnt_al = jnp.floor((cnt + (SEG_ALIGN - 1)) * (1.0 / SEG_ALIGN)) * SEG_ALIGN
    estart = _dot(jnp.broadcast_to(cnt_al, (8, LANES)).astype(BF16), lower)[0:1, :]
    pos_all = before + estart

    pg = jnp.zeros((tm, LANES), F32)
    for k in range(TOP_K):
        pos = jnp.sum(jnp.where(hots[k], pos_all, 0.0), axis=-1, keepdims=True)
        pg = jnp.where(lane == k, pos, pg)
        pg = jnp.where(lane == TOP_K + k, ex[k] * inv, pg)
    pg_ref[...] = pg
    pgt_ref[...] = pg.T[0:8, :]
    cnt_ref[...] = cnt.astype(jnp.int32)


def _router(x, tile_row, mod_l, norm_w, router_w_pad, router_b_pad):
    n = x.shape[0]
    tm = ROUTE_TILE
    nt = n // tm
    rw_hi = router_w_pad.astype(BF16)
    rw_lo = (router_w_pad - rw_hi.astype(F32)).astype(BF16)
    return pl.pallas_call(
        _router_kernel,
        grid_spec=pltpu.PrefetchScalarGridSpec(
            num_scalar_prefetch=1,
            grid=(nt,),
            in_specs=[
                pl.BlockSpec((tm, D_MODEL), lambda i, row: (i, 0)),
                pl.BlockSpec((None, 6, D_MODEL), lambda i, row: (row[i], 0, 0)),
                pl.BlockSpec((1, D_MODEL), lambda i, row: (0, 0)),
                pl.BlockSpec((D_MODEL, LANES), lambda i, row: (0, 0)),
                pl.BlockSpec((D_MODEL, LANES), lambda i, row: (0, 0)),
                pl.BlockSpec((1, LANES), lambda i, row: (0, 0)),
            ],
            out_specs=[
                pl.BlockSpec((tm, D_MODEL), lambda i, row: (i, 0)),
                pl.BlockSpec((tm, LANES), lambda i, row: (i, 0)),
                pl.BlockSpec((None, 8, tm), lambda i, row: (i, 0, 0)),
                pl.BlockSpec((None, 1, LANES), lambda i, row: (i, 0, 0)),
            ],
        ),
        out_shape=[
            jax.ShapeDtypeStruct((n, D_MODEL), BF16),
            jax.ShapeDtypeStruct((n, LANES), F32),
            jax.ShapeDtypeStruct((nt, 8, tm), F32),
            jax.ShapeDtypeStruct((nt, 1, LANES), jnp.int32),
        ],
        compiler_params=_cparams("arbitrary"),
        name="router",
    )(tile_row, x, mod_l, norm_w.reshape(1, D_MODEL), rw_hi, rw_lo, router_b_pad)


SEG_STRIDE = 4 * N_EXPERTS


def _segment_table(tbl_ref, t, e):
    base = t * SEG_STRIDE
    al = lambda v: pl.multiple_of(v, SEG_ALIGN)
    return (al(tbl_ref[base + e]), al(tbl_ref[base + N_EXPERTS + e]),
            al(tbl_ref[base + 2 * N_EXPERTS + e]))


def _tile_rows_total(tbl_ref, t):
    return pl.multiple_of(tbl_ref[t * SEG_STRIDE + 3 * N_EXPERTS], SEG_ALIGN)


def _dispatch_kernel(tbl_ref, pad_ref, nu_ref, hf_ref, pgt_ref, xb_ref, stage_s, zero_s, sem, zsem,
                     *, n_blocks):
    t = pl.program_id(0)
    nt = pl.num_programs(0)
    slot = t % 2
    half = zero_s.shape[0]

    def zero_rows(start, rows):
        start = pl.multiple_of(start, SEG_ALIGN)
        rows = pl.multiple_of(rows, SEG_ALIGN)

        @pl.when(rows > 0)
        def _():
            pltpu.make_async_copy(zero_s.at[pl.ds(0, rows), :], xb_ref.at[pl.ds(start, rows), :],
                                  zsem).start()

    @pl.when(t == 0)
    def _():
        zero_s[...] = jnp.zeros_like(zero_s)

        def pad_body(e, carry):
            start = pad_ref[e]
            rows = pad_ref[N_EXPERTS + e]
            first = jnp.minimum(rows, half)
            zero_rows(start, first)
            zero_rows(start + half, rows - first)
            return carry

        lax.fori_loop(0, N_EXPERTS, pad_body, 0)

        def tail_body(b, carry):
            for part in range(EXPERT_TILE // half):
                zero_rows(b * EXPERT_TILE + part * half, half)
            return carry

        lax.fori_loop(nu_ref[0], n_blocks, tail_body, 0)

    def wait_tile(tile, sl):
        rows = _tile_rows_total(tbl_ref, tile)
        pltpu.make_async_copy(stage_s.at[sl, pl.ds(0, rows), :], xb_ref.at[pl.ds(0, rows), :],
                              sem.at[sl]).wait()

    @pl.when(t >= 2)
    def _():
        wait_tile(t - 2, slot)

    s_iota = lax.broadcasted_iota(jnp.int32, (SLOTS, ROUTE_TILE), 0).astype(F32)
    pgt = pgt_ref[...]
    hit = s_iota == pgt[0:1, :]
    for k in range(1, TOP_K):
        hit = hit | (s_iota == pgt[k:k + 1, :])
    perm = jnp.where(hit, 1.0, 0.0).astype(BF16)
    stage_s[slot] = _pack_pairs(_dot(perm, hf_ref[...]))

    def seg_body(e, carry):
        rows, src, dst = _segment_table(tbl_ref, t, e)

        @pl.when(rows > 0)
        def _():
            pltpu.make_async_copy(stage_s.at[slot, pl.ds(src, rows), :],
                                  xb_ref.at[pl.ds(dst, rows), :], sem.at[slot]).start()
        return carry

    lax.fori_loop(0, N_EXPERTS, seg_body, 0)

    @pl.when(t == nt - 1)
    def _():
        wait_tile(t, slot)

        @pl.when(nt > 1)
        def _():
            wait_tile(t - 1, 1 - slot)

        zeroed = pl.multiple_of(pad_ref[2 * N_EXPERTS], SEG_ALIGN)

        @pl.when(zeroed > 0)
        def _():
            pltpu.make_async_copy(xb_ref.at[pl.ds(0, zeroed), :], xb_ref.at[pl.ds(0, zeroed), :],
                                  zsem).wait()


def _dispatch(hf, pgt, seg_tbl, pad_tbl, n_used, n_blocks):
    n = hf.shape[0]
    tm = ROUTE_TILE
    return pl.pallas_call(
        functools.partial(_dispatch_kernel, n_blocks=n_blocks),
        grid_spec=pltpu.PrefetchScalarGridSpec(
            num_scalar_prefetch=3,
            grid=(n // tm,),
            in_specs=[
                pl.BlockSpec((tm, D_MODEL), lambda i, a, b, c: (i, 0)),
                pl.BlockSpec((None, 8, tm), lambda i, a, b, c: (i, 0, 0)),
            ],
            out_specs=pl.BlockSpec(memory_space=pl.ANY),
            scratch_shapes=[
                pltpu.VMEM((2, SLOTS, PACKED), jnp.uint32),
                pltpu.VMEM((EXPERT_TILE // 2, PACKED), jnp.uint32),
                pltpu.SemaphoreType.DMA((2,)),
                pltpu.SemaphoreType.DMA,
            ],
        ),
        out_shape=jax.ShapeDtypeStruct((n_blocks * EXPERT_TILE, PACKED), jnp.uint32),
        compiler_params=_cparams("arbitrary"),
        name="dispatch",
    )(seg_tbl, pad_tbl, n_used, hf, pgt)


def _expert_kernel(be_ref, nu_ref, nxt_ref, slot_ref, xb_ref, wgu_hbm, bgu_ref, wd_hbm, bd_ref, yb_ref,
                   wgu_f, wd_f, wgu_s, wd_s, sem, *, layer):
    i = pl.program_id(0)

    def fetch(e, slot):
        return (pltpu.make_async_copy(wgu_hbm.at[layer, e], wgu_f.at[slot], sem.at[slot]),
                pltpu.make_async_copy(wd_hbm.at[layer, e], wd_f.at[slot], sem.at[slot]))

    @pl.when(i < nu_ref[0])
    def _():
        e = be_ref[i]
        slot = slot_ref[i]
        first_block = jnp.logical_or(i == 0, e != be_ref[jnp.maximum(i - 1, 0)])

        @pl.when(first_block)
        def _():
            @pl.when(i == 0)
            def _():
                for cp in fetch(e, slot):
                    cp.start()

            for cp in fetch(e, slot):
                cp.wait()
            wgu_s[...] = wgu_f[slot].astype(BF16)
            wd_s[...] = wd_f[slot].astype(BF16)

            @pl.when(nxt_ref[i] >= 0)
            def _():
                for cp in fetch(nxt_ref[i], 1 - slot):
                    cp.start()

        gu = _dot(_unpack_pairs(xb_ref[...]), wgu_s[...]) + bgu_ref[...]
        gate = jnp.minimum(gu[:, :D_FF], SWIGLU_LIMIT)
        up = jnp.clip(gu[:, D_FF:], -SWIGLU_LIMIT, SWIGLU_LIMIT)
        act = (up + 1.0) * gate * _sigmoid(SWIGLU_ALPHA * gate)
        y = _dot(act.astype(BF16), wd_s[...]) + bd_ref[...]
        yb_ref[...] = _pack_pairs(y.astype(BF16).astype(F32))

    @pl.when(i >= nu_ref[0])
    def _():
        yb_ref[...] = jnp.zeros_like(yb_ref)


def _experts(xb, block_e, n_used, next_e, slot, layer, w_gu, b_gu, w_down, b_down):
    te = EXPERT_TILE
    n_blocks = xb.shape[0] // te
    depth = w_gu.shape[0]

    def x_map(i, be, nu, nx, sl):
        return (jnp.maximum(jnp.minimum(i, nu[0] - 1), 0), 0)

    def bias_spec(width):
        return pl.BlockSpec((None, None, 1, width), lambda i, be, nu, nx, sl: (layer, be[i], 0, 0))

    return pl.pallas_call(
        functools.partial(_expert_kernel, layer=layer),
        grid_spec=pltpu.PrefetchScalarGridSpec(
            num_scalar_prefetch=4,
            grid=(n_blocks,),
            in_specs=[
                pl.BlockSpec((te, PACKED), x_map),
                pl.BlockSpec(memory_space=pl.ANY),
                bias_spec(2 * D_FF),
                pl.BlockSpec(memory_space=pl.ANY),
                bias_spec(D_MODEL),
            ],
            out_specs=pl.BlockSpec((te, PACKED), lambda i, be, nu, nx, sl: (i, 0)),
            scratch_shapes=[
                pltpu.VMEM((2, D_MODEL, 2 * D_FF), F32),
                pltpu.VMEM((2, D_FF, D_MODEL), F32),
                pltpu.VMEM((D_MODEL, 2 * D_FF), BF16),
                pltpu.VMEM((D_FF, D_MODEL), BF16),
                pltpu.SemaphoreType.DMA((2,)),
            ],
        ),
        out_shape=jax.ShapeDtypeStruct(xb.shape, jnp.uint32),
        compiler_params=_cparams("arbitrary"),
        name="experts",
    )(block_e, n_used, next_e, slot, xb, w_gu, b_gu.reshape(depth, N_EXPERTS, 1, 2 * D_FF), w_down,
      b_down.reshape(depth, N_EXPERTS, 1, D_MODEL))


def _combine_kernel(row_ref, tbl_ref, x_ref, pg_ref, mod_ref, fw_ref, yb_ref, *rest, split_tiles):
    del row_ref
    stage_s, sem = rest[-2:]
    t = pl.program_id(0)
    nt = pl.num_programs(0)
    slot = t % 2

    def fetch(tile, into):
        def seg_body(e, carry):
            rows, dst, src = _segment_table(tbl_ref, tile, e)

            @pl.when(rows > 0)
            def _():
                pltpu.make_async_copy(yb_ref.at[pl.ds(src, rows), :],
                                      stage_s.at[into, pl.ds(dst, rows), :], sem.at[into]).start()
            return carry

        lax.fori_loop(0, N_EXPERTS, seg_body, 0)

    @pl.when(t == 0)
    def _():
        stage_s[...] = jnp.zeros_like(stage_s)
        fetch(0, 0)

    @pl.when(t + 1 < nt)
    def _():
        fetch(t + 1, 1 - slot)

    fetched = _tile_rows_total(tbl_ref, t)
    pltpu.make_async_copy(yb_ref.at[pl.ds(0, fetched), :], stage_s.at[slot, pl.ds(0, fetched), :],
                          sem.at[slot]).wait()
    rows_sorted = _unpack_pairs(stage_s[slot])
    pg = pg_ref[...]
    s_iota = lax.broadcasted_iota(jnp.int32, (ROUTE_TILE, SLOTS), 1).astype(F32)
    sel = jnp.zeros((ROUTE_TILE, SLOTS), F32)
    for k in range(TOP_K):
        sel = jnp.where(s_iota == pg[:, k:k + 1], pg[:, TOP_K + k:TOP_K + k + 1], sel)
    sel_hi = sel.astype(BF16)
    sel_lo = (sel - sel_hi.astype(F32)).astype(BF16)
    y = _dot(sel_hi, rows_sorted) + _dot(sel_lo, rows_sorted)
    x = x_ref[...] + mod_ref[5:6, :] * y
    if split_tiles is None:
        rest[0][...] = x
    else:
        x = _rms(x, fw_ref[...])

        @pl.when(t < split_tiles)
        def _():
            rest[0][...] = x

        @pl.when(t >= split_tiles)
        def _():
            rest[1][...] = x


def _combine(x, pg, seg_tbl, yb, tile_row, mod_l, final_w, n_prompt=None):
    n = x.shape[0]
    tm = ROUTE_TILE
    if n_prompt is None:
        split = None
        out_specs = pl.BlockSpec((tm, D_MODEL), lambda i, row, tbl: (i, 0))
        out_shape = jax.ShapeDtypeStruct((n, D_MODEL), F32)
    else:
        split = n_prompt // tm
        out_specs = [
            pl.BlockSpec((tm, D_MODEL), lambda i, row, tbl: (jnp.minimum(i, split - 1), 0)),
            pl.BlockSpec((tm, D_MODEL), lambda i, row, tbl: (jnp.maximum(i - split, 0), 0)),
        ]
        out_shape = [jax.ShapeDtypeStruct((n_prompt, D_MODEL), F32),
                     jax.ShapeDtypeStruct((n - n_prompt, D_MODEL), F32)]
    return pl.pallas_call(
        functools.partial(_combine_kernel, split_tiles=split),
        grid_spec=pltpu.PrefetchScalarGridSpec(
            num_scalar_prefetch=2,
            grid=(n // tm,),
            in_specs=[
                pl.BlockSpec((tm, D_MODEL), lambda i, row, tbl: (i, 0)),
                pl.BlockSpec((tm, LANES), lambda i, row, tbl: (i, 0)),
                pl.BlockSpec((None, 6, D_MODEL), lambda i, row, tbl: (row[i], 0, 0)),
                pl.BlockSpec((1, D_MODEL), lambda i, row, tbl: (0, 0)),
                pl.BlockSpec(memory_space=pl.ANY),
            ],
            out_specs=out_specs,
            scratch_shapes=[pltpu.VMEM((2, SLOTS, PACKED), jnp.uint32), pltpu.SemaphoreType.DMA((2,))],
        ),
        out_shape=out_shape,
        compiler_params=_cparams("arbitrary"),
        name="combine",
    )(tile_row, seg_tbl, x, pg, mod_l, final_w.reshape(1, D_MODEL), yb)


def _moe(x, tile_row, mod_l, norm_w, router_w, router_b, layer, w_gu, b_gu, w_down, b_down,
         final_w, n_prompt=None):
    n = x.shape[0]
    rw = jnp.zeros((D_MODEL, LANES), F32).at[:, :N_EXPERTS].set(router_w)
    rb = jnp.full((1, LANES), -jnp.inf, F32).at[0, :N_EXPERTS].set(router_b)
    hf, pg, pgt, cnt = _router(x, tile_row, mod_l, norm_w, rw, rb)

    te = EXPERT_TILE
    nt = n // ROUTE_TILE
    n_blocks = (n * TOP_K + nt * N_EXPERTS * (SEG_ALIGN - 1)) // te + 1 + N_EXPERTS
    cnt = cnt[:, 0, :N_EXPERTS]
    cnt = (cnt + SEG_ALIGN - 1) // SEG_ALIGN * SEG_ALIGN
    total = jnp.sum(cnt, axis=0)
    blocks_e = (total + te - 1) // te
    block_end = jnp.cumsum(blocks_e)
    pstart = (block_end - blocks_e) * te
    n_used = block_end[-1]
    first_row = pstart[None, :] + jnp.cumsum(cnt, axis=0) - cnt
    tile_off = jnp.cumsum(cnt, axis=1) - cnt
    tile_total = jnp.broadcast_to(jnp.sum(cnt, axis=1, keepdims=True), cnt.shape)
    seg_tbl = jnp.concatenate([cnt, tile_off, first_row, tile_total], axis=1)
    seg_tbl = seg_tbl.reshape(-1).astype(jnp.int32)
    pad_rows = blocks_e * te - total
    zeroed = jnp.sum(pad_rows) + (n_blocks - n_used) * te
    pad_tbl = jnp.concatenate([pstart + total, pad_rows, zeroed[None]]).astype(jnp.int32)
    block_ids = jnp.arange(n_blocks, dtype=jnp.int32)
    clamped = jnp.minimum(block_ids, n_used - 1)
    block_e = jnp.sum((clamped[:, None] >= block_end[None, :]).astype(jnp.int32), axis=1)
    block_e = jnp.minimum(block_e, N_EXPERTS - 1).astype(jnp.int32)
    n_used = n_used.astype(jnp.int32).reshape(1)
    owns = (blocks_e > 0)[None, :]
    ids = jnp.arange(N_EXPERTS, dtype=jnp.int32)[None, :]
    mine = block_e[:, None]
    next_e = jnp.min(jnp.where((ids > mine) & owns, ids, N_EXPERTS), axis=1)
    next_e = jnp.where(next_e < N_EXPERTS, next_e, -1).astype(jnp.int32)
    slot = ((jnp.sum(((ids <= mine) & owns).astype(jnp.int32), axis=1) + 1) % 2).astype(jnp.int32)

    xb = _dispatch(hf, pgt, seg_tbl, pad_tbl, n_used, n_blocks)
    yb = _experts(xb, block_e, n_used, next_e, slot, layer, w_gu, b_gu, w_down, b_down)
    return _combine(x, pg, seg_tbl, yb, tile_row, mod_l, final_w, n_prompt)


def _tile_rows(n_prompt_tok, n_sample_seq, sample_len, tile):
    starts = np.arange(0, n_prompt_tok + n_sample_seq * sample_len, tile)
    row = np.where(starts < n_prompt_tok, 0, 1 + (starts - n_prompt_tok) // sample_len)
    return jnp.asarray(row, dtype=jnp.int32)


def kernel(x_prompt, x_sample, state_hgrn, c, c_ctx, w_mod, b_mod, norm_mix, norm_ffn, hg_w_in,
           hg_lb_logits, hg_gnorm, hg_w_out, cv_w_in, cv_w, cv_w_out, router_w, router_b,
           moe_w_gu, moe_b_gu, moe_w_down, moe_b_down, final_norm):
    bp, tp, d = x_prompt.shape
    bs, ts, _ = x_sample.shape
    depth = w_mod.shape[0]
    n_prompt = bp * tp
    n = n_prompt + bs * ts
    assert d == D_MODEL and depth == 2 and 1 + bs <= 16
    assert n_prompt % SCAN_ROWS == 0 and ts == SCAN_ROWS and tp == SCAN_PIECE

    x_p = x_prompt.reshape(n_prompt, d)
    x_s = x_sample.reshape(bs * ts, d)
    cond16 = jnp.zeros((16, d), F32).at[0].set(c_ctx).at[1:1 + bs].set(c)
    mod = _modulation(cond16, w_mod, b_mod).reshape(depth, 16, 6, d)

    tile_row = _tile_rows(n_prompt, bs, ts, TOKEN_TILE)
    tile_row_route = _tile_rows(n_prompt, bs, ts, ROUTE_TILE)
    starts = np.arange(0, n, TOKEN_TILE)
    tile_width = jnp.asarray(np.where(starts < n_prompt, tp, GRID_W), dtype=jnp.int32)

    lb_all = jnp.cumsum(jax.nn.softmax(hg_lb_logits.astype(F32), axis=1), axis=1)

    w_in_bf = hg_w_in[0].astype(BF16)
    gn = hg_gnorm[0].reshape(-1)
    zero_state = jnp.zeros((SCAN_ROWS // tp, 2, HEADS, HEAD_DIM, HEAD_DIM), F32)
    o_p, s_new = _hgrn_scan(x_p, mod[0], 0, 0, norm_mix[0], w_in_bf, lb_all[0, 0], lb_all[1, 0], gn,
                            zero_state, tp, True)
    o_s, _ = _hgrn_scan(x_s, mod[0], 1, 1, norm_mix[0], w_in_bf, lb_all[0, 0], lb_all[1, 0], gn,
                        state_hgrn[:, 0], ts, False)
    x = _proj_residual(x_p, x_s, o_p, o_s, hg_w_out[0].astype(BF16), tile_row, mod[0])
    x = _moe(x, tile_row_route, mod[0], norm_ffn[0], router_w[0], router_b[0],
             0, moe_w_gu, moe_b_gu, moe_w_down, moe_b_down, final_norm)

    x = _conv_mixer(x, tile_row, tile_width, mod[1], norm_mix[1], cv_w_in[0].astype(BF16), cv_w[0],
                    cv_w_out[0].astype(BF16))
    y_p, y_s = _moe(x, tile_row_route, mod[1], norm_ffn[1], router_w[1], router_b[1],
                    1, moe_w_gu, moe_b_gu, moe_w_down, moe_b_down, final_norm, n_prompt)

    y_prompt = y_p.reshape(bp, tp, d)
    y_sample = y_s.reshape(bs, ts, d)
    return (y_prompt, y_sample, s_new.reshape(bp, 1, 2, HEADS, HEAD_DIM, HEAD_DIM))
```

```python
import functools

import numpy as np
import jax
import jax.numpy as jnp
from jax import lax
from jax.experimental import pallas as pl
from jax.experimental.pallas import tpu as pltpu

F32 = jnp.float32
BF16 = jnp.bfloat16

D_MODEL = 1024
HEADS = 8
HEAD_DIM = 128
CHUNK = 64
GRID_W = 64
N_EXPERTS = 32
TOP_K = 4
D_FF = 1024
SWIGLU_LIMIT = 7.0
SWIGLU_ALPHA = 1.702
EPS = 1e-6

LANES = 128
SCAN_ROWS = 2048
SCAN_PIECE = 256
TOKEN_TILE = 512
EXPERT_TILE = 512
ROUTE_TILE = 256
SEG_ALIGN = 8
SLOTS = ROUTE_TILE * TOP_K + N_EXPERTS * SEG_ALIGN
VMEM_LIMIT = 56 * 1024 * 1024


def _cparams(*sem):
    return pltpu.CompilerParams(dimension_semantics=sem, vmem_limit_bytes=VMEM_LIMIT)


def _sigmoid(x):
    return 1.0 / (1.0 + jnp.exp(-x))


def _rms(x, w):
    return x * lax.rsqrt(jnp.mean(x * x, axis=-1, keepdims=True) + EPS) * w


def _dot(a, b):
    return jnp.dot(a, b, preferred_element_type=F32)


def _dot_nt(a, b):
    return lax.dot_general(a, b, (((1,), (1,)), ((), ())), preferred_element_type=F32)


def _dot_tn(a, b):
    return lax.dot_general(a, b, (((0,), (0,)), ((), ())), preferred_element_type=F32)


PACKED = D_MODEL // 2
_HIGH_HALF = 0xFFFF0000


def _pack_pairs(x):
    c = x.shape[1] // 2
    lo = lax.bitcast_convert_type(x[:, :c], jnp.uint32) >> 16
    hi = lax.bitcast_convert_type(x[:, c:], jnp.uint32) & jnp.uint32(_HIGH_HALF)
    return hi | lo


def _unpack_pairs(u):
    lo = lax.bitcast_convert_type(u << 16, F32)
    hi = lax.bitcast_convert_type(u & jnp.uint32(_HIGH_HALF), F32)
    return jnp.concatenate([lo, hi], axis=1).astype(BF16)


def _mod_kernel(cond_ref, w_ref, b_ref, o_ref):
    c = cond_ref[...]
    s = (c * _sigmoid(c)).astype(BF16)
    o_ref[...] = _dot(s, w_ref[...].astype(BF16)) + b_ref[...]


def _modulation(cond16, w_mod, b_mod):
    depth = w_mod.shape[0]
    n_out = w_mod.shape[2]
    tn = 1024
    return pl.pallas_call(
        _mod_kernel,
        grid=(depth, n_out // tn),
        in_specs=[
            pl.BlockSpec((16, D_MODEL), lambda l, j: (0, 0)),
            pl.BlockSpec((None, D_MODEL, tn), lambda l, j: (l, 0, j)),
            pl.BlockSpec((None, 1, tn), lambda l, j: (l, 0, j)),
        ],
        out_specs=pl.BlockSpec((None, 16, tn), lambda l, j: (l, 0, j)),
        out_shape=jax.ShapeDtypeStruct((depth, 16, n_out), F32),
        compiler_params=_cparams("arbitrary", "arbitrary"),
        name="modulation",
    )(cond16, w_mod, b_mod.reshape(depth, 1, n_out))


def _two_group_specs(tm, prompt_tiles):
    return [
        pl.BlockSpec((tm, D_MODEL), lambda i, *_: (jnp.minimum(i, prompt_tiles - 1), 0)),
        pl.BlockSpec((tm, D_MODEL), lambda i, *_: (jnp.maximum(i - prompt_tiles, 0), 0)),
    ]


def _pick_group(a_ref, b_ref, prompt_tiles):
    return jnp.where(pl.program_id(0) < prompt_tiles, a_ref[...], b_ref[...])


def _hgrn_kernel(x_ref, mod_ref, nw_ref, wq_ref, wff_ref, wfb_ref, wv_ref, wg_ref, lbf_ref, lbb_ref,
                 gn_ref, s0_ref, o_ref, so_ref, hm_s, q_s, vt_s, kf_s, lf_s, kb_s, lb_s, of_s, ob_s, st_s,
                 *, seq_len, zero_init):
    rows = x_ref.shape[0]
    piece = SCAN_PIECE
    n_pieces = rows // piece
    cpp = piece // CHUNK
    per_piece_seq = seq_len == piece
    assert per_piece_seq or seq_len == rows

    @pl.when(pl.program_id(1) == 0)
    def _():
        y = _rms(x_ref[...], nw_ref[...])
        hm_s[...] = (y * (1.0 + mod_ref[1:2, :]) + mod_ref[0:1, :]).astype(BF16)

    hm = hm_s[...]

    zq = _dot(hm, wq_ref[...])
    q_s[...] = zq * _sigmoid(zq)
    v = _dot(hm, wv_ref[...])
    for p in range(n_pieces):
        vt_s[p] = v[p * piece:(p + 1) * piece, :].T.astype(BF16)
    for w_ref, lbr, k_s, l_s in ((wff_ref, lbf_ref, kf_s, lf_s), (wfb_ref, lbb_ref, kb_s, lb_s)):
        lb = lbr[...]
        f = lb + (1.0 - lb) * _sigmoid(_dot(hm, w_ref[...]))
        k_s[...] = 1.0 - f
        l_s[...] = jnp.log(f)

    def load_state(j, d):
        for h in range(2):
            if zero_init:
                st_s[d, h] = jnp.zeros((HEAD_DIM, HEAD_DIM), F32)
            else:
                st_s[d, h] = s0_ref[j, d, h].T

    def store_state(j, d):
        for h in range(2):
            so_ref[j, d, h] = st_s[d, h].T

    def piece_body(i, carry):
        r = lax.broadcasted_iota(jnp.int32, (piece, piece), 0)
        c = lax.broadcasted_iota(jnp.int32, (piece, piece), 1)
        same = (r // CHUNK) == (c // CHUNK)
        row_chunk = lax.broadcasted_iota(jnp.int32, (piece, 1), 0) // CHUNK
        col_chunk = lax.broadcasted_iota(jnp.int32, (1, piece), 1) // CHUNK
        for d in range(2):
            k_s, l_s, o_s = ((kf_s, lf_s, of_s), (kb_s, lb_s, ob_s))[d]
            p = i if d == 0 else n_pieces - 1 - i
            if per_piece_seq:
                load_state(p, d)
            sl = pl.ds(pl.multiple_of(p * piece, piece), piece)
            keep = same & ((c <= r) if d == 0 else (c >= r))
            tri = jnp.where(keep, 1.0, 0.0).astype(BF16)
            lf = l_s[sl, :]
            hi = lf.astype(BF16)
            mid = (lf - hi.astype(F32)).astype(BF16)
            b = _dot(tri, hi) + _dot(tri, mid)
            b3 = b.reshape(cpp, CHUNK, 2 * HEAD_DIM)
            edge = CHUNK - 1 if d == 0 else 0
            bl3 = b3[:, edge:edge + 1, :]
            centre = CHUNK // 2 - 1 if d == 0 else CHUNK // 2
            bm3 = b3[:, centre:centre + 1, :]
            q = q_s[sl, :]
            q3 = q.reshape(cpp, CHUNK, 2 * HEAD_DIM)
            k3 = k_s[sl, :].reshape(cpp, CHUNK, 2 * HEAD_DIM)
            flat = lambda a: a.reshape(piece, 2 * HEAD_DIM).astype(BF16)
            qd = (q * jnp.exp(b)).astype(BF16)
            qi = flat(q3 * jnp.exp(b3 - bm3))
            ki = flat(k3 * jnp.exp(bm3 - b3))
            ks = flat(k3 * jnp.exp(bl3 - b3))
            dec = jnp.exp(bl3)
            vt = vt_s[p]
            for h in range(2):
                hs = slice(h * HEAD_DIM, (h + 1) * HEAD_DIM)
                att = jnp.where(keep, _dot_nt(qi[:, hs], ki[:, hs]), 0.0).astype(BF16)
                vt_h = vt[hs, :]
                v_exp = jnp.concatenate(
                    [jnp.where(col_chunk == ci, vt_h, jnp.zeros_like(vt_h)) for ci in range(cpp)], axis=0)
                incr = _dot(v_exp, ks[:, hs])
                st = st_s[d, h]
                before = [None] * cpp
                for ci in (range(cpp) if d == 0 else range(cpp - 1, -1, -1)):
                    before[ci] = st.astype(BF16)
                    st = st * dec[ci, :, hs] + incr[ci * HEAD_DIM:(ci + 1) * HEAD_DIM, :]
                st_s[d, h] = st
                qd_h = qd[:, hs]
                q_exp = [jnp.where(row_chunk == ci, qd_h, jnp.zeros_like(qd_h)) for ci in range(cpp)]
                lhs = jnp.concatenate([att] + q_exp, axis=1)
                rhs_t = jnp.concatenate([vt_h] + before, axis=1)
                o_s[sl, hs] = _dot_nt(lhs, rhs_t)
            if per_piece_seq:
                store_state(p, d)
        return carry

    if not per_piece_seq:
        load_state(0, 0)
        load_state(0, 1)
    lax.fori_loop(0, n_pieces, piece_body, 0)
    if not per_piece_seq:
        store_state(0, 0)
        store_state(0, 1)

    o = of_s[...] + ob_s[...]
    gn = gn_ref[...]
    o = jnp.concatenate(
        [_rms(o[:, h * HEAD_DIM:(h + 1) * HEAD_DIM], gn[:, h * HEAD_DIM:(h + 1) * HEAD_DIM])
         for h in range(2)], axis=1)
    o_ref[...] = (o * _sigmoid(_dot(hm, wg_ref[...]))).astype(o_ref.dtype)


def _hgrn_scan(x, mod_l, mod_row0, mod_row_step, norm_w, w_in_bf, lb_f, lb_b, gnorm, s0, seq_len,
               zero_init):
    n_rows = x.shape[0]
    n_seq_total = n_rows // seq_len
    rows = SCAN_ROWS
    seq_per_step = rows // seq_len
    pair = 2 * HEAD_DIM
    n_pairs = HEADS // 2

    def w_spec(seg):
        return pl.BlockSpec((D_MODEL, pair), lambda sb, hp: (0, seg * n_pairs + hp))

    vec_spec = pl.BlockSpec((1, pair), lambda sb, hp: (0, hp))
    st_block = (seq_per_step, 2, 2, HEAD_DIM, HEAD_DIM)
    st_spec = pl.BlockSpec(st_block, lambda sb, hp: (sb, 0, hp, 0, 0))
    s0_spec = pl.BlockSpec(st_block, (lambda sb, hp: (0, 0, hp, 0, 0)) if zero_init
                           else (lambda sb, hp: (sb, 0, hp, 0, 0)))
    scratch = [
        pltpu.VMEM((rows, D_MODEL), BF16),
        pltpu.VMEM((rows, pair), F32),
        pltpu.VMEM((rows // SCAN_PIECE, pair, SCAN_PIECE), BF16),
        pltpu.VMEM((rows, pair), F32),
        pltpu.VMEM((rows, pair), F32),
        pltpu.VMEM((rows, pair), F32),
        pltpu.VMEM((rows, pair), F32),
        pltpu.VMEM((rows, pair), F32),
        pltpu.VMEM((rows, pair), F32),
        pltpu.VMEM((2, 2, HEAD_DIM, HEAD_DIM), F32),
    ]
    in_specs = [
        pl.BlockSpec((rows, D_MODEL), lambda sb, hp: (sb, 0), pipeline_mode=pl.Buffered(1)),
        pl.BlockSpec((None, 6, D_MODEL), lambda sb, hp: (mod_row0 + sb * mod_row_step, 0, 0)),
        pl.BlockSpec((1, D_MODEL), lambda sb, hp: (0, 0)),
        w_spec(0), w_spec(1), w_spec(2), w_spec(3), w_spec(4),
        vec_spec, vec_spec, vec_spec,
        s0_spec,
    ]
    args = [x, mod_l, norm_w.reshape(1, D_MODEL), w_in_bf, w_in_bf, w_in_bf, w_in_bf, w_in_bf,
            lb_f.reshape(1, D_MODEL), lb_b.reshape(1, D_MODEL), gnorm.reshape(1, D_MODEL), s0]
    return pl.pallas_call(
        functools.partial(_hgrn_kernel, seq_len=seq_len, zero_init=zero_init),
        grid=(n_rows // rows, n_pairs),
        in_specs=in_specs,
        out_specs=[
            pl.BlockSpec((rows, pair), lambda sb, hp: (sb, hp)),
            st_spec,
        ],
        out_shape=[
            jax.ShapeDtypeStruct((n_rows, D_MODEL), BF16),
            jax.ShapeDtypeStruct((n_seq_total, 2, HEADS, HEAD_DIM, HEAD_DIM), F32),
        ],
        scratch_shapes=scratch,
        compiler_params=_cparams("arbitrary", "arbitrary"),
        name="hgrn_scan",
    )(*args)


def _proj_res_kernel(row_ref, xp_ref, xs_ref, op_ref, os_ref, w_ref, mod_ref, xo_ref, *, prompt_tiles):
    del row_ref
    x = _pick_group(xp_ref, xs_ref, prompt_tiles)
    o = _pick_group(op_ref, os_ref, prompt_tiles)
    xo_ref[...] = x + mod_ref[2:3, :] * _dot(o, w_ref[...])


def _proj_residual(x_p, x_s, o_prompt, o_sample, w_out_bf, tile_row, mod_l):
    n = x_p.shape[0] + x_s.shape[0]
    tm = TOKEN_TILE
    pt = o_prompt.shape[0] // tm
    return pl.pallas_call(
        functools.partial(_proj_res_kernel, prompt_tiles=pt),
        grid_spec=pltpu.PrefetchScalarGridSpec(
            num_scalar_prefetch=1,
            grid=(n // tm,),
            in_specs=_two_group_specs(tm, pt) + _two_group_specs(tm, pt) + [
                pl.BlockSpec((D_MODEL, D_MODEL), lambda i, row: (0, 0)),
                pl.BlockSpec((None, 6, D_MODEL), lambda i, row: (row[i], 0, 0)),
            ],
            out_specs=pl.BlockSpec((tm, D_MODEL), lambda i, row: (i, 0)),
        ),
        out_shape=jax.ShapeDtypeStruct((n, D_MODEL), F32),
        compiler_params=_cparams("arbitrary"),
        name="proj_residual",
    )(tile_row, x_p, x_s, o_prompt, o_sample, w_out_bf, mod_l)


def _conv_kernel(row_ref, width_ref, x_ref, mod_ref, nw_ref, win_ref, cw_ref, wout_ref, xo_ref):
    del row_ref
    i = pl.program_id(0)
    x = x_ref[...]
    hm = (_rms(x, nw_ref[...]) * (1.0 + mod_ref[1:2, :]) + mod_ref[0:1, :]).astype(BF16)
    z = _dot(hm, win_ref[...])
    bg = z[:, :D_MODEL]
    u = z[:, D_MODEL:2 * D_MODEL] * z[:, 2 * D_MODEL:]
    tm = x.shape[0]
    pos = lax.broadcasted_iota(jnp.int32, (tm, 1), 0) & (width_ref[i] - 1)
    prev = jnp.where(pos == 0, 0.0, pltpu.roll(u, 1, axis=0))
    nxt = jnp.where(pos == width_ref[i] - 1, 0.0, pltpu.roll(u, tm - 1, axis=0))
    v = cw_ref[0:1, :] * prev + cw_ref[1:2, :] * u + cw_ref[2:3, :] * nxt
    y = _dot((bg * v).astype(BF16), wout_ref[...])
    xo_ref[...] = x + mod_ref[2:3, :] * y


def _conv_mixer(x, tile_row, tile_width, mod_l, norm_w, w_in_bf, conv_w, w_out_bf):
    n = x.shape[0]
    tm = TOKEN_TILE
    return pl.pallas_call(
        _conv_kernel,
        grid_spec=pltpu.PrefetchScalarGridSpec(
            num_scalar_prefetch=2,
            grid=(n // tm,),
            in_specs=[
                pl.BlockSpec((tm, D_MODEL), lambda i, row, w: (i, 0)),
                pl.BlockSpec((None, 6, D_MODEL), lambda i, row, w: (row[i], 0, 0)),
                pl.BlockSpec((1, D_MODEL), lambda i, row, w: (0, 0)),
                pl.BlockSpec((D_MODEL, 3 * D_MODEL), lambda i, row, w: (0, 0)),
                pl.BlockSpec((3, D_MODEL), lambda i, row, w: (0, 0)),
                pl.BlockSpec((D_MODEL, D_MODEL), lambda i, row, w: (0, 0)),
            ],
            out_specs=pl.BlockSpec((tm, D_MODEL), lambda i, row, w: (i, 0)),
        ),
        out_shape=jax.ShapeDtypeStruct((n, D_MODEL), F32),
        compiler_params=_cparams("arbitrary"),
        name="conv_mixer",
    )(tile_row, tile_width, x, mod_l, norm_w.reshape(1, D_MODEL), w_in_bf, conv_w, w_out_bf)


def _router_kernel(row_ref, x_ref, mod_ref, nw_ref, rwh_ref, rwl_ref, rb_ref,
                   hf_ref, pg_ref, pgt_ref, cnt_ref):
    del row_ref
    hf = _rms(x_ref[...], nw_ref[...]) * (1.0 + mod_ref[4:5, :]) + mod_ref[3:4, :]
    hf_hi = hf.astype(BF16)
    hf_ref[...] = hf_hi
    hf_lo = (hf - hf_hi.astype(F32)).astype(BF16)
    logits = (_dot(hf_hi, rwh_ref[...]) + (_dot(hf_hi, rwl_ref[...]) + _dot(hf_lo, rwh_ref[...]))
              + rb_ref[...])
    tm = hf.shape[0]
    lane = lax.broadcasted_iota(jnp.int32, (tm, LANES), 1)
    lane_f = lane.astype(F32)
    work = logits
    vals, hots = [], []
    for _ in range(TOP_K):
        m = jnp.max(work, axis=-1, keepdims=True)
        idx_f = jnp.min(jnp.where(work == m, lane_f, float(LANES)), axis=-1, keepdims=True)
        hot = lane_f == idx_f
        vals.append(m)
        hots.append(hot)
        work = jnp.where(hot, -jnp.inf, work)
    ex = [jnp.exp(v - vals[0]) for v in vals]
    inv = 1.0 / (ex[0] + ex[1] + ex[2] + ex[3])

    multi = jnp.where(hots[0] | hots[1] | hots[2] | hots[3], 1.0, 0.0)
    tr = lax.broadcasted_iota(jnp.int32, (tm, tm), 0)
    tc = lax.broadcasted_iota(jnp.int32, (tm, tm), 1)
    earlier = jnp.where(tc < tr, 1.0, 0.0).astype(BF16)
    before = _dot(earlier, multi.astype(BF16))
    cnt = jnp.sum(multi, axis=0, keepdims=True)
    er = lax.broadcasted_iota(jnp.int32, (LANES, LANES), 0)
    ec = lax.broadcasted_iota(jnp.int32, (LANES, LANES), 1)
    lower = jnp.where(er < ec, 1.0, 0.0).astype(BF16)
    cnt_al = jnp.floor((cnt + (SEG_ALIGN - 1)) * (1.0 / SEG_ALIGN)) * SEG_ALIGN
    estart = _dot(jnp.broadcast_to(cnt_al, (8, LANES)).astype(BF16), lower)[0:1, :]
    pos_all = before + estart

    pg = jnp.zeros((tm, LANES), F32)
    for k in range(TOP_K):
        pos = jnp.sum(jnp.where(hots[k], pos_all, 0.0), axis=-1, keepdims=True)
        pg = jnp.where(lane == k, pos, pg)
        pg = jnp.where(lane == TOP_K + k, ex[k] * inv, pg)
    pg_ref[...] = pg
    pgt_ref[...] = pg.T[0:8, :]
    cnt_ref[...] = cnt.astype(jnp.int32)


def _router(x, tile_row, mod_l, norm_w, router_w_pad, router_b_pad):
    n = x.shape[0]
    tm = ROUTE_TILE
    nt = n // tm
    rw_hi = router_w_pad.astype(BF16)
    rw_lo = (router_w_pad - rw_hi.astype(F32)).astype(BF16)
    return pl.pallas_call(
        _router_kernel,
        grid_spec=pltpu.PrefetchScalarGridSpec(
            num_scalar_prefetch=1,
            grid=(nt,),
            in_specs=[
                pl.BlockSpec((tm, D_MODEL), lambda i, row: (i, 0)),
                pl.BlockSpec((None, 6, D_MODEL), lambda i, row: (row[i], 0, 0)),
                pl.BlockSpec((1, D_MODEL), lambda i, row: (0, 0)),
                pl.BlockSpec((D_MODEL, LANES), lambda i, row: (0, 0)),
                pl.BlockSpec((D_MODEL, LANES), lambda i, row: (0, 0)),
                pl.BlockSpec((1, LANES), lambda i, row: (0, 0)),
            ],
            out_specs=[
                pl.BlockSpec((tm, D_MODEL), lambda i, row: (i, 0)),
                pl.BlockSpec((tm, LANES), lambda i, row: (i, 0)),
                pl.BlockSpec((None, 8, tm), lambda i, row: (i, 0, 0)),
                pl.BlockSpec((None, 1, LANES), lambda i, row: (i, 0, 0)),
            ],
        ),
        out_shape=[
            jax.ShapeDtypeStruct((n, D_MODEL), BF16),
            jax.ShapeDtypeStruct((n, LANES), F32),
            jax.ShapeDtypeStruct((nt, 8, tm), F32),
            jax.ShapeDtypeStruct((nt, 1, LANES), jnp.int32),
        ],
        compiler_params=_cparams("arbitrary"),
        name="router",
    )(tile_row, x, mod_l, norm_w.reshape(1, D_MODEL), rw_hi, rw_lo, router_b_pad)


SEG_STRIDE = 4 * N_EXPERTS


def _segment_table(tbl_ref, t, e):
    base = t * SEG_STRIDE
    al = lambda v: pl.multiple_of(v, SEG_ALIGN)
    return (al(tbl_ref[base + e]), al(tbl_ref[base + N_EXPERTS + e]),
            al(tbl_ref[base + 2 * N_EXPERTS + e]))


def _tile_rows_total(tbl_ref, t):
    return pl.multiple_of(tbl_ref[t * SEG_STRIDE + 3 * N_EXPERTS], SEG_ALIGN)


def _dispatch_kernel(tbl_ref, pad_ref, nu_ref, hf_ref, pgt_ref, xb_ref, stage_s, zero_s, sem, zsem,
                     *, n_blocks):
    t = pl.program_id(0)
    nt = pl.num_programs(0)
    slot = t % 2
    half = zero_s.shape[0]

    def zero_rows(start, rows):
        start = pl.multiple_of(start, SEG_ALIGN)
        rows = pl.multiple_of(rows, SEG_ALIGN)

        @pl.when(rows > 0)
        def _():
            pltpu.make_async_copy(zero_s.at[pl.ds(0, rows), :], xb_ref.at[pl.ds(start, rows), :],
                                  zsem).start()

    @pl.when(t == 0)
    def _():
        zero_s[...] = jnp.zeros_like(zero_s)

        def pad_body(e, carry):
            start = pad_ref[e]
            rows = pad_ref[N_EXPERTS + e]
            first = jnp.minimum(rows, half)
            zero_rows(start, first)
            zero_rows(start + half, rows - first)
            return carry

        lax.fori_loop(0, N_EXPERTS, pad_body, 0)

        def tail_body(b, carry):
            for part in range(EXPERT_TILE // half):
                zero_rows(b * EXPERT_TILE + part * half, half)
            return carry

        lax.fori_loop(nu_ref[0], n_blocks, tail_body, 0)

    def wait_tile(tile, sl):
        rows = _tile_rows_total(tbl_ref, tile)
        pltpu.make_async_copy(stage_s.at[sl, pl.ds(0, rows), :], xb_ref.at[pl.ds(0, rows), :],
                              sem.at[sl]).wait()

    @pl.when(t >= 2)
    def _():
        wait_tile(t - 2, slot)

    s_iota = lax.broadcasted_iota(jnp.int32, (SLOTS, ROUTE_TILE), 0).astype(F32)
    pgt = pgt_ref[...]
    hit = s_iota == pgt[0:1, :]
    for k in range(1, TOP_K):
        hit = hit | (s_iota == pgt[k:k + 1, :])
    perm = jnp.where(hit, 1.0, 0.0).astype(BF16)
    stage_s[slot] = _pack_pairs(_dot(perm, hf_ref[...]))

    def seg_body(e, carry):
        rows, src, dst = _segment_table(tbl_ref, t, e)

        @pl.when(rows > 0)
        def _():
            pltpu.make_async_copy(stage_s.at[slot, pl.ds(src, rows), :],
                                  xb_ref.at[pl.ds(dst, rows), :], sem.at[slot]).start()
        return carry

    lax.fori_loop(0, N_EXPERTS, seg_body, 0)

    @pl.when(t == nt - 1)
    def _():
        wait_tile(t, slot)

        @pl.when(nt > 1)
        def _():
            wait_tile(t - 1, 1 - slot)

        zeroed = pl.multiple_of(pad_ref[2 * N_EXPERTS], SEG_ALIGN)

        @pl.when(zeroed > 0)
        def _():
            pltpu.make_async_copy(xb_ref.at[pl.ds(0, zeroed), :], xb_ref.at[pl.ds(0, zeroed), :],
                                  zsem).wait()


def _dispatch(hf, pgt, seg_tbl, pad_tbl, n_used, n_blocks):
    n = hf.shape[0]
    tm = ROUTE_TILE
    return pl.pallas_call(
        functools.partial(_dispatch_kernel, n_blocks=n_blocks),
        grid_spec=pltpu.PrefetchScalarGridSpec(
            num_scalar_prefetch=3,
            grid=(n // tm,),
            in_specs=[
                pl.BlockSpec((tm, D_MODEL), lambda i, a, b, c: (i, 0)),
                pl.BlockSpec((None, 8, tm), lambda i, a, b, c: (i, 0, 0)),
            ],
            out_specs=pl.BlockSpec(memory_space=pl.ANY),
            scratch_shapes=[
                pltpu.VMEM((2, SLOTS, PACKED), jnp.uint32),
                pltpu.VMEM((EXPERT_TILE // 2, PACKED), jnp.uint32),
                pltpu.SemaphoreType.DMA((2,)),
                pltpu.SemaphoreType.DMA,
            ],
        ),
        out_shape=jax.ShapeDtypeStruct((n_blocks * EXPERT_TILE, PACKED), jnp.uint32),
        compiler_params=_cparams("arbitrary"),
        name="dispatch",
    )(seg_tbl, pad_tbl, n_used, hf, pgt)


def _expert_kernel(be_ref, nu_ref, nxt_ref, slot_ref, xb_ref, wgu_hbm, bgu_ref, wd_hbm, bd_ref, yb_ref,
                   wgu_f, wd_f, wgu_s, wd_s, sem, *, layer):
    i = pl.program_id(0)

    def fetch(e, slot):
        return (pltpu.make_async_copy(wgu_hbm.at[layer, e], wgu_f.at[slot], sem.at[slot]),
                pltpu.make_async_copy(wd_hbm.at[layer, e], wd_f.at[slot], sem.at[slot]))

    @pl.when(i < nu_ref[0])
    def _():
        e = be_ref[i]
        slot = slot_ref[i]
        first_block = jnp.logical_or(i == 0, e != be_ref[jnp.maximum(i - 1, 0)])

        @pl.when(first_block)
        def _():
            @pl.when(i == 0)
            def _():
                for cp in fetch(e, slot):
                    cp.start()

            for cp in fetch(e, slot):
                cp.wait()
            wgu_s[...] = wgu_f[slot].astype(BF16)
            wd_s[...] = wd_f[slot].astype(BF16)

            @pl.when(nxt_ref[i] >= 0)
            def _():
                for cp in fetch(nxt_ref[i], 1 - slot):
                    cp.start()

        gu = _dot(_unpack_pairs(xb_ref[...]), wgu_s[...]) + bgu_ref[...]
        gate = jnp.minimum(gu[:, :D_FF], SWIGLU_LIMIT)
        up = jnp.clip(gu[:, D_FF:], -SWIGLU_LIMIT, SWIGLU_LIMIT)
        act = (up + 1.0) * gate * _sigmoid(SWIGLU_ALPHA * gate)
        y = _dot(act.astype(BF16), wd_s[...]) + bd_ref[...]
        yb_ref[...] = _pack_pairs(y.astype(BF16).astype(F32))

    @pl.when(i >= nu_ref[0])
    def _():
        yb_ref[...] = jnp.zeros_like(yb_ref)


def _experts(xb, block_e, n_used, next_e, slot, layer, w_gu, b_gu, w_down, b_down):
    te = EXPERT_TILE
    n_blocks = xb.shape[0] // te
    depth = w_gu.shape[0]

    def x_map(i, be, nu, nx, sl):
        return (jnp.maximum(jnp.minimum(i, nu[0] - 1), 0), 0)

    def bias_spec(width):
        return pl.BlockSpec((None, None, 1, width), lambda i, be, nu, nx, sl: (layer, be[i], 0, 0))

    return pl.pallas_call(
        functools.partial(_expert_kernel, layer=layer),
        grid_spec=pltpu.PrefetchScalarGridSpec(
            num_scalar_prefetch=4,
            grid=(n_blocks,),
            in_specs=[
                pl.BlockSpec((te, PACKED), x_map),
                pl.BlockSpec(memory_space=pl.ANY),
                bias_spec(2 * D_FF),
                pl.BlockSpec(memory_space=pl.ANY),
                bias_spec(D_MODEL),
            ],
            out_specs=pl.BlockSpec((te, PACKED), lambda i, be, nu, nx, sl: (i, 0)),
            scratch_shapes=[
                pltpu.VMEM((2, D_MODEL, 2 * D_FF), F32),
                pltpu.VMEM((2, D_FF, D_MODEL), F32),
                pltpu.VMEM((D_MODEL, 2 * D_FF), BF16),
                pltpu.VMEM((D_FF, D_MODEL), BF16),
                pltpu.SemaphoreType.DMA((2,)),
            ],
        ),
        out_shape=jax.ShapeDtypeStruct(xb.shape, jnp.uint32),
        compiler_params=_cparams("arbitrary"),
        name="experts",
    )(block_e, n_used, next_e, slot, xb, w_gu, b_gu.reshape(depth, N_EXPERTS, 1, 2 * D_FF), w_down,
      b_down.reshape(depth, N_EXPERTS, 1, D_MODEL))


def _combine_kernel(row_ref, tbl_ref, x_ref, pg_ref, mod_ref, fw_ref, yb_ref, *rest, split_tiles):
    del row_ref
    stage_s, sem = rest[-2:]
    t = pl.program_id(0)
    nt = pl.num_programs(0)
    slot = t % 2

    def fetch(tile, into):
        def seg_body(e, carry):
            rows, dst, src = _segment_table(tbl_ref, tile, e)

            @pl.when(rows > 0)
            def _():
                pltpu.make_async_copy(yb_ref.at[pl.ds(src, rows), :],
                                      stage_s.at[into, pl.ds(dst, rows), :], sem.at[into]).start()
            return carry

        lax.fori_loop(0, N_EXPERTS, seg_body, 0)

    @pl.when(t == 0)
    def _():
        stage_s[...] = jnp.zeros_like(stage_s)
        fetch(0, 0)

    @pl.when(t + 1 < nt)
    def _():
        fetch(t + 1, 1 - slot)

    fetched = _tile_rows_total(tbl_ref, t)
    pltpu.make_async_copy(yb_ref.at[pl.ds(0, fetched), :], stage_s.at[slot, pl.ds(0, fetched), :],
                          sem.at[slot]).wait()
    rows_sorted = _unpack_pairs(stage_s[slot])
    pg = pg_ref[...]
    s_iota = lax.broadcasted_iota(jnp.int32, (ROUTE_TILE, SLOTS), 1).astype(F32)
    sel = jnp.zeros((ROUTE_TILE, SLOTS), F32)
    for k in range(TOP_K):
        sel = jnp.where(s_iota == pg[:, k:k + 1], pg[:, TOP_K + k:TOP_K + k + 1], sel)
    y = _dot(sel.astype(BF16), rows_sorted)
    x = x_ref[...] + mod_ref[5:6, :] * y
    if split_tiles is None:
        rest[0][...] = x
    else:
        x = _rms(x, fw_ref[...])

        @pl.when(t < split_tiles)
        def _():
            rest[0][...] = x

        @pl.when(t >= split_tiles)
        def _():
            rest[1][...] = x


def _combine(x, pg, seg_tbl, yb, tile_row, mod_l, final_w, n_prompt=None):
    n = x.shape[0]
    tm = ROUTE_TILE
    if n_prompt is None:
        split = None
        out_specs = pl.BlockSpec((tm, D_MODEL), lambda i, row, tbl: (i, 0))
        out_shape = jax.ShapeDtypeStruct((n, D_MODEL), F32)
    else:
        split = n_prompt // tm
        out_specs = [
            pl.BlockSpec((tm, D_MODEL), lambda i, row, tbl: (jnp.minimum(i, split - 1), 0)),
            pl.BlockSpec((tm, D_MODEL), lambda i, row, tbl: (jnp.maximum(i - split, 0), 0)),
        ]
        out_shape = [jax.ShapeDtypeStruct((n_prompt, D_MODEL), F32),
                     jax.ShapeDtypeStruct((n - n_prompt, D_MODEL), F32)]
    return pl.pallas_call(
        functools.partial(_combine_kernel, split_tiles=split),
        grid_spec=pltpu.PrefetchScalarGridSpec(
            num_scalar_prefetch=2,
            grid=(n // tm,),
            in_specs=[
                pl.BlockSpec((tm, D_MODEL), lambda i, row, tbl: (i, 0)),
                pl.BlockSpec((tm, LANES), lambda i, row, tbl: (i, 0)),
                pl.BlockSpec((None, 6, D_MODEL), lambda i, row, tbl: (row[i], 0, 0)),
                pl.BlockSpec((1, D_MODEL), lambda i, row, tbl: (0, 0)),
                pl.BlockSpec(memory_space=pl.ANY),
            ],
            out_specs=out_specs,
            scratch_shapes=[pltpu.VMEM((2, SLOTS, PACKED), jnp.uint32), pltpu.SemaphoreType.DMA((2,))],
        ),
        out_shape=out_shape,
        compiler_params=_cparams("arbitrary"),
        name="combine",
    )(tile_row, seg_tbl, x, pg, mod_l, final_w.reshape(1, D_MODEL), yb)


def _moe(x, tile_row, mod_l, norm_w, router_w, router_b, layer, w_gu, b_gu, w_down, b_down,
         final_w, n_prompt=None):
    n = x.shape[0]
    rw = jnp.zeros((D_MODEL, LANES), F32).at[:, :N_EXPERTS].set(router_w)
    rb = jnp.full((1, LANES), -jnp.inf, F32).at[0, :N_EXPERTS].set(router_b)
    hf, pg, pgt, cnt = _router(x, tile_row, mod_l, norm_w, rw, rb)

    te = EXPERT_TILE
    nt = n // ROUTE_TILE
    n_blocks = (n * TOP_K + nt * N_EXPERTS * (SEG_ALIGN - 1)) // te + 1 + N_EXPERTS
    cnt = cnt[:, 0, :N_EXPERTS]
    cnt = (cnt + SEG_ALIGN - 1) // SEG_ALIGN * SEG_ALIGN
    total = jnp.sum(cnt, axis=0)
    blocks_e = (total + te - 1) // te
    block_end = jnp.cumsum(blocks_e)
    pstart = (block_end - blocks_e) * te
    n_used = block_end[-1]
    first_row = pstart[None, :] + jnp.cumsum(cnt, axis=0) - cnt
    tile_off = jnp.cumsum(cnt, axis=1) - cnt
    tile_total = jnp.broadcast_to(jnp.sum(cnt, axis=1, keepdims=True), cnt.shape)
    seg_tbl = jnp.concatenate([cnt, tile_off, first_row, tile_total], axis=1)
    seg_tbl = seg_tbl.reshape(-1).astype(jnp.int32)
    pad_rows = blocks_e * te - total
    zeroed = jnp.sum(pad_rows) + (n_blocks - n_used) * te
    pad_tbl = jnp.concatenate([pstart + total, pad_rows, zeroed[None]]).astype(jnp.int32)
    block_ids = jnp.arange(n_blocks, dtype=jnp.int32)
    clamped = jnp.minimum(block_ids, n_used - 1)
    block_e = jnp.sum((clamped[:, None] >= block_end[None, :]).astype(jnp.int32), axis=1)
    block_e = jnp.minimum(block_e, N_EXPERTS - 1).astype(jnp.int32)
    n_used = n_used.astype(jnp.int32).reshape(1)
    owns = (blocks_e > 0)[None, :]
    ids = jnp.arange(N_EXPERTS, dtype=jnp.int32)[None, :]
    mine = block_e[:, None]
    next_e = jnp.min(jnp.where((ids > mine) & owns, ids, N_EXPERTS), axis=1)
    next_e = jnp.where(next_e < N_EXPERTS, next_e, -1).astype(jnp.int32)
    slot = ((jnp.sum(((ids <= mine) & owns).astype(jnp.int32), axis=1) + 1) % 2).astype(jnp.int32)

    xb = _dispatch(hf, pgt, seg_tbl, pad_tbl, n_used, n_blocks)
    yb = _experts(xb, block_e, n_used, next_e, slot, layer, w_gu, b_gu, w_down, b_down)
    return _combine(x, pg, seg_tbl, yb, tile_row, mod_l, final_w, n_prompt)


def _tile_rows(n_prompt_tok, n_sample_seq, sample_len, tile):
    starts = np.arange(0, n_prompt_tok + n_sample_seq * sample_len, tile)
    row = np.where(starts < n_prompt_tok, 0, 1 + (starts - n_prompt_tok) // sample_len)
    return jnp.asarray(row, dtype=jnp.int32)


def kernel(x_prompt, x_sample, state_hgrn, c, c_ctx, w_mod, b_mod, norm_mix, norm_ffn, hg_w_in,
           hg_lb_logits, hg_gnorm, hg_w_out, cv_w_in, cv_w, cv_w_out, router_w, router_b,
           moe_w_gu, moe_b_gu, moe_w_down, moe_b_down, final_norm):
    bp, tp, d = x_prompt.shape
    bs, ts, _ = x_sample.shape
    depth = w_mod.shape[0]
    n_prompt = bp * tp
    n = n_prompt + bs * ts
    assert d == D_MODEL and depth == 2 and 1 + bs <= 16
    assert n_prompt % SCAN_ROWS == 0 and ts == SCAN_ROWS and tp == SCAN_PIECE

    x_p = x_prompt.reshape(n_prompt, d)
    x_s = x_sample.reshape(bs * ts, d)
    cond16 = jnp.zeros((16, d), F32).at[0].set(c_ctx).at[1:1 + bs].set(c)
    mod = _modulation(cond16, w_mod, b_mod).reshape(depth, 16, 6, d)

    tile_row = _tile_rows(n_prompt, bs, ts, TOKEN_TILE)
    tile_row_route = _tile_rows(n_prompt, bs, ts, ROUTE_TILE)
    starts = np.arange(0, n, TOKEN_TILE)
    tile_width = jnp.asarray(np.where(starts < n_prompt, tp, GRID_W), dtype=jnp.int32)

    lb_all = jnp.cumsum(jax.nn.softmax(hg_lb_logits.astype(F32), axis=1), axis=1)

    w_in_bf = hg_w_in[0].astype(BF16)
    gn = hg_gnorm[0].reshape(-1)
    zero_state = jnp.zeros((SCAN_ROWS // tp, 2, HEADS, HEAD_DIM, HEAD_DIM), F32)
    o_p, s_new = _hgrn_scan(x_p, mod[0], 0, 0, norm_mix[0], w_in_bf, lb_all[0, 0], lb_all[1, 0], gn,
                            zero_state, tp, True)
    o_s, _ = _hgrn_scan(x_s, mod[0], 1, 1, norm_mix[0], w_in_bf, lb_all[0, 0], lb_all[1, 0], gn,
                        state_hgrn[:, 0], ts, False)
    x = _proj_residual(x_p, x_s, o_p, o_s, hg_w_out[0].astype(BF16), tile_row, mod[0])
    x = _moe(x, tile_row_route, mod[0], norm_ffn[0], router_w[0], router_b[0],
             0, moe_w_gu, moe_b_gu, moe_w_down, moe_b_down, final_norm)

    x = _conv_mixer(x, tile_row, tile_width, mod[1], norm_mix[1], cv_w_in[0].astype(BF16), cv_w[0],
                    cv_w_out[0].astype(BF16))
    y_p, y_s = _moe(x, tile_row_route, mod[1], norm_ffn[1], router_w[1], router_b[1],
                    1, moe_w_gu, moe_b_gu, moe_w_down, moe_b_down, final_norm, n_prompt)

    y_prompt = y_p.reshape(bp, tp, d)
    y_sample = y_s.reshape(bs, ts, d)
    return (y_prompt, y_sample, s_new.reshape(bp, 1, 2, HEADS, HEAD_DIM, HEAD_DIM))
```

```python
import functools

import numpy as np
import jax
import jax.numpy as jnp
from jax import lax
from jax.experimental import pallas as pl
from jax.experimental.pallas import tpu as pltpu

F32 = jnp.float32
BF16 = jnp.bfloat16

D_MODEL = 1024
HEADS = 8
HEAD_DIM = 128
CHUNK = 64
GRID_W = 64
N_EXPERTS = 32
TOP_K = 4
D_FF = 1024
SWIGLU_LIMIT = 7.0
SWIGLU_ALPHA = 1.702
EPS = 1e-6

LANES = 128
SCAN_ROWS = 2048
SCAN_PIECE = 256
TOKEN_TILE = 512
EXPERT_TILE = 512
EXPERT_SUB = 256
ROUTE_TILE = 256
SEG_ALIGN = 8
SLOTS = ROUTE_TILE * TOP_K + N_EXPERTS * SEG_ALIGN
VMEM_LIMIT = 56 * 1024 * 1024


def _cparams(*sem):
    return pltpu.CompilerParams(dimension_semantics=sem, vmem_limit_bytes=VMEM_LIMIT)


def _sigmoid(x):
    return 1.0 / (1.0 + jnp.exp(-x))


def _rms(x, w):
    return x * lax.rsqrt(jnp.mean(x * x, axis=-1, keepdims=True) + EPS) * w


def _dot(a, b):
    return jnp.dot(a, b, preferred_element_type=F32)


def _dot_nt(a, b):
    return lax.dot_general(a, b, (((1,), (1,)), ((), ())), preferred_element_type=F32)


def _dot_tn(a, b):
    return lax.dot_general(a, b, (((0,), (0,)), ((), ())), preferred_element_type=F32)


PACKED = D_MODEL // 2
_HIGH_HALF = 0xFFFF0000


def _pack_pairs(x):
    c = x.shape[1] // 2
    lo = lax.bitcast_convert_type(x[:, :c], jnp.uint32) >> 16
    hi = lax.bitcast_convert_type(x[:, c:], jnp.uint32) & jnp.uint32(_HIGH_HALF)
    return hi | lo


def _unpack_pairs(u):
    lo = lax.bitcast_convert_type(u << 16, F32)
    hi = lax.bitcast_convert_type(u & jnp.uint32(_HIGH_HALF), F32)
    return jnp.concatenate([lo, hi], axis=1).astype(BF16)


def _mod_kernel(cond_ref, w_ref, b_ref, o_ref):
    c = cond_ref[...]
    s = (c * _sigmoid(c)).astype(BF16)
    o_ref[...] = _dot(s, w_ref[...].astype(BF16)) + b_ref[...]


def _modulation(cond16, w_mod, b_mod):
    depth = w_mod.shape[0]
    n_out = w_mod.shape[2]
    tn = 1024
    return pl.pallas_call(
        _mod_kernel,
        grid=(depth, n_out // tn),
        in_specs=[
            pl.BlockSpec((16, D_MODEL), lambda l, j: (0, 0)),
            pl.BlockSpec((None, D_MODEL, tn), lambda l, j: (l, 0, j)),
            pl.BlockSpec((None, 1, tn), lambda l, j: (l, 0, j)),
        ],
        out_specs=pl.BlockSpec((None, 16, tn), lambda l, j: (l, 0, j)),
        out_shape=jax.ShapeDtypeStruct((depth, 16, n_out), F32),
        compiler_params=_cparams("arbitrary", "arbitrary"),
        name="modulation",
    )(cond16, w_mod, b_mod.reshape(depth, 1, n_out))


def _two_group_specs(tm, prompt_tiles):
    return [
        pl.BlockSpec((tm, D_MODEL), lambda i, *_: (jnp.minimum(i, prompt_tiles - 1), 0)),
        pl.BlockSpec((tm, D_MODEL), lambda i, *_: (jnp.maximum(i - prompt_tiles, 0), 0)),
    ]


def _pick_group(a_ref, b_ref, prompt_tiles):
    return jnp.where(pl.program_id(0) < prompt_tiles, a_ref[...], b_ref[...])


def _hgrn_kernel(x_ref, mod_ref, nw_ref, wq_ref, wff_ref, wfb_ref, wv_ref, wg_ref, lbf_ref, lbb_ref,
                 gn_ref, s0_ref, o_ref, so_ref, hm_s, q_s, vt_s, kf_s, lf_s, kb_s, lb_s, of_s, ob_s, st_s,
                 *, seq_len, zero_init):
    rows = x_ref.shape[0]
    piece = SCAN_PIECE
    n_pieces = rows // piece
    cpp = piece // CHUNK
    per_piece_seq = seq_len == piece
    assert per_piece_seq or seq_len == rows

    @pl.when(pl.program_id(1) == 0)
    def _():
        y = _rms(x_ref[...], nw_ref[...])
        hm_s[...] = (y * (1.0 + mod_ref[1:2, :]) + mod_ref[0:1, :]).astype(BF16)

    hm = hm_s[...]

    zq = _dot(hm, wq_ref[...])
    q_s[...] = zq * _sigmoid(zq)
    v = _dot(hm, wv_ref[...])
    for p in range(n_pieces):
        vt_s[p] = v[p * piece:(p + 1) * piece, :].T.astype(BF16)
    for w_ref, lbr, k_s, l_s in ((wff_ref, lbf_ref, kf_s, lf_s), (wfb_ref, lbb_ref, kb_s, lb_s)):
        lb = lbr[...]
        f = lb + (1.0 - lb) * _sigmoid(_dot(hm, w_ref[...]))
        k_s[...] = 1.0 - f
        l_s[...] = jnp.log(f)

    def load_state(j, d):
        for h in range(2):
            if zero_init:
                st_s[d, h] = jnp.zeros((HEAD_DIM, HEAD_DIM), F32)
            else:
                st_s[d, h] = s0_ref[j, d, h].T

    def store_state(j, d):
        for h in range(2):
            so_ref[j, d, h] = st_s[d, h].T

    def piece_body(i, carry):
        r = lax.broadcasted_iota(jnp.int32, (piece, piece), 0)
        c = lax.broadcasted_iota(jnp.int32, (piece, piece), 1)
        same = (r // CHUNK) == (c // CHUNK)
        row_chunk = lax.broadcasted_iota(jnp.int32, (piece, 1), 0) // CHUNK
        col_chunk = lax.broadcasted_iota(jnp.int32, (1, piece), 1) // CHUNK
        for d in range(2):
            k_s, l_s, o_s = ((kf_s, lf_s, of_s), (kb_s, lb_s, ob_s))[d]
            p = i if d == 0 else n_pieces - 1 - i
            if per_piece_seq:
                load_state(p, d)
            sl = pl.ds(pl.multiple_of(p * piece, piece), piece)
            keep = same & ((c <= r) if d == 0 else (c >= r))
            tri = jnp.where(keep, 1.0, 0.0).astype(BF16)
            lf = l_s[sl, :]
            hi = lf.astype(BF16)
            mid = (lf - hi.astype(F32)).astype(BF16)
            b = _dot(tri, hi) + _dot(tri, mid)
            b3 = b.reshape(cpp, CHUNK, 2 * HEAD_DIM)
            edge = CHUNK - 1 if d == 0 else 0
            bl3 = b3[:, edge:edge + 1, :]
            centre = CHUNK // 2 - 1 if d == 0 else CHUNK // 2
            bm3 = b3[:, centre:centre + 1, :]
            q = q_s[sl, :]
            q3 = q.reshape(cpp, CHUNK, 2 * HEAD_DIM)
            k3 = k_s[sl, :].reshape(cpp, CHUNK, 2 * HEAD_DIM)
            flat = lambda a: a.reshape(piece, 2 * HEAD_DIM).astype(BF16)
            qd = (q * jnp.exp(b)).astype(BF16)
            qi = flat(q3 * jnp.exp(b3 - bm3))
            ki = flat(k3 * jnp.exp(bm3 - b3))
            ks = flat(k3 * jnp.exp(bl3 - b3))
            dec = jnp.exp(bl3)
            vt = vt_s[p]
            for h in range(2):
                hs = slice(h * HEAD_DIM, (h + 1) * HEAD_DIM)
                att = jnp.where(keep, _dot_nt(qi[:, hs], ki[:, hs]), 0.0).astype(BF16)
                vt_h = vt[hs, :]
                v_exp = jnp.concatenate(
                    [jnp.where(col_chunk == ci, vt_h, jnp.zeros_like(vt_h)) for ci in range(cpp)], axis=0)
                incr = _dot(v_exp, ks[:, hs])
                st = st_s[d, h]
                before = [None] * cpp
                for ci in (range(cpp) if d == 0 else range(cpp - 1, -1, -1)):
                    before[ci] = st.astype(BF16)
                    st = st * dec[ci, :, hs] + incr[ci * HEAD_DIM:(ci + 1) * HEAD_DIM, :]
                st_s[d, h] = st
                qd_h = qd[:, hs]
                q_exp = [jnp.where(row_chunk == ci, qd_h, jnp.zeros_like(qd_h)) for ci in range(cpp)]
                lhs = jnp.concatenate([att] + q_exp, axis=1)
                rhs_t = jnp.concatenate([vt_h] + before, axis=1)
                o_s[sl, hs] = _dot_nt(lhs, rhs_t)
            if per_piece_seq:
                store_state(p, d)
        return carry

    if not per_piece_seq:
        load_state(0, 0)
        load_state(0, 1)
    lax.fori_loop(0, n_pieces, piece_body, 0)
    if not per_piece_seq:
        store_state(0, 0)
        store_state(0, 1)

    o = of_s[...] + ob_s[...]
    gn = gn_ref[...]
    o = jnp.concatenate(
        [_rms(o[:, h * HEAD_DIM:(h + 1) * HEAD_DIM], gn[:, h * HEAD_DIM:(h + 1) * HEAD_DIM])
         for h in range(2)], axis=1)
    o_ref[...] = (o * _sigmoid(_dot(hm, wg_ref[...]))).astype(o_ref.dtype)


def _hgrn_scan(x, mod_l, mod_row0, mod_row_step, norm_w, w_in_bf, lb_f, lb_b, gnorm, s0, seq_len,
               zero_init):
    n_rows = x.shape[0]
    n_seq_total = n_rows // seq_len
    rows = SCAN_ROWS
    seq_per_step = rows // seq_len
    pair = 2 * HEAD_DIM
    n_pairs = HEADS // 2

    def w_spec(seg):
        return pl.BlockSpec((D_MODEL, pair), lambda sb, hp: (0, seg * n_pairs + hp))

    vec_spec = pl.BlockSpec((1, pair), lambda sb, hp: (0, hp))
    st_block = (seq_per_step, 2, 2, HEAD_DIM, HEAD_DIM)
    st_spec = pl.BlockSpec(st_block, lambda sb, hp: (sb, 0, hp, 0, 0))
    s0_spec = pl.BlockSpec(st_block, (lambda sb, hp: (0, 0, hp, 0, 0)) if zero_init
                           else (lambda sb, hp: (sb, 0, hp, 0, 0)))
    scratch = [
        pltpu.VMEM((rows, D_MODEL), BF16),
        pltpu.VMEM((rows, pair), F32),
        pltpu.VMEM((rows // SCAN_PIECE, pair, SCAN_PIECE), BF16),
        pltpu.VMEM((rows, pair), F32),
        pltpu.VMEM((rows, pair), F32),
        pltpu.VMEM((rows, pair), F32),
        pltpu.VMEM((rows, pair), F32),
        pltpu.VMEM((rows, pair), F32),
        pltpu.VMEM((rows, pair), F32),
        pltpu.VMEM((2, 2, HEAD_DIM, HEAD_DIM), F32),
    ]
    in_specs = [
        pl.BlockSpec((rows, D_MODEL), lambda sb, hp: (sb, 0), pipeline_mode=pl.Buffered(1)),
        pl.BlockSpec((None, 6, D_MODEL), lambda sb, hp: (mod_row0 + sb * mod_row_step, 0, 0)),
        pl.BlockSpec((1, D_MODEL), lambda sb, hp: (0, 0)),
        w_spec(0), w_spec(1), w_spec(2), w_spec(3), w_spec(4),
        vec_spec, vec_spec, vec_spec,
        s0_spec,
    ]
    args = [x, mod_l, norm_w.reshape(1, D_MODEL), w_in_bf, w_in_bf, w_in_bf, w_in_bf, w_in_bf,
            lb_f.reshape(1, D_MODEL), lb_b.reshape(1, D_MODEL), gnorm.reshape(1, D_MODEL), s0]
    return pl.pallas_call(
        functools.partial(_hgrn_kernel, seq_len=seq_len, zero_init=zero_init),
        grid=(n_rows // rows, n_pairs),
        in_specs=in_specs,
        out_specs=[
            pl.BlockSpec((rows, pair), lambda sb, hp: (sb, hp)),
            st_spec,
        ],
        out_shape=[
            jax.ShapeDtypeStruct((n_rows, D_MODEL), BF16),
            jax.ShapeDtypeStruct((n_seq_total, 2, HEADS, HEAD_DIM, HEAD_DIM), F32),
        ],
        scratch_shapes=scratch,
        compiler_params=_cparams("arbitrary", "arbitrary"),
        name="hgrn_scan",
    )(*args)


def _proj_res_kernel(row_ref, xp_ref, xs_ref, op_ref, os_ref, w_ref, mod_ref, xo_ref, *, prompt_tiles):
    del row_ref
    x = _pick_group(xp_ref, xs_ref, prompt_tiles)
    o = _pick_group(op_ref, os_ref, prompt_tiles)
    xo_ref[...] = x + mod_ref[2:3, :] * _dot(o, w_ref[...])


def _proj_residual(x_p, x_s, o_prompt, o_sample, w_out_bf, tile_row, mod_l):
    n = x_p.shape[0] + x_s.shape[0]
    tm = TOKEN_TILE
    pt = o_prompt.shape[0] // tm
    return pl.pallas_call(
        functools.partial(_proj_res_kernel, prompt_tiles=pt),
        grid_spec=pltpu.PrefetchScalarGridSpec(
            num_scalar_prefetch=1,
            grid=(n // tm,),
            in_specs=_two_group_specs(tm, pt) + _two_group_specs(tm, pt) + [
                pl.BlockSpec((D_MODEL, D_MODEL), lambda i, row: (0, 0)),
                pl.BlockSpec((None, 6, D_MODEL), lambda i, row: (row[i], 0, 0)),
            ],
            out_specs=pl.BlockSpec((tm, D_MODEL), lambda i, row: (i, 0)),
        ),
        out_shape=jax.ShapeDtypeStruct((n, D_MODEL), F32),
        compiler_params=_cparams("arbitrary"),
        name="proj_residual",
    )(tile_row, x_p, x_s, o_prompt, o_sample, w_out_bf, mod_l)


def _conv_kernel(row_ref, width_ref, x_ref, mod_ref, nw_ref, win_ref, cw_ref, wout_ref, xo_ref):
    del row_ref
    i = pl.program_id(0)
    x = x_ref[...]
    hm = (_rms(x, nw_ref[...]) * (1.0 + mod_ref[1:2, :]) + mod_ref[0:1, :]).astype(BF16)
    z = _dot(hm, win_ref[...])
    bg = z[:, :D_MODEL]
    u = z[:, D_MODEL:2 * D_MODEL] * z[:, 2 * D_MODEL:]
    tm = x.shape[0]
    pos = lax.broadcasted_iota(jnp.int32, (tm, 1), 0) & (width_ref[i] - 1)
    prev = jnp.where(pos == 0, 0.0, pltpu.roll(u, 1, axis=0))
    nxt = jnp.where(pos == width_ref[i] - 1, 0.0, pltpu.roll(u, tm - 1, axis=0))
    v = cw_ref[0:1, :] * prev + cw_ref[1:2, :] * u + cw_ref[2:3, :] * nxt
    y = _dot((bg * v).astype(BF16), wout_ref[...])
    xo_ref[...] = x + mod_ref[2:3, :] * y


def _conv_mixer(x, tile_row, tile_width, mod_l, norm_w, w_in_bf, conv_w, w_out_bf):
    n = x.shape[0]
    tm = TOKEN_TILE
    return pl.pallas_call(
        _conv_kernel,
        grid_spec=pltpu.PrefetchScalarGridSpec(
            num_scalar_prefetch=2,
            grid=(n // tm,),
            in_specs=[
                pl.BlockSpec((tm, D_MODEL), lambda i, row, w: (i, 0)),
                pl.BlockSpec((None, 6, D_MODEL), lambda i, row, w: (row[i], 0, 0)),
                pl.BlockSpec((1, D_MODEL), lambda i, row, w: (0, 0)),
                pl.BlockSpec((D_MODEL, 3 * D_MODEL), lambda i, row, w: (0, 0)),
                pl.BlockSpec((3, D_MODEL), lambda i, row, w: (0, 0)),
                pl.BlockSpec((D_MODEL, D_MODEL), lambda i, row, w: (0, 0)),
            ],
            out_specs=pl.BlockSpec((tm, D_MODEL), lambda i, row, w: (i, 0)),
        ),
        out_shape=jax.ShapeDtypeStruct((n, D_MODEL), F32),
        compiler_params=_cparams("arbitrary"),
        name="conv_mixer",
    )(tile_row, tile_width, x, mod_l, norm_w.reshape(1, D_MODEL), w_in_bf, conv_w, w_out_bf)


def _router_kernel(row_ref, x_ref, mod_ref, nw_ref, rwh_ref, rwl_ref, rb_ref,
                   hf_ref, pg_ref, pgt_ref, cnt_ref):
    del row_ref
    hf = _rms(x_ref[...], nw_ref[...]) * (1.0 + mod_ref[4:5, :]) + mod_ref[3:4, :]
    hf_hi = hf.astype(BF16)
    hf_ref[...] = hf_hi
    hf_lo = (hf - hf_hi.astype(F32)).astype(BF16)
    logits = (_dot(hf_hi, rwh_ref[...]) + (_dot(hf_hi, rwl_ref[...]) + _dot(hf_lo, rwh_ref[...]))
              + rb_ref[...])
    tm = hf.shape[0]
    lane = lax.broadcasted_iota(jnp.int32, (tm, LANES), 1)
    lane_f = lane.astype(F32)
    work = logits
    vals, hots = [], []
    for _ in range(TOP_K):
        m = jnp.max(work, axis=-1, keepdims=True)
        idx_f = jnp.min(jnp.where(work == m, lane_f, float(LANES)), axis=-1, keepdims=True)
        hot = lane_f == idx_f
        vals.append(m)
        hots.append(hot)
        work = jnp.where(hot, -jnp.inf, work)
    ex = [jnp.exp(v - vals[0]) for v in vals]
    inv = 1.0 / (ex[0] + ex[1] + ex[2] + ex[3])

    multi = jnp.where(hots[0] | hots[1] | hots[2] | hots[3], 1.0, 0.0)
    tr = lax.broadcasted_iota(jnp.int32, (tm, tm), 0)
    tc = lax.broadcasted_iota(jnp.int32, (tm, tm), 1)
    earlier = jnp.where(tc < tr, 1.0, 0.0).astype(BF16)
    before = _dot(earlier, multi.astype(BF16))
    cnt = jnp.sum(multi, axis=0, keepdims=True)
    er = lax.broadcasted_iota(jnp.int32, (LANES, LANES), 0)
    ec = lax.broadcasted_iota(jnp.int32, (LANES, LANES), 1)
    lower = jnp.where(er < ec, 1.0, 0.0).astype(BF16)
    cnt_al = jnp.floor((cnt + (SEG_ALIGN - 1)) * (1.0 / SEG_ALIGN)) * SEG_ALIGN
    estart = _dot(jnp.broadcast_to(cnt_al, (8, LANES)).astype(BF16), lower)[0:1, :]
    pos_all = before + estart

    pg = jnp.zeros((tm, LANES), F32)
    for k in range(TOP_K):
        pos = jnp.sum(jnp.where(hots[k], pos_all, 0.0), axis=-1, keepdims=True)
        pg = jnp.where(lane == k, pos, pg)
        pg = jnp.where(lane == TOP_K + k, ex[k] * inv, pg)
    pg_ref[...] = pg
    pgt_ref[...] = pg.T[0:8, :]
    cnt_ref[...] = cnt.astype(jnp.int32)


def _router(x, tile_row, mod_l, norm_w, router_w_pad, router_b_pad):
    n = x.shape[0]
    tm = ROUTE_TILE
    nt = n // tm
    rw_hi = router_w_pad.astype(BF16)
    rw_lo = (router_w_pad - rw_hi.astype(F32)).astype(BF16)
    return pl.pallas_call(
        _router_kernel,
        grid_spec=pltpu.PrefetchScalarGridSpec(
            num_scalar_prefetch=1,
            grid=(nt,),
            in_specs=[
                pl.BlockSpec((tm, D_MODEL), lambda i, row: (i, 0)),
                pl.BlockSpec((None, 6, D_MODEL), lambda i, row: (row[i], 0, 0)),
                pl.BlockSpec((1, D_MODEL), lambda i, row: (0, 0)),
                pl.BlockSpec((D_MODEL, LANES), lambda i, row: (0, 0)),
                pl.BlockSpec((D_MODEL, LANES), lambda i, row: (0, 0)),
                pl.BlockSpec((1, LANES), lambda i, row: (0, 0)),
            ],
            out_specs=[
                pl.BlockSpec((tm, D_MODEL), lambda i, row: (i, 0)),
                pl.BlockSpec((tm, LANES), lambda i, row: (i, 0)),
                pl.BlockSpec((None, 8, tm), lambda i, row: (i, 0, 0)),
                pl.BlockSpec((None, 1, LANES), lambda i, row: (i, 0, 0)),
            ],
        ),
        out_shape=[
            jax.ShapeDtypeStruct((n, D_MODEL), BF16),
            jax.ShapeDtypeStruct((n, LANES), F32),
            jax.ShapeDtypeStruct((nt, 8, tm), F32),
            jax.ShapeDtypeStruct((nt, 1, LANES), jnp.int32),
        ],
        compiler_params=_cparams("arbitrary"),
        name="router",
    )(tile_row, x, mod_l, norm_w.reshape(1, D_MODEL), rw_hi, rw_lo, router_b_pad)


SEG_STRIDE = 4 * N_EXPERTS


def _segment_table(tbl_ref, t, e):
    base = t * SEG_STRIDE
    al = lambda v: pl.multiple_of(v, SEG_ALIGN)
    return (al(tbl_ref[base + e]), al(tbl_ref[base + N_EXPERTS + e]),
            al(tbl_ref[base + 2 * N_EXPERTS + e]))


def _tile_rows_total(tbl_ref, t):
    return pl.multiple_of(tbl_ref[t * SEG_STRIDE + 3 * N_EXPERTS], SEG_ALIGN)


def _dispatch_kernel(tbl_ref, pad_ref, nu_ref, hf_ref, pgt_ref, xb_ref, stage_s, zero_s, sem, zsem,
                     *, n_blocks):
    t = pl.program_id(0)
    nt = pl.num_programs(0)
    slot = t % 2
    half = zero_s.shape[0]

    def zero_rows(start, rows):
        start = pl.multiple_of(start, SEG_ALIGN)
        rows = pl.multiple_of(rows, SEG_ALIGN)

        @pl.when(rows > 0)
        def _():
            pltpu.make_async_copy(zero_s.at[pl.ds(0, rows), :], xb_ref.at[pl.ds(start, rows), :],
                                  zsem).start()

    @pl.when(t == 0)
    def _():
        zero_s[...] = jnp.zeros_like(zero_s)

        def pad_body(e, carry):
            start = pad_ref[e]
            rows = pad_ref[N_EXPERTS + e]
            first = jnp.minimum(rows, half)
            zero_rows(start, first)
            zero_rows(start + half, rows - first)
            return carry

        lax.fori_loop(0, N_EXPERTS, pad_body, 0)

        def tail_body(b, carry):
            for part in range(EXPERT_TILE // half):
                zero_rows(b * EXPERT_TILE + part * half, half)
            return carry

        lax.fori_loop(nu_ref[0], n_blocks, tail_body, 0)

    def wait_tile(tile, sl):
        rows = _tile_rows_total(tbl_ref, tile)
        pltpu.make_async_copy(stage_s.at[sl, pl.ds(0, rows), :], xb_ref.at[pl.ds(0, rows), :],
                              sem.at[sl]).wait()

    @pl.when(t >= 2)
    def _():
        wait_tile(t - 2, slot)

    s_iota = lax.broadcasted_iota(jnp.int32, (SLOTS, ROUTE_TILE), 0).astype(F32)
    pgt = pgt_ref[...]
    hit = s_iota == pgt[0:1, :]
    for k in range(1, TOP_K):
        hit = hit | (s_iota == pgt[k:k + 1, :])
    perm = jnp.where(hit, 1.0, 0.0).astype(BF16)
    stage_s[slot] = _pack_pairs(_dot(perm, hf_ref[...]))

    def seg_body(e, carry):
        rows, src, dst = _segment_table(tbl_ref, t, e)

        @pl.when(rows > 0)
        def _():
            pltpu.make_async_copy(stage_s.at[slot, pl.ds(src, rows), :],
                                  xb_ref.at[pl.ds(dst, rows), :], sem.at[slot]).start()
        return carry

    lax.fori_loop(0, N_EXPERTS, seg_body, 0)

    @pl.when(t == nt - 1)
    def _():
        wait_tile(t, slot)

        @pl.when(nt > 1)
        def _():
            wait_tile(t - 1, 1 - slot)

        zeroed = pl.multiple_of(pad_ref[2 * N_EXPERTS], SEG_ALIGN)

        @pl.when(zeroed > 0)
        def _():
            pltpu.make_async_copy(xb_ref.at[pl.ds(0, zeroed), :], xb_ref.at[pl.ds(0, zeroed), :],
                                  zsem).wait()


def _dispatch(hf, pgt, seg_tbl, pad_tbl, n_used, n_blocks):
    n = hf.shape[0]
    tm = ROUTE_TILE
    return pl.pallas_call(
        functools.partial(_dispatch_kernel, n_blocks=n_blocks),
        grid_spec=pltpu.PrefetchScalarGridSpec(
            num_scalar_prefetch=3,
            grid=(n // tm,),
            in_specs=[
                pl.BlockSpec((tm, D_MODEL), lambda i, a, b, c: (i, 0)),
                pl.BlockSpec((None, 8, tm), lambda i, a, b, c: (i, 0, 0)),
            ],
            out_specs=pl.BlockSpec(memory_space=pl.ANY),
            scratch_shapes=[
                pltpu.VMEM((2, SLOTS, PACKED), jnp.uint32),
                pltpu.VMEM((EXPERT_TILE // 2, PACKED), jnp.uint32),
                pltpu.SemaphoreType.DMA((2,)),
                pltpu.SemaphoreType.DMA,
            ],
        ),
        out_shape=jax.ShapeDtypeStruct((n_blocks * EXPERT_TILE, PACKED), jnp.uint32),
        compiler_params=_cparams("arbitrary"),
        name="dispatch",
    )(seg_tbl, pad_tbl, n_used, hf, pgt)


def _expert_kernel(be_ref, nu_ref, nxt_ref, slot_ref, rows_ref, xb_ref, wgu_hbm, bgu_ref, wd_hbm, bd_ref, yb_ref,
                   wgu_f, wd_f, wgu_s, wd_s, sem, *, layer):
    i = pl.program_id(0)

    def fetch(e, slot):
        return (pltpu.make_async_copy(wgu_hbm.at[layer, e], wgu_f.at[slot], sem.at[slot]),
                pltpu.make_async_copy(wd_hbm.at[layer, e], wd_f.at[slot], sem.at[slot]))

    @pl.when(i < nu_ref[0])
    def _():
        e = be_ref[i]
        slot = slot_ref[i]
        first_block = jnp.logical_or(i == 0, e != be_ref[jnp.maximum(i - 1, 0)])

        @pl.when(first_block)
        def _():
            @pl.when(i == 0)
            def _():
                for cp in fetch(e, slot):
                    cp.start()

            for cp in fetch(e, slot):
                cp.wait()
            wgu_s[...] = wgu_f[slot].astype(BF16)
            wd_s[...] = wd_f[slot].astype(BF16)

            @pl.when(nxt_ref[i] >= 0)
            def _():
                for cp in fetch(nxt_ref[i], 1 - slot):
                    cp.start()

        def ffn(sl):
            gu = _dot(_unpack_pairs(xb_ref[sl, :]), wgu_s[...]) + bgu_ref[...]
            gate = jnp.minimum(gu[:, :D_FF], SWIGLU_LIMIT)
            up = jnp.clip(gu[:, D_FF:], -SWIGLU_LIMIT, SWIGLU_LIMIT)
            act = (up + 1.0) * gate * _sigmoid(SWIGLU_ALPHA * gate)
            y = _dot(act.astype(BF16), wd_s[...]) + bd_ref[...]
            yb_ref[sl, :] = _pack_pairs(y.astype(BF16).astype(F32))

        @pl.when(rows_ref[i] > EXPERT_SUB)
        def _():
            ffn(slice(0, EXPERT_TILE))

        @pl.when(rows_ref[i] <= EXPERT_SUB)
        def _():
            ffn(slice(0, EXPERT_SUB))
            yb_ref[EXPERT_SUB:, :] = jnp.zeros((EXPERT_TILE - EXPERT_SUB, PACKED), jnp.uint32)

    @pl.when(i >= nu_ref[0])
    def _():
        yb_ref[...] = jnp.zeros_like(yb_ref)


def _experts(xb, block_e, n_used, next_e, slot, block_rows, layer, w_gu, b_gu, w_down, b_down):
    te = EXPERT_TILE
    n_blocks = xb.shape[0] // te
    depth = w_gu.shape[0]

    def x_map(i, be, nu, *_):
        return (jnp.maximum(jnp.minimum(i, nu[0] - 1), 0), 0)

    def bias_spec(width):
        return pl.BlockSpec((None, None, 1, width), lambda i, be, *_: (layer, be[i], 0, 0))

    return pl.pallas_call(
        functools.partial(_expert_kernel, layer=layer),
        grid_spec=pltpu.PrefetchScalarGridSpec(
            num_scalar_prefetch=5,
            grid=(n_blocks,),
            in_specs=[
                pl.BlockSpec((te, PACKED), x_map),
                pl.BlockSpec(memory_space=pl.ANY),
                bias_spec(2 * D_FF),
                pl.BlockSpec(memory_space=pl.ANY),
                bias_spec(D_MODEL),
            ],
            out_specs=pl.BlockSpec((te, PACKED), lambda i, *_: (i, 0)),
            scratch_shapes=[
                pltpu.VMEM((2, D_MODEL, 2 * D_FF), F32),
                pltpu.VMEM((2, D_FF, D_MODEL), F32),
                pltpu.VMEM((D_MODEL, 2 * D_FF), BF16),
                pltpu.VMEM((D_FF, D_MODEL), BF16),
                pltpu.SemaphoreType.DMA((2,)),
            ],
        ),
        out_shape=jax.ShapeDtypeStruct(xb.shape, jnp.uint32),
        compiler_params=_cparams("arbitrary"),
        name="experts",
    )(block_e, n_used, next_e, slot, block_rows, xb, w_gu, b_gu.reshape(depth, N_EXPERTS, 1, 2 * D_FF), w_down,
      b_down.reshape(depth, N_EXPERTS, 1, D_MODEL))


def _combine_kernel(row_ref, tbl_ref, x_ref, pg_ref, mod_ref, fw_ref, yb_ref, *rest, split_tiles):
    del row_ref
    stage_s, sem = rest[-2:]
    t = pl.program_id(0)
    nt = pl.num_programs(0)
    slot = t % 2

    def fetch(tile, into):
        def seg_body(e, carry):
            rows, dst, src = _segment_table(tbl_ref, tile, e)

            @pl.when(rows > 0)
            def _():
                pltpu.make_async_copy(yb_ref.at[pl.ds(src, rows), :],
                                      stage_s.at[into, pl.ds(dst, rows), :], sem.at[into]).start()
            return carry

        lax.fori_loop(0, N_EXPERTS, seg_body, 0)

    @pl.when(t == 0)
    def _():
        stage_s[...] = jnp.zeros_like(stage_s)
        fetch(0, 0)

    @pl.when(t + 1 < nt)
    def _():
        fetch(t + 1, 1 - slot)

    fetched = _tile_rows_total(tbl_ref, t)
    pltpu.make_async_copy(yb_ref.at[pl.ds(0, fetched), :], stage_s.at[slot, pl.ds(0, fetched), :],
                          sem.at[slot]).wait()
    rows_sorted = _unpack_pairs(stage_s[slot])
    pg = pg_ref[...]
    s_iota = lax.broadcasted_iota(jnp.int32, (ROUTE_TILE, SLOTS), 1).astype(F32)
    sel = jnp.zeros((ROUTE_TILE, SLOTS), F32)
    for k in range(TOP_K):
        sel = jnp.where(s_iota == pg[:, k:k + 1], pg[:, TOP_K + k:TOP_K + k + 1], sel)
    y = _dot(sel.astype(BF16), rows_sorted)
    x = x_ref[...] + mod_ref[5:6, :] * y
    if split_tiles is None:
        rest[0][...] = x
    else:
        x = _rms(x, fw_ref[...])

        @pl.when(t < split_tiles)
        def _():
            rest[0][...] = x

        @pl.when(t >= split_tiles)
        def _():
            rest[1][...] = x


def _combine(x, pg, seg_tbl, yb, tile_row, mod_l, final_w, n_prompt=None):
    n = x.shape[0]
    tm = ROUTE_TILE
    if n_prompt is None:
        split = None
        out_specs = pl.BlockSpec((tm, D_MODEL), lambda i, row, tbl: (i, 0))
        out_shape = jax.ShapeDtypeStruct((n, D_MODEL), F32)
    else:
        split = n_prompt // tm
        out_specs = [
            pl.BlockSpec((tm, D_MODEL), lambda i, row, tbl: (jnp.minimum(i, split - 1), 0)),
            pl.BlockSpec((tm, D_MODEL), lambda i, row, tbl: (jnp.maximum(i - split, 0), 0)),
        ]
        out_shape = [jax.ShapeDtypeStruct((n_prompt, D_MODEL), F32),
                     jax.ShapeDtypeStruct((n - n_prompt, D_MODEL), F32)]
    return pl.pallas_call(
        functools.partial(_combine_kernel, split_tiles=split),
        grid_spec=pltpu.PrefetchScalarGridSpec(
            num_scalar_prefetch=2,
            grid=(n // tm,),
            in_specs=[
                pl.BlockSpec((tm, D_MODEL), lambda i, row, tbl: (i, 0)),
                pl.BlockSpec((tm, LANES), lambda i, row, tbl: (i, 0)),
                pl.BlockSpec((None, 6, D_MODEL), lambda i, row, tbl: (row[i], 0, 0)),
                pl.BlockSpec((1, D_MODEL), lambda i, row, tbl: (0, 0)),
                pl.BlockSpec(memory_space=pl.ANY),
            ],
            out_specs=out_specs,
            scratch_shapes=[pltpu.VMEM((2, SLOTS, PACKED), jnp.uint32), pltpu.SemaphoreType.DMA((2,))],
        ),
        out_shape=out_shape,
        compiler_params=_cparams("arbitrary"),
        name="combine",
    )(tile_row, seg_tbl, x, pg, mod_l, final_w.reshape(1, D_MODEL), yb)


def _moe(x, tile_row, mod_l, norm_w, router_w, router_b, layer, w_gu, b_gu, w_down, b_down,
         final_w, n_prompt=None):
    n = x.shape[0]
    rw = jnp.zeros((D_MODEL, LANES), F32).at[:, :N_EXPERTS].set(router_w)
    rb = jnp.full((1, LANES), -jnp.inf, F32).at[0, :N_EXPERTS].set(router_b)
    hf, pg, pgt, cnt = _router(x, tile_row, mod_l, norm_w, rw, rb)

    te = EXPERT_TILE
    nt = n // ROUTE_TILE
    n_blocks = (n * TOP_K + nt * N_EXPERTS * (SEG_ALIGN - 1)) // te + 1 + N_EXPERTS
    cnt = cnt[:, 0, :N_EXPERTS]
    cnt = (cnt + SEG_ALIGN - 1) // SEG_ALIGN * SEG_ALIGN
    total = jnp.sum(cnt, axis=0)
    blocks_e = (total + te - 1) // te
    block_end = jnp.cumsum(blocks_e)
    pstart = (block_end - blocks_e) * te
    n_used = block_end[-1]
    first_row = pstart[None, :] + jnp.cumsum(cnt, axis=0) - cnt
    tile_off = jnp.cumsum(cnt, axis=1) - cnt
    tile_total = jnp.broadcast_to(jnp.sum(cnt, axis=1, keepdims=True), cnt.shape)
    seg_tbl = jnp.concatenate([cnt, tile_off, first_row, tile_total], axis=1)
    seg_tbl = seg_tbl.reshape(-1).astype(jnp.int32)
    pad_rows = blocks_e * te - total
    zeroed = jnp.sum(pad_rows) + (n_blocks - n_used) * te
    pad_tbl = jnp.concatenate([pstart + total, pad_rows, zeroed[None]]).astype(jnp.int32)
    block_ids = jnp.arange(n_blocks, dtype=jnp.int32)
    clamped = jnp.minimum(block_ids, n_used - 1)
    block_e = jnp.sum((clamped[:, None] >= block_end[None, :]).astype(jnp.int32), axis=1)
    block_e = jnp.minimum(block_e, N_EXPERTS - 1).astype(jnp.int32)
    n_used = n_used.astype(jnp.int32).reshape(1)
    owns = (blocks_e > 0)[None, :]
    ids = jnp.arange(N_EXPERTS, dtype=jnp.int32)[None, :]
    mine = block_e[:, None]
    next_e = jnp.min(jnp.where((ids > mine) & owns, ids, N_EXPERTS), axis=1)
    next_e = jnp.where(next_e < N_EXPERTS, next_e, -1).astype(jnp.int32)
    slot = ((jnp.sum(((ids <= mine) & owns).astype(jnp.int32), axis=1) + 1) % 2).astype(jnp.int32)

    row_end = jnp.sum(jnp.where(ids == mine, (pstart + total)[None, :], 0), axis=1)
    block_rows = jnp.clip(row_end - block_ids * te, 0, te).astype(jnp.int32)

    xb = _dispatch(hf, pgt, seg_tbl, pad_tbl, n_used, n_blocks)
    yb = _experts(xb, block_e, n_used, next_e, slot, block_rows, layer, w_gu, b_gu, w_down, b_down)
    return _combine(x, pg, seg_tbl, yb, tile_row, mod_l, final_w, n_prompt)


def _tile_rows(n_prompt_tok, n_sample_seq, sample_len, tile):
    starts = np.arange(0, n_prompt_tok + n_sample_seq * sample_len, tile)
    row = np.where(starts < n_prompt_tok, 0, 1 + (starts - n_prompt_tok) // sample_len)
    return jnp.asarray(row, dtype=jnp.int32)


def kernel(x_prompt, x_sample, state_hgrn, c, c_ctx, w_mod, b_mod, norm_mix, norm_ffn, hg_w_in,
           hg_lb_logits, hg_gnorm, hg_w_out, cv_w_in, cv_w, cv_w_out, router_w, router_b,
           moe_w_gu, moe_b_gu, moe_w_down, moe_b_down, final_norm):
    bp, tp, d = x_prompt.shape
    bs, ts, _ = x_sample.shape
    depth = w_mod.shape[0]
    n_prompt = bp * tp
    n = n_prompt + bs * ts
    assert d == D_MODEL and depth == 2 and 1 + bs <= 16
    assert n_prompt % SCAN_ROWS == 0 and ts == SCAN_ROWS and tp == SCAN_PIECE

    x_p = x_prompt.reshape(n_prompt, d)
    x_s = x_sample.reshape(bs * ts, d)
    cond16 = jnp.zeros((16, d), F32).at[0].set(c_ctx).at[1:1 + bs].set(c)
    mod = _modulation(cond16, w_mod, b_mod).reshape(depth, 16, 6, d)

    tile_row = _tile_rows(n_prompt, bs, ts, TOKEN_TILE)
    tile_row_route = _tile_rows(n_prompt, bs, ts, ROUTE_TILE)
    starts = np.arange(0, n, TOKEN_TILE)
    tile_width = jnp.asarray(np.where(starts < n_prompt, tp, GRID_W), dtype=jnp.int32)

    lb_all = jnp.cumsum(jax.nn.softmax(hg_lb_logits.astype(F32), axis=1), axis=1)

    w_in_bf = hg_w_in[0].astype(BF16)
    gn = hg_gnorm[0].reshape(-1)
    zero_state = jnp.zeros((SCAN_ROWS // tp, 2, HEADS, HEAD_DIM, HEAD_DIM), F32)
    o_p, s_new = _hgrn_scan(x_p, mod[0], 0, 0, norm_mix[0], w_in_bf, lb_all[0, 0], lb_all[1, 0], gn,
                            zero_state, tp, True)
    o_s, _ = _hgrn_scan(x_s, mod[0], 1, 1, norm_mix[0], w_in_bf, lb_all[0, 0], lb_all[1, 0], gn,
                        state_hgrn[:, 0], ts, False)
    x = _proj_residual(x_p, x_s, o_p, o_s, hg_w_out[0].astype(BF16), tile_row, mod[0])
    x = _moe(x, tile_row_route, mod[0], norm_ffn[0], router_w[0], router_b[0],
             0, moe_w_gu, moe_b_gu, moe_w_down, moe_b_down, final_norm)

    x = _conv_mixer(x, tile_row, tile_width, mod[1], norm_mix[1], cv_w_in[0].astype(BF16), cv_w[0],
                    cv_w_out[0].astype(BF16))
    y_p, y_s = _moe(x, tile_row_route, mod[1], norm_ffn[1], router_w[1], router_b[1],
                    1, moe_w_gu, moe_b_gu, moe_w_down, moe_b_down, final_norm, n_prompt)

    y_prompt = y_p.reshape(bp, tp, d)
    y_sample = y_s.reshape(bs, ts, d)
    return (y_prompt, y_sample, s_new.reshape(bp, 1, 2, HEADS, HEAD_DIM, HEAD_DIM))
```

```python
import functools

import numpy as np
import jax
import jax.numpy as jnp
from jax import lax
from jax.experimental import pallas as pl
from jax.experimental.pallas import tpu as pltpu

F32 = jnp.float32
BF16 = jnp.bfloat16

D_MODEL = 1024
HEADS = 8
HEAD_DIM = 128
CHUNK = 64
GRID_W = 64
N_EXPERTS = 32
TOP_K = 4
D_FF = 1024
SWIGLU_LIMIT = 7.0
SWIGLU_ALPHA = 1.702
EPS = 1e-6

LANES = 128
SCAN_ROWS = 2048
SCAN_PIECE = 256
TOKEN_TILE = 512
EXPERT_TILE = 512
EXPERT_SUB = 256
ROUTE_TILE = 256
SEG_ALIGN = 8
SLOTS = ROUTE_TILE * TOP_K + N_EXPERTS * SEG_ALIGN
VMEM_LIMIT = 56 * 1024 * 1024


def _cparams(*sem):
    return pltpu.CompilerParams(dimension_semantics=sem, vmem_limit_bytes=VMEM_LIMIT)


def _sigmoid(x):
    return 1.0 / (1.0 + jnp.exp(-x))


def _rms(x, w):
    return x * lax.rsqrt(jnp.mean(x * x, axis=-1, keepdims=True) + EPS) * w


def _dot(a, b):
    return jnp.dot(a, b, preferred_element_type=F32)


def _dot_nt(a, b):
    return lax.dot_general(a, b, (((1,), (1,)), ((), ())), preferred_element_type=F32)


def _dot_tn(a, b):
    return lax.dot_general(a, b, (((0,), (0,)), ((), ())), preferred_element_type=F32)


PACKED = D_MODEL // 2
_HIGH_HALF = 0xFFFF0000


def _pack_pairs(x):
    c = x.shape[1] // 2
    lo = lax.bitcast_convert_type(x[:, :c], jnp.uint32) >> 16
    hi = lax.bitcast_convert_type(x[:, c:], jnp.uint32) & jnp.uint32(_HIGH_HALF)
    return hi | lo


def _unpack_pairs(u):
    lo = lax.bitcast_convert_type(u << 16, F32)
    hi = lax.bitcast_convert_type(u & jnp.uint32(_HIGH_HALF), F32)
    return jnp.concatenate([lo, hi], axis=1).astype(BF16)


def _mod_kernel(cond_ref, w_ref, b_ref, o_ref):
    c = cond_ref[...]
    s = (c * _sigmoid(c)).astype(BF16)
    o_ref[...] = _dot(s, w_ref[...].astype(BF16)) + b_ref[...]


def _modulation(cond16, w_mod, b_mod):
    depth = w_mod.shape[0]
    n_out = w_mod.shape[2]
    tn = 1024
    return pl.pallas_call(
        _mod_kernel,
        grid=(depth, n_out // tn),
        in_specs=[
            pl.BlockSpec((16, D_MODEL), lambda l, j: (0, 0)),
            pl.BlockSpec((None, D_MODEL, tn), lambda l, j: (l, 0, j)),
            pl.BlockSpec((None, 1, tn), lambda l, j: (l, 0, j)),
        ],
        out_specs=pl.BlockSpec((None, 16, tn), lambda l, j: (l, 0, j)),
        out_shape=jax.ShapeDtypeStruct((depth, 16, n_out), F32),
        compiler_params=_cparams("arbitrary", "arbitrary"),
        name="modulation",
    )(cond16, w_mod, b_mod.reshape(depth, 1, n_out))


def _two_group_specs(tm, prompt_tiles):
    return [
        pl.BlockSpec((tm, D_MODEL), lambda i, *_: (jnp.minimum(i, prompt_tiles - 1), 0)),
        pl.BlockSpec((tm, D_MODEL), lambda i, *_: (jnp.maximum(i - prompt_tiles, 0), 0)),
    ]


def _pick_group(a_ref, b_ref, prompt_tiles):
    return jnp.where(pl.program_id(0) < prompt_tiles, a_ref[...], b_ref[...])


def _hgrn_kernel(x_ref, mod_ref, nw_ref, wq_ref, wff_ref, wfb_ref, wv_ref, wg_ref, lbf_ref, lbb_ref,
                 gn_ref, s0_ref, o_ref, so_ref, hm_s, q_s, vt_s, kf_s, lf_s, kb_s, lb_s, of_s, ob_s, st_s,
                 *, seq_len, zero_init):
    rows = x_ref.shape[0]
    piece = SCAN_PIECE
    n_pieces = rows // piece
    cpp = piece // CHUNK
    per_piece_seq = seq_len == piece
    assert per_piece_seq or seq_len == rows

    @pl.when(pl.program_id(1) == 0)
    def _():
        y = _rms(x_ref[...], nw_ref[...])
        hm_s[...] = (y * (1.0 + mod_ref[1:2, :]) + mod_ref[0:1, :]).astype(BF16)

    hm = hm_s[...]

    zq = _dot(hm, wq_ref[...])
    q_s[...] = zq * _sigmoid(zq)
    v = _dot(hm, wv_ref[...])
    for p in range(n_pieces):
        vt_s[p] = v[p * piece:(p + 1) * piece, :].T.astype(BF16)
    for w_ref, lbr, k_s, l_s in ((wff_ref, lbf_ref, kf_s, lf_s), (wfb_ref, lbb_ref, kb_s, lb_s)):
        lb = lbr[...]
        f = lb + (1.0 - lb) * _sigmoid(_dot(hm, w_ref[...]))
        k_s[...] = 1.0 - f
        l_s[...] = jnp.log(f)

    def load_state(j, d):
        for h in range(2):
            if zero_init:
                st_s[d, h] = jnp.zeros((HEAD_DIM, HEAD_DIM), F32)
            else:
                st_s[d, h] = s0_ref[j, d, h].T

    def store_state(j, d):
        for h in range(2):
            so_ref[j, d, h] = st_s[d, h].T

    def piece_body(i, carry):
        r = lax.broadcasted_iota(jnp.int32, (piece, piece), 0)
        c = lax.broadcasted_iota(jnp.int32, (piece, piece), 1)
        same = (r // CHUNK) == (c // CHUNK)
        row_chunk = lax.broadcasted_iota(jnp.int32, (piece, 1), 0) // CHUNK
        col_chunk = lax.broadcasted_iota(jnp.int32, (1, piece), 1) // CHUNK
        for d in range(2):
            k_s, l_s, o_s = ((kf_s, lf_s, of_s), (kb_s, lb_s, ob_s))[d]
            p = i if d == 0 else n_pieces - 1 - i
            if per_piece_seq:
                load_state(p, d)
            sl = pl.ds(pl.multiple_of(p * piece, piece), piece)
            keep = same & ((c <= r) if d == 0 else (c >= r))
            tri = jnp.where(keep, 1.0, 0.0).astype(BF16)
            lf = l_s[sl, :]
            hi = lf.astype(BF16)
            mid = (lf - hi.astype(F32)).astype(BF16)
            b = _dot(tri, hi) + _dot(tri, mid)
            b3 = b.reshape(cpp, CHUNK, 2 * HEAD_DIM)
            edge = CHUNK - 1 if d == 0 else 0
            bl3 = b3[:, edge:edge + 1, :]
            centre = CHUNK // 2 - 1 if d == 0 else CHUNK // 2
            bm3 = b3[:, centre:centre + 1, :]
            q = q_s[sl, :]
            q3 = q.reshape(cpp, CHUNK, 2 * HEAD_DIM)
            k3 = k_s[sl, :].reshape(cpp, CHUNK, 2 * HEAD_DIM)
            flat = lambda a: a.reshape(piece, 2 * HEAD_DIM).astype(BF16)
            qd = (q * jnp.exp(b)).astype(BF16)
            qi = flat(q3 * jnp.exp(b3 - bm3))
            ki = flat(k3 * jnp.exp(bm3 - b3))
            ks = flat(k3 * jnp.exp(bl3 - b3))
            dec = jnp.exp(bl3)
            vt = vt_s[p]
            for h in range(2):
                hs = slice(h * HEAD_DIM, (h + 1) * HEAD_DIM)
                att = jnp.where(keep, _dot_nt(qi[:, hs], ki[:, hs]), 0.0).astype(BF16)
                vt_h = vt[hs, :]
                v_exp = jnp.concatenate(
                    [jnp.where(col_chunk == ci, vt_h, jnp.zeros_like(vt_h)) for ci in range(cpp)], axis=0)
                incr = _dot(v_exp, ks[:, hs])
                st = st_s[d, h]
                before = [None] * cpp
                for ci in (range(cpp) if d == 0 else range(cpp - 1, -1, -1)):
                    before[ci] = st.astype(BF16)
                    st = st * dec[ci, :, hs] + incr[ci * HEAD_DIM:(ci + 1) * HEAD_DIM, :]
                st_s[d, h] = st
                qd_h = qd[:, hs]
                q_exp = [jnp.where(row_chunk == ci, qd_h, jnp.zeros_like(qd_h)) for ci in range(cpp)]
                lhs = jnp.concatenate([att] + q_exp, axis=1)
                rhs_t = jnp.concatenate([vt_h] + before, axis=1)
                o_s[sl, hs] = _dot_nt(lhs, rhs_t)
            if per_piece_seq:
                store_state(p, d)
        return carry

    if not per_piece_seq:
        load_state(0, 0)
        load_state(0, 1)
    lax.fori_loop(0, n_pieces, piece_body, 0, unroll=4)
    if not per_piece_seq:
        store_state(0, 0)
        store_state(0, 1)

    o = of_s[...] + ob_s[...]
    gn = gn_ref[...]
    o = jnp.concatenate(
        [_rms(o[:, h * HEAD_DIM:(h + 1) * HEAD_DIM], gn[:, h * HEAD_DIM:(h + 1) * HEAD_DIM])
         for h in range(2)], axis=1)
    o_ref[...] = (o * _sigmoid(_dot(hm, wg_ref[...]))).astype(o_ref.dtype)


def _hgrn_scan(x, mod_l, mod_row0, mod_row_step, norm_w, w_in_bf, lb_f, lb_b, gnorm, s0, seq_len,
               zero_init):
    n_rows = x.shape[0]
    n_seq_total = n_rows // seq_len
    rows = SCAN_ROWS
    seq_per_step = rows // seq_len
    pair = 2 * HEAD_DIM
    n_pairs = HEADS // 2

    def w_spec(seg):
        return pl.BlockSpec((D_MODEL, pair), lambda sb, hp: (0, seg * n_pairs + hp))

    vec_spec = pl.BlockSpec((1, pair), lambda sb, hp: (0, hp))
    st_block = (seq_per_step, 2, 2, HEAD_DIM, HEAD_DIM)
    st_spec = pl.BlockSpec(st_block, lambda sb, hp: (sb, 0, hp, 0, 0))
    s0_spec = pl.BlockSpec(st_block, (lambda sb, hp: (0, 0, hp, 0, 0)) if zero_init
                           else (lambda sb, hp: (sb, 0, hp, 0, 0)))
    scratch = [
        pltpu.VMEM((rows, D_MODEL), BF16),
        pltpu.VMEM((rows, pair), F32),
        pltpu.VMEM((rows // SCAN_PIECE, pair, SCAN_PIECE), BF16),
        pltpu.VMEM((rows, pair), F32),
        pltpu.VMEM((rows, pair), F32),
        pltpu.VMEM((rows, pair), F32),
        pltpu.VMEM((rows, pair), F32),
        pltpu.VMEM((rows, pair), F32),
        pltpu.VMEM((rows, pair), F32),
        pltpu.VMEM((2, 2, HEAD_DIM, HEAD_DIM), F32),
    ]
    in_specs = [
        pl.BlockSpec((rows, D_MODEL), lambda sb, hp: (sb, 0), pipeline_mode=pl.Buffered(1)),
        pl.BlockSpec((None, 6, D_MODEL), lambda sb, hp: (mod_row0 + sb * mod_row_step, 0, 0)),
        pl.BlockSpec((1, D_MODEL), lambda sb, hp: (0, 0)),
        w_spec(0), w_spec(1), w_spec(2), w_spec(3), w_spec(4),
        vec_spec, vec_spec, vec_spec,
        s0_spec,
    ]
    args = [x, mod_l, norm_w.reshape(1, D_MODEL), w_in_bf, w_in_bf, w_in_bf, w_in_bf, w_in_bf,
            lb_f.reshape(1, D_MODEL), lb_b.reshape(1, D_MODEL), gnorm.reshape(1, D_MODEL), s0]
    return pl.pallas_call(
        functools.partial(_hgrn_kernel, seq_len=seq_len, zero_init=zero_init),
        grid=(n_rows // rows, n_pairs),
        in_specs=in_specs,
        out_specs=[
            pl.BlockSpec((rows, pair), lambda sb, hp: (sb, hp)),
            st_spec,
        ],
        out_shape=[
            jax.ShapeDtypeStruct((n_rows, D_MODEL), BF16),
            jax.ShapeDtypeStruct((n_seq_total, 2, HEADS, HEAD_DIM, HEAD_DIM), F32),
        ],
        scratch_shapes=scratch,
        compiler_params=_cparams("arbitrary", "arbitrary"),
        name="hgrn_scan",
    )(*args)


def _proj_res_kernel(row_ref, xp_ref, xs_ref, op_ref, os_ref, w_ref, mod_ref, xo_ref, *, prompt_tiles):
    del row_ref
    x = _pick_group(xp_ref, xs_ref, prompt_tiles)
    o = _pick_group(op_ref, os_ref, prompt_tiles)
    xo_ref[...] = x + mod_ref[2:3, :] * _dot(o, w_ref[...])


def _proj_residual(x_p, x_s, o_prompt, o_sample, w_out_bf, tile_row, mod_l):
    n = x_p.shape[0] + x_s.shape[0]
    tm = TOKEN_TILE
    pt = o_prompt.shape[0] // tm
    return pl.pallas_call(
        functools.partial(_proj_res_kernel, prompt_tiles=pt),
        grid_spec=pltpu.PrefetchScalarGridSpec(
            num_scalar_prefetch=1,
            grid=(n // tm,),
            in_specs=_two_group_specs(tm, pt) + _two_group_specs(tm, pt) + [
                pl.BlockSpec((D_MODEL, D_MODEL), lambda i, row: (0, 0)),
                pl.BlockSpec((None, 6, D_MODEL), lambda i, row: (row[i], 0, 0)),
            ],
            out_specs=pl.BlockSpec((tm, D_MODEL), lambda i, row: (i, 0)),
        ),
        out_shape=jax.ShapeDtypeStruct((n, D_MODEL), F32),
        compiler_params=_cparams("arbitrary"),
        name="proj_residual",
    )(tile_row, x_p, x_s, o_prompt, o_sample, w_out_bf, mod_l)


def _conv_kernel(row_ref, width_ref, x_ref, mod_ref, nw_ref, win_ref, cw_ref, wout_ref, xo_ref):
    del row_ref
    i = pl.program_id(0)
    x = x_ref[...]
    hm = (_rms(x, nw_ref[...]) * (1.0 + mod_ref[1:2, :]) + mod_ref[0:1, :]).astype(BF16)
    z = _dot(hm, win_ref[...])
    bg = z[:, :D_MODEL]
    u = z[:, D_MODEL:2 * D_MODEL] * z[:, 2 * D_MODEL:]
    tm = x.shape[0]
    pos = lax.broadcasted_iota(jnp.int32, (tm, 1), 0) & (width_ref[i] - 1)
    prev = jnp.where(pos == 0, 0.0, pltpu.roll(u, 1, axis=0))
    nxt = jnp.where(pos == width_ref[i] - 1, 0.0, pltpu.roll(u, tm - 1, axis=0))
    v = cw_ref[0:1, :] * prev + cw_ref[1:2, :] * u + cw_ref[2:3, :] * nxt
    y = _dot((bg * v).astype(BF16), wout_ref[...])
    xo_ref[...] = x + mod_ref[2:3, :] * y


def _conv_mixer(x, tile_row, tile_width, mod_l, norm_w, w_in_bf, conv_w, w_out_bf):
    n = x.shape[0]
    tm = TOKEN_TILE
    return pl.pallas_call(
        _conv_kernel,
        grid_spec=pltpu.PrefetchScalarGridSpec(
            num_scalar_prefetch=2,
            grid=(n // tm,),
            in_specs=[
                pl.BlockSpec((tm, D_MODEL), lambda i, row, w: (i, 0)),
                pl.BlockSpec((None, 6, D_MODEL), lambda i, row, w: (row[i], 0, 0)),
                pl.BlockSpec((1, D_MODEL), lambda i, row, w: (0, 0)),
                pl.BlockSpec((D_MODEL, 3 * D_MODEL), lambda i, row, w: (0, 0)),
                pl.BlockSpec((3, D_MODEL), lambda i, row, w: (0, 0)),
                pl.BlockSpec((D_MODEL, D_MODEL), lambda i, row, w: (0, 0)),
            ],
            out_specs=pl.BlockSpec((tm, D_MODEL), lambda i, row, w: (i, 0)),
        ),
        out_shape=jax.ShapeDtypeStruct((n, D_MODEL), F32),
        compiler_params=_cparams("arbitrary"),
        name="conv_mixer",
    )(tile_row, tile_width, x, mod_l, norm_w.reshape(1, D_MODEL), w_in_bf, conv_w, w_out_bf)


def _router_kernel(row_ref, x_ref, mod_ref, nw_ref, rwh_ref, rwl_ref, rb_ref,
                   hf_ref, pg_ref, pgt_ref, cnt_ref):
    del row_ref
    hf = _rms(x_ref[...], nw_ref[...]) * (1.0 + mod_ref[4:5, :]) + mod_ref[3:4, :]
    hf_hi = hf.astype(BF16)
    hf_ref[...] = hf_hi
    hf_lo = (hf - hf_hi.astype(F32)).astype(BF16)
    logits = (_dot(hf_hi, rwh_ref[...]) + (_dot(hf_hi, rwl_ref[...]) + _dot(hf_lo, rwh_ref[...]))
              + rb_ref[...])
    tm = hf.shape[0]
    lane = lax.broadcasted_iota(jnp.int32, (tm, LANES), 1)
    lane_f = lane.astype(F32)
    work = logits
    vals, hots = [], []
    for _ in range(TOP_K):
        m = jnp.max(work, axis=-1, keepdims=True)
        idx_f = jnp.min(jnp.where(work == m, lane_f, float(LANES)), axis=-1, keepdims=True)
        hot = lane_f == idx_f
        vals.append(m)
        hots.append(hot)
        work = jnp.where(hot, -jnp.inf, work)
    ex = [jnp.exp(v - vals[0]) for v in vals]
    inv = 1.0 / (ex[0] + ex[1] + ex[2] + ex[3])

    multi = jnp.where(hots[0] | hots[1] | hots[2] | hots[3], 1.0, 0.0)
    tr = lax.broadcasted_iota(jnp.int32, (tm, tm), 0)
    tc = lax.broadcasted_iota(jnp.int32, (tm, tm), 1)
    earlier = jnp.where(tc < tr, 1.0, 0.0).astype(BF16)
    before = _dot(earlier, multi.astype(BF16))
    cnt = jnp.sum(multi, axis=0, keepdims=True)
    er = lax.broadcasted_iota(jnp.int32, (LANES, LANES), 0)
    ec = lax.broadcasted_iota(jnp.int32, (LANES, LANES), 1)
    lower = jnp.where(er < ec, 1.0, 0.0).astype(BF16)
    cnt_al = jnp.floor((cnt + (SEG_ALIGN - 1)) * (1.0 / SEG_ALIGN)) * SEG_ALIGN
    estart = _dot(jnp.broadcast_to(cnt_al, (8, LANES)).astype(BF16), lower)[0:1, :]
    pos_all = before + estart

    pg = jnp.zeros((tm, LANES), F32)
    for k in range(TOP_K):
        pos = jnp.sum(jnp.where(hots[k], pos_all, 0.0), axis=-1, keepdims=True)
        pg = jnp.where(lane == k, pos, pg)
        pg = jnp.where(lane == TOP_K + k, ex[k] * inv, pg)
    pg_ref[...] = pg
    pgt_ref[...] = pg.T[0:8, :]
    cnt_ref[...] = cnt.astype(jnp.int32)


def _router(x, tile_row, mod_l, norm_w, router_w_pad, router_b_pad):
    n = x.shape[0]
    tm = ROUTE_TILE
    nt = n // tm
    rw_hi = router_w_pad.astype(BF16)
    rw_lo = (router_w_pad - rw_hi.astype(F32)).astype(BF16)
    return pl.pallas_call(
        _router_kernel,
        grid_spec=pltpu.PrefetchScalarGridSpec(
            num_scalar_prefetch=1,
            grid=(nt,),
            in_specs=[
                pl.BlockSpec((tm, D_MODEL), lambda i, row: (i, 0)),
                pl.BlockSpec((None, 6, D_MODEL), lambda i, row: (row[i], 0, 0)),
                pl.BlockSpec((1, D_MODEL), lambda i, row: (0, 0)),
                pl.BlockSpec((D_MODEL, LANES), lambda i, row: (0, 0)),
                pl.BlockSpec((D_MODEL, LANES), lambda i, row: (0, 0)),
                pl.BlockSpec((1, LANES), lambda i, row: (0, 0)),
            ],
            out_specs=[
                pl.BlockSpec((tm, D_MODEL), lambda i, row: (i, 0)),
                pl.BlockSpec((tm, LANES), lambda i, row: (i, 0)),
                pl.BlockSpec((None, 8, tm), lambda i, row: (i, 0, 0)),
                pl.BlockSpec((None, 1, LANES), lambda i, row: (i, 0, 0)),
            ],
        ),
        out_shape=[
            jax.ShapeDtypeStruct((n, D_MODEL), BF16),
            jax.ShapeDtypeStruct((n, LANES), F32),
            jax.ShapeDtypeStruct((nt, 8, tm), F32),
            jax.ShapeDtypeStruct((nt, 1, LANES), jnp.int32),
        ],
        compiler_params=_cparams("arbitrary"),
        name="router",
    )(tile_row, x, mod_l, norm_w.reshape(1, D_MODEL), rw_hi, rw_lo, router_b_pad)


SEG_STRIDE = 4 * N_EXPERTS


def _segment_table(tbl_ref, t, e):
    base = t * SEG_STRIDE
    al = lambda v: pl.multiple_of(v, SEG_ALIGN)
    return (al(tbl_ref[base + e]), al(tbl_ref[base + N_EXPERTS + e]),
            al(tbl_ref[base + 2 * N_EXPERTS + e]))


def _tile_rows_total(tbl_ref, t):
    return pl.multiple_of(tbl_ref[t * SEG_STRIDE + 3 * N_EXPERTS], SEG_ALIGN)


def _dispatch_kernel(tbl_ref, pad_ref, nu_ref, hf_ref, pgt_ref, xb_ref, stage_s, zero_s, sem, zsem,
                     *, n_blocks):
    t = pl.program_id(0)
    nt = pl.num_programs(0)
    slot = t % 2
    half = zero_s.shape[0]

    def zero_rows(start, rows):
        start = pl.multiple_of(start, SEG_ALIGN)
        rows = pl.multiple_of(rows, SEG_ALIGN)

        @pl.when(rows > 0)
        def _():
            pltpu.make_async_copy(zero_s.at[pl.ds(0, rows), :], xb_ref.at[pl.ds(start, rows), :],
                                  zsem).start()

    @pl.when(t == 0)
    def _():
        zero_s[...] = jnp.zeros_like(zero_s)

        def pad_body(e, carry):
            start = pad_ref[e]
            rows = pad_ref[N_EXPERTS + e]
            first = jnp.minimum(rows, half)
            zero_rows(start, first)
            zero_rows(start + half, rows - first)
            return carry

        lax.fori_loop(0, N_EXPERTS, pad_body, 0)

        def tail_body(b, carry):
            for part in range(EXPERT_TILE // half):
                zero_rows(b * EXPERT_TILE + part * half, half)
            return carry

        lax.fori_loop(nu_ref[0], n_blocks, tail_body, 0)

    def wait_tile(tile, sl):
        rows = _tile_rows_total(tbl_ref, tile)
        pltpu.make_async_copy(stage_s.at[sl, pl.ds(0, rows), :], xb_ref.at[pl.ds(0, rows), :],
                              sem.at[sl]).wait()

    @pl.when(t >= 2)
    def _():
        wait_tile(t - 2, slot)

    s_iota = lax.broadcasted_iota(jnp.int32, (SLOTS, ROUTE_TILE), 0).astype(F32)
    pgt = pgt_ref[...]
    hit = s_iota == pgt[0:1, :]
    for k in range(1, TOP_K):
        hit = hit | (s_iota == pgt[k:k + 1, :])
    perm = jnp.where(hit, 1.0, 0.0).astype(BF16)
    stage_s[slot] = _pack_pairs(_dot(perm, hf_ref[...]))

    def seg_body(e, carry):
        rows, src, dst = _segment_table(tbl_ref, t, e)

        @pl.when(rows > 0)
        def _():
            pltpu.make_async_copy(stage_s.at[slot, pl.ds(src, rows), :],
                                  xb_ref.at[pl.ds(dst, rows), :], sem.at[slot]).start()
        return carry

    lax.fori_loop(0, N_EXPERTS, seg_body, 0)

    @pl.when(t == nt - 1)
    def _():
        wait_tile(t, slot)

        @pl.when(nt > 1)
        def _():
            wait_tile(t - 1, 1 - slot)

        zeroed = pl.multiple_of(pad_ref[2 * N_EXPERTS], SEG_ALIGN)

        @pl.when(zeroed > 0)
        def _():
            pltpu.make_async_copy(xb_ref.at[pl.ds(0, zeroed), :], xb_ref.at[pl.ds(0, zeroed), :],
                                  zsem).wait()


def _dispatch(hf, pgt, seg_tbl, pad_tbl, n_used, n_blocks):
    n = hf.shape[0]
    tm = ROUTE_TILE
    return pl.pallas_call(
        functools.partial(_dispatch_kernel, n_blocks=n_blocks),
        grid_spec=pltpu.PrefetchScalarGridSpec(
            num_scalar_prefetch=3,
            grid=(n // tm,),
            in_specs=[
                pl.BlockSpec((tm, D_MODEL), lambda i, a, b, c: (i, 0)),
                pl.BlockSpec((None, 8, tm), lambda i, a, b, c: (i, 0, 0)),
            ],
            out_specs=pl.BlockSpec(memory_space=pl.ANY),
            scratch_shapes=[
                pltpu.VMEM((2, SLOTS, PACKED), jnp.uint32),
                pltpu.VMEM((EXPERT_TILE // 2, PACKED), jnp.uint32),
                pltpu.SemaphoreType.DMA((2,)),
                pltpu.SemaphoreType.DMA,
            ],
        ),
        out_shape=jax.ShapeDtypeStruct((n_blocks * EXPERT_TILE, PACKED), jnp.uint32),
        compiler_params=_cparams("arbitrary"),
        name="dispatch",
    )(seg_tbl, pad_tbl, n_used, hf, pgt)


def _expert_kernel(be_ref, nu_ref, nxt_ref, slot_ref, rows_ref, xb_ref, wgu_hbm, bgu_ref, wd_hbm, bd_ref, yb_ref,
                   wgu_f, wd_f, wgu_s, wd_s, sem, *, layer):
    i = pl.program_id(0)

    def fetch(e, slot):
        return (pltpu.make_async_copy(wgu_hbm.at[layer, e], wgu_f.at[slot], sem.at[slot]),
                pltpu.make_async_copy(wd_hbm.at[layer, e], wd_f.at[slot], sem.at[slot]))

    @pl.when(i < nu_ref[0])
    def _():
        e = be_ref[i]
        slot = slot_ref[i]
        first_block = jnp.logical_or(i == 0, e != be_ref[jnp.maximum(i - 1, 0)])

        @pl.when(first_block)
        def _():
            @pl.when(i == 0)
            def _():
                for cp in fetch(e, slot):
                    cp.start()

            for cp in fetch(e, slot):
                cp.wait()
            wgu_s[...] = wgu_f[slot].astype(BF16)
            wd_s[...] = wd_f[slot].astype(BF16)

            @pl.when(nxt_ref[i] >= 0)
            def _():
                for cp in fetch(nxt_ref[i], 1 - slot):
                    cp.start()

        def ffn(sl):
            gu = _dot(_unpack_pairs(xb_ref[sl, :]), wgu_s[...]) + bgu_ref[...]
            gate = jnp.minimum(gu[:, :D_FF], SWIGLU_LIMIT)
            up = jnp.clip(gu[:, D_FF:], -SWIGLU_LIMIT, SWIGLU_LIMIT)
            act = (up + 1.0) * gate * _sigmoid(SWIGLU_ALPHA * gate)
            y = _dot(act.astype(BF16), wd_s[...]) + bd_ref[...]
            yb_ref[sl, :] = _pack_pairs(y.astype(BF16).astype(F32))

        @pl.when(rows_ref[i] > EXPERT_SUB)
        def _():
            ffn(slice(0, EXPERT_TILE))

        @pl.when(rows_ref[i] <= EXPERT_SUB)
        def _():
            ffn(slice(0, EXPERT_SUB))
            yb_ref[EXPERT_SUB:, :] = jnp.zeros((EXPERT_TILE - EXPERT_SUB, PACKED), jnp.uint32)

    @pl.when(i >= nu_ref[0])
    def _():
        yb_ref[...] = jnp.zeros_like(yb_ref)


def _experts(xb, block_e, n_used, next_e, slot, block_rows, layer, w_gu, b_gu, w_down, b_down):
    te = EXPERT_TILE
    n_blocks = xb.shape[0] // te
    depth = w_gu.shape[0]

    def x_map(i, be, nu, *_):
        return (jnp.maximum(jnp.minimum(i, nu[0] - 1), 0), 0)

    def bias_spec(width):
        return pl.BlockSpec((None, None, 1, width), lambda i, be, *_: (layer, be[i], 0, 0))

    return pl.pallas_call(
        functools.partial(_expert_kernel, layer=layer),
        grid_spec=pltpu.PrefetchScalarGridSpec(
            num_scalar_prefetch=5,
            grid=(n_blocks,),
            in_specs=[
                pl.BlockSpec((te, PACKED), x_map),
                pl.BlockSpec(memory_space=pl.ANY),
                bias_spec(2 * D_FF),
                pl.BlockSpec(memory_space=pl.ANY),
                bias_spec(D_MODEL),
            ],
            out_specs=pl.BlockSpec((te, PACKED), lambda i, *_: (i, 0)),
            scratch_shapes=[
                pltpu.VMEM((2, D_MODEL, 2 * D_FF), F32),
                pltpu.VMEM((2, D_FF, D_MODEL), F32),
                pltpu.VMEM((D_MODEL, 2 * D_FF), BF16),
                pltpu.VMEM((D_FF, D_MODEL), BF16),
                pltpu.SemaphoreType.DMA((2,)),
            ],
        ),
        out_shape=jax.ShapeDtypeStruct(xb.shape, jnp.uint32),
        compiler_params=_cparams("arbitrary"),
        name="experts",
    )(block_e, n_used, next_e, slot, block_rows, xb, w_gu, b_gu.reshape(depth, N_EXPERTS, 1, 2 * D_FF), w_down,
      b_down.reshape(depth, N_EXPERTS, 1, D_MODEL))


def _combine_kernel(row_ref, tbl_ref, x_ref, pg_ref, mod_ref, fw_ref, yb_ref, *rest, split_tiles):
    del row_ref
    stage_s, sem = rest[-2:]
    t = pl.program_id(0)
    nt = pl.num_programs(0)
    slot = t % 2

    def fetch(tile, into):
        def seg_body(e, carry):
            rows, dst, src = _segment_table(tbl_ref, tile, e)

            @pl.when(rows > 0)
            def _():
                pltpu.make_async_copy(yb_ref.at[pl.ds(src, rows), :],
                                      stage_s.at[into, pl.ds(dst, rows), :], sem.at[into]).start()
            return carry

        lax.fori_loop(0, N_EXPERTS, seg_body, 0)

    @pl.when(t == 0)
    def _():
        stage_s[...] = jnp.zeros_like(stage_s)
        fetch(0, 0)

    @pl.when(t + 1 < nt)
    def _():
        fetch(t + 1, 1 - slot)

    fetched = _tile_rows_total(tbl_ref, t)
    pltpu.make_async_copy(yb_ref.at[pl.ds(0, fetched), :], stage_s.at[slot, pl.ds(0, fetched), :],
                          sem.at[slot]).wait()
    rows_sorted = _unpack_pairs(stage_s[slot])
    pg = pg_ref[...]
    s_iota = lax.broadcasted_iota(jnp.int32, (ROUTE_TILE, SLOTS), 1).astype(F32)
    sel = jnp.zeros((ROUTE_TILE, SLOTS), F32)
    for k in range(TOP_K):
        sel = jnp.where(s_iota == pg[:, k:k + 1], pg[:, TOP_K + k:TOP_K + k + 1], sel)
    y = _dot(sel.astype(BF16), rows_sorted)
    x = x_ref[...] + mod_ref[5:6, :] * y
    if split_tiles is None:
        rest[0][...] = x
    else:
        x = _rms(x, fw_ref[...])

        @pl.when(t < split_tiles)
        def _():
            rest[0][...] = x

        @pl.when(t >= split_tiles)
        def _():
            rest[1][...] = x


def _combine(x, pg, seg_tbl, yb, tile_row, mod_l, final_w, n_prompt=None):
    n = x.shape[0]
    tm = ROUTE_TILE
    if n_prompt is None:
        split = None
        out_specs = pl.BlockSpec((tm, D_MODEL), lambda i, row, tbl: (i, 0))
        out_shape = jax.ShapeDtypeStruct((n, D_MODEL), F32)
    else:
        split = n_prompt // tm
        out_specs = [
            pl.BlockSpec((tm, D_MODEL), lambda i, row, tbl: (jnp.minimum(i, split - 1), 0)),
            pl.BlockSpec((tm, D_MODEL), lambda i, row, tbl: (jnp.maximum(i - split, 0), 0)),
        ]
        out_shape = [jax.ShapeDtypeStruct((n_prompt, D_MODEL), F32),
                     jax.ShapeDtypeStruct((n - n_prompt, D_MODEL), F32)]
    return pl.pallas_call(
        functools.partial(_combine_kernel, split_tiles=split),
        grid_spec=pltpu.PrefetchScalarGridSpec(
            num_scalar_prefetch=2,
            grid=(n // tm,),
            in_specs=[
                pl.BlockSpec((tm, D_MODEL), lambda i, row, tbl: (i, 0)),
                pl.BlockSpec((tm, LANES), lambda i, row, tbl: (i, 0)),
                pl.BlockSpec((None, 6, D_MODEL), lambda i, row, tbl: (row[i], 0, 0)),
                pl.BlockSpec((1, D_MODEL), lambda i, row, tbl: (0, 0)),
                pl.BlockSpec(memory_space=pl.ANY),
            ],
            out_specs=out_specs,
            scratch_shapes=[pltpu.VMEM((2, SLOTS, PACKED), jnp.uint32), pltpu.SemaphoreType.DMA((2,))],
        ),
        out_shape=out_shape,
        compiler_params=_cparams("arbitrary"),
        name="combine",
    )(tile_row, seg_tbl, x, pg, mod_l, final_w.reshape(1, D_MODEL), yb)


def _moe(x, tile_row, mod_l, norm_w, router_w, router_b, layer, w_gu, b_gu, w_down, b_down,
         final_w, n_prompt=None):
    n = x.shape[0]
    rw = jnp.zeros((D_MODEL, LANES), F32).at[:, :N_EXPERTS].set(router_w)
    rb = jnp.full((1, LANES), -jnp.inf, F32).at[0, :N_EXPERTS].set(router_b)
    hf, pg, pgt, cnt = _router(x, tile_row, mod_l, norm_w, rw, rb)

    te = EXPERT_TILE
    nt = n // ROUTE_TILE
    n_blocks = (n * TOP_K + nt * N_EXPERTS * (SEG_ALIGN - 1)) // te + 1 + N_EXPERTS
    cnt = cnt[:, 0, :N_EXPERTS]
    cnt = (cnt + SEG_ALIGN - 1) // SEG_ALIGN * SEG_ALIGN
    total = jnp.sum(cnt, axis=0)
    blocks_e = (total + te - 1) // te
    block_end = jnp.cumsum(blocks_e)
    pstart = (block_end - blocks_e) * te
    n_used = block_end[-1]
    first_row = pstart[None, :] + jnp.cumsum(cnt, axis=0) - cnt
    tile_off = jnp.cumsum(cnt, axis=1) - cnt
    tile_total = jnp.broadcast_to(jnp.sum(cnt, axis=1, keepdims=True), cnt.shape)
    seg_tbl = jnp.concatenate([cnt, tile_off, first_row, tile_total], axis=1)
    seg_tbl = seg_tbl.reshape(-1).astype(jnp.int32)
    pad_rows = blocks_e * te - total
    zeroed = jnp.sum(pad_rows) + (n_blocks - n_used) * te
    pad_tbl = jnp.concatenate([pstart + total, pad_rows, zeroed[None]]).astype(jnp.int32)
    block_ids = jnp.arange(n_blocks, dtype=jnp.int32)
    clamped = jnp.minimum(block_ids, n_used - 1)
    block_e = jnp.sum((clamped[:, None] >= block_end[None, :]).astype(jnp.int32), axis=1)
    block_e = jnp.minimum(block_e, N_EXPERTS - 1).astype(jnp.int32)
    n_used = n_used.astype(jnp.int32).reshape(1)
    owns = (blocks_e > 0)[None, :]
    ids = jnp.arange(N_EXPERTS, dtype=jnp.int32)[None, :]
    mine = block_e[:, None]
    next_e = jnp.min(jnp.where((ids > mine) & owns, ids, N_EXPERTS), axis=1)
    next_e = jnp.where(next_e < N_EXPERTS, next_e, -1).astype(jnp.int32)
    slot = ((jnp.sum(((ids <= mine) & owns).astype(jnp.int32), axis=1) + 1) % 2).astype(jnp.int32)

    row_end = jnp.sum(jnp.where(ids == mine, (pstart + total)[None, :], 0), axis=1)
    block_rows = jnp.clip(row_end - block_ids * te, 0, te).astype(jnp.int32)

    xb = _dispatch(hf, pgt, seg_tbl, pad_tbl, n_used, n_blocks)
    yb = _experts(xb, block_e, n_used, next_e, slot, block_rows, layer, w_gu, b_gu, w_down, b_down)
    return _combine(x, pg, seg_tbl, yb, tile_row, mod_l, final_w, n_prompt)


def _tile_rows(n_prompt_tok, n_sample_seq, sample_len, tile):
    starts = np.arange(0, n_prompt_tok + n_sample_seq * sample_len, tile)
    row = np.where(starts < n_prompt_tok, 0, 1 + (starts - n_prompt_tok) // sample_len)
    return jnp.asarray(row, dtype=jnp.int32)


def kernel(x_prompt, x_sample, state_hgrn, c, c_ctx, w_mod, b_mod, norm_mix, norm_ffn, hg_w_in,
           hg_lb_logits, hg_gnorm, hg_w_out, cv_w_in, cv_w, cv_w_out, router_w, router_b,
           moe_w_gu, moe_b_gu, moe_w_down, moe_b_down, final_norm):
    bp, tp, d = x_prompt.shape
    bs, ts, _ = x_sample.shape
    depth = w_mod.shape[0]
    n_prompt = bp * tp
    n = n_prompt + bs * ts
    assert d == D_MODEL and depth == 2 and 1 + bs <= 16
    assert n_prompt % SCAN_ROWS == 0 and ts == SCAN_ROWS and tp == SCAN_PIECE

    x_p = x_prompt.reshape(n_prompt, d)
    x_s = x_sample.reshape(bs * ts, d)
    cond16 = jnp.zeros((16, d), F32).at[0].set(c_ctx).at[1:1 + bs].set(c)
    mod = _modulation(cond16, w_mod, b_mod).reshape(depth, 16, 6, d)

    tile_row = _tile_rows(n_prompt, bs, ts, TOKEN_TILE)
    tile_row_route = _tile_rows(n_prompt, bs, ts, ROUTE_TILE)
    starts = np.arange(0, n, TOKEN_TILE)
    tile_width = jnp.asarray(np.where(starts < n_prompt, tp, GRID_W), dtype=jnp.int32)

    lb_all = jnp.cumsum(jax.nn.softmax(hg_lb_logits.astype(F32), axis=1), axis=1)

    w_in_bf = hg_w_in[0].astype(BF16)
    gn = hg_gnorm[0].reshape(-1)
    zero_state = jnp.zeros((SCAN_ROWS // tp, 2, HEADS, HEAD_DIM, HEAD_DIM), F32)
    o_p, s_new = _hgrn_scan(x_p, mod[0], 0, 0, norm_mix[0], w_in_bf, lb_all[0, 0], lb_all[1, 0], gn,
                            zero_state, tp, True)
    o_s, _ = _hgrn_scan(x_s, mod[0], 1, 1, norm_mix[0], w_in_bf, lb_all[0, 0], lb_all[1, 0], gn,
                        state_hgrn[:, 0], ts, False)
    x = _proj_residual(x_p, x_s, o_p, o_s, hg_w_out[0].astype(BF16), tile_row, mod[0])
    x = _moe(x, tile_row_route, mod[0], norm_ffn[0], router_w[0], router_b[0],
             0, moe_w_gu, moe_b_gu, moe_w_down, moe_b_down, final_norm)

    x = _conv_mixer(x, tile_row, tile_width, mod[1], norm_mix[1], cv_w_in[0].astype(BF16), cv_w[0],
                    cv_w_out[0].astype(BF16))
    y_p, y_s = _moe(x, tile_row_route, mod[1], norm_ffn[1], router_w[1], router_b[1],
                    1, moe_w_gu, moe_b_gu, moe_w_down, moe_b_down, final_norm, n_prompt)

    y_prompt = y_p.reshape(bp, tp, d)
    y_sample = y_s.reshape(bs, ts, d)
    return (y_prompt, y_sample, s_new.reshape(bp, 1, 2, HEADS, HEAD_DIM, HEAD_DIM))
```

```python
import functools

import numpy as np
import jax
import jax.numpy as jnp
from jax import lax
from jax.experimental import pallas as pl
from jax.experimental.pallas import tpu as pltpu

F32 = jnp.float32
BF16 = jnp.bfloat16

D_MODEL = 1024
HEADS = 8
HEAD_DIM = 128
CHUNK = 64
GRID_W = 64
N_EXPERTS = 32
TOP_K = 4
D_FF = 1024
SWIGLU_LIMIT = 7.0
SWIGLU_ALPHA = 1.702
EPS = 1e-6

LANES = 128
SCAN_ROWS = 2048
SCAN_PIECE = 256
TOKEN_TILE = 512
EXPERT_TILE = 512
EXPERT_SUB = 256
ROUTE_TILE = 256
SEG_ALIGN = 8
SLOTS = ROUTE_TILE * TOP_K + N_EXPERTS * SEG_ALIGN
VMEM_LIMIT = 56 * 1024 * 1024


def _cparams(*sem):
    return pltpu.CompilerParams(dimension_semantics=sem, vmem_limit_bytes=VMEM_LIMIT)


def _sigmoid(x):
    return 1.0 / (1.0 + jnp.exp(-x))


def _rms(x, w):
    return x * lax.rsqrt(jnp.mean(x * x, axis=-1, keepdims=True) + EPS) * w


def _dot(a, b):
    return jnp.dot(a, b, preferred_element_type=F32)


def _dot_nt(a, b):
    return lax.dot_general(a, b, (((1,), (1,)), ((), ())), preferred_element_type=F32)


PACKED = D_MODEL // 2
_HIGH_HALF = 0xFFFF0000


def _pack_pairs(x):
    c = x.shape[1] // 2
    lo = lax.bitcast_convert_type(x[:, :c], jnp.uint32) >> 16
    hi = lax.bitcast_convert_type(x[:, c:], jnp.uint32) & jnp.uint32(_HIGH_HALF)
    return hi | lo


def _unpack_pairs(u):
    lo = lax.bitcast_convert_type(u << 16, F32)
    hi = lax.bitcast_convert_type(u & jnp.uint32(_HIGH_HALF), F32)
    return jnp.concatenate([lo, hi], axis=1).astype(BF16)


def _mod_kernel(cond_ref, w_ref, b_ref, o_ref):
    c = cond_ref[...]
    s = (c * _sigmoid(c)).astype(BF16)
    o_ref[...] = _dot(s, w_ref[...].astype(BF16)) + b_ref[...]


def _modulation(cond16, w_mod, b_mod):
    depth = w_mod.shape[0]
    n_out = w_mod.shape[2]
    tn = 1024
    return pl.pallas_call(
        _mod_kernel,
        grid=(depth, n_out // tn),
        in_specs=[
            pl.BlockSpec((16, D_MODEL), lambda l, j: (0, 0)),
            pl.BlockSpec((None, D_MODEL, tn), lambda l, j: (l, 0, j)),
            pl.BlockSpec((None, 1, tn), lambda l, j: (l, 0, j)),
        ],
        out_specs=pl.BlockSpec((None, 16, tn), lambda l, j: (l, 0, j)),
        out_shape=jax.ShapeDtypeStruct((depth, 16, n_out), F32),
        compiler_params=_cparams("arbitrary", "arbitrary"),
        name="modulation",
    )(cond16, w_mod, b_mod.reshape(depth, 1, n_out))


def _two_group_specs(tm, prompt_tiles):
    return [
        pl.BlockSpec((tm, D_MODEL), lambda i, *_: (jnp.minimum(i, prompt_tiles - 1), 0)),
        pl.BlockSpec((tm, D_MODEL), lambda i, *_: (jnp.maximum(i - prompt_tiles, 0), 0)),
    ]


def _pick_group(a_ref, b_ref, prompt_tiles):
    return jnp.where(pl.program_id(0) < prompt_tiles, a_ref[...], b_ref[...])


def _hgrn_kernel(x_ref, mod_ref, nw_ref, wq_ref, wff_ref, wfb_ref, wv_ref, wg_ref, lbf_ref, lbb_ref,
                 gn_ref, s0_ref, o_ref, so_ref, hm_s, q_s, vt_s, kf_s, lf_s, kb_s, lb_s, of_s, ob_s, st_s,
                 *, seq_len, zero_init):
    rows = x_ref.shape[0]
    piece = SCAN_PIECE
    n_pieces = rows // piece
    cpp = piece // CHUNK
    per_piece_seq = seq_len == piece
    assert per_piece_seq or seq_len == rows

    @pl.when(pl.program_id(1) == 0)
    def _():
        y = _rms(x_ref[...], nw_ref[...])
        hm_s[...] = (y * (1.0 + mod_ref[1:2, :]) + mod_ref[0:1, :]).astype(BF16)

    hm = hm_s[...]

    zq = _dot(hm, wq_ref[...])
    q_s[...] = zq * _sigmoid(zq)
    v = _dot(hm, wv_ref[...])
    for p in range(n_pieces):
        vt_s[p] = v[p * piece:(p + 1) * piece, :].T.astype(BF16)
    for w_ref, lbr, k_s, l_s in ((wff_ref, lbf_ref, kf_s, lf_s), (wfb_ref, lbb_ref, kb_s, lb_s)):
        lb = lbr[...]
        f = lb + (1.0 - lb) * _sigmoid(_dot(hm, w_ref[...]))
        k_s[...] = 1.0 - f
        l_s[...] = jnp.log(f)

    def load_state(j, d):
        for h in range(2):
            if zero_init:
                st_s[d, h] = jnp.zeros((HEAD_DIM, HEAD_DIM), F32)
            else:
                st_s[d, h] = s0_ref[j, d, h].T

    def store_state(j, d):
        for h in range(2):
            so_ref[j, d, h] = st_s[d, h].T

    def piece_body(i, carry):
        r = lax.broadcasted_iota(jnp.int32, (piece, piece), 0)
        c = lax.broadcasted_iota(jnp.int32, (piece, piece), 1)
        same = (r // CHUNK) == (c // CHUNK)
        row_chunk = lax.broadcasted_iota(jnp.int32, (piece, 1), 0) // CHUNK
        col_chunk = lax.broadcasted_iota(jnp.int32, (1, piece), 1) // CHUNK
        for d in range(2):
            k_s, l_s, o_s = ((kf_s, lf_s, of_s), (kb_s, lb_s, ob_s))[d]
            p = i if d == 0 else n_pieces - 1 - i
            if per_piece_seq:
                load_state(p, d)
            sl = pl.ds(pl.multiple_of(p * piece, piece), piece)
            keep = same & ((c <= r) if d == 0 else (c >= r))
            tri = jnp.where(keep, 1.0, 0.0).astype(BF16)
            lf = l_s[sl, :]
            hi = lf.astype(BF16)
            mid = (lf - hi.astype(F32)).astype(BF16)
            b = _dot(tri, hi) + _dot(tri, mid)
            b3 = b.reshape(cpp, CHUNK, 2 * HEAD_DIM)
            edge = CHUNK - 1 if d == 0 else 0
            bl3 = b3[:, edge:edge + 1, :]
            centre = CHUNK // 2 - 1 if d == 0 else CHUNK // 2
            bm3 = b3[:, centre:centre + 1, :]
            q = q_s[sl, :]
            q3 = q.reshape(cpp, CHUNK, 2 * HEAD_DIM)
            k3 = k_s[sl, :].reshape(cpp, CHUNK, 2 * HEAD_DIM)
            flat = lambda a: a.reshape(piece, 2 * HEAD_DIM).astype(BF16)
            qd = (q * jnp.exp(b)).astype(BF16)
            qi = flat(q3 * jnp.exp(b3 - bm3))
            ki = flat(k3 * jnp.exp(bm3 - b3))
            ks = flat(k3 * jnp.exp(bl3 - b3))
            dec = jnp.exp(bl3)
            vt = vt_s[p]
            for h in range(2):
                hs = slice(h * HEAD_DIM, (h + 1) * HEAD_DIM)
                att = jnp.where(keep, _dot_nt(qi[:, hs], ki[:, hs]), 0.0).astype(BF16)
                vt_h = vt[hs, :]
                v_exp = jnp.concatenate(
                    [jnp.where(col_chunk == ci, vt_h, jnp.zeros_like(vt_h)) for ci in range(cpp)], axis=0)
                incr = _dot(v_exp, ks[:, hs])
                st = st_s[d, h]
                before = [None] * cpp
                for ci in (range(cpp) if d == 0 else range(cpp - 1, -1, -1)):
                    before[ci] = st.astype(BF16)
                    st = st * dec[ci, :, hs] + incr[ci * HEAD_DIM:(ci + 1) * HEAD_DIM, :]
                st_s[d, h] = st
                qd_h = qd[:, hs]
                q_exp = [jnp.where(row_chunk == ci, qd_h, jnp.zeros_like(qd_h)) for ci in range(cpp)]
                lhs = jnp.concatenate([att] + q_exp, axis=1)
                rhs_t = jnp.concatenate([vt_h] + before, axis=1)
                o_s[sl, hs] = _dot_nt(lhs, rhs_t)
            if per_piece_seq:
                store_state(p, d)
        return carry

    if not per_piece_seq:
        load_state(0, 0)
        load_state(0, 1)
    lax.fori_loop(0, n_pieces, piece_body, 0, unroll=4)
    if not per_piece_seq:
        store_state(0, 0)
        store_state(0, 1)

    o = of_s[...] + ob_s[...]
    gn = gn_ref[...]
    o = jnp.concatenate(
        [_rms(o[:, h * HEAD_DIM:(h + 1) * HEAD_DIM], gn[:, h * HEAD_DIM:(h + 1) * HEAD_DIM])
         for h in range(2)], axis=1)
    o_ref[...] = (o * _sigmoid(_dot(hm, wg_ref[...]))).astype(o_ref.dtype)


def _hgrn_scan(x, mod_l, mod_row0, mod_row_step, norm_w, w_in_bf, lb_f, lb_b, gnorm, s0, seq_len,
               zero_init):
    n_rows = x.shape[0]
    n_seq_total = n_rows // seq_len
    rows = SCAN_ROWS
    seq_per_step = rows // seq_len
    pair = 2 * HEAD_DIM
    n_pairs = HEADS // 2

    def w_spec(seg):
        return pl.BlockSpec((D_MODEL, pair), lambda sb, hp: (0, seg * n_pairs + hp))

    vec_spec = pl.BlockSpec((1, pair), lambda sb, hp: (0, hp))
    st_block = (seq_per_step, 2, 2, HEAD_DIM, HEAD_DIM)
    st_spec = pl.BlockSpec(st_block, lambda sb, hp: (sb, 0, hp, 0, 0))
    s0_spec = pl.BlockSpec(st_block, (lambda sb, hp: (0, 0, hp, 0, 0)) if zero_init
                           else (lambda sb, hp: (sb, 0, hp, 0, 0)))
    scratch = [
        pltpu.VMEM((rows, D_MODEL), BF16),
        pltpu.VMEM((rows, pair), F32),
        pltpu.VMEM((rows // SCAN_PIECE, pair, SCAN_PIECE), BF16),
        pltpu.VMEM((rows, pair), F32),
        pltpu.VMEM((rows, pair), F32),
        pltpu.VMEM((rows, pair), F32),
        pltpu.VMEM((rows, pair), F32),
        pltpu.VMEM((rows, pair), F32),
        pltpu.VMEM((rows, pair), F32),
        pltpu.VMEM((2, 2, HEAD_DIM, HEAD_DIM), F32),
    ]
    in_specs = [
        pl.BlockSpec((rows, D_MODEL), lambda sb, hp: (sb, 0), pipeline_mode=pl.Buffered(1)),
        pl.BlockSpec((None, 6, D_MODEL), lambda sb, hp: (mod_row0 + sb * mod_row_step, 0, 0)),
        pl.BlockSpec((1, D_MODEL), lambda sb, hp: (0, 0)),
        w_spec(0), w_spec(1), w_spec(2), w_spec(3), w_spec(4),
        vec_spec, vec_spec, vec_spec,
        s0_spec,
    ]
    args = [x, mod_l, norm_w.reshape(1, D_MODEL), w_in_bf, w_in_bf, w_in_bf, w_in_bf, w_in_bf,
            lb_f.reshape(1, D_MODEL), lb_b.reshape(1, D_MODEL), gnorm.reshape(1, D_MODEL), s0]
    return pl.pallas_call(
        functools.partial(_hgrn_kernel, seq_len=seq_len, zero_init=zero_init),
        grid=(n_rows // rows, n_pairs),
        in_specs=in_specs,
        out_specs=[
            pl.BlockSpec((rows, pair), lambda sb, hp: (sb, hp)),
            st_spec,
        ],
        out_shape=[
            jax.ShapeDtypeStruct((n_rows, D_MODEL), BF16),
            jax.ShapeDtypeStruct((n_seq_total, 2, HEADS, HEAD_DIM, HEAD_DIM), F32),
        ],
        scratch_shapes=scratch,
        compiler_params=_cparams("arbitrary", "arbitrary"),
        name="hgrn_scan",
    )(*args)


def _route_rows(x1, half, mod_ref, nw_ref, rwh_ref, rwl_ref, rb_ref, hf_ref, pg_ref, pgt_ref, cnt_ref):
    tm = ROUTE_TILE
    rows = slice(half * tm, (half + 1) * tm)
    hf = _rms(x1, nw_ref[...]) * (1.0 + mod_ref[4:5, :]) + mod_ref[3:4, :]
    hf_hi = hf.astype(BF16)
    hf_ref[rows, :] = hf_hi
    hf_lo = (hf - hf_hi.astype(F32)).astype(BF16)
    logits = (_dot(hf_hi, rwh_ref[...]) + (_dot(hf_hi, rwl_ref[...]) + _dot(hf_lo, rwh_ref[...]))
              + rb_ref[...])
    lane = lax.broadcasted_iota(jnp.int32, (tm, LANES), 1)
    lane_f = lane.astype(F32)
    work = logits
    vals, hots = [], []
    for _ in range(TOP_K):
        m = jnp.max(work, axis=-1, keepdims=True)
        idx_f = jnp.min(jnp.where(work == m, lane_f, float(LANES)), axis=-1, keepdims=True)
        hot = lane_f == idx_f
        vals.append(m)
        hots.append(hot)
        work = jnp.where(hot, -jnp.inf, work)
    ex = [jnp.exp(v - vals[0]) for v in vals]
    inv = 1.0 / (ex[0] + ex[1] + ex[2] + ex[3])

    multi = jnp.where(hots[0] | hots[1] | hots[2] | hots[3], 1.0, 0.0)
    tr = lax.broadcasted_iota(jnp.int32, (tm, tm), 0)
    tc = lax.broadcasted_iota(jnp.int32, (tm, tm), 1)
    earlier = jnp.where(tc < tr, 1.0, 0.0).astype(BF16)
    before = _dot(earlier, multi.astype(BF16))
    cnt = jnp.sum(multi, axis=0, keepdims=True)
    er = lax.broadcasted_iota(jnp.int32, (LANES, LANES), 0)
    ec = lax.broadcasted_iota(jnp.int32, (LANES, LANES), 1)
    lower = jnp.where(er < ec, 1.0, 0.0).astype(BF16)
    cnt_al = jnp.floor((cnt + (SEG_ALIGN - 1)) * (1.0 / SEG_ALIGN)) * SEG_ALIGN
    estart = _dot(jnp.broadcast_to(cnt_al, (8, LANES)).astype(BF16), lower)[0:1, :]
    pos_all = before + estart

    pg = jnp.zeros((tm, LANES), F32)
    for k in range(TOP_K):
        pos = jnp.sum(jnp.where(hots[k], pos_all, 0.0), axis=-1, keepdims=True)
        pg = jnp.where(lane == k, pos, pg)
        pg = jnp.where(lane == TOP_K + k, ex[k] * inv, pg)
    pg_ref[rows, :] = pg
    pgt_ref[half] = pg.T[0:8, :]
    cnt_ref[half] = cnt.astype(jnp.int32)


def _route_tile(x1, route_refs):
    x1_ref = route_refs[-5]
    x1_ref[...] = x1
    for half in range(TOKEN_TILE // ROUTE_TILE):
        _route_rows(x1[half * ROUTE_TILE:(half + 1) * ROUTE_TILE, :], half,
                    *route_refs[:5], *route_refs[-4:])


def _proj_router_kernel(row_ref, xp_ref, xs_ref, op_ref, os_ref, w_ref, mod_ref, *route_refs,
                        prompt_tiles):
    del row_ref
    x = _pick_group(xp_ref, xs_ref, prompt_tiles)
    o = _pick_group(op_ref, os_ref, prompt_tiles)
    _route_tile(x + mod_ref[2:3, :] * _dot(o, w_ref[...]), (mod_ref,) + route_refs)


def _conv_router_kernel(row_ref, width_ref, x_ref, nwm_ref, win_ref, cw_ref, wout_ref, mod_ref,
                        *route_refs):
    del row_ref
    i = pl.program_id(0)
    x = x_ref[...]
    hm = (_rms(x, nwm_ref[...]) * (1.0 + mod_ref[1:2, :]) + mod_ref[0:1, :]).astype(BF16)
    z = _dot(hm, win_ref[...])
    bg = z[:, :D_MODEL]
    u = z[:, D_MODEL:2 * D_MODEL] * z[:, 2 * D_MODEL:]
    tm = x.shape[0]
    pos = lax.broadcasted_iota(jnp.int32, (tm, 1), 0) & (width_ref[i] - 1)
    prev = jnp.where(pos == 0, 0.0, pltpu.roll(u, 1, axis=0))
    nxt = jnp.where(pos == width_ref[i] - 1, 0.0, pltpu.roll(u, tm - 1, axis=0))
    v = cw_ref[0:1, :] * prev + cw_ref[1:2, :] * u + cw_ref[2:3, :] * nxt
    y = _dot((bg * v).astype(BF16), wout_ref[...])
    _route_tile(x + mod_ref[2:3, :] * y, (mod_ref,) + route_refs)


def _mixer_router_call(kernel_fn, n, n_prefetch, prefetch, in_specs, args, mod_l, norm_ffn,
                       router_w, router_b, name):
    tm = TOKEN_TILE
    sub = tm // ROUTE_TILE
    rw = jnp.zeros((D_MODEL, LANES), F32).at[:, :N_EXPERTS].set(router_w)
    rb = jnp.full((1, LANES), -jnp.inf, F32).at[0, :N_EXPERTS].set(router_b)
    rw_hi = rw.astype(BF16)
    rw_lo = (rw - rw_hi.astype(F32)).astype(BF16)
    const = lambda shape: pl.BlockSpec(shape, lambda i, *_: (0,) * len(shape))
    row_blk = lambda width: pl.BlockSpec((tm, width), lambda i, *_: (i, 0))
    return pl.pallas_call(
        kernel_fn,
        grid_spec=pltpu.PrefetchScalarGridSpec(
            num_scalar_prefetch=n_prefetch,
            grid=(n // tm,),
            in_specs=in_specs + [
                pl.BlockSpec((None, 6, D_MODEL), lambda i, row, *_: (row[i], 0, 0)),
                const((1, D_MODEL)), const((D_MODEL, LANES)), const((D_MODEL, LANES)), const((1, LANES)),
            ],
            out_specs=[
                row_blk(D_MODEL), row_blk(D_MODEL), row_blk(LANES),
                pl.BlockSpec((sub, 8, ROUTE_TILE), lambda i, *_: (i, 0, 0)),
                pl.BlockSpec((sub, 1, LANES), lambda i, *_: (i, 0, 0)),
            ],
        ),
        out_shape=[
            jax.ShapeDtypeStruct((n, D_MODEL), F32),
            jax.ShapeDtypeStruct((n, D_MODEL), BF16),
            jax.ShapeDtypeStruct((n, LANES), F32),
            jax.ShapeDtypeStruct((n // ROUTE_TILE, 8, ROUTE_TILE), F32),
            jax.ShapeDtypeStruct((n // ROUTE_TILE, 1, LANES), jnp.int32),
        ],
        compiler_params=_cparams("arbitrary"),
        name=name,
    )(*prefetch, *args, mod_l, norm_ffn.reshape(1, D_MODEL), rw_hi, rw_lo, rb)


def _proj_router(x_p, x_s, o_prompt, o_sample, w_out_bf, tile_row, mod_l, norm_ffn, router_w, router_b):
    n = x_p.shape[0] + x_s.shape[0]
    pt = x_p.shape[0] // TOKEN_TILE
    in_specs = _two_group_specs(TOKEN_TILE, pt) + _two_group_specs(TOKEN_TILE, pt) + [
        pl.BlockSpec((D_MODEL, D_MODEL), lambda i, *_: (0, 0))]
    return _mixer_router_call(functools.partial(_proj_router_kernel, prompt_tiles=pt), n, 1,
                              (tile_row,), in_specs, (x_p, x_s, o_prompt, o_sample, w_out_bf),
                              mod_l, norm_ffn, router_w, router_b, "proj_router")


def _conv_router(x, tile_row, tile_width, mod_l, norm_mix, w_in_bf, conv_w, w_out_bf, norm_ffn,
                 router_w, router_b):
    n = x.shape[0]
    in_specs = [
        pl.BlockSpec((TOKEN_TILE, D_MODEL), lambda i, *_: (i, 0)),
        pl.BlockSpec((1, D_MODEL), lambda i, *_: (0, 0)),
        pl.BlockSpec((D_MODEL, 3 * D_MODEL), lambda i, *_: (0, 0)),
        pl.BlockSpec((3, D_MODEL), lambda i, *_: (0, 0)),
        pl.BlockSpec((D_MODEL, D_MODEL), lambda i, *_: (0, 0)),
    ]
    return _mixer_router_call(_conv_router_kernel, n, 2, (tile_row, tile_width), in_specs,
                              (x, norm_mix.reshape(1, D_MODEL), w_in_bf, conv_w, w_out_bf),
                              mod_l, norm_ffn, router_w, router_b, "conv_router")


SEG_STRIDE = 4 * N_EXPERTS


def _segment_table(tbl_ref, t, e):
    base = t * SEG_STRIDE
    al = lambda v: pl.multiple_of(v, SEG_ALIGN)
    return (al(tbl_ref[base + e]), al(tbl_ref[base + N_EXPERTS + e]),
            al(tbl_ref[base + 2 * N_EXPERTS + e]))


def _tile_rows_total(tbl_ref, t):
    return pl.multiple_of(tbl_ref[t * SEG_STRIDE + 3 * N_EXPERTS], SEG_ALIGN)


def _dispatch_kernel(tbl_ref, pad_ref, nu_ref, hf_ref, pgt_ref, xb_ref, stage_s, zero_s, sem, zsem,
                     *, n_blocks):
    t = pl.program_id(0)
    nt = pl.num_programs(0)
    slot = t % 2
    half = zero_s.shape[0]

    def zero_rows(start, rows):
        start = pl.multiple_of(start, SEG_ALIGN)
        rows = pl.multiple_of(rows, SEG_ALIGN)

        @pl.when(rows > 0)
        def _():
            pltpu.make_async_copy(zero_s.at[pl.ds(0, rows), :], xb_ref.at[pl.ds(start, rows), :],
                                  zsem).start()

    @pl.when(t == 0)
    def _():
        zero_s[...] = jnp.zeros_like(zero_s)

        def pad_body(e, carry):
            start = pad_ref[e]
            rows = pad_ref[N_EXPERTS + e]
            first = jnp.minimum(rows, half)
            zero_rows(start, first)
            zero_rows(start + half, rows - first)
            return carry

        lax.fori_loop(0, N_EXPERTS, pad_body, 0)

        def tail_body(b, carry):
            for part in range(EXPERT_TILE // half):
                zero_rows(b * EXPERT_TILE + part * half, half)
            return carry

        lax.fori_loop(nu_ref[0], n_blocks, tail_body, 0)

    def wait_tile(tile, sl):
        rows = _tile_rows_total(tbl_ref, tile)
        pltpu.make_async_copy(stage_s.at[sl, pl.ds(0, rows), :], xb_ref.at[pl.ds(0, rows), :],
                              sem.at[sl]).wait()

    @pl.when(t >= 2)
    def _():
        wait_tile(t - 2, slot)

    s_iota = lax.broadcasted_iota(jnp.int32, (SLOTS, ROUTE_TILE), 0).astype(F32)
    pgt = pgt_ref[...]
    hit = s_iota == pgt[0:1, :]
    for k in range(1, TOP_K):
        hit = hit | (s_iota == pgt[k:k + 1, :])
    perm = jnp.where(hit, 1.0, 0.0).astype(BF16)
    stage_s[slot] = _pack_pairs(_dot(perm, hf_ref[...]))

    def seg_body(e, carry):
        rows, src, dst = _segment_table(tbl_ref, t, e)

        @pl.when(rows > 0)
        def _():
            pltpu.make_async_copy(stage_s.at[slot, pl.ds(src, rows), :],
                                  xb_ref.at[pl.ds(dst, rows), :], sem.at[slot]).start()
        return carry

    lax.fori_loop(0, N_EXPERTS, seg_body, 0)

    @pl.when(t == nt - 1)
    def _():
        wait_tile(t, slot)

        @pl.when(nt > 1)
        def _():
            wait_tile(t - 1, 1 - slot)

        zeroed = pl.multiple_of(pad_ref[2 * N_EXPERTS], SEG_ALIGN)

        @pl.when(zeroed > 0)
        def _():
            pltpu.make_async_copy(xb_ref.at[pl.ds(0, zeroed), :], xb_ref.at[pl.ds(0, zeroed), :],
                                  zsem).wait()


def _dispatch(hf, pgt, seg_tbl, pad_tbl, n_used, n_blocks):
    n = hf.shape[0]
    tm = ROUTE_TILE
    return pl.pallas_call(
        functools.partial(_dispatch_kernel, n_blocks=n_blocks),
        grid_spec=pltpu.PrefetchScalarGridSpec(
            num_scalar_prefetch=3,
            grid=(n // tm,),
            in_specs=[
                pl.BlockSpec((tm, D_MODEL), lambda i, a, b, c: (i, 0)),
                pl.BlockSpec((None, 8, tm), lambda i, a, b, c: (i, 0, 0)),
            ],
            out_specs=pl.BlockSpec(memory_space=pl.ANY),
            scratch_shapes=[
                pltpu.VMEM((2, SLOTS, PACKED), jnp.uint32),
                pltpu.VMEM((EXPERT_TILE // 2, PACKED), jnp.uint32),
                pltpu.SemaphoreType.DMA((2,)),
                pltpu.SemaphoreType.DMA,
            ],
        ),
        out_shape=jax.ShapeDtypeStruct((n_blocks * EXPERT_TILE, PACKED), jnp.uint32),
        compiler_params=_cparams("arbitrary"),
        name="dispatch",
    )(seg_tbl, pad_tbl, n_used, hf, pgt)


def _expert_kernel(be_ref, nu_ref, nxt_ref, slot_ref, rows_ref, xb_ref, wgu_hbm, bgu_ref, wd_hbm, bd_ref,
                   yb_ref, wgu_f, wd_f, wgu_s, wd_s, sem, *, layer):
    i = pl.program_id(0)

    def fetch(e, slot):
        return (pltpu.make_async_copy(wgu_hbm.at[layer, e], wgu_f.at[slot], sem.at[slot]),
                pltpu.make_async_copy(wd_hbm.at[layer, e], wd_f.at[slot], sem.at[slot]))

    @pl.when(i < nu_ref[0])
    def _():
        e = be_ref[i]
        slot = slot_ref[i]
        first_block = jnp.logical_or(i == 0, e != be_ref[jnp.maximum(i - 1, 0)])

        @pl.when(first_block)
        def _():
            @pl.when(i == 0)
            def _():
                for cp in fetch(e, slot):
                    cp.start()

            for cp in fetch(e, slot):
                cp.wait()
            wgu_s[...] = wgu_f[slot].astype(BF16)
            wd_s[...] = wd_f[slot].astype(BF16)

            @pl.when(nxt_ref[i] >= 0)
            def _():
                for cp in fetch(nxt_ref[i], 1 - slot):
                    cp.start()

        def ffn(sl):
            gu = _dot(_unpack_pairs(xb_ref[sl, :]), wgu_s[...]) + bgu_ref[...]
            gate = jnp.minimum(gu[:, :D_FF], SWIGLU_LIMIT)
            up = jnp.clip(gu[:, D_FF:], -SWIGLU_LIMIT, SWIGLU_LIMIT)
            act = (up + 1.0) * gate * _sigmoid(SWIGLU_ALPHA * gate)
            y = _dot(act.astype(BF16), wd_s[...]) + bd_ref[...]
            yb_ref[sl, :] = _pack_pairs(y.astype(BF16).astype(F32))

        @pl.when(rows_ref[i] > EXPERT_SUB)
        def _():
            ffn(slice(0, EXPERT_TILE))

        @pl.when(rows_ref[i] <= EXPERT_SUB)
        def _():
            ffn(slice(0, EXPERT_SUB))
            yb_ref[EXPERT_SUB:, :] = jnp.zeros((EXPERT_TILE - EXPERT_SUB, PACKED), jnp.uint32)

    @pl.when(i >= nu_ref[0])
    def _():
        yb_ref[...] = jnp.zeros_like(yb_ref)


def _experts(xb, block_e, n_used, next_e, slot, block_rows, layer, w_gu, b_gu, w_down, b_down):
    te = EXPERT_TILE
    n_blocks = xb.shape[0] // te
    depth = w_gu.shape[0]

    def x_map(i, be, nu, *_):
        return (jnp.maximum(jnp.minimum(i, nu[0] - 1), 0), 0)

    def bias_spec(width):
        return pl.BlockSpec((None, None, 1, width), lambda i, be, *_: (layer, be[i], 0, 0))

    return pl.pallas_call(
        functools.partial(_expert_kernel, layer=layer),
        grid_spec=pltpu.PrefetchScalarGridSpec(
            num_scalar_prefetch=5,
            grid=(n_blocks,),
            in_specs=[
                pl.BlockSpec((te, PACKED), x_map),
                pl.BlockSpec(memory_space=pl.ANY),
                bias_spec(2 * D_FF),
                pl.BlockSpec(memory_space=pl.ANY),
                bias_spec(D_MODEL),
            ],
            out_specs=pl.BlockSpec((te, PACKED), lambda i, *_: (i, 0)),
            scratch_shapes=[
                pltpu.VMEM((2, D_MODEL, 2 * D_FF), F32),
                pltpu.VMEM((2, D_FF, D_MODEL), F32),
                pltpu.VMEM((D_MODEL, 2 * D_FF), BF16),
                pltpu.VMEM((D_FF, D_MODEL), BF16),
                pltpu.SemaphoreType.DMA((2,)),
            ],
        ),
        out_shape=jax.ShapeDtypeStruct(xb.shape, jnp.uint32),
        compiler_params=_cparams("arbitrary"),
        name="experts",
    )(block_e, n_used, next_e, slot, block_rows, xb, w_gu,
      b_gu.reshape(depth, N_EXPERTS, 1, 2 * D_FF), w_down, b_down.reshape(depth, N_EXPERTS, 1, D_MODEL))


def _combine_kernel(row_ref, tbl_ref, x_ref, pg_ref, mod_ref, fw_ref, yb_ref, *rest, split_tiles):
    del row_ref
    stage_s, sem = rest[-2:]
    t = pl.program_id(0)
    nt = pl.num_programs(0)
    slot = t % 2

    def fetch(tile, into):
        def seg_body(e, carry):
            rows, dst, src = _segment_table(tbl_ref, tile, e)

            @pl.when(rows > 0)
            def _():
                pltpu.make_async_copy(yb_ref.at[pl.ds(src, rows), :],
                                      stage_s.at[into, pl.ds(dst, rows), :], sem.at[into]).start()
            return carry

        lax.fori_loop(0, N_EXPERTS, seg_body, 0)

    @pl.when(t == 0)
    def _():
        stage_s[...] = jnp.zeros_like(stage_s)
        fetch(0, 0)

    @pl.when(t + 1 < nt)
    def _():
        fetch(t + 1, 1 - slot)

    fetched = _tile_rows_total(tbl_ref, t)
    pltpu.make_async_copy(yb_ref.at[pl.ds(0, fetched), :], stage_s.at[slot, pl.ds(0, fetched), :],
                          sem.at[slot]).wait()
    rows_sorted = _unpack_pairs(stage_s[slot])
    pg = pg_ref[...]
    s_iota = lax.broadcasted_iota(jnp.int32, (ROUTE_TILE, SLOTS), 1).astype(F32)
    sel = jnp.zeros((ROUTE_TILE, SLOTS), F32)
    for k in range(TOP_K):
        sel = jnp.where(s_iota == pg[:, k:k + 1], pg[:, TOP_K + k:TOP_K + k + 1], sel)
    y = _dot(sel.astype(BF16), rows_sorted)
    x = x_ref[...] + mod_ref[5:6, :] * y
    if split_tiles is None:
        rest[0][...] = x
    else:
        x = _rms(x, fw_ref[...])

        @pl.when(t < split_tiles)
        def _():
            rest[0][...] = x

        @pl.when(t >= split_tiles)
        def _():
            rest[1][...] = x


def _combine(x, pg, seg_tbl, yb, tile_row, mod_l, final_w, n_prompt=None):
    n = x.shape[0]
    tm = ROUTE_TILE
    if n_prompt is None:
        split = None
        out_specs = pl.BlockSpec((tm, D_MODEL), lambda i, row, tbl: (i, 0))
        out_shape = jax.ShapeDtypeStruct((n, D_MODEL), F32)
    else:
        split = n_prompt // tm
        out_specs = [
            pl.BlockSpec((tm, D_MODEL), lambda i, row, tbl: (jnp.minimum(i, split - 1), 0)),
            pl.BlockSpec((tm, D_MODEL), lambda i, row, tbl: (jnp.maximum(i - split, 0), 0)),
        ]
        out_shape = [jax.ShapeDtypeStruct((n_prompt, D_MODEL), F32),
                     jax.ShapeDtypeStruct((n - n_prompt, D_MODEL), F32)]
    return pl.pallas_call(
        functools.partial(_combine_kernel, split_tiles=split),
        grid_spec=pltpu.PrefetchScalarGridSpec(
            num_scalar_prefetch=2,
            grid=(n // tm,),
            in_specs=[
                pl.BlockSpec((tm, D_MODEL), lambda i, row, tbl: (i, 0)),
                pl.BlockSpec((tm, LANES), lambda i, row, tbl: (i, 0)),
                pl.BlockSpec((None, 6, D_MODEL), lambda i, row, tbl: (row[i], 0, 0)),
                pl.BlockSpec((1, D_MODEL), lambda i, row, tbl: (0, 0)),
                pl.BlockSpec(memory_space=pl.ANY),
            ],
            out_specs=out_specs,
            scratch_shapes=[pltpu.VMEM((2, SLOTS, PACKED), jnp.uint32), pltpu.SemaphoreType.DMA((2,))],
        ),
        out_shape=out_shape,
        compiler_params=_cparams("arbitrary"),
        name="combine",
    )(tile_row, seg_tbl, x, pg, mod_l, final_w.reshape(1, D_MODEL), yb)


def _moe(routed, tile_row, mod_l, layer, w_gu, b_gu, w_down, b_down, final_w, n_prompt=None):
    x, hf, pg, pgt, cnt = routed
    n = x.shape[0]

    te = EXPERT_TILE
    nt = n // ROUTE_TILE
    n_blocks = (n * TOP_K + nt * N_EXPERTS * (SEG_ALIGN - 1)) // te + 1 + N_EXPERTS
    cnt = cnt[:, 0, :N_EXPERTS]
    cnt = (cnt + SEG_ALIGN - 1) // SEG_ALIGN * SEG_ALIGN
    total = jnp.sum(cnt, axis=0)
    blocks_e = (total + te - 1) // te
    block_end = jnp.cumsum(blocks_e)
    pstart = (block_end - blocks_e) * te
    n_used = block_end[-1]
    first_row = pstart[None, :] + jnp.cumsum(cnt, axis=0) - cnt
    tile_off = jnp.cumsum(cnt, axis=1) - cnt
    tile_total = jnp.broadcast_to(jnp.sum(cnt, axis=1, keepdims=True), cnt.shape)
    seg_tbl = jnp.concatenate([cnt, tile_off, first_row, tile_total], axis=1)
    seg_tbl = seg_tbl.reshape(-1).astype(jnp.int32)
    pad_rows = blocks_e * te - total
    zeroed = jnp.sum(pad_rows) + (n_blocks - n_used) * te
    pad_tbl = jnp.concatenate([pstart + total, pad_rows, zeroed[None]]).astype(jnp.int32)
    block_ids = jnp.arange(n_blocks, dtype=jnp.int32)
    clamped = jnp.minimum(block_ids, n_used - 1)
    block_e = jnp.sum((clamped[:, None] >= block_end[None, :]).astype(jnp.int32), axis=1)
    block_e = jnp.minimum(block_e, N_EXPERTS - 1).astype(jnp.int32)
    n_used = n_used.astype(jnp.int32).reshape(1)
    owns = (blocks_e > 0)[None, :]
    ids = jnp.arange(N_EXPERTS, dtype=jnp.int32)[None, :]
    mine = block_e[:, None]
    next_e = jnp.min(jnp.where((ids > mine) & owns, ids, N_EXPERTS), axis=1)
    next_e = jnp.where(next_e < N_EXPERTS, next_e, -1).astype(jnp.int32)
    slot = ((jnp.sum(((ids <= mine) & owns).astype(jnp.int32), axis=1) + 1) % 2).astype(jnp.int32)

    row_end = jnp.sum(jnp.where(ids == mine, (pstart + total)[None, :], 0), axis=1)
    block_rows = jnp.clip(row_end - block_ids * te, 0, te).astype(jnp.int32)

    xb = _dispatch(hf, pgt, seg_tbl, pad_tbl, n_used, n_blocks)
    yb = _experts(xb, block_e, n_used, next_e, slot, block_rows, layer, w_gu, b_gu, w_down, b_down)
    return _combine(x, pg, seg_tbl, yb, tile_row, mod_l, final_w, n_prompt)


def _tile_rows(n_prompt_tok, n_sample_seq, sample_len, tile):
    starts = np.arange(0, n_prompt_tok + n_sample_seq * sample_len, tile)
    row = np.where(starts < n_prompt_tok, 0, 1 + (starts - n_prompt_tok) // sample_len)
    return jnp.asarray(row, dtype=jnp.int32)


def kernel(x_prompt, x_sample, state_hgrn, c, c_ctx, w_mod, b_mod, norm_mix, norm_ffn, hg_w_in,
           hg_lb_logits, hg_gnorm, hg_w_out, cv_w_in, cv_w, cv_w_out, router_w, router_b,
           moe_w_gu, moe_b_gu, moe_w_down, moe_b_down, final_norm):
    bp, tp, d = x_prompt.shape
    bs, ts, _ = x_sample.shape
    depth = w_mod.shape[0]
    n_prompt = bp * tp
    n = n_prompt + bs * ts
    assert d == D_MODEL and depth == 2 and 1 + bs <= 16
    assert n_prompt % SCAN_ROWS == 0 and ts == SCAN_ROWS and tp == SCAN_PIECE

    x_p = x_prompt.reshape(n_prompt, d)
    x_s = x_sample.reshape(bs * ts, d)
    cond16 = jnp.zeros((16, d), F32).at[0].set(c_ctx).at[1:1 + bs].set(c)
    mod = _modulation(cond16, w_mod, b_mod).reshape(depth, 16, 6, d)

    tile_row = _tile_rows(n_prompt, bs, ts, TOKEN_TILE)
    tile_row_route = _tile_rows(n_prompt, bs, ts, ROUTE_TILE)
    starts = np.arange(0, n, TOKEN_TILE)
    tile_width = jnp.asarray(np.where(starts < n_prompt, tp, GRID_W), dtype=jnp.int32)

    lb_all = jnp.cumsum(jax.nn.softmax(hg_lb_logits.astype(F32), axis=1), axis=1)

    w_in_bf = hg_w_in[0].astype(BF16)
    gn = hg_gnorm[0].reshape(-1)
    zero_state = jnp.zeros((SCAN_ROWS // tp, 2, HEADS, HEAD_DIM, HEAD_DIM), F32)
    o_p, s_new = _hgrn_scan(x_p, mod[0], 0, 0, norm_mix[0], w_in_bf, lb_all[0, 0], lb_all[1, 0], gn,
                            zero_state, tp, True)
    o_s, _ = _hgrn_scan(x_s, mod[0], 1, 1, norm_mix[0], w_in_bf, lb_all[0, 0], lb_all[1, 0], gn,
                        state_hgrn[:, 0], ts, False)
    routed = _proj_router(x_p, x_s, o_p, o_s, hg_w_out[0].astype(BF16), tile_row, mod[0], norm_ffn[0],
                          router_w[0], router_b[0])
    x = _moe(routed, tile_row_route, mod[0], 0, moe_w_gu, moe_b_gu, moe_w_down, moe_b_down, final_norm)

    routed = _conv_router(x, tile_row, tile_width, mod[1], norm_mix[1], cv_w_in[0].astype(BF16),
                          cv_w[0], cv_w_out[0].astype(BF16), norm_ffn[1], router_w[1], router_b[1])
    y_p, y_s = _moe(routed, tile_row_route, mod[1], 1, moe_w_gu, moe_b_gu, moe_w_down, moe_b_down,
                    final_norm, n_prompt)

    y_prompt = y_p.reshape(bp, tp, d)
    y_sample = y_s.reshape(bs, ts, d)
    return (y_prompt, y_sample, s_new.reshape(bp, 1, 2, HEADS, HEAD_DIM, HEAD_DIM))
```

```python
import functools

import numpy as np
import jax
import jax.numpy as jnp
from jax import lax
from jax.experimental import pallas as pl
from jax.experimental.pallas import tpu as pltpu

F32 = jnp.float32
BF16 = jnp.bfloat16

D_MODEL = 1024
HEADS = 8
HEAD_DIM = 128
CHUNK = 64
GRID_W = 64
N_EXPERTS = 32
TOP_K = 4
D_FF = 1024
SWIGLU_LIMIT = 7.0
SWIGLU_ALPHA = 1.702
EPS = 1e-6

LANES = 128
SCAN_ROWS = 2048
SCAN_PIECE = 256
TOKEN_TILE = 512
EXPERT_TILE = 1024
EXPERT_CHUNK = 512
EXPERT_SUB = 256
ROUTE_TILE = 256
SEG_ALIGN = 8
SLOTS = ROUTE_TILE * TOP_K + N_EXPERTS * SEG_ALIGN
VMEM_LIMIT = 56 * 1024 * 1024


def _cparams(*sem):
    return pltpu.CompilerParams(dimension_semantics=sem, vmem_limit_bytes=VMEM_LIMIT)


def _sigmoid(x):
    return 1.0 / (1.0 + jnp.exp(-x))


def _rms(x, w):
    return x * lax.rsqrt(jnp.mean(x * x, axis=-1, keepdims=True) + EPS) * w


def _dot(a, b):
    return jnp.dot(a, b, preferred_element_type=F32)


def _dot_nt(a, b):
    return lax.dot_general(a, b, (((1,), (1,)), ((), ())), preferred_element_type=F32)


PACKED = D_MODEL // 2
_HIGH_HALF = 0xFFFF0000


def _pack_pairs(x):
    c = x.shape[1] // 2
    lo = lax.bitcast_convert_type(x[:, :c], jnp.uint32) >> 16
    hi = lax.bitcast_convert_type(x[:, c:], jnp.uint32) & jnp.uint32(_HIGH_HALF)
    return hi | lo


def _unpack_pairs(u):
    lo = lax.bitcast_convert_type(u << 16, F32)
    hi = lax.bitcast_convert_type(u & jnp.uint32(_HIGH_HALF), F32)
    return jnp.concatenate([lo, hi], axis=1).astype(BF16)


def _mod_kernel(cond_ref, w_ref, b_ref, o_ref):
    c = cond_ref[...]
    s = (c * _sigmoid(c)).astype(BF16)
    o_ref[...] = _dot(s, w_ref[...].astype(BF16)) + b_ref[...]


def _modulation(cond16, w_mod, b_mod):
    depth = w_mod.shape[0]
    n_out = w_mod.shape[2]
    tn = 1024
    return pl.pallas_call(
        _mod_kernel,
        grid=(depth, n_out // tn),
        in_specs=[
            pl.BlockSpec((16, D_MODEL), lambda l, j: (0, 0)),
            pl.BlockSpec((None, D_MODEL, tn), lambda l, j: (l, 0, j)),
            pl.BlockSpec((None, 1, tn), lambda l, j: (l, 0, j)),
        ],
        out_specs=pl.BlockSpec((None, 16, tn), lambda l, j: (l, 0, j)),
        out_shape=jax.ShapeDtypeStruct((depth, 16, n_out), F32),
        compiler_params=_cparams("arbitrary", "arbitrary"),
        name="modulation",
    )(cond16, w_mod, b_mod.reshape(depth, 1, n_out))


def _two_group_specs(tm, prompt_tiles):
    return [
        pl.BlockSpec((tm, D_MODEL), lambda i, *_: (jnp.minimum(i, prompt_tiles - 1), 0)),
        pl.BlockSpec((tm, D_MODEL), lambda i, *_: (jnp.maximum(i - prompt_tiles, 0), 0)),
    ]


def _pick_group(a_ref, b_ref, prompt_tiles):
    return jnp.where(pl.program_id(0) < prompt_tiles, a_ref[...], b_ref[...])


def _hgrn_kernel(x_ref, mod_ref, nw_ref, wq_ref, wff_ref, wfb_ref, wv_ref, wg_ref, lbf_ref, lbb_ref,
                 gn_ref, s0_ref, o_ref, so_ref, hm_s, q_s, vt_s, kf_s, lf_s, kb_s, lb_s, of_s, ob_s, st_s,
                 *, seq_len, zero_init):
    rows = x_ref.shape[0]
    piece = SCAN_PIECE
    n_pieces = rows // piece
    cpp = piece // CHUNK
    per_piece_seq = seq_len == piece
    assert per_piece_seq or seq_len == rows

    @pl.when(pl.program_id(1) == 0)
    def _():
        y = _rms(x_ref[...], nw_ref[...])
        hm_s[...] = (y * (1.0 + mod_ref[1:2, :]) + mod_ref[0:1, :]).astype(BF16)

    hm = hm_s[...]

    zq = _dot(hm, wq_ref[...])
    q_s[...] = zq * _sigmoid(zq)
    v = _dot(hm, wv_ref[...])
    for p in range(n_pieces):
        vt_s[p] = v[p * piece:(p + 1) * piece, :].T.astype(BF16)
    for w_ref, lbr, k_s, l_s in ((wff_ref, lbf_ref, kf_s, lf_s), (wfb_ref, lbb_ref, kb_s, lb_s)):
        lb = lbr[...]
        f = lb + (1.0 - lb) * _sigmoid(_dot(hm, w_ref[...]))
        k_s[...] = 1.0 - f
        l_s[...] = jnp.log(f)

    def load_state(j, d):
        for h in range(2):
            if zero_init:
                st_s[d, h] = jnp.zeros((HEAD_DIM, HEAD_DIM), F32)
            else:
                st_s[d, h] = s0_ref[j, d, h].T

    def store_state(j, d):
        for h in range(2):
            so_ref[j, d, h] = st_s[d, h].T

    def piece_body(i, carry):
        r = lax.broadcasted_iota(jnp.int32, (piece, piece), 0)
        c = lax.broadcasted_iota(jnp.int32, (piece, piece), 1)
        same = (r // CHUNK) == (c // CHUNK)
        row_chunk = lax.broadcasted_iota(jnp.int32, (piece, 1), 0) // CHUNK
        col_chunk = lax.broadcasted_iota(jnp.int32, (1, piece), 1) // CHUNK
        for d in range(2):
            k_s, l_s, o_s = ((kf_s, lf_s, of_s), (kb_s, lb_s, ob_s))[d]
            p = i if d == 0 else n_pieces - 1 - i
            if per_piece_seq:
                load_state(p, d)
            sl = pl.ds(pl.multiple_of(p * piece, piece), piece)
            keep = same & ((c <= r) if d == 0 else (c >= r))
            tri = jnp.where(keep, 1.0, 0.0).astype(BF16)
            lf = l_s[sl, :]
            hi = lf.astype(BF16)
            mid = (lf - hi.astype(F32)).astype(BF16)
            b = _dot(tri, hi) + _dot(tri, mid)
            b3 = b.reshape(cpp, CHUNK, 2 * HEAD_DIM)
            edge = CHUNK - 1 if d == 0 else 0
            bl3 = b3[:, edge:edge + 1, :]
            centre = CHUNK // 2 - 1 if d == 0 else CHUNK // 2
            bm3 = b3[:, centre:centre + 1, :]
            q = q_s[sl, :]
            q3 = q.reshape(cpp, CHUNK, 2 * HEAD_DIM)
            k3 = k_s[sl, :].reshape(cpp, CHUNK, 2 * HEAD_DIM)
            flat = lambda a: a.reshape(piece, 2 * HEAD_DIM).astype(BF16)
            qd = (q * jnp.exp(b)).astype(BF16)
            qi = flat(q3 * jnp.exp(b3 - bm3))
            ki = flat(k3 * jnp.exp(bm3 - b3))
            ks = flat(k3 * jnp.exp(bl3 - b3))
            dec = jnp.exp(bl3)
            vt = vt_s[p]
            for h in range(2):
                hs = slice(h * HEAD_DIM, (h + 1) * HEAD_DIM)
                att = jnp.where(keep, _dot_nt(qi[:, hs], ki[:, hs]), 0.0).astype(BF16)
                vt_h = vt[hs, :]
                v_exp = jnp.concatenate(
                    [jnp.where(col_chunk == ci, vt_h, jnp.zeros_like(vt_h)) for ci in range(cpp)], axis=0)
                incr = _dot(v_exp, ks[:, hs])
                st = st_s[d, h]
                before = [None] * cpp
                for ci in (range(cpp) if d == 0 else range(cpp - 1, -1, -1)):
                    before[ci] = st.astype(BF16)
                    st = st * dec[ci, :, hs] + incr[ci * HEAD_DIM:(ci + 1) * HEAD_DIM, :]
                st_s[d, h] = st
                qd_h = qd[:, hs]
                q_exp = [jnp.where(row_chunk == ci, qd_h, jnp.zeros_like(qd_h)) for ci in range(cpp)]
                lhs = jnp.concatenate([att] + q_exp, axis=1)
                rhs_t = jnp.concatenate([vt_h] + before, axis=1)
                o_s[sl, hs] = _dot_nt(lhs, rhs_t)
            if per_piece_seq:
                store_state(p, d)
        return carry

    if not per_piece_seq:
        load_state(0, 0)
        load_state(0, 1)
    lax.fori_loop(0, n_pieces, piece_body, 0, unroll=4)
    if not per_piece_seq:
        store_state(0, 0)
        store_state(0, 1)

    o = of_s[...] + ob_s[...]
    gn = gn_ref[...]
    o = jnp.concatenate(
        [_rms(o[:, h * HEAD_DIM:(h + 1) * HEAD_DIM], gn[:, h * HEAD_DIM:(h + 1) * HEAD_DIM])
         for h in range(2)], axis=1)
    o_ref[...] = (o * _sigmoid(_dot(hm, wg_ref[...]))).astype(o_ref.dtype)


def _hgrn_scan(x, mod_l, mod_row0, mod_row_step, norm_w, w_in_bf, lb_f, lb_b, gnorm, s0, seq_len,
               zero_init):
    n_rows = x.shape[0]
    n_seq_total = n_rows // seq_len
    rows = SCAN_ROWS
    seq_per_step = rows // seq_len
    pair = 2 * HEAD_DIM
    n_pairs = HEADS // 2

    def w_spec(seg):
        return pl.BlockSpec((D_MODEL, pair), lambda sb, hp: (0, seg * n_pairs + hp))

    vec_spec = pl.BlockSpec((1, pair), lambda sb, hp: (0, hp))
    st_block = (seq_per_step, 2, 2, HEAD_DIM, HEAD_DIM)
    st_spec = pl.BlockSpec(st_block, lambda sb, hp: (sb, 0, hp, 0, 0))
    s0_spec = pl.BlockSpec(st_block, (lambda sb, hp: (0, 0, hp, 0, 0)) if zero_init
                           else (lambda sb, hp: (sb, 0, hp, 0, 0)))
    scratch = [
        pltpu.VMEM((rows, D_MODEL), BF16),
        pltpu.VMEM((rows, pair), F32),
        pltpu.VMEM((rows // SCAN_PIECE, pair, SCAN_PIECE), BF16),
        pltpu.VMEM((rows, pair), F32),
        pltpu.VMEM((rows, pair), F32),
        pltpu.VMEM((rows, pair), F32),
        pltpu.VMEM((rows, pair), F32),
        pltpu.VMEM((rows, pair), F32),
        pltpu.VMEM((rows, pair), F32),
        pltpu.VMEM((2, 2, HEAD_DIM, HEAD_DIM), F32),
    ]
    in_specs = [
        pl.BlockSpec((rows, D_MODEL), lambda sb, hp: (sb, 0), pipeline_mode=pl.Buffered(1)),
        pl.BlockSpec((None, 6, D_MODEL), lambda sb, hp: (mod_row0 + sb * mod_row_step, 0, 0)),
        pl.BlockSpec((1, D_MODEL), lambda sb, hp: (0, 0)),
        w_spec(0), w_spec(1), w_spec(2), w_spec(3), w_spec(4),
        vec_spec, vec_spec, vec_spec,
        s0_spec,
    ]
    args = [x, mod_l, norm_w.reshape(1, D_MODEL), w_in_bf, w_in_bf, w_in_bf, w_in_bf, w_in_bf,
            lb_f.reshape(1, D_MODEL), lb_b.reshape(1, D_MODEL), gnorm.reshape(1, D_MODEL), s0]
    return pl.pallas_call(
        functools.partial(_hgrn_kernel, seq_len=seq_len, zero_init=zero_init),
        grid=(n_rows // rows, n_pairs),
        in_specs=in_specs,
        out_specs=[
            pl.BlockSpec((rows, pair), lambda sb, hp: (sb, hp)),
            st_spec,
        ],
        out_shape=[
            jax.ShapeDtypeStruct((n_rows, D_MODEL), BF16),
            jax.ShapeDtypeStruct((n_seq_total, 2, HEADS, HEAD_DIM, HEAD_DIM), F32),
        ],
        scratch_shapes=scratch,
        compiler_params=_cparams("arbitrary", "arbitrary"),
        name="hgrn_scan",
    )(*args)


def _route_rows(x1, half, mod_ref, nw_ref, rwh_ref, rwl_ref, rb_ref, hf_ref, pg_ref, pgt_ref, cnt_ref):
    tm = ROUTE_TILE
    rows = slice(half * tm, (half + 1) * tm)
    hf = _rms(x1, nw_ref[...]) * (1.0 + mod_ref[4:5, :]) + mod_ref[3:4, :]
    hf_hi = hf.astype(BF16)
    hf_ref[rows, :] = hf_hi
    hf_lo = (hf - hf_hi.astype(F32)).astype(BF16)
    logits = (_dot(hf_hi, rwh_ref[...]) + (_dot(hf_hi, rwl_ref[...]) + _dot(hf_lo, rwh_ref[...]))
              + rb_ref[...])
    lane = lax.broadcasted_iota(jnp.int32, (tm, LANES), 1)
    lane_f = lane.astype(F32)
    work = logits
    vals, hots = [], []
    for _ in range(TOP_K):
        m = jnp.max(work, axis=-1, keepdims=True)
        idx_f = jnp.min(jnp.where(work == m, lane_f, float(LANES)), axis=-1, keepdims=True)
        hot = lane_f == idx_f
        vals.append(m)
        hots.append(hot)
        work = jnp.where(hot, -jnp.inf, work)
    ex = [jnp.exp(v - vals[0]) for v in vals]
    inv = 1.0 / (ex[0] + ex[1] + ex[2] + ex[3])

    multi = jnp.where(hots[0] | hots[1] | hots[2] | hots[3], 1.0, 0.0)
    tr = lax.broadcasted_iota(jnp.int32, (tm, tm), 0)
    tc = lax.broadcasted_iota(jnp.int32, (tm, tm), 1)
    earlier = jnp.where(tc < tr, 1.0, 0.0).astype(BF16)
    before = _dot(earlier, multi.astype(BF16))
    cnt = jnp.sum(multi, axis=0, keepdims=True)
    er = lax.broadcasted_iota(jnp.int32, (LANES, LANES), 0)
    ec = lax.broadcasted_iota(jnp.int32, (LANES, LANES), 1)
    lower = jnp.where(er < ec, 1.0, 0.0).astype(BF16)
    cnt_al = jnp.floor((cnt + (SEG_ALIGN - 1)) * (1.0 / SEG_ALIGN)) * SEG_ALIGN
    estart = _dot(jnp.broadcast_to(cnt_al, (8, LANES)).astype(BF16), lower)[0:1, :]
    pos_all = before + estart

    pg = jnp.zeros((tm, LANES), F32)
    for k in range(TOP_K):
        pos = jnp.sum(jnp.where(hots[k], pos_all, 0.0), axis=-1, keepdims=True)
        pg = jnp.where(lane == k, pos, pg)
        pg = jnp.where(lane == TOP_K + k, ex[k] * inv, pg)
    pg_ref[rows, :] = pg
    pgt_ref[half] = pg.T[0:8, :]
    cnt_ref[half] = cnt.astype(jnp.int32)


def _route_tile(x1, route_refs):
    x1_ref = route_refs[-5]
    x1_ref[...] = x1
    for half in range(TOKEN_TILE // ROUTE_TILE):
        _route_rows(x1[half * ROUTE_TILE:(half + 1) * ROUTE_TILE, :], half,
                    *route_refs[:5], *route_refs[-4:])


def _proj_router_kernel(row_ref, xp_ref, xs_ref, op_ref, os_ref, w_ref, mod_ref, *route_refs,
                        prompt_tiles):
    del row_ref
    x = _pick_group(xp_ref, xs_ref, prompt_tiles)
    o = _pick_group(op_ref, os_ref, prompt_tiles)
    _route_tile(x + mod_ref[2:3, :] * _dot(o, w_ref[...]), (mod_ref,) + route_refs)


def _conv_router_kernel(row_ref, width_ref, x_ref, nwm_ref, win_ref, cw_ref, wout_ref, mod_ref,
                        *route_refs):
    del row_ref
    i = pl.program_id(0)
    x = x_ref[...]
    hm = (_rms(x, nwm_ref[...]) * (1.0 + mod_ref[1:2, :]) + mod_ref[0:1, :]).astype(BF16)
    z = _dot(hm, win_ref[...])
    bg = z[:, :D_MODEL]
    u = z[:, D_MODEL:2 * D_MODEL] * z[:, 2 * D_MODEL:]
    tm = x.shape[0]
    pos = lax.broadcasted_iota(jnp.int32, (tm, 1), 0) & (width_ref[i] - 1)
    prev = jnp.where(pos == 0, 0.0, pltpu.roll(u, 1, axis=0))
    nxt = jnp.where(pos == width_ref[i] - 1, 0.0, pltpu.roll(u, tm - 1, axis=0))
    v = cw_ref[0:1, :] * prev + cw_ref[1:2, :] * u + cw_ref[2:3, :] * nxt
    y = _dot((bg * v).astype(BF16), wout_ref[...])
    _route_tile(x + mod_ref[2:3, :] * y, (mod_ref,) + route_refs)


def _mixer_router_call(kernel_fn, n, n_prefetch, prefetch, in_specs, args, mod_l, norm_ffn,
                       router_w, router_b, name):
    tm = TOKEN_TILE
    sub = tm // ROUTE_TILE
    rw = jnp.zeros((D_MODEL, LANES), F32).at[:, :N_EXPERTS].set(router_w)
    rb = jnp.full((1, LANES), -jnp.inf, F32).at[0, :N_EXPERTS].set(router_b)
    rw_hi = rw.astype(BF16)
    rw_lo = (rw - rw_hi.astype(F32)).astype(BF16)
    const = lambda shape: pl.BlockSpec(shape, lambda i, *_: (0,) * len(shape))
    row_blk = lambda width: pl.BlockSpec((tm, width), lambda i, *_: (i, 0))
    return pl.pallas_call(
        kernel_fn,
        grid_spec=pltpu.PrefetchScalarGridSpec(
            num_scalar_prefetch=n_prefetch,
            grid=(n // tm,),
            in_specs=in_specs + [
                pl.BlockSpec((None, 6, D_MODEL), lambda i, row, *_: (row[i], 0, 0)),
                const((1, D_MODEL)), const((D_MODEL, LANES)), const((D_MODEL, LANES)), const((1, LANES)),
            ],
            out_specs=[
                row_blk(D_MODEL), row_blk(D_MODEL), row_blk(LANES),
                pl.BlockSpec((sub, 8, ROUTE_TILE), lambda i, *_: (i, 0, 0)),
                pl.BlockSpec((sub, 1, LANES), lambda i, *_: (i, 0, 0)),
            ],
        ),
        out_shape=[
            jax.ShapeDtypeStruct((n, D_MODEL), F32),
            jax.ShapeDtypeStruct((n, D_MODEL), BF16),
            jax.ShapeDtypeStruct((n, LANES), F32),
            jax.ShapeDtypeStruct((n // ROUTE_TILE, 8, ROUTE_TILE), F32),
            jax.ShapeDtypeStruct((n // ROUTE_TILE, 1, LANES), jnp.int32),
        ],
        compiler_params=_cparams("arbitrary"),
        name=name,
    )(*prefetch, *args, mod_l, norm_ffn.reshape(1, D_MODEL), rw_hi, rw_lo, rb)


def _proj_router(x_p, x_s, o_prompt, o_sample, w_out_bf, tile_row, mod_l, norm_ffn, router_w, router_b):
    n = x_p.shape[0] + x_s.shape[0]
    pt = x_p.shape[0] // TOKEN_TILE
    in_specs = _two_group_specs(TOKEN_TILE, pt) + _two_group_specs(TOKEN_TILE, pt) + [
        pl.BlockSpec((D_MODEL, D_MODEL), lambda i, *_: (0, 0))]
    return _mixer_router_call(functools.partial(_proj_router_kernel, prompt_tiles=pt), n, 1,
                              (tile_row,), in_specs, (x_p, x_s, o_prompt, o_sample, w_out_bf),
                              mod_l, norm_ffn, router_w, router_b, "proj_router")


def _conv_router(x, tile_row, tile_width, mod_l, norm_mix, w_in_bf, conv_w, w_out_bf, norm_ffn,
                 router_w, router_b):
    n = x.shape[0]
    in_specs = [
        pl.BlockSpec((TOKEN_TILE, D_MODEL), lambda i, *_: (i, 0)),
        pl.BlockSpec((1, D_MODEL), lambda i, *_: (0, 0)),
        pl.BlockSpec((D_MODEL, 3 * D_MODEL), lambda i, *_: (0, 0)),
        pl.BlockSpec((3, D_MODEL), lambda i, *_: (0, 0)),
        pl.BlockSpec((D_MODEL, D_MODEL), lambda i, *_: (0, 0)),
    ]
    return _mixer_router_call(_conv_router_kernel, n, 2, (tile_row, tile_width), in_specs,
                              (x, norm_mix.reshape(1, D_MODEL), w_in_bf, conv_w, w_out_bf),
                              mod_l, norm_ffn, router_w, router_b, "conv_router")


SEG_STRIDE = 4 * N_EXPERTS


def _segment_table(tbl_ref, t, e):
    base = t * SEG_STRIDE
    al = lambda v: pl.multiple_of(v, SEG_ALIGN)
    return (al(tbl_ref[base + e]), al(tbl_ref[base + N_EXPERTS + e]),
            al(tbl_ref[base + 2 * N_EXPERTS + e]))


def _tile_rows_total(tbl_ref, t):
    return pl.multiple_of(tbl_ref[t * SEG_STRIDE + 3 * N_EXPERTS], SEG_ALIGN)


def _dispatch_kernel(tbl_ref, pad_ref, nu_ref, hf_ref, pgt_ref, xb_ref, stage_s, zero_s, sem, zsem,
                     *, n_blocks):
    t = pl.program_id(0)
    nt = pl.num_programs(0)
    slot = t % 2
    half = zero_s.shape[0]

    def zero_rows(start, rows):
        start = pl.multiple_of(start, SEG_ALIGN)
        rows = pl.multiple_of(rows, SEG_ALIGN)

        @pl.when(rows > 0)
        def _():
            pltpu.make_async_copy(zero_s.at[pl.ds(0, rows), :], xb_ref.at[pl.ds(start, rows), :],
                                  zsem).start()

    @pl.when(t == 0)
    def _():
        zero_s[...] = jnp.zeros_like(zero_s)

        def pad_body(e, carry):
            start = pad_ref[e]
            rows = pad_ref[N_EXPERTS + e]
            first = jnp.minimum(rows, half)
            zero_rows(start, first)
            zero_rows(start + half, rows - first)
            return carry

        lax.fori_loop(0, N_EXPERTS, pad_body, 0)

        def tail_body(b, carry):
            for part in range(EXPERT_TILE // half):
                zero_rows(b * EXPERT_TILE + part * half, half)
            return carry

        lax.fori_loop(nu_ref[0], n_blocks, tail_body, 0)

    def wait_tile(tile, sl):
        rows = _tile_rows_total(tbl_ref, tile)
        pltpu.make_async_copy(stage_s.at[sl, pl.ds(0, rows), :], xb_ref.at[pl.ds(0, rows), :],
                              sem.at[sl]).wait()

    @pl.when(t >= 2)
    def _():
        wait_tile(t - 2, slot)

    s_iota = lax.broadcasted_iota(jnp.int32, (SLOTS, ROUTE_TILE), 0).astype(F32)
    pgt = pgt_ref[...]
    hit = s_iota == pgt[0:1, :]
    for k in range(1, TOP_K):
        hit = hit | (s_iota == pgt[k:k + 1, :])
    perm = jnp.where(hit, 1.0, 0.0).astype(BF16)
    stage_s[slot] = _pack_pairs(_dot(perm, hf_ref[...]))

    def seg_body(e, carry):
        rows, src, dst = _segment_table(tbl_ref, t, e)

        @pl.when(rows > 0)
        def _():
            pltpu.make_async_copy(stage_s.at[slot, pl.ds(src, rows), :],
                                  xb_ref.at[pl.ds(dst, rows), :], sem.at[slot]).start()
        return carry

    lax.fori_loop(0, N_EXPERTS, seg_body, 0)

    @pl.when(t == nt - 1)
    def _():
        wait_tile(t, slot)

        @pl.when(nt > 1)
        def _():
            wait_tile(t - 1, 1 - slot)

        zeroed = pl.multiple_of(pad_ref[2 * N_EXPERTS], SEG_ALIGN)

        @pl.when(zeroed > 0)
        def _():
            pltpu.make_async_copy(xb_ref.at[pl.ds(0, zeroed), :], xb_ref.at[pl.ds(0, zeroed), :],
                                  zsem).wait()


def _dispatch(hf, pgt, seg_tbl, pad_tbl, n_used, n_blocks):
    n = hf.shape[0]
    tm = ROUTE_TILE
    return pl.pallas_call(
        functools.partial(_dispatch_kernel, n_blocks=n_blocks),
        grid_spec=pltpu.PrefetchScalarGridSpec(
            num_scalar_prefetch=3,
            grid=(n // tm,),
            in_specs=[
                pl.BlockSpec((tm, D_MODEL), lambda i, a, b, c: (i, 0)),
                pl.BlockSpec((None, 8, tm), lambda i, a, b, c: (i, 0, 0)),
            ],
            out_specs=pl.BlockSpec(memory_space=pl.ANY),
            scratch_shapes=[
                pltpu.VMEM((2, SLOTS, PACKED), jnp.uint32),
                pltpu.VMEM((EXPERT_TILE // 2, PACKED), jnp.uint32),
                pltpu.SemaphoreType.DMA((2,)),
                pltpu.SemaphoreType.DMA,
            ],
        ),
        out_shape=jax.ShapeDtypeStruct((n_blocks * EXPERT_TILE, PACKED), jnp.uint32),
        compiler_params=_cparams("arbitrary"),
        name="dispatch",
    )(seg_tbl, pad_tbl, n_used, hf, pgt)


def _expert_kernel(be_ref, nu_ref, nxt_ref, slot_ref, rows_ref, xb_ref, wgu_hbm, bgu_ref, wd_hbm, bd_ref,
                   yb_ref, wgu_f, wd_f, wgu_s, wd_s, sem, *, layer):
    i = pl.program_id(0)

    def fetch(e, slot):
        return (pltpu.make_async_copy(wgu_hbm.at[layer, e], wgu_f.at[slot], sem.at[slot]),
                pltpu.make_async_copy(wd_hbm.at[layer, e], wd_f.at[slot], sem.at[slot]))

    @pl.when(i < nu_ref[0])
    def _():
        e = be_ref[i]
        slot = slot_ref[i]
        first_block = jnp.logical_or(i == 0, e != be_ref[jnp.maximum(i - 1, 0)])

        @pl.when(first_block)
        def _():
            @pl.when(i == 0)
            def _():
                for cp in fetch(e, slot):
                    cp.start()

            for cp in fetch(e, slot):
                cp.wait()
            wgu_s[...] = wgu_f[slot].astype(BF16)
            wd_s[...] = wd_f[slot].astype(BF16)

            @pl.when(nxt_ref[i] >= 0)
            def _():
                for cp in fetch(nxt_ref[i], 1 - slot):
                    cp.start()

        def ffn(sl):
            gu = _dot(_unpack_pairs(xb_ref[sl, :]), wgu_s[...]) + bgu_ref[...]
            gate = jnp.minimum(gu[:, :D_FF], SWIGLU_LIMIT)
            up = jnp.clip(gu[:, D_FF:], -SWIGLU_LIMIT, SWIGLU_LIMIT)
            act = (up + 1.0) * gate * _sigmoid(SWIGLU_ALPHA * gate)
            y = _dot(act.astype(BF16), wd_s[...]) + bd_ref[...]
            yb_ref[sl, :] = _pack_pairs(y.astype(BF16).astype(F32))

        n_sub = EXPERT_TILE // EXPERT_SUB
        for k in range(1, n_sub + 1):
            used = k * EXPERT_SUB
            holds_k_parts = rows_ref[i] > used - EXPERT_SUB
            if k < n_sub:
                holds_k_parts = jnp.logical_and(holds_k_parts, rows_ref[i] <= used)

            @pl.when(holds_k_parts)
            def _(used=used):
                for start in range(0, used, EXPERT_CHUNK):
                    ffn(slice(start, min(start + EXPERT_CHUNK, used)))
                if used < EXPERT_TILE:
                    yb_ref[used:, :] = jnp.zeros((EXPERT_TILE - used, PACKED), jnp.uint32)

    @pl.when(i >= nu_ref[0])
    def _():
        yb_ref[...] = jnp.zeros_like(yb_ref)


def _experts(xb, block_e, n_used, next_e, slot, block_rows, layer, w_gu, b_gu, w_down, b_down):
    te = EXPERT_TILE
    n_blocks = xb.shape[0] // te
    depth = w_gu.shape[0]

    def x_map(i, be, nu, *_):
        return (jnp.maximum(jnp.minimum(i, nu[0] - 1), 0), 0)

    def bias_spec(width):
        return pl.BlockSpec((None, None, 1, width), lambda i, be, *_: (layer, be[i], 0, 0))

    return pl.pallas_call(
        functools.partial(_expert_kernel, layer=layer),
        grid_spec=pltpu.PrefetchScalarGridSpec(
            num_scalar_prefetch=5,
            grid=(n_blocks,),
            in_specs=[
                pl.BlockSpec((te, PACKED), x_map),
                pl.BlockSpec(memory_space=pl.ANY),
                bias_spec(2 * D_FF),
                pl.BlockSpec(memory_space=pl.ANY),
                bias_spec(D_MODEL),
            ],
            out_specs=pl.BlockSpec((te, PACKED), lambda i, *_: (i, 0)),
            scratch_shapes=[
                pltpu.VMEM((2, D_MODEL, 2 * D_FF), F32),
                pltpu.VMEM((2, D_FF, D_MODEL), F32),
                pltpu.VMEM((D_MODEL, 2 * D_FF), BF16),
                pltpu.VMEM((D_FF, D_MODEL), BF16),
                pltpu.SemaphoreType.DMA((2,)),
            ],
        ),
        out_shape=jax.ShapeDtypeStruct(xb.shape, jnp.uint32),
        compiler_params=_cparams("arbitrary"),
        name="experts",
    )(block_e, n_used, next_e, slot, block_rows, xb, w_gu,
      b_gu.reshape(depth, N_EXPERTS, 1, 2 * D_FF), w_down, b_down.reshape(depth, N_EXPERTS, 1, D_MODEL))


def _combine_kernel(row_ref, tbl_ref, x_ref, pg_ref, mod_ref, fw_ref, yb_ref, *rest, split_tiles):
    del row_ref
    stage_s, sem = rest[-2:]
    t = pl.program_id(0)
    nt = pl.num_programs(0)
    slot = t % 2

    def fetch(tile, into):
        def seg_body(e, carry):
            rows, dst, src = _segment_table(tbl_ref, tile, e)

            @pl.when(rows > 0)
            def _():
                pltpu.make_async_copy(yb_ref.at[pl.ds(src, rows), :],
                                      stage_s.at[into, pl.ds(dst, rows), :], sem.at[into]).start()
            return carry

        lax.fori_loop(0, N_EXPERTS, seg_body, 0)

    @pl.when(t == 0)
    def _():
        stage_s[...] = jnp.zeros_like(stage_s)
        fetch(0, 0)

    @pl.when(t + 1 < nt)
    def _():
        fetch(t + 1, 1 - slot)

    fetched = _tile_rows_total(tbl_ref, t)
    pltpu.make_async_copy(yb_ref.at[pl.ds(0, fetched), :], stage_s.at[slot, pl.ds(0, fetched), :],
                          sem.at[slot]).wait()
    rows_sorted = _unpack_pairs(stage_s[slot])
    pg = pg_ref[...]
    s_iota = lax.broadcasted_iota(jnp.int32, (ROUTE_TILE, SLOTS), 1).astype(F32)
    sel = jnp.zeros((ROUTE_TILE, SLOTS), F32)
    for k in range(TOP_K):
        sel = jnp.where(s_iota == pg[:, k:k + 1], pg[:, TOP_K + k:TOP_K + k + 1], sel)
    y = _dot(sel.astype(BF16), rows_sorted)
    x = x_ref[...] + mod_ref[5:6, :] * y
    if split_tiles is None:
        rest[0][...] = x
    else:
        x = _rms(x, fw_ref[...])

        @pl.when(t < split_tiles)
        def _():
            rest[0][...] = x

        @pl.when(t >= split_tiles)
        def _():
            rest[1][...] = x


def _combine(x, pg, seg_tbl, yb, tile_row, mod_l, final_w, n_prompt=None):
    n = x.shape[0]
    tm = ROUTE_TILE
    if n_prompt is None:
        split = None
        out_specs = pl.BlockSpec((tm, D_MODEL), lambda i, row, tbl: (i, 0))
        out_shape = jax.ShapeDtypeStruct((n, D_MODEL), F32)
    else:
        split = n_prompt // tm
        out_specs = [
            pl.BlockSpec((tm, D_MODEL), lambda i, row, tbl: (jnp.minimum(i, split - 1), 0)),
            pl.BlockSpec((tm, D_MODEL), lambda i, row, tbl: (jnp.maximum(i - split, 0), 0)),
        ]
        out_shape = [jax.ShapeDtypeStruct((n_prompt, D_MODEL), F32),
                     jax.ShapeDtypeStruct((n - n_prompt, D_MODEL), F32)]
    return pl.pallas_call(
        functools.partial(_combine_kernel, split_tiles=split),
        grid_spec=pltpu.PrefetchScalarGridSpec(
            num_scalar_prefetch=2,
            grid=(n // tm,),
            in_specs=[
                pl.BlockSpec((tm, D_MODEL), lambda i, row, tbl: (i, 0)),
                pl.BlockSpec((tm, LANES), lambda i, row, tbl: (i, 0)),
                pl.BlockSpec((None, 6, D_MODEL), lambda i, row, tbl: (row[i], 0, 0)),
                pl.BlockSpec((1, D_MODEL), lambda i, row, tbl: (0, 0)),
                pl.BlockSpec(memory_space=pl.ANY),
            ],
            out_specs=out_specs,
            scratch_shapes=[pltpu.VMEM((2, SLOTS, PACKED), jnp.uint32), pltpu.SemaphoreType.DMA((2,))],
        ),
        out_shape=out_shape,
        compiler_params=_cparams("arbitrary"),
        name="combine",
    )(tile_row, seg_tbl, x, pg, mod_l, final_w.reshape(1, D_MODEL), yb)


def _moe(routed, tile_row, mod_l, layer, w_gu, b_gu, w_down, b_down, final_w, n_prompt=None):
    x, hf, pg, pgt, cnt = routed
    n = x.shape[0]

    te = EXPERT_TILE
    nt = n // ROUTE_TILE
    n_blocks = (n * TOP_K + nt * N_EXPERTS * (SEG_ALIGN - 1)) // te + 1 + N_EXPERTS
    cnt = cnt[:, 0, :N_EXPERTS]
    cnt = (cnt + SEG_ALIGN - 1) // SEG_ALIGN * SEG_ALIGN
    total = jnp.sum(cnt, axis=0)
    blocks_e = (total + te - 1) // te
    block_end = jnp.cumsum(blocks_e)
    pstart = (block_end - blocks_e) * te
    n_used = block_end[-1]
    first_row = pstart[None, :] + jnp.cumsum(cnt, axis=0) - cnt
    tile_off = jnp.cumsum(cnt, axis=1) - cnt
    tile_total = jnp.broadcast_to(jnp.sum(cnt, axis=1, keepdims=True), cnt.shape)
    seg_tbl = jnp.concatenate([cnt, tile_off, first_row, tile_total], axis=1)
    seg_tbl = seg_tbl.reshape(-1).astype(jnp.int32)
    pad_rows = blocks_e * te - total
    zeroed = jnp.sum(pad_rows) + (n_blocks - n_used) * te
    pad_tbl = jnp.concatenate([pstart + total, pad_rows, zeroed[None]]).astype(jnp.int32)
    block_ids = jnp.arange(n_blocks, dtype=jnp.int32)
    clamped = jnp.minimum(block_ids, n_used - 1)
    block_e = jnp.sum((clamped[:, None] >= block_end[None, :]).astype(jnp.int32), axis=1)
    block_e = jnp.minimum(block_e, N_EXPERTS - 1).astype(jnp.int32)
    n_used = n_used.astype(jnp.int32).reshape(1)
    owns = (blocks_e > 0)[None, :]
    ids = jnp.arange(N_EXPERTS, dtype=jnp.int32)[None, :]
    mine = block_e[:, None]
    next_e = jnp.min(jnp.where((ids > mine) & owns, ids, N_EXPERTS), axis=1)
    next_e = jnp.where(next_e < N_EXPERTS, next_e, -1).astype(jnp.int32)
    slot = ((jnp.sum(((ids <= mine) & owns).astype(jnp.int32), axis=1) + 1) % 2).astype(jnp.int32)

    row_end = jnp.sum(jnp.where(ids == mine, (pstart + total)[None, :], 0), axis=1)
    block_rows = jnp.clip(row_end - block_ids * te, 0, te).astype(jnp.int32)

    xb = _dispatch(hf, pgt, seg_tbl, pad_tbl, n_used, n_blocks)
    yb = _experts(xb, block_e, n_used, next_e, slot, block_rows, layer, w_gu, b_gu, w_down, b_down)
    return _combine(x, pg, seg_tbl, yb, tile_row, mod_l, final_w, n_prompt)


def _tile_rows(n_prompt_tok, n_sample_seq, sample_len, tile):
    starts = np.arange(0, n_prompt_tok + n_sample_seq * sample_len, tile)
    row = np.where(starts < n_prompt_tok, 0, 1 + (starts - n_prompt_tok) // sample_len)
    return jnp.asarray(row, dtype=jnp.int32)


def kernel(x_prompt, x_sample, state_hgrn, c, c_ctx, w_mod, b_mod, norm_mix, norm_ffn, hg_w_in,
           hg_lb_logits, hg_gnorm, hg_w_out, cv_w_in, cv_w, cv_w_out, router_w, router_b,
           moe_w_gu, moe_b_gu, moe_w_down, moe_b_down, final_norm):
    bp, tp, d = x_prompt.shape
    bs, ts, _ = x_sample.shape
    depth = w_mod.shape[0]
    n_prompt = bp * tp
    n = n_prompt + bs * ts
    assert d == D_MODEL and depth == 2 and 1 + bs <= 16
    assert n_prompt % SCAN_ROWS == 0 and ts == SCAN_ROWS and tp == SCAN_PIECE

    x_p = x_prompt.reshape(n_prompt, d)
    x_s = x_sample.reshape(bs * ts, d)
    cond16 = jnp.zeros((16, d), F32).at[0].set(c_ctx).at[1:1 + bs].set(c)
    mod = _modulation(cond16, w_mod, b_mod).reshape(depth, 16, 6, d)

    tile_row = _tile_rows(n_prompt, bs, ts, TOKEN_TILE)
    tile_row_route = _tile_rows(n_prompt, bs, ts, ROUTE_TILE)
    starts = np.arange(0, n, TOKEN_TILE)
    tile_width = jnp.asarray(np.where(starts < n_prompt, tp, GRID_W), dtype=jnp.int32)

    lb_all = jnp.cumsum(jax.nn.softmax(hg_lb_logits.astype(F32), axis=1), axis=1)

    w_in_bf = hg_w_in[0].astype(BF16)
    gn = hg_gnorm[0].reshape(-1)
    zero_state = jnp.zeros((SCAN_ROWS // tp, 2, HEADS, HEAD_DIM, HEAD_DIM), F32)
    o_p, s_new = _hgrn_scan(x_p, mod[0], 0, 0, norm_mix[0], w_in_bf, lb_all[0, 0], lb_all[1, 0], gn,
                            zero_state, tp, True)
    o_s, _ = _hgrn_scan(x_s, mod[0], 1, 1, norm_mix[0], w_in_bf, lb_all[0, 0], lb_all[1, 0], gn,
                        state_hgrn[:, 0], ts, False)
    routed = _proj_router(x_p, x_s, o_p, o_s, hg_w_out[0].astype(BF16), tile_row, mod[0], norm_ffn[0],
                          router_w[0], router_b[0])
    x = _moe(routed, tile_row_route, mod[0], 0, moe_w_gu, moe_b_gu, moe_w_down, moe_b_down, final_norm)

    routed = _conv_router(x, tile_row, tile_width, mod[1], norm_mix[1], cv_w_in[0].astype(BF16),
                          cv_w[0], cv_w_out[0].astype(BF16), norm_ffn[1], router_w[1], router_b[1])
    y_p, y_s = _moe(routed, tile_row_route, mod[1], 1, moe_w_gu, moe_b_gu, moe_w_down, moe_b_down,
                    final_norm, n_prompt)

    y_prompt = y_p.reshape(bp, tp, d)
    y_sample = y_s.reshape(bs, ts, d)
    return (y_prompt, y_sample, s_new.reshape(bp, 1, 2, HEADS, HEAD_DIM, HEAD_DIM))
```

```python
import functools

import numpy as np
import jax
import jax.numpy as jnp
from jax import lax
from jax.experimental import pallas as pl
from jax.experimental.pallas import tpu as pltpu

F32 = jnp.float32
BF16 = jnp.bfloat16

D_MODEL = 1024
HEADS = 8
HEAD_DIM = 128
CHUNK = 64
GRID_W = 64
N_EXPERTS = 32
TOP_K = 4
D_FF = 1024
SWIGLU_LIMIT = 7.0
SWIGLU_ALPHA = 1.702
EPS = 1e-6

LANES = 128
SCAN_ROWS = 2048
SCAN_PIECE = 256
TOKEN_TILE = 512
EXPERT_TILE = 1024
EXPERT_CHUNK = 512
EXPERT_SUB = 256
ROUTE_TILE = 256
SEG_ALIGN = 8
SLOTS = ROUTE_TILE * TOP_K + N_EXPERTS * SEG_ALIGN
VMEM_LIMIT = 56 * 1024 * 1024


def _cparams(*sem):
    return pltpu.CompilerParams(dimension_semantics=sem, vmem_limit_bytes=VMEM_LIMIT)


def _sigmoid(x):
    return 1.0 / (1.0 + jnp.exp(-x))


def _rms(x, w):
    return x * lax.rsqrt(jnp.mean(x * x, axis=-1, keepdims=True) + EPS) * w


def _dot(a, b):
    return jnp.dot(a, b, preferred_element_type=F32)


def _dot_nt(a, b):
    return lax.dot_general(a, b, (((1,), (1,)), ((), ())), preferred_element_type=F32)


PACKED = D_MODEL // 2
_HIGH_HALF = 0xFFFF0000


def _pack_pairs(x):
    c = x.shape[1] // 2
    lo = lax.bitcast_convert_type(x[:, :c], jnp.uint32) >> 16
    hi = lax.bitcast_convert_type(x[:, c:], jnp.uint32) & jnp.uint32(_HIGH_HALF)
    return hi | lo


def _unpack_pairs(u):
    lo = lax.bitcast_convert_type(u << 16, F32)
    hi = lax.bitcast_convert_type(u & jnp.uint32(_HIGH_HALF), F32)
    return jnp.concatenate([lo, hi], axis=1).astype(BF16)


def _mod_kernel(cond_ref, w_ref, b_ref, o_ref):
    c = cond_ref[...]
    s = (c * _sigmoid(c)).astype(BF16)
    o_ref[...] = _dot(s, w_ref[...].astype(BF16)) + b_ref[...]


def _modulation(cond16, w_mod, b_mod):
    depth = w_mod.shape[0]
    n_out = w_mod.shape[2]
    tn = 1024
    return pl.pallas_call(
        _mod_kernel,
        grid=(depth, n_out // tn),
        in_specs=[
            pl.BlockSpec((16, D_MODEL), lambda l, j: (0, 0)),
            pl.BlockSpec((None, D_MODEL, tn), lambda l, j: (l, 0, j)),
            pl.BlockSpec((None, 1, tn), lambda l, j: (l, 0, j)),
        ],
        out_specs=pl.BlockSpec((None, 16, tn), lambda l, j: (l, 0, j)),
        out_shape=jax.ShapeDtypeStruct((depth, 16, n_out), F32),
        compiler_params=_cparams("arbitrary", "arbitrary"),
        name="modulation",
    )(cond16, w_mod, b_mod.reshape(depth, 1, n_out))


def _two_group_specs(tm, prompt_tiles):
    return [
        pl.BlockSpec((tm, D_MODEL), lambda i, *_: (jnp.minimum(i, prompt_tiles - 1), 0)),
        pl.BlockSpec((tm, D_MODEL), lambda i, *_: (jnp.maximum(i - prompt_tiles, 0), 0)),
    ]


def _pick_group(a_ref, b_ref, prompt_tiles):
    return jnp.where(pl.program_id(0) < prompt_tiles, a_ref[...], b_ref[...])


def _hgrn_kernel(x_ref, mod_ref, nw_ref, wq_ref, wff_ref, wfb_ref, wv_ref, wg_ref, lbf_ref, lbb_ref,
                 gn_ref, s0_ref, o_ref, so_ref, hm_s, q_s, vt_s, kf_s, lf_s, kb_s, lb_s, of_s, ob_s, st_s,
                 *, seq_len, zero_init):
    rows = x_ref.shape[0]
    piece = SCAN_PIECE
    n_pieces = rows // piece
    cpp = piece // CHUNK
    per_piece_seq = seq_len == piece
    assert per_piece_seq or seq_len == rows

    @pl.when(pl.program_id(1) == 0)
    def _():
        y = _rms(x_ref[...], nw_ref[...])
        hm_s[...] = (y * (1.0 + mod_ref[1:2, :]) + mod_ref[0:1, :]).astype(BF16)

    hm = hm_s[...]

    zq = _dot(hm, wq_ref[...])
    q_s[...] = zq * _sigmoid(zq)
    v = _dot(hm, wv_ref[...])
    for p in range(n_pieces):
        vt_s[p] = v[p * piece:(p + 1) * piece, :].T.astype(BF16)
    for w_ref, lbr, k_s, l_s in ((wff_ref, lbf_ref, kf_s, lf_s), (wfb_ref, lbb_ref, kb_s, lb_s)):
        lb = lbr[...]
        f = lb + (1.0 - lb) * _sigmoid(_dot(hm, w_ref[...]))
        k_s[...] = 1.0 - f
        l_s[...] = jnp.log(f)

    def load_state(j, d):
        for h in range(2):
            if zero_init:
                st_s[d, h] = jnp.zeros((HEAD_DIM, HEAD_DIM), F32)
            else:
                st_s[d, h] = s0_ref[j, d, h].T

    def store_state(j, d):
        for h in range(2):
            so_ref[j, d, h] = st_s[d, h].T

    def piece_body(i, carry):
        r = lax.broadcasted_iota(jnp.int32, (piece, piece), 0)
        c = lax.broadcasted_iota(jnp.int32, (piece, piece), 1)
        same = (r // CHUNK) == (c // CHUNK)
        row_chunk = lax.broadcasted_iota(jnp.int32, (piece, 1), 0) // CHUNK
        col_chunk = lax.broadcasted_iota(jnp.int32, (1, piece), 1) // CHUNK
        for d in range(2):
            k_s, l_s, o_s = ((kf_s, lf_s, of_s), (kb_s, lb_s, ob_s))[d]
            p = i if d == 0 else n_pieces - 1 - i
            if per_piece_seq:
                load_state(p, d)
            sl = pl.ds(pl.multiple_of(p * piece, piece), piece)
            keep = same & ((c <= r) if d == 0 else (c >= r))
            tri = jnp.where(keep, 1.0, 0.0).astype(BF16)
            lf = l_s[sl, :]
            hi = lf.astype(BF16)
            mid = (lf - hi.astype(F32)).astype(BF16)
            b = _dot(tri, hi) + _dot(tri, mid)
            b3 = b.reshape(cpp, CHUNK, 2 * HEAD_DIM)
            edge = CHUNK - 1 if d == 0 else 0
            bl3 = b3[:, edge:edge + 1, :]
            centre = CHUNK // 2 - 1 if d == 0 else CHUNK // 2
            bm3 = b3[:, centre:centre + 1, :]
            q = q_s[sl, :]
            q3 = q.reshape(cpp, CHUNK, 2 * HEAD_DIM)
            k3 = k_s[sl, :].reshape(cpp, CHUNK, 2 * HEAD_DIM)
            flat = lambda a: a.reshape(piece, 2 * HEAD_DIM).astype(BF16)
            qd = (q * jnp.exp(b)).astype(BF16)
            qi = flat(q3 * jnp.exp(b3 - bm3))
            ki = flat(k3 * jnp.exp(bm3 - b3))
            ks = flat(k3 * jnp.exp(bl3 - b3))
            dec = jnp.exp(bl3)
            vt = vt_s[p]
            for h in range(2):
                hs = slice(h * HEAD_DIM, (h + 1) * HEAD_DIM)
                att = jnp.where(keep, _dot_nt(qi[:, hs], ki[:, hs]), 0.0).astype(BF16)
                vt_h = vt[hs, :]
                v_exp = jnp.concatenate(
                    [jnp.where(col_chunk == ci, vt_h, jnp.zeros_like(vt_h)) for ci in range(cpp)], axis=0)
                incr = _dot(v_exp, ks[:, hs])
                st = st_s[d, h]
                before = [None] * cpp
                for ci in (range(cpp) if d == 0 else range(cpp - 1, -1, -1)):
                    before[ci] = st.astype(BF16)
                    st = st * dec[ci, :, hs] + incr[ci * HEAD_DIM:(ci + 1) * HEAD_DIM, :]
                st_s[d, h] = st
                qd_h = qd[:, hs]
                q_exp = [jnp.where(row_chunk == ci, qd_h, jnp.zeros_like(qd_h)) for ci in range(cpp)]
                lhs = jnp.concatenate([att] + q_exp, axis=1)
                rhs_t = jnp.concatenate([vt_h] + before, axis=1)
                o_s[sl, hs] = _dot_nt(lhs, rhs_t)
            if per_piece_seq:
                store_state(p, d)
        return carry

    if not per_piece_seq:
        load_state(0, 0)
        load_state(0, 1)
    lax.fori_loop(0, n_pieces, piece_body, 0, unroll=4)
    if not per_piece_seq:
        store_state(0, 0)
        store_state(0, 1)

    o = of_s[...] + ob_s[...]
    gn = gn_ref[...]
    o = jnp.concatenate(
        [_rms(o[:, h * HEAD_DIM:(h + 1) * HEAD_DIM], gn[:, h * HEAD_DIM:(h + 1) * HEAD_DIM])
         for h in range(2)], axis=1)
    o_ref[...] = (o * _sigmoid(_dot(hm, wg_ref[...]))).astype(o_ref.dtype)


def _hgrn_scan(x, mod_l, mod_row0, mod_row_step, norm_w, w_in_bf, lb_f, lb_b, gnorm, s0, seq_len,
               zero_init):
    n_rows = x.shape[0]
    n_seq_total = n_rows // seq_len
    rows = SCAN_ROWS
    seq_per_step = rows // seq_len
    pair = 2 * HEAD_DIM
    n_pairs = HEADS // 2

    def w_spec(seg):
        return pl.BlockSpec((D_MODEL, pair), lambda sb, hp: (0, seg * n_pairs + hp))

    vec_spec = pl.BlockSpec((1, pair), lambda sb, hp: (0, hp))
    st_block = (seq_per_step, 2, 2, HEAD_DIM, HEAD_DIM)
    st_spec = pl.BlockSpec(st_block, lambda sb, hp: (sb, 0, hp, 0, 0))
    s0_spec = pl.BlockSpec(st_block, (lambda sb, hp: (0, 0, hp, 0, 0)) if zero_init
                           else (lambda sb, hp: (sb, 0, hp, 0, 0)))
    scratch = [
        pltpu.VMEM((rows, D_MODEL), BF16),
        pltpu.VMEM((rows, pair), F32),
        pltpu.VMEM((rows // SCAN_PIECE, pair, SCAN_PIECE), BF16),
        pltpu.VMEM((rows, pair), F32),
        pltpu.VMEM((rows, pair), F32),
        pltpu.VMEM((rows, pair), F32),
        pltpu.VMEM((rows, pair), F32),
        pltpu.VMEM((rows, pair), F32),
        pltpu.VMEM((rows, pair), F32),
        pltpu.VMEM((2, 2, HEAD_DIM, HEAD_DIM), F32),
    ]
    in_specs = [
        pl.BlockSpec((rows, D_MODEL), lambda sb, hp: (sb, 0), pipeline_mode=pl.Buffered(1)),
        pl.BlockSpec((None, 6, D_MODEL), lambda sb, hp: (mod_row0 + sb * mod_row_step, 0, 0)),
        pl.BlockSpec((1, D_MODEL), lambda sb, hp: (0, 0)),
        w_spec(0), w_spec(1), w_spec(2), w_spec(3), w_spec(4),
        vec_spec, vec_spec, vec_spec,
        s0_spec,
    ]
    args = [x, mod_l, norm_w.reshape(1, D_MODEL), w_in_bf, w_in_bf, w_in_bf, w_in_bf, w_in_bf,
            lb_f.reshape(1, D_MODEL), lb_b.reshape(1, D_MODEL), gnorm.reshape(1, D_MODEL), s0]
    return pl.pallas_call(
        functools.partial(_hgrn_kernel, seq_len=seq_len, zero_init=zero_init),
        grid=(n_rows // rows, n_pairs),
        in_specs=in_specs,
        out_specs=[
            pl.BlockSpec((rows, pair), lambda sb, hp: (sb, hp)),
            st_spec,
        ],
        out_shape=[
            jax.ShapeDtypeStruct((n_rows, D_MODEL), BF16),
            jax.ShapeDtypeStruct((n_seq_total, 2, HEADS, HEAD_DIM, HEAD_DIM), F32),
        ],
        scratch_shapes=scratch,
        compiler_params=_cparams("arbitrary", "arbitrary"),
        name="hgrn_scan",
    )(*args)


def _route_rows(x1, half, mod_ref, nw_ref, rwh_ref, rwl_ref, rb_ref, hf_ref, pg_ref, pgt_ref, cnt_ref):
    tm = ROUTE_TILE
    rows = slice(half * tm, (half + 1) * tm)
    hf = _rms(x1, nw_ref[...]) * (1.0 + mod_ref[4:5, :]) + mod_ref[3:4, :]
    hf_hi = hf.astype(BF16)
    hf_ref[rows, :] = hf_hi
    hf_lo = (hf - hf_hi.astype(F32)).astype(BF16)
    logits = (_dot(hf_hi, rwh_ref[...]) + (_dot(hf_hi, rwl_ref[...]) + _dot(hf_lo, rwh_ref[...]))
              + rb_ref[...])
    lane = lax.broadcasted_iota(jnp.int32, (tm, LANES), 1)
    lane_f = lane.astype(F32)
    work = logits
    vals, hots = [], []
    for _ in range(TOP_K):
        m = jnp.max(work, axis=-1, keepdims=True)
        idx_f = jnp.min(jnp.where(work == m, lane_f, float(LANES)), axis=-1, keepdims=True)
        hot = lane_f == idx_f
        vals.append(m)
        hots.append(hot)
        work = jnp.where(hot, -jnp.inf, work)
    ex = [jnp.exp(v - vals[0]) for v in vals]
    inv = 1.0 / (ex[0] + ex[1] + ex[2] + ex[3])

    multi = jnp.where(hots[0] | hots[1] | hots[2] | hots[3], 1.0, 0.0)
    tr = lax.broadcasted_iota(jnp.int32, (tm, tm), 0)
    tc = lax.broadcasted_iota(jnp.int32, (tm, tm), 1)
    earlier = jnp.where(tc < tr, 1.0, 0.0).astype(BF16)
    before = _dot(earlier, multi.astype(BF16))
    cnt = jnp.sum(multi, axis=0, keepdims=True)
    er = lax.broadcasted_iota(jnp.int32, (LANES, LANES), 0)
    ec = lax.broadcasted_iota(jnp.int32, (LANES, LANES), 1)
    lower = jnp.where(er < ec, 1.0, 0.0).astype(BF16)
    cnt_al = jnp.floor((cnt + (SEG_ALIGN - 1)) * (1.0 / SEG_ALIGN)) * SEG_ALIGN
    estart = _dot(jnp.broadcast_to(cnt_al, (8, LANES)).astype(BF16), lower)[0:1, :]
    pos_all = before + estart

    pg = jnp.zeros((tm, LANES), F32)
    for k in range(TOP_K):
        pos = jnp.sum(jnp.where(hots[k], pos_all, 0.0), axis=-1, keepdims=True)
        pg = jnp.where(lane == k, pos, pg)
        pg = jnp.where(lane == TOP_K + k, ex[k] * inv, pg)
    pg_ref[rows, :] = pg
    pgt_ref[half] = pg.T[0:8, :]
    cnt_ref[half] = cnt.astype(jnp.int32)


def _route_tile(x1, route_refs):
    x1_ref = route_refs[-5]
    x1_ref[...] = x1
    for half in range(TOKEN_TILE // ROUTE_TILE):
        _route_rows(x1[half * ROUTE_TILE:(half + 1) * ROUTE_TILE, :], half,
                    *route_refs[:5], *route_refs[-4:])


def _proj_router_kernel(row_ref, xp_ref, xs_ref, op_ref, os_ref, w_ref, mod_ref, *route_refs,
                        prompt_tiles):
    del row_ref
    x = _pick_group(xp_ref, xs_ref, prompt_tiles)
    o = _pick_group(op_ref, os_ref, prompt_tiles)
    _route_tile(x + mod_ref[2:3, :] * _dot(o, w_ref[...]), (mod_ref,) + route_refs)


def _conv_router_kernel(row_ref, width_ref, x_ref, nwm_ref, win_ref, cw_ref, wout_ref, mod_ref,
                        *route_refs):
    del row_ref
    i = pl.program_id(0)
    x = x_ref[...]
    hm = (_rms(x, nwm_ref[...]) * (1.0 + mod_ref[1:2, :]) + mod_ref[0:1, :]).astype(BF16)
    z = _dot(hm, win_ref[...])
    bg = z[:, :D_MODEL]
    u = z[:, D_MODEL:2 * D_MODEL] * z[:, 2 * D_MODEL:]
    tm = x.shape[0]
    pos = lax.broadcasted_iota(jnp.int32, (tm, 1), 0) & (width_ref[i] - 1)
    prev = jnp.where(pos == 0, 0.0, pltpu.roll(u, 1, axis=0))
    nxt = jnp.where(pos == width_ref[i] - 1, 0.0, pltpu.roll(u, tm - 1, axis=0))
    v = cw_ref[0:1, :] * prev + cw_ref[1:2, :] * u + cw_ref[2:3, :] * nxt
    y = _dot((bg * v).astype(BF16), wout_ref[...])
    _route_tile(x + mod_ref[2:3, :] * y, (mod_ref,) + route_refs)


def _mixer_router_call(kernel_fn, n, n_prefetch, prefetch, in_specs, args, mod_l, norm_ffn,
                       router_w, router_b, name):
    tm = TOKEN_TILE
    sub = tm // ROUTE_TILE
    rw = jnp.zeros((D_MODEL, LANES), F32).at[:, :N_EXPERTS].set(router_w)
    rb = jnp.full((1, LANES), -jnp.inf, F32).at[0, :N_EXPERTS].set(router_b)
    rw_hi = rw.astype(BF16)
    rw_lo = (rw - rw_hi.astype(F32)).astype(BF16)
    const = lambda shape: pl.BlockSpec(shape, lambda i, *_: (0,) * len(shape))
    row_blk = lambda width: pl.BlockSpec((tm, width), lambda i, *_: (i, 0))
    return pl.pallas_call(
        kernel_fn,
        grid_spec=pltpu.PrefetchScalarGridSpec(
            num_scalar_prefetch=n_prefetch,
            grid=(n // tm,),
            in_specs=in_specs + [
                pl.BlockSpec((None, 6, D_MODEL), lambda i, row, *_: (row[i], 0, 0)),
                const((1, D_MODEL)), const((D_MODEL, LANES)), const((D_MODEL, LANES)), const((1, LANES)),
            ],
            out_specs=[
                row_blk(D_MODEL), row_blk(D_MODEL), row_blk(LANES),
                pl.BlockSpec((sub, 8, ROUTE_TILE), lambda i, *_: (i, 0, 0)),
                pl.BlockSpec((sub, 1, LANES), lambda i, *_: (i, 0, 0)),
            ],
        ),
        out_shape=[
            jax.ShapeDtypeStruct((n, D_MODEL), F32),
            jax.ShapeDtypeStruct((n, D_MODEL), BF16),
            jax.ShapeDtypeStruct((n, LANES), F32),
            jax.ShapeDtypeStruct((n // ROUTE_TILE, 8, ROUTE_TILE), F32),
            jax.ShapeDtypeStruct((n // ROUTE_TILE, 1, LANES), jnp.int32),
        ],
        compiler_params=_cparams("arbitrary"),
        name=name,
    )(*prefetch, *args, mod_l, norm_ffn.reshape(1, D_MODEL), rw_hi, rw_lo, rb)


def _proj_router(x_p, x_s, o_prompt, o_sample, w_out_bf, tile_row, mod_l, norm_ffn, router_w, router_b):
    n = x_p.shape[0] + x_s.shape[0]
    pt = x_p.shape[0] // TOKEN_TILE
    in_specs = _two_group_specs(TOKEN_TILE, pt) + _two_group_specs(TOKEN_TILE, pt) + [
        pl.BlockSpec((D_MODEL, D_MODEL), lambda i, *_: (0, 0))]
    return _mixer_router_call(functools.partial(_proj_router_kernel, prompt_tiles=pt), n, 1,
                              (tile_row,), in_specs, (x_p, x_s, o_prompt, o_sample, w_out_bf),
                              mod_l, norm_ffn, router_w, router_b, "proj_router")


def _conv_router(x, tile_row, tile_width, mod_l, norm_mix, w_in_bf, conv_w, w_out_bf, norm_ffn,
                 router_w, router_b):
    n = x.shape[0]
    in_specs = [
        pl.BlockSpec((TOKEN_TILE, D_MODEL), lambda i, *_: (i, 0)),
        pl.BlockSpec((1, D_MODEL), lambda i, *_: (0, 0)),
        pl.BlockSpec((D_MODEL, 3 * D_MODEL), lambda i, *_: (0, 0)),
        pl.BlockSpec((3, D_MODEL), lambda i, *_: (0, 0)),
        pl.BlockSpec((D_MODEL, D_MODEL), lambda i, *_: (0, 0)),
    ]
    return _mixer_router_call(_conv_router_kernel, n, 2, (tile_row, tile_width), in_specs,
                              (x, norm_mix.reshape(1, D_MODEL), w_in_bf, conv_w, w_out_bf),
                              mod_l, norm_ffn, router_w, router_b, "conv_router")


SEG_STRIDE = 4 * N_EXPERTS


def _segment_table(tbl_ref, t, e):
    base = t * SEG_STRIDE
    al = lambda v: pl.multiple_of(v, SEG_ALIGN)
    return (al(tbl_ref[base + e]), al(tbl_ref[base + N_EXPERTS + e]),
            al(tbl_ref[base + 2 * N_EXPERTS + e]))


def _tile_rows_total(tbl_ref, t):
    return pl.multiple_of(tbl_ref[t * SEG_STRIDE + 3 * N_EXPERTS], SEG_ALIGN)


def _dispatch_kernel(tbl_ref, pad_ref, nu_ref, hf_ref, pgt_ref, xb_ref, stage_s, zero_s, sem, zsem,
                     *, n_blocks):
    step = pl.program_id(0)
    n_steps = pl.num_programs(0)
    sub = TOKEN_TILE // ROUTE_TILE
    half = zero_s.shape[0]

    def zero_rows(start, rows):
        start = pl.multiple_of(start, SEG_ALIGN)
        rows = pl.multiple_of(rows, SEG_ALIGN)

        @pl.when(rows > 0)
        def _():
            pltpu.make_async_copy(zero_s.at[pl.ds(0, rows), :], xb_ref.at[pl.ds(start, rows), :],
                                  zsem).start()

    @pl.when(step == 0)
    def _():
        zero_s[...] = jnp.zeros_like(zero_s)

        def pad_body(e, carry):
            start = pad_ref[e]
            rows = pad_ref[N_EXPERTS + e]
            first = jnp.minimum(rows, half)
            zero_rows(start, first)
            zero_rows(start + half, rows - first)
            return carry

        lax.fori_loop(0, N_EXPERTS, pad_body, 0)

        def tail_body(b, carry):
            for part in range(EXPERT_TILE // half):
                zero_rows(b * EXPERT_TILE + part * half, half)
            return carry

        lax.fori_loop(nu_ref[0], n_blocks, tail_body, 0)

    def wait_tile(tile, sl):
        rows = _tile_rows_total(tbl_ref, tile)
        pltpu.make_async_copy(stage_s.at[sl, pl.ds(0, rows), :], xb_ref.at[pl.ds(0, rows), :],
                              sem.at[sl]).wait()

    group = (step % 2) * sub

    @pl.when(step >= 2)
    def _():
        for j in range(sub):
            wait_tile((step - 2) * sub + j, group + j)

    s_iota = lax.broadcasted_iota(jnp.int32, (SLOTS, ROUTE_TILE), 0).astype(F32)
    for j in range(sub):
        pgt = pgt_ref[j]
        hit = s_iota == pgt[0:1, :]
        for k in range(1, TOP_K):
            hit = hit | (s_iota == pgt[k:k + 1, :])
        perm = jnp.where(hit, 1.0, 0.0).astype(BF16)
        hf = hf_ref[j * ROUTE_TILE:(j + 1) * ROUTE_TILE, :]
        stage_s[group + j] = _pack_pairs(_dot(perm, hf))

    for j in range(sub):
        def seg_body(e, carry, j=j):
            rows, src, dst = _segment_table(tbl_ref, step * sub + j, e)

            @pl.when(rows > 0)
            def _():
                pltpu.make_async_copy(stage_s.at[group + j, pl.ds(src, rows), :],
                                      xb_ref.at[pl.ds(dst, rows), :], sem.at[group + j]).start()
            return carry

        lax.fori_loop(0, N_EXPERTS, seg_body, 0)

    @pl.when(step == n_steps - 1)
    def _():
        for j in range(sub):
            wait_tile(step * sub + j, group + j)

        @pl.when(n_steps > 1)
        def _():
            for j in range(sub):
                wait_tile((step - 1) * sub + j, sub - group + j)

        zeroed = pl.multiple_of(pad_ref[2 * N_EXPERTS], SEG_ALIGN)

        @pl.when(zeroed > 0)
        def _():
            pltpu.make_async_copy(xb_ref.at[pl.ds(0, zeroed), :], xb_ref.at[pl.ds(0, zeroed), :],
                                  zsem).wait()


def _dispatch(hf, pgt, seg_tbl, pad_tbl, n_used, n_blocks):
    n = hf.shape[0]
    sub = TOKEN_TILE // ROUTE_TILE
    return pl.pallas_call(
        functools.partial(_dispatch_kernel, n_blocks=n_blocks),
        grid_spec=pltpu.PrefetchScalarGridSpec(
            num_scalar_prefetch=3,
            grid=(n // TOKEN_TILE,),
            in_specs=[
                pl.BlockSpec((TOKEN_TILE, D_MODEL), lambda i, a, b, c: (i, 0)),
                pl.BlockSpec((sub, 8, ROUTE_TILE), lambda i, a, b, c: (i, 0, 0)),
            ],
            out_specs=pl.BlockSpec(memory_space=pl.ANY),
            scratch_shapes=[
                pltpu.VMEM((2 * sub, SLOTS, PACKED), jnp.uint32),
                pltpu.VMEM((EXPERT_TILE // 2, PACKED), jnp.uint32),
                pltpu.SemaphoreType.DMA((2 * sub,)),
                pltpu.SemaphoreType.DMA,
            ],
        ),
        out_shape=jax.ShapeDtypeStruct((n_blocks * EXPERT_TILE, PACKED), jnp.uint32),
        compiler_params=_cparams("arbitrary"),
        name="dispatch",
    )(seg_tbl, pad_tbl, n_used, hf, pgt)


def _expert_kernel(be_ref, nu_ref, nxt_ref, slot_ref, rows_ref, xb_ref, wgu_hbm, bgu_ref, wd_hbm, bd_ref,
                   yb_ref, wgu_f, wd_f, wgu_s, wd_s, sem, *, layer):
    i = pl.program_id(0)

    def fetch(e, slot):
        return (pltpu.make_async_copy(wgu_hbm.at[layer, e], wgu_f.at[slot], sem.at[slot]),
                pltpu.make_async_copy(wd_hbm.at[layer, e], wd_f.at[slot], sem.at[slot]))

    @pl.when(i < nu_ref[0])
    def _():
        e = be_ref[i]
        slot = slot_ref[i]
        first_block = jnp.logical_or(i == 0, e != be_ref[jnp.maximum(i - 1, 0)])

        @pl.when(first_block)
        def _():
            @pl.when(i == 0)
            def _():
                for cp in fetch(e, slot):
                    cp.start()

            for cp in fetch(e, slot):
                cp.wait()
            wgu_s[...] = wgu_f[slot].astype(BF16)
            wd_s[...] = wd_f[slot].astype(BF16)

            @pl.when(nxt_ref[i] >= 0)
            def _():
                for cp in fetch(nxt_ref[i], 1 - slot):
                    cp.start()

        def ffn(sl):
            gu = _dot(_unpack_pairs(xb_ref[sl, :]), wgu_s[...]) + bgu_ref[...]
            gate = jnp.minimum(gu[:, :D_FF], SWIGLU_LIMIT)
            up = jnp.clip(gu[:, D_FF:], -SWIGLU_LIMIT, SWIGLU_LIMIT)
            act = (up + 1.0) * gate * _sigmoid(SWIGLU_ALPHA * gate)
            y = _dot(act.astype(BF16), wd_s[...]) + bd_ref[...]
            yb_ref[sl, :] = _pack_pairs(y.astype(BF16).astype(F32))

        n_sub = EXPERT_TILE // EXPERT_SUB
        for k in range(1, n_sub + 1):
            used = k * EXPERT_SUB
            holds_k_parts = rows_ref[i] > used - EXPERT_SUB
            if k < n_sub:
                holds_k_parts = jnp.logical_and(holds_k_parts, rows_ref[i] <= used)

            @pl.when(holds_k_parts)
            def _(used=used):
                for start in range(0, used, EXPERT_CHUNK):
                    ffn(slice(start, min(start + EXPERT_CHUNK, used)))
                if used < EXPERT_TILE:
                    yb_ref[used:, :] = jnp.zeros((EXPERT_TILE - used, PACKED), jnp.uint32)

    @pl.when(i >= nu_ref[0])
    def _():
        yb_ref[...] = jnp.zeros_like(yb_ref)


def _experts(xb, block_e, n_used, next_e, slot, block_rows, layer, w_gu, b_gu, w_down, b_down):
    te = EXPERT_TILE
    n_blocks = xb.shape[0] // te
    depth = w_gu.shape[0]

    def x_map(i, be, nu, *_):
        return (jnp.maximum(jnp.minimum(i, nu[0] - 1), 0), 0)

    def bias_spec(width):
        return pl.BlockSpec((None, None, 1, width), lambda i, be, *_: (layer, be[i], 0, 0))

    return pl.pallas_call(
        functools.partial(_expert_kernel, layer=layer),
        grid_spec=pltpu.PrefetchScalarGridSpec(
            num_scalar_prefetch=5,
            grid=(n_blocks,),
            in_specs=[
                pl.BlockSpec((te, PACKED), x_map),
                pl.BlockSpec(memory_space=pl.ANY),
                bias_spec(2 * D_FF),
                pl.BlockSpec(memory_space=pl.ANY),
                bias_spec(D_MODEL),
            ],
            out_specs=pl.BlockSpec((te, PACKED), lambda i, *_: (i, 0)),
            scratch_shapes=[
                pltpu.VMEM((2, D_MODEL, 2 * D_FF), F32),
                pltpu.VMEM((2, D_FF, D_MODEL), F32),
                pltpu.VMEM((D_MODEL, 2 * D_FF), BF16),
                pltpu.VMEM((D_FF, D_MODEL), BF16),
                pltpu.SemaphoreType.DMA((2,)),
            ],
        ),
        out_shape=jax.ShapeDtypeStruct(xb.shape, jnp.uint32),
        compiler_params=_cparams("arbitrary"),
        name="experts",
    )(block_e, n_used, next_e, slot, block_rows, xb, w_gu,
      b_gu.reshape(depth, N_EXPERTS, 1, 2 * D_FF), w_down, b_down.reshape(depth, N_EXPERTS, 1, D_MODEL))


def _combine_kernel(row_ref, tbl_ref, x_ref, pg_ref, mod_ref, fw_ref, yb_ref, *rest, split_tiles):
    del row_ref
    stage_s, sem = rest[-2:]
    step = pl.program_id(0)
    n_steps = pl.num_programs(0)
    sub = TOKEN_TILE // ROUTE_TILE
    group = (step % 2) * sub

    def fetch(tile, into):
        def seg_body(e, carry):
            rows, dst, src = _segment_table(tbl_ref, tile, e)

            @pl.when(rows > 0)
            def _():
                pltpu.make_async_copy(yb_ref.at[pl.ds(src, rows), :],
                                      stage_s.at[into, pl.ds(dst, rows), :], sem.at[into]).start()
            return carry

        lax.fori_loop(0, N_EXPERTS, seg_body, 0)

    @pl.when(step == 0)
    def _():
        stage_s[...] = jnp.zeros_like(stage_s)
        for j in range(sub):
            fetch(j, j)

    @pl.when(step + 1 < n_steps)
    def _():
        for j in range(sub):
            fetch((step + 1) * sub + j, sub - group + j)

    s_iota = lax.broadcasted_iota(jnp.int32, (ROUTE_TILE, SLOTS), 1).astype(F32)
    ys = []
    for j in range(sub):
        fetched = _tile_rows_total(tbl_ref, step * sub + j)
        pltpu.make_async_copy(yb_ref.at[pl.ds(0, fetched), :],
                              stage_s.at[group + j, pl.ds(0, fetched), :], sem.at[group + j]).wait()
        rows_sorted = _unpack_pairs(stage_s[group + j])
        pg = pg_ref[j * ROUTE_TILE:(j + 1) * ROUTE_TILE, :]
        sel = jnp.zeros((ROUTE_TILE, SLOTS), F32)
        for k in range(TOP_K):
            sel = jnp.where(s_iota == pg[:, k:k + 1], pg[:, TOP_K + k:TOP_K + k + 1], sel)
        ys.append(_dot(sel.astype(BF16), rows_sorted))
    x = x_ref[...] + mod_ref[5:6, :] * jnp.concatenate(ys, axis=0)
    if split_tiles is None:
        rest[0][...] = x
    else:
        x = _rms(x, fw_ref[...])

        @pl.when(step < split_tiles)
        def _():
            rest[0][...] = x

        @pl.when(step >= split_tiles)
        def _():
            rest[1][...] = x


def _combine(x, pg, seg_tbl, yb, tile_row, mod_l, final_w, n_prompt=None):
    n = x.shape[0]
    tm = TOKEN_TILE
    sub = tm // ROUTE_TILE
    if n_prompt is None:
        split = None
        out_specs = pl.BlockSpec((tm, D_MODEL), lambda i, row, tbl: (i, 0))
        out_shape = jax.ShapeDtypeStruct((n, D_MODEL), F32)
    else:
        split = n_prompt // tm
        out_specs = [
            pl.BlockSpec((tm, D_MODEL), lambda i, row, tbl: (jnp.minimum(i, split - 1), 0)),
            pl.BlockSpec((tm, D_MODEL), lambda i, row, tbl: (jnp.maximum(i - split, 0), 0)),
        ]
        out_shape = [jax.ShapeDtypeStruct((n_prompt, D_MODEL), F32),
                     jax.ShapeDtypeStruct((n - n_prompt, D_MODEL), F32)]
    return pl.pallas_call(
        functools.partial(_combine_kernel, split_tiles=split),
        grid_spec=pltpu.PrefetchScalarGridSpec(
            num_scalar_prefetch=2,
            grid=(n // tm,),
            in_specs=[
                pl.BlockSpec((tm, D_MODEL), lambda i, row, tbl: (i, 0)),
                pl.BlockSpec((tm, LANES), lambda i, row, tbl: (i, 0)),
                pl.BlockSpec((None, 6, D_MODEL), lambda i, row, tbl: (row[i], 0, 0)),
                pl.BlockSpec((1, D_MODEL), lambda i, row, tbl: (0, 0)),
                pl.BlockSpec(memory_space=pl.ANY),
            ],
            out_specs=out_specs,
            scratch_shapes=[pltpu.VMEM((2 * sub, SLOTS, PACKED), jnp.uint32),
                            pltpu.SemaphoreType.DMA((2 * sub,))],
        ),
        out_shape=out_shape,
        compiler_params=_cparams("arbitrary"),
        name="combine",
    )(tile_row, seg_tbl, x, pg, mod_l, final_w.reshape(1, D_MODEL), yb)


def _moe(routed, tile_row, mod_l, layer, w_gu, b_gu, w_down, b_down, final_w, n_prompt=None):
    x, hf, pg, pgt, cnt = routed
    n = x.shape[0]

    te = EXPERT_TILE
    nt = n // ROUTE_TILE
    n_blocks = (n * TOP_K + nt * N_EXPERTS * (SEG_ALIGN - 1)) // te + 1 + N_EXPERTS
    cnt = cnt[:, 0, :N_EXPERTS]
    cnt = (cnt + SEG_ALIGN - 1) // SEG_ALIGN * SEG_ALIGN
    total = jnp.sum(cnt, axis=0)
    blocks_e = (total + te - 1) // te
    block_end = jnp.cumsum(blocks_e)
    pstart = (block_end - blocks_e) * te
    n_used = block_end[-1]
    first_row = pstart[None, :] + jnp.cumsum(cnt, axis=0) - cnt
    tile_off = jnp.cumsum(cnt, axis=1) - cnt
    tile_total = jnp.broadcast_to(jnp.sum(cnt, axis=1, keepdims=True), cnt.shape)
    seg_tbl = jnp.concatenate([cnt, tile_off, first_row, tile_total], axis=1)
    seg_tbl = seg_tbl.reshape(-1).astype(jnp.int32)
    pad_rows = blocks_e * te - total
    zeroed = jnp.sum(pad_rows) + (n_blocks - n_used) * te
    pad_tbl = jnp.concatenate([pstart + total, pad_rows, zeroed[None]]).astype(jnp.int32)
    block_ids = jnp.arange(n_blocks, dtype=jnp.int32)
    clamped = jnp.minimum(block_ids, n_used - 1)
    block_e = jnp.sum((clamped[:, None] >= block_end[None, :]).astype(jnp.int32), axis=1)
    block_e = jnp.minimum(block_e, N_EXPERTS - 1).astype(jnp.int32)
    n_used = n_used.astype(jnp.int32).reshape(1)
    owns = (blocks_e > 0)[None, :]
    ids = jnp.arange(N_EXPERTS, dtype=jnp.int32)[None, :]
    mine = block_e[:, None]
    next_e = jnp.min(jnp.where((ids > mine) & owns, ids, N_EXPERTS), axis=1)
    next_e = jnp.where(next_e < N_EXPERTS, next_e, -1).astype(jnp.int32)
    slot = ((jnp.sum(((ids <= mine) & owns).astype(jnp.int32), axis=1) + 1) % 2).astype(jnp.int32)

    row_end = jnp.sum(jnp.where(ids == mine, (pstart + total)[None, :], 0), axis=1)
    block_rows = jnp.clip(row_end - block_ids * te, 0, te).astype(jnp.int32)

    xb = _dispatch(hf, pgt, seg_tbl, pad_tbl, n_used, n_blocks)
    yb = _experts(xb, block_e, n_used, next_e, slot, block_rows, layer, w_gu, b_gu, w_down, b_down)
    return _combine(x, pg, seg_tbl, yb, tile_row, mod_l, final_w, n_prompt)


def _tile_rows(n_prompt_tok, n_sample_seq, sample_len, tile):
    starts = np.arange(0, n_prompt_tok + n_sample_seq * sample_len, tile)
    row = np.where(starts < n_prompt_tok, 0, 1 + (starts - n_prompt_tok) // sample_len)
    return jnp.asarray(row, dtype=jnp.int32)


def kernel(x_prompt, x_sample, state_hgrn, c, c_ctx, w_mod, b_mod, norm_mix, norm_ffn, hg_w_in,
           hg_lb_logits, hg_gnorm, hg_w_out, cv_w_in, cv_w, cv_w_out, router_w, router_b,
           moe_w_gu, moe_b_gu, moe_w_down, moe_b_down, final_norm):
    bp, tp, d = x_prompt.shape
    bs, ts, _ = x_sample.shape
    depth = w_mod.shape[0]
    n_prompt = bp * tp
    n = n_prompt + bs * ts
    assert d == D_MODEL and depth == 2 and 1 + bs <= 16
    assert n_prompt % SCAN_ROWS == 0 and ts == SCAN_ROWS and tp == SCAN_PIECE

    x_p = x_prompt.reshape(n_prompt, d)
    x_s = x_sample.reshape(bs * ts, d)
    cond16 = jnp.zeros((16, d), F32).at[0].set(c_ctx).at[1:1 + bs].set(c)
    mod = _modulation(cond16, w_mod, b_mod).reshape(depth, 16, 6, d)

    tile_row = _tile_rows(n_prompt, bs, ts, TOKEN_TILE)
    starts = np.arange(0, n, TOKEN_TILE)
    tile_width = jnp.asarray(np.where(starts < n_prompt, tp, GRID_W), dtype=jnp.int32)

    lb_all = jnp.cumsum(jax.nn.softmax(hg_lb_logits.astype(F32), axis=1), axis=1)

    w_in_bf = hg_w_in[0].astype(BF16)
    gn = hg_gnorm[0].reshape(-1)
    zero_state = jnp.zeros((SCAN_ROWS // tp, 2, HEADS, HEAD_DIM, HEAD_DIM), F32)
    o_p, s_new = _hgrn_scan(x_p, mod[0], 0, 0, norm_mix[0], w_in_bf, lb_all[0, 0], lb_all[1, 0], gn,
                            zero_state, tp, True)
    o_s, _ = _hgrn_scan(x_s, mod[0], 1, 1, norm_mix[0], w_in_bf, lb_all[0, 0], lb_all[1, 0], gn,
                        state_hgrn[:, 0], ts, False)
    routed = _proj_router(x_p, x_s, o_p, o_s, hg_w_out[0].astype(BF16), tile_row, mod[0], norm_ffn[0],
                          router_w[0], router_b[0])
    x = _moe(routed, tile_row, mod[0], 0, moe_w_gu, moe_b_gu, moe_w_down, moe_b_down, final_norm)

    routed = _conv_router(x, tile_row, tile_width, mod[1], norm_mix[1], cv_w_in[0].astype(BF16),
                          cv_w[0], cv_w_out[0].astype(BF16), norm_ffn[1], router_w[1], router_b[1])
    y_p, y_s = _moe(routed, tile_row, mod[1], 1, moe_w_gu, moe_b_gu, moe_w_down, moe_b_down,
                    final_norm, n_prompt)

    y_prompt = y_p.reshape(bp, tp, d)
    y_sample = y_s.reshape(bs, ts, d)
    return (y_prompt, y_sample, s_new.reshape(bp, 1, 2, HEADS, HEAD_DIM, HEAD_DIM))
```

```python
import functools

import numpy as np
import jax
import jax.numpy as jnp
from jax import lax
from jax.experimental import pallas as pl
from jax.experimental.pallas import tpu as pltpu

F32 = jnp.float32
BF16 = jnp.bfloat16

D_MODEL = 1024
HEADS = 8
HEAD_DIM = 128
CHUNK = 64
GRID_W = 64
N_EXPERTS = 32
TOP_K = 4
D_FF = 1024
SWIGLU_LIMIT = 7.0
SWIGLU_ALPHA = 1.702
EPS = 1e-6

MOD_PARTS = 6
COND_ROWS = 16

LANES = 128
SUBLANES = 8
SCAN_ROWS = 2048
SCAN_PIECE = 256
TOKEN_TILE = 512
MOVE_TILE = 1024
EXPERT_TILE = 1024
EXPERT_CHUNK = 512
EXPERT_SUB = 256
ROUTE_TILE = 256
SEG_ALIGN = 8
SLOTS = ROUTE_TILE * TOP_K + N_EXPERTS * SEG_ALIGN
VMEM_LIMIT = 56 * 1024 * 1024


def _cparams(*sem):
    return pltpu.CompilerParams(dimension_semantics=sem, vmem_limit_bytes=VMEM_LIMIT)


def _sigmoid(x):
    return 1.0 / (1.0 + jnp.exp(-x))


def _rms(x, w):
    return x * lax.rsqrt(jnp.mean(x * x, axis=-1, keepdims=True) + EPS) * w


def _dot(a, b):
    return jnp.dot(a, b, preferred_element_type=F32)


def _dot_nt(a, b):
    return lax.dot_general(a, b, (((1,), (1,)), ((), ())), preferred_element_type=F32)


PACKED = D_MODEL // 2
_HIGH_HALF = 0xFFFF0000


def _pack_pairs(x):
    c = x.shape[1] // 2
    lo = lax.bitcast_convert_type(x[:, :c], jnp.uint32) >> 16
    hi = lax.bitcast_convert_type(x[:, c:], jnp.uint32) & jnp.uint32(_HIGH_HALF)
    return hi | lo


def _unpack_pairs(u):
    lo = lax.bitcast_convert_type(u << 16, F32)
    hi = lax.bitcast_convert_type(u & jnp.uint32(_HIGH_HALF), F32)
    return jnp.concatenate([lo, hi], axis=1).astype(BF16)


def _mod_kernel(cond_ref, w_ref, b_ref, o_ref):
    c = cond_ref[...]
    s = (c * _sigmoid(c)).astype(BF16)
    o_ref[...] = _dot(s, w_ref[...].astype(BF16)) + b_ref[...]


def _modulation(cond16, w_mod, b_mod):
    depth = w_mod.shape[0]
    n_out = w_mod.shape[2]
    tn = 1024
    return pl.pallas_call(
        _mod_kernel,
        grid=(depth, n_out // tn),
        in_specs=[
            pl.BlockSpec((COND_ROWS, D_MODEL), lambda l, j: (0, 0)),
            pl.BlockSpec((None, D_MODEL, tn), lambda l, j: (l, 0, j)),
            pl.BlockSpec((None, 1, tn), lambda l, j: (l, 0, j)),
        ],
        out_specs=pl.BlockSpec((None, COND_ROWS, tn), lambda l, j: (l, 0, j)),
        out_shape=jax.ShapeDtypeStruct((depth, COND_ROWS, n_out), F32),
        compiler_params=_cparams("arbitrary", "arbitrary"),
        name="modulation",
    )(cond16, w_mod, b_mod.reshape(depth, 1, n_out))


def _two_group_specs(tm, prompt_tiles):
    return [
        pl.BlockSpec((tm, D_MODEL), lambda i, *_: (jnp.minimum(i, prompt_tiles - 1), 0)),
        pl.BlockSpec((tm, D_MODEL), lambda i, *_: (jnp.maximum(i - prompt_tiles, 0), 0)),
    ]


def _pick_group(a_ref, b_ref, prompt_tiles):
    return jnp.where(pl.program_id(0) < prompt_tiles, a_ref[...], b_ref[...])


def _hgrn_kernel(x_ref, mod_ref, nw_ref, wq_ref, wff_ref, wfb_ref, wv_ref, wg_ref, lbf_ref, lbb_ref,
                 gn_ref, s0_ref, o_ref, so_ref, hm_s, q_s, vt_s, kf_s, lf_s, kb_s, lb_s, of_s, ob_s, st_s,
                 *, seq_len, zero_init):
    rows = x_ref.shape[0]
    piece = SCAN_PIECE
    n_pieces = rows // piece
    cpp = piece // CHUNK
    per_piece_seq = seq_len == piece
    assert per_piece_seq or seq_len == rows

    @pl.when(pl.program_id(1) == 0)
    def _():
        y = _rms(x_ref[...], nw_ref[...])
        hm_s[...] = (y * (1.0 + mod_ref[1:2, :]) + mod_ref[0:1, :]).astype(BF16)

    hm = hm_s[...]

    zq = _dot(hm, wq_ref[...])
    q_s[...] = zq * _sigmoid(zq)
    v = _dot(hm, wv_ref[...])
    for p in range(n_pieces):
        vt_s[p] = v[p * piece:(p + 1) * piece, :].T.astype(BF16)
    for w_ref, lbr, k_s, l_s in ((wff_ref, lbf_ref, kf_s, lf_s), (wfb_ref, lbb_ref, kb_s, lb_s)):
        lb = lbr[...]
        f = lb + (1.0 - lb) * _sigmoid(_dot(hm, w_ref[...]))
        k_s[...] = 1.0 - f
        l_s[...] = jnp.log(f)

    def load_state(j, d):
        for h in range(2):
            if zero_init:
                st_s[d, h] = jnp.zeros((HEAD_DIM, HEAD_DIM), F32)
            else:
                st_s[d, h] = s0_ref[j, d, h].T

    def store_state(j, d):
        for h in range(2):
            so_ref[j, d, h] = st_s[d, h].T

    def piece_body(i, carry):
        r = lax.broadcasted_iota(jnp.int32, (piece, piece), 0)
        c = lax.broadcasted_iota(jnp.int32, (piece, piece), 1)
        same = (r // CHUNK) == (c // CHUNK)
        row_chunk = lax.broadcasted_iota(jnp.int32, (piece, 1), 0) // CHUNK
        col_chunk = lax.broadcasted_iota(jnp.int32, (1, piece), 1) // CHUNK
        for d in range(2):
            k_s, l_s, o_s = ((kf_s, lf_s, of_s), (kb_s, lb_s, ob_s))[d]
            p = i if d == 0 else n_pieces - 1 - i
            if per_piece_seq:
                load_state(p, d)
            sl = pl.ds(pl.multiple_of(p * piece, piece), piece)
            keep = same & ((c <= r) if d == 0 else (c >= r))
            tri = jnp.where(keep, 1.0, 0.0).astype(BF16)
            lf = l_s[sl, :]
            hi = lf.astype(BF16)
            mid = (lf - hi.astype(F32)).astype(BF16)
            b = _dot(tri, hi) + _dot(tri, mid)
            b3 = b.reshape(cpp, CHUNK, 2 * HEAD_DIM)
            edge = CHUNK - 1 if d == 0 else 0
            bl3 = b3[:, edge:edge + 1, :]
            centre = CHUNK // 2 - 1 if d == 0 else CHUNK // 2
            bm3 = b3[:, centre:centre + 1, :]
            q = q_s[sl, :]
            q3 = q.reshape(cpp, CHUNK, 2 * HEAD_DIM)
            k3 = k_s[sl, :].reshape(cpp, CHUNK, 2 * HEAD_DIM)
            flat = lambda a: a.reshape(piece, 2 * HEAD_DIM).astype(BF16)
            qd = (q * jnp.exp(b)).astype(BF16)
            qi = flat(q3 * jnp.exp(b3 - bm3))
            ki = flat(k3 * jnp.exp(bm3 - b3))
            ks = flat(k3 * jnp.exp(bl3 - b3))
            dec = jnp.exp(bl3)
            vt = vt_s[p]
            for h in range(2):
                hs = slice(h * HEAD_DIM, (h + 1) * HEAD_DIM)
                att = jnp.where(keep, _dot_nt(qi[:, hs], ki[:, hs]), 0.0).astype(BF16)
                vt_h = vt[hs, :]
                v_exp = jnp.concatenate(
                    [jnp.where(col_chunk == ci, vt_h, jnp.zeros_like(vt_h)) for ci in range(cpp)], axis=0)
                incr = _dot(v_exp, ks[:, hs])
                st = st_s[d, h]
                before = [None] * cpp
                for ci in (range(cpp) if d == 0 else range(cpp - 1, -1, -1)):
                    before[ci] = st.astype(BF16)
                    st = st * dec[ci, :, hs] + incr[ci * HEAD_DIM:(ci + 1) * HEAD_DIM, :]
                st_s[d, h] = st
                qd_h = qd[:, hs]
                q_exp = [jnp.where(row_chunk == ci, qd_h, jnp.zeros_like(qd_h)) for ci in range(cpp)]
                lhs = jnp.concatenate([att] + q_exp, axis=1)
                rhs_t = jnp.concatenate([vt_h] + before, axis=1)
                o_s[sl, hs] = _dot_nt(lhs, rhs_t)
            if per_piece_seq:
                store_state(p, d)
        return carry

    if not per_piece_seq:
        load_state(0, 0)
        load_state(0, 1)
    lax.fori_loop(0, n_pieces, piece_body, 0, unroll=4)
    if not per_piece_seq:
        store_state(0, 0)
        store_state(0, 1)

    o = of_s[...] + ob_s[...]
    gn = gn_ref[...]
    o = jnp.concatenate(
        [_rms(o[:, h * HEAD_DIM:(h + 1) * HEAD_DIM], gn[:, h * HEAD_DIM:(h + 1) * HEAD_DIM])
         for h in range(2)], axis=1)
    o_ref[...] = (o * _sigmoid(_dot(hm, wg_ref[...]))).astype(o_ref.dtype)


def _hgrn_scan(x, mod_l, mod_row0, mod_row_step, norm_w, w_in_bf, lb_f, lb_b, gnorm, s0, seq_len,
               zero_init):
    n_rows = x.shape[0]
    n_seq_total = n_rows // seq_len
    rows = SCAN_ROWS
    seq_per_step = rows // seq_len
    pair = 2 * HEAD_DIM
    n_pairs = HEADS // 2

    def w_spec(seg):
        return pl.BlockSpec((D_MODEL, pair), lambda sb, hp: (0, seg * n_pairs + hp))

    vec_spec = pl.BlockSpec((1, pair), lambda sb, hp: (0, hp))
    st_block = (seq_per_step, 2, 2, HEAD_DIM, HEAD_DIM)
    st_spec = pl.BlockSpec(st_block, lambda sb, hp: (sb, 0, hp, 0, 0))
    s0_spec = pl.BlockSpec(st_block, (lambda sb, hp: (0, 0, hp, 0, 0)) if zero_init
                           else (lambda sb, hp: (sb, 0, hp, 0, 0)))
    scratch = [
        pltpu.VMEM((rows, D_MODEL), BF16),
        pltpu.VMEM((rows, pair), F32),
        pltpu.VMEM((rows // SCAN_PIECE, pair, SCAN_PIECE), BF16),
        pltpu.VMEM((rows, pair), F32),
        pltpu.VMEM((rows, pair), F32),
        pltpu.VMEM((rows, pair), F32),
        pltpu.VMEM((rows, pair), F32),
        pltpu.VMEM((rows, pair), F32),
        pltpu.VMEM((rows, pair), F32),
        pltpu.VMEM((2, 2, HEAD_DIM, HEAD_DIM), F32),
    ]
    in_specs = [
        pl.BlockSpec((rows, D_MODEL), lambda sb, hp: (sb, 0), pipeline_mode=pl.Buffered(1)),
        pl.BlockSpec((None, MOD_PARTS, D_MODEL), lambda sb, hp: (mod_row0 + sb * mod_row_step, 0, 0)),
        pl.BlockSpec((1, D_MODEL), lambda sb, hp: (0, 0)),
        w_spec(0), w_spec(1), w_spec(2), w_spec(3), w_spec(4),
        vec_spec, vec_spec, vec_spec,
        s0_spec,
    ]
    args = [x, mod_l, norm_w.reshape(1, D_MODEL), w_in_bf, w_in_bf, w_in_bf, w_in_bf, w_in_bf,
            lb_f.reshape(1, D_MODEL), lb_b.reshape(1, D_MODEL), gnorm.reshape(1, D_MODEL), s0]
    return pl.pallas_call(
        functools.partial(_hgrn_kernel, seq_len=seq_len, zero_init=zero_init),
        grid=(n_rows // rows, n_pairs),
        in_specs=in_specs,
        out_specs=[
            pl.BlockSpec((rows, pair), lambda sb, hp: (sb, hp)),
            st_spec,
        ],
        out_shape=[
            jax.ShapeDtypeStruct((n_rows, D_MODEL), BF16),
            jax.ShapeDtypeStruct((n_seq_total, 2, HEADS, HEAD_DIM, HEAD_DIM), F32),
        ],
        scratch_shapes=scratch,
        compiler_params=_cparams("arbitrary", "arbitrary"),
        name="hgrn_scan",
    )(*args)


def _route_rows(x1, half, mod_ref, nw_ref, rwh_ref, rwl_ref, rb_ref, hf_ref, pg_ref, pgt_ref, cnt_ref):
    tm = ROUTE_TILE
    rows = slice(half * tm, (half + 1) * tm)
    hf = _rms(x1, nw_ref[...]) * (1.0 + mod_ref[4:5, :]) + mod_ref[3:4, :]
    hf_hi = hf.astype(BF16)
    hf_ref[rows, :] = hf_hi
    hf_lo = (hf - hf_hi.astype(F32)).astype(BF16)
    logits = (_dot(hf_hi, rwh_ref[...]) + (_dot(hf_hi, rwl_ref[...]) + _dot(hf_lo, rwh_ref[...]))
              + rb_ref[...])
    lane = lax.broadcasted_iota(jnp.int32, (tm, LANES), 1)
    lane_f = lane.astype(F32)
    work = logits
    vals, hots = [], []
    for _ in range(TOP_K):
        m = jnp.max(work, axis=-1, keepdims=True)
        idx_f = jnp.min(jnp.where(work == m, lane_f, float(LANES)), axis=-1, keepdims=True)
        hot = lane_f == idx_f
        vals.append(m)
        hots.append(hot)
        work = jnp.where(hot, -jnp.inf, work)
    ex = [jnp.exp(v - vals[0]) for v in vals]
    inv = 1.0 / (ex[0] + ex[1] + ex[2] + ex[3])

    multi = jnp.where(hots[0] | hots[1] | hots[2] | hots[3], 1.0, 0.0)
    tr = lax.broadcasted_iota(jnp.int32, (tm, tm), 0)
    tc = lax.broadcasted_iota(jnp.int32, (tm, tm), 1)
    earlier = jnp.where(tc < tr, 1.0, 0.0).astype(BF16)
    before = _dot(earlier, multi.astype(BF16))
    cnt = jnp.sum(multi, axis=0, keepdims=True)
    er = lax.broadcasted_iota(jnp.int32, (LANES, LANES), 0)
    ec = lax.broadcasted_iota(jnp.int32, (LANES, LANES), 1)
    lower = jnp.where(er < ec, 1.0, 0.0).astype(BF16)
    cnt_al = jnp.floor((cnt + (SEG_ALIGN - 1)) * (1.0 / SEG_ALIGN)) * SEG_ALIGN
    estart = _dot(jnp.broadcast_to(cnt_al, (SUBLANES, LANES)).astype(BF16), lower)[0:1, :]
    pos_all = before + estart

    pg = jnp.zeros((tm, LANES), F32)
    for k in range(TOP_K):
        pos = jnp.sum(jnp.where(hots[k], pos_all, 0.0), axis=-1, keepdims=True)
        pg = jnp.where(lane == k, pos, pg)
        pg = jnp.where(lane == TOP_K + k, ex[k] * inv, pg)
    pg_ref[rows, :] = pg
    pgt_ref[half] = pg.T[0:8, :]
    cnt_ref[half] = cnt.astype(jnp.int32)


def _route_tile(x1, route_refs):
    x1_ref = route_refs[-5]
    x1_ref[...] = x1
    for half in range(TOKEN_TILE // ROUTE_TILE):
        _route_rows(x1[half * ROUTE_TILE:(half + 1) * ROUTE_TILE, :], half,
                    *route_refs[:5], *route_refs[-4:])


def _proj_router_kernel(row_ref, xp_ref, xs_ref, op_ref, os_ref, w_ref, mod_ref, *route_refs,
                        prompt_tiles):
    del row_ref
    x = _pick_group(xp_ref, xs_ref, prompt_tiles)
    o = _pick_group(op_ref, os_ref, prompt_tiles)
    _route_tile(x + mod_ref[2:3, :] * _dot(o, w_ref[...]), (mod_ref,) + route_refs)


def _conv_router_kernel(row_ref, width_ref, x_ref, nwm_ref, win_ref, cw_ref, wout_ref, mod_ref,
                        *route_refs):
    del row_ref
    i = pl.program_id(0)
    x = x_ref[...]
    hm = (_rms(x, nwm_ref[...]) * (1.0 + mod_ref[1:2, :]) + mod_ref[0:1, :]).astype(BF16)
    z = _dot(hm, win_ref[...])
    bg = z[:, :D_MODEL]
    u = z[:, D_MODEL:2 * D_MODEL] * z[:, 2 * D_MODEL:]
    tm = x.shape[0]
    pos = lax.broadcasted_iota(jnp.int32, (tm, 1), 0) & (width_ref[i] - 1)
    prev = jnp.where(pos == 0, 0.0, pltpu.roll(u, 1, axis=0))
    nxt = jnp.where(pos == width_ref[i] - 1, 0.0, pltpu.roll(u, tm - 1, axis=0))
    v = cw_ref[0:1, :] * prev + cw_ref[1:2, :] * u + cw_ref[2:3, :] * nxt
    y = _dot((bg * v).astype(BF16), wout_ref[...])
    _route_tile(x + mod_ref[2:3, :] * y, (mod_ref,) + route_refs)


def _mixer_router_call(kernel_fn, n, n_prefetch, prefetch, in_specs, args, mod_l, norm_ffn,
                       router_w, router_b, name):
    tm = TOKEN_TILE
    sub = tm // ROUTE_TILE
    rw = jnp.zeros((D_MODEL, LANES), F32).at[:, :N_EXPERTS].set(router_w)
    rb = jnp.full((1, LANES), -jnp.inf, F32).at[0, :N_EXPERTS].set(router_b)
    rw_hi = rw.astype(BF16)
    rw_lo = (rw - rw_hi.astype(F32)).astype(BF16)
    const = lambda shape: pl.BlockSpec(shape, lambda i, *_: (0,) * len(shape))
    row_blk = lambda width: pl.BlockSpec((tm, width), lambda i, *_: (i, 0))
    return pl.pallas_call(
        kernel_fn,
        grid_spec=pltpu.PrefetchScalarGridSpec(
            num_scalar_prefetch=n_prefetch,
            grid=(n // tm,),
            in_specs=in_specs + [
                pl.BlockSpec((None, MOD_PARTS, D_MODEL), lambda i, row, *_: (row[i], 0, 0)),
                const((1, D_MODEL)), const((D_MODEL, LANES)), const((D_MODEL, LANES)), const((1, LANES)),
            ],
            out_specs=[
                row_blk(D_MODEL), row_blk(D_MODEL), row_blk(LANES),
                pl.BlockSpec((sub, 8, ROUTE_TILE), lambda i, *_: (i, 0, 0)),
                pl.BlockSpec((sub, 1, LANES), lambda i, *_: (i, 0, 0)),
            ],
        ),
        out_shape=[
            jax.ShapeDtypeStruct((n, D_MODEL), F32),
            jax.ShapeDtypeStruct((n, D_MODEL), BF16),
            jax.ShapeDtypeStruct((n, LANES), F32),
            jax.ShapeDtypeStruct((n // ROUTE_TILE, 8, ROUTE_TILE), F32),
            jax.ShapeDtypeStruct((n // ROUTE_TILE, 1, LANES), jnp.int32),
        ],
        compiler_params=_cparams("arbitrary"),
        name=name,
    )(*prefetch, *args, mod_l, norm_ffn.reshape(1, D_MODEL), rw_hi, rw_lo, rb)


def _proj_router(x_p, x_s, o_prompt, o_sample, w_out_bf, tile_row, mod_l, norm_ffn, router_w, router_b):
    n = x_p.shape[0] + x_s.shape[0]
    pt = x_p.shape[0] // TOKEN_TILE
    in_specs = _two_group_specs(TOKEN_TILE, pt) + _two_group_specs(TOKEN_TILE, pt) + [
        pl.BlockSpec((D_MODEL, D_MODEL), lambda i, *_: (0, 0))]
    return _mixer_router_call(functools.partial(_proj_router_kernel, prompt_tiles=pt), n, 1,
                              (tile_row,), in_specs, (x_p, x_s, o_prompt, o_sample, w_out_bf),
                              mod_l, norm_ffn, router_w, router_b, "proj_router")


def _conv_router(x, tile_row, tile_width, mod_l, norm_mix, w_in_bf, conv_w, w_out_bf, norm_ffn,
                 router_w, router_b):
    n = x.shape[0]
    in_specs = [
        pl.BlockSpec((TOKEN_TILE, D_MODEL), lambda i, *_: (i, 0)),
        pl.BlockSpec((1, D_MODEL), lambda i, *_: (0, 0)),
        pl.BlockSpec((D_MODEL, 3 * D_MODEL), lambda i, *_: (0, 0)),
        pl.BlockSpec((3, D_MODEL), lambda i, *_: (0, 0)),
        pl.BlockSpec((D_MODEL, D_MODEL), lambda i, *_: (0, 0)),
    ]
    return _mixer_router_call(_conv_router_kernel, n, 2, (tile_row, tile_width), in_specs,
                              (x, norm_mix.reshape(1, D_MODEL), w_in_bf, conv_w, w_out_bf),
                              mod_l, norm_ffn, router_w, router_b, "conv_router")


SEG_STRIDE = 4 * N_EXPERTS


def _segment_table(tbl_ref, t, e):
    base = t * SEG_STRIDE
    al = lambda v: pl.multiple_of(v, SEG_ALIGN)
    return (al(tbl_ref[base + e]), al(tbl_ref[base + N_EXPERTS + e]),
            al(tbl_ref[base + 2 * N_EXPERTS + e]))


def _tile_rows_total(tbl_ref, t):
    return pl.multiple_of(tbl_ref[t * SEG_STRIDE + 3 * N_EXPERTS], SEG_ALIGN)


def _dispatch_kernel(tbl_ref, pad_ref, nu_ref, hf_ref, pgt_ref, xb_ref, stage_s, zero_s, sem, zsem,
                     *, n_blocks):
    step = pl.program_id(0)
    n_steps = pl.num_programs(0)
    sub = MOVE_TILE // ROUTE_TILE
    half = zero_s.shape[0]

    def zero_rows(start, rows):
        start = pl.multiple_of(start, SEG_ALIGN)
        rows = pl.multiple_of(rows, SEG_ALIGN)

        @pl.when(rows > 0)
        def _():
            pltpu.make_async_copy(zero_s.at[pl.ds(0, rows), :], xb_ref.at[pl.ds(start, rows), :],
                                  zsem).start()

    @pl.when(step == 0)
    def _():
        zero_s[...] = jnp.zeros_like(zero_s)

        def pad_body(e, carry):
            start = pad_ref[e]
            rows = pad_ref[N_EXPERTS + e]
            first = jnp.minimum(rows, half)
            zero_rows(start, first)
            zero_rows(start + half, rows - first)
            return carry

        lax.fori_loop(0, N_EXPERTS, pad_body, 0)

        def tail_body(b, carry):
            for part in range(EXPERT_TILE // half):
                zero_rows(b * EXPERT_TILE + part * half, half)
            return carry

        lax.fori_loop(nu_ref[0], n_blocks, tail_body, 0)

    def wait_tile(tile, sl):
        rows = _tile_rows_total(tbl_ref, tile)
        pltpu.make_async_copy(stage_s.at[sl, pl.ds(0, rows), :], xb_ref.at[pl.ds(0, rows), :],
                              sem.at[sl]).wait()

    group = (step % 2) * sub

    @pl.when(step >= 2)
    def _():
        for j in range(sub):
            wait_tile((step - 2) * sub + j, group + j)

    s_iota = lax.broadcasted_iota(jnp.int32, (SLOTS, ROUTE_TILE), 0).astype(F32)
    for j in range(sub):
        pgt = pgt_ref[j]
        hit = s_iota == pgt[0:1, :]
        for k in range(1, TOP_K):
            hit = hit | (s_iota == pgt[k:k + 1, :])
        perm = jnp.where(hit, 1.0, 0.0).astype(BF16)
        hf = hf_ref[j * ROUTE_TILE:(j + 1) * ROUTE_TILE, :]
        stage_s[group + j] = _pack_pairs(_dot(perm, hf))

    for j in range(sub):
        def seg_body(e, carry, j=j):
            rows, src, dst = _segment_table(tbl_ref, step * sub + j, e)

            @pl.when(rows > 0)
            def _():
                pltpu.make_async_copy(stage_s.at[group + j, pl.ds(src, rows), :],
                                      xb_ref.at[pl.ds(dst, rows), :], sem.at[group + j]).start()
            return carry

        lax.fori_loop(0, N_EXPERTS, seg_body, 0)

    @pl.when(step == n_steps - 1)
    def _():
        for j in range(sub):
            wait_tile(step * sub + j, group + j)

        @pl.when(n_steps > 1)
        def _():
            for j in range(sub):
                wait_tile((step - 1) * sub + j, sub - group + j)

        zeroed = pl.multiple_of(pad_ref[2 * N_EXPERTS], SEG_ALIGN)

        @pl.when(zeroed > 0)
        def _():
            pltpu.make_async_copy(xb_ref.at[pl.ds(0, zeroed), :], xb_ref.at[pl.ds(0, zeroed), :],
                                  zsem).wait()


def _dispatch(hf, pgt, seg_tbl, pad_tbl, n_used, n_blocks):
    n = hf.shape[0]
    sub = MOVE_TILE // ROUTE_TILE
    return pl.pallas_call(
        functools.partial(_dispatch_kernel, n_blocks=n_blocks),
        grid_spec=pltpu.PrefetchScalarGridSpec(
            num_scalar_prefetch=3,
            grid=(n // MOVE_TILE,),
            in_specs=[
                pl.BlockSpec((MOVE_TILE, D_MODEL), lambda i, a, b, c: (i, 0)),
                pl.BlockSpec((sub, 8, ROUTE_TILE), lambda i, a, b, c: (i, 0, 0)),
            ],
            out_specs=pl.BlockSpec(memory_space=pl.ANY),
            scratch_shapes=[
                pltpu.VMEM((2 * sub, SLOTS, PACKED), jnp.uint32),
                pltpu.VMEM((EXPERT_TILE // 2, PACKED), jnp.uint32),
                pltpu.SemaphoreType.DMA((2 * sub,)),
                pltpu.SemaphoreType.DMA,
            ],
        ),
        out_shape=jax.ShapeDtypeStruct((n_blocks * EXPERT_TILE, PACKED), jnp.uint32),
        compiler_params=_cparams("arbitrary"),
        name="dispatch",
    )(seg_tbl, pad_tbl, n_used, hf, pgt)


def _expert_kernel(be_ref, nu_ref, nxt_ref, slot_ref, rows_ref, xb_ref, wgu_hbm, bgu_ref, wd_hbm, bd_ref,
                   yb_ref, wgu_f, wd_f, wgu_s, wd_s, sem, *, layer):
    i = pl.program_id(0)

    def fetch(e, slot):
        return (pltpu.make_async_copy(wgu_hbm.at[layer, e], wgu_f.at[slot], sem.at[slot]),
                pltpu.make_async_copy(wd_hbm.at[layer, e], wd_f.at[slot], sem.at[slot]))

    @pl.when(i < nu_ref[0])
    def _():
        e = be_ref[i]
        slot = slot_ref[i]
        first_block = jnp.logical_or(i == 0, e != be_ref[jnp.maximum(i - 1, 0)])

        @pl.when(first_block)
        def _():
            @pl.when(i == 0)
            def _():
                for cp in fetch(e, slot):
                    cp.start()

            for cp in fetch(e, slot):
                cp.wait()
            wgu_s[...] = wgu_f[slot].astype(BF16)
            wd_s[...] = wd_f[slot].astype(BF16)

            @pl.when(nxt_ref[i] >= 0)
            def _():
                for cp in fetch(nxt_ref[i], 1 - slot):
                    cp.start()

        def ffn(sl):
            gu = _dot(_unpack_pairs(xb_ref[sl, :]), wgu_s[...]) + bgu_ref[...]
            gate = jnp.minimum(gu[:, :D_FF], SWIGLU_LIMIT)
            up = jnp.clip(gu[:, D_FF:], -SWIGLU_LIMIT, SWIGLU_LIMIT)
            act = (up + 1.0) * gate * _sigmoid(SWIGLU_ALPHA * gate)
            y = _dot(act.astype(BF16), wd_s[...]) + bd_ref[...]
            yb_ref[sl, :] = _pack_pairs(y.astype(BF16).astype(F32))

        n_sub = EXPERT_TILE // EXPERT_SUB
        for k in range(1, n_sub + 1):
            used = k * EXPERT_SUB
            holds_k_parts = rows_ref[i] > used - EXPERT_SUB
            if k < n_sub:
                holds_k_parts = jnp.logical_and(holds_k_parts, rows_ref[i] <= used)

            @pl.when(holds_k_parts)
            def _(used=used):
                for start in range(0, used, EXPERT_CHUNK):
                    ffn(slice(start, min(start + EXPERT_CHUNK, used)))
                if used < EXPERT_TILE:
                    yb_ref[used:, :] = jnp.zeros((EXPERT_TILE - used, PACKED), jnp.uint32)

    @pl.when(i >= nu_ref[0])
    def _():
        yb_ref[...] = jnp.zeros_like(yb_ref)


def _experts(xb, block_e, n_used, next_e, slot, block_rows, layer, w_gu, b_gu, w_down, b_down):
    te = EXPERT_TILE
    n_blocks = xb.shape[0] // te
    depth = w_gu.shape[0]

    def x_map(i, be, nu, *_):
        return (jnp.maximum(jnp.minimum(i, nu[0] - 1), 0), 0)

    def bias_spec(width):
        return pl.BlockSpec((None, None, 1, width), lambda i, be, *_: (layer, be[i], 0, 0))

    return pl.pallas_call(
        functools.partial(_expert_kernel, layer=layer),
        grid_spec=pltpu.PrefetchScalarGridSpec(
            num_scalar_prefetch=5,
            grid=(n_blocks,),
            in_specs=[
                pl.BlockSpec((te, PACKED), x_map),
                pl.BlockSpec(memory_space=pl.ANY),
                bias_spec(2 * D_FF),
                pl.BlockSpec(memory_space=pl.ANY),
                bias_spec(D_MODEL),
            ],
            out_specs=pl.BlockSpec((te, PACKED), lambda i, *_: (i, 0)),
            scratch_shapes=[
                pltpu.VMEM((2, D_MODEL, 2 * D_FF), F32),
                pltpu.VMEM((2, D_FF, D_MODEL), F32),
                pltpu.VMEM((D_MODEL, 2 * D_FF), BF16),
                pltpu.VMEM((D_FF, D_MODEL), BF16),
                pltpu.SemaphoreType.DMA((2,)),
            ],
        ),
        out_shape=jax.ShapeDtypeStruct(xb.shape, jnp.uint32),
        compiler_params=_cparams("arbitrary"),
        name="experts",
    )(block_e, n_used, next_e, slot, block_rows, xb, w_gu,
      b_gu.reshape(depth, N_EXPERTS, 1, 2 * D_FF), w_down, b_down.reshape(depth, N_EXPERTS, 1, D_MODEL))


def _combine_kernel(row_ref, tbl_ref, x_ref, pg_ref, mod_ref, fw_ref, yb_ref, *rest, split_tiles):
    del row_ref
    stage_s, sem = rest[-2:]
    step = pl.program_id(0)
    n_steps = pl.num_programs(0)
    sub = MOVE_TILE // ROUTE_TILE
    group = (step % 2) * sub

    def fetch(tile, into):
        def seg_body(e, carry):
            rows, dst, src = _segment_table(tbl_ref, tile, e)

            @pl.when(rows > 0)
            def _():
                pltpu.make_async_copy(yb_ref.at[pl.ds(src, rows), :],
                                      stage_s.at[into, pl.ds(dst, rows), :], sem.at[into]).start()
            return carry

        lax.fori_loop(0, N_EXPERTS, seg_body, 0)

    @pl.when(step == 0)
    def _():
        stage_s[...] = jnp.zeros_like(stage_s)
        for j in range(sub):
            fetch(j, j)

    @pl.when(step + 1 < n_steps)
    def _():
        for j in range(sub):
            fetch((step + 1) * sub + j, sub - group + j)

    s_iota = lax.broadcasted_iota(jnp.int32, (ROUTE_TILE, SLOTS), 1).astype(F32)
    ys = []
    for j in range(sub):
        fetched = _tile_rows_total(tbl_ref, step * sub + j)
        pltpu.make_async_copy(yb_ref.at[pl.ds(0, fetched), :],
                              stage_s.at[group + j, pl.ds(0, fetched), :], sem.at[group + j]).wait()
        rows_sorted = _unpack_pairs(stage_s[group + j])
        pg = pg_ref[j * ROUTE_TILE:(j + 1) * ROUTE_TILE, :]
        sel = jnp.zeros((ROUTE_TILE, SLOTS), F32)
        for k in range(TOP_K):
            sel = jnp.where(s_iota == pg[:, k:k + 1], pg[:, TOP_K + k:TOP_K + k + 1], sel)
        ys.append(_dot(sel.astype(BF16), rows_sorted))
    x = x_ref[...] + mod_ref[5:6, :] * jnp.concatenate(ys, axis=0)
    if split_tiles is None:
        rest[0][...] = x
    else:
        x = _rms(x, fw_ref[...])

        @pl.when(step < split_tiles)
        def _():
            rest[0][...] = x

        @pl.when(step >= split_tiles)
        def _():
            rest[1][...] = x


def _combine(x, pg, seg_tbl, yb, tile_row, mod_l, final_w, n_prompt=None):
    n = x.shape[0]
    tm = MOVE_TILE
    sub = tm // ROUTE_TILE
    if n_prompt is None:
        split = None
        out_specs = pl.BlockSpec((tm, D_MODEL), lambda i, row, tbl: (i, 0))
        out_shape = jax.ShapeDtypeStruct((n, D_MODEL), F32)
    else:
        split = n_prompt // tm
        out_specs = [
            pl.BlockSpec((tm, D_MODEL), lambda i, row, tbl: (jnp.minimum(i, split - 1), 0)),
            pl.BlockSpec((tm, D_MODEL), lambda i, row, tbl: (jnp.maximum(i - split, 0), 0)),
        ]
        out_shape = [jax.ShapeDtypeStruct((n_prompt, D_MODEL), F32),
                     jax.ShapeDtypeStruct((n - n_prompt, D_MODEL), F32)]
    return pl.pallas_call(
        functools.partial(_combine_kernel, split_tiles=split),
        grid_spec=pltpu.PrefetchScalarGridSpec(
            num_scalar_prefetch=2,
            grid=(n // tm,),
            in_specs=[
                pl.BlockSpec((tm, D_MODEL), lambda i, row, tbl: (i, 0)),
                pl.BlockSpec((tm, LANES), lambda i, row, tbl: (i, 0)),
                pl.BlockSpec((None, MOD_PARTS, D_MODEL), lambda i, row, tbl: (row[i], 0, 0)),
                pl.BlockSpec((1, D_MODEL), lambda i, row, tbl: (0, 0)),
                pl.BlockSpec(memory_space=pl.ANY),
            ],
            out_specs=out_specs,
            scratch_shapes=[pltpu.VMEM((2 * sub, SLOTS, PACKED), jnp.uint32),
                            pltpu.SemaphoreType.DMA((2 * sub,))],
        ),
        out_shape=out_shape,
        compiler_params=_cparams("arbitrary"),
        name="combine",
    )(tile_row, seg_tbl, x, pg, mod_l, final_w.reshape(1, D_MODEL), yb)


def _moe(routed, tile_row, mod_l, layer, w_gu, b_gu, w_down, b_down, final_w, n_prompt=None):
    x, hf, pg, pgt, cnt = routed
    n = x.shape[0]

    te = EXPERT_TILE
    nt = n // ROUTE_TILE
    n_blocks = (n * TOP_K + nt * N_EXPERTS * (SEG_ALIGN - 1)) // te + 1 + N_EXPERTS
    cnt = cnt[:, 0, :N_EXPERTS]
    cnt = (cnt + SEG_ALIGN - 1) // SEG_ALIGN * SEG_ALIGN
    total = jnp.sum(cnt, axis=0)
    blocks_e = (total + te - 1) // te
    block_end = jnp.cumsum(blocks_e)
    pstart = (block_end - blocks_e) * te
    n_used = block_end[-1]
    first_row = pstart[None, :] + jnp.cumsum(cnt, axis=0) - cnt
    tile_off = jnp.cumsum(cnt, axis=1) - cnt
    tile_total = jnp.broadcast_to(jnp.sum(cnt, axis=1, keepdims=True), cnt.shape)
    seg_tbl = jnp.concatenate([cnt, tile_off, first_row, tile_total], axis=1)
    seg_tbl = seg_tbl.reshape(-1).astype(jnp.int32)
    pad_rows = blocks_e * te - total
    zeroed = jnp.sum(pad_rows) + (n_blocks - n_used) * te
    pad_tbl = jnp.concatenate([pstart + total, pad_rows, zeroed[None]]).astype(jnp.int32)
    block_ids = jnp.arange(n_blocks, dtype=jnp.int32)
    clamped = jnp.minimum(block_ids, n_used - 1)
    block_e = jnp.sum((clamped[:, None] >= block_end[None, :]).astype(jnp.int32), axis=1)
    block_e = jnp.minimum(block_e, N_EXPERTS - 1).astype(jnp.int32)
    n_used = n_used.astype(jnp.int32).reshape(1)
    owns = (blocks_e > 0)[None, :]
    ids = jnp.arange(N_EXPERTS, dtype=jnp.int32)[None, :]
    mine = block_e[:, None]
    next_e = jnp.min(jnp.where((ids > mine) & owns, ids, N_EXPERTS), axis=1)
    next_e = jnp.where(next_e < N_EXPERTS, next_e, -1).astype(jnp.int32)
    slot = ((jnp.sum(((ids <= mine) & owns).astype(jnp.int32), axis=1) + 1) % 2).astype(jnp.int32)

    row_end = jnp.sum(jnp.where(ids == mine, (pstart + total)[None, :], 0), axis=1)
    block_rows = jnp.clip(row_end - block_ids * te, 0, te).astype(jnp.int32)

    xb = _dispatch(hf, pgt, seg_tbl, pad_tbl, n_used, n_blocks)
    yb = _experts(xb, block_e, n_used, next_e, slot, block_rows, layer, w_gu, b_gu, w_down, b_down)
    return _combine(x, pg, seg_tbl, yb, tile_row, mod_l, final_w, n_prompt)


def _tile_rows(n_prompt_tok, n_sample_seq, sample_len, tile):
    starts = np.arange(0, n_prompt_tok + n_sample_seq * sample_len, tile)
    row = np.where(starts < n_prompt_tok, 0, 1 + (starts - n_prompt_tok) // sample_len)
    return jnp.asarray(row, dtype=jnp.int32)


def kernel(x_prompt, x_sample, state_hgrn, c, c_ctx, w_mod, b_mod, norm_mix, norm_ffn, hg_w_in,
           hg_lb_logits, hg_gnorm, hg_w_out, cv_w_in, cv_w, cv_w_out, router_w, router_b,
           moe_w_gu, moe_b_gu, moe_w_down, moe_b_down, final_norm):
    bp, tp, d = x_prompt.shape
    bs, ts, _ = x_sample.shape
    depth = w_mod.shape[0]
    n_prompt = bp * tp
    n = n_prompt + bs * ts
    assert d == D_MODEL and depth == 2 and 1 + bs <= COND_ROWS
    assert n_prompt % SCAN_ROWS == 0 and ts == SCAN_ROWS and tp == SCAN_PIECE

    x_p = x_prompt.reshape(n_prompt, d)
    x_s = x_sample.reshape(bs * ts, d)
    cond = jnp.zeros((COND_ROWS, d), F32).at[0].set(c_ctx).at[1:1 + bs].set(c)
    mod = _modulation(cond, w_mod, b_mod).reshape(depth, COND_ROWS, MOD_PARTS, d)

    tile_row = _tile_rows(n_prompt, bs, ts, TOKEN_TILE)
    move_row = _tile_rows(n_prompt, bs, ts, MOVE_TILE)
    starts = np.arange(0, n, TOKEN_TILE)
    tile_width = jnp.asarray(np.where(starts < n_prompt, tp, GRID_W), dtype=jnp.int32)

    lb_all = jnp.cumsum(jax.nn.softmax(hg_lb_logits.astype(F32), axis=1), axis=1)

    w_in_bf = hg_w_in[0].astype(BF16)
    gn = hg_gnorm[0].reshape(-1)
    zero_state = jnp.zeros((SCAN_ROWS // tp, 2, HEADS, HEAD_DIM, HEAD_DIM), F32)
    o_p, s_new = _hgrn_scan(x_p, mod[0], 0, 0, norm_mix[0], w_in_bf, lb_all[0, 0], lb_all[1, 0], gn,
                            zero_state, tp, True)
    o_s, _ = _hgrn_scan(x_s, mod[0], 1, 1, norm_mix[0], w_in_bf, lb_all[0, 0], lb_all[1, 0], gn,
                        state_hgrn[:, 0], ts, False)
    routed = _proj_router(x_p, x_s, o_p, o_s, hg_w_out[0].astype(BF16), tile_row, mod[0], norm_ffn[0],
                          router_w[0], router_b[0])
    x = _moe(routed, move_row, mod[0], 0, moe_w_gu, moe_b_gu, moe_w_down, moe_b_down, final_norm)

    routed = _conv_router(x, tile_row, tile_width, mod[1], norm_mix[1], cv_w_in[0].astype(BF16),
                          cv_w[0], cv_w_out[0].astype(BF16), norm_ffn[1], router_w[1], router_b[1])
    y_p, y_s = _moe(routed, move_row, mod[1], 1, moe_w_gu, moe_b_gu, moe_w_down, moe_b_down,
                    final_norm, n_prompt)

    y_prompt = y_p.reshape(bp, tp, d)
    y_sample = y_s.reshape(bs, ts, d)
    return (y_prompt, y_sample, s_new.reshape(bp, 1, 2, HEADS, HEAD_DIM, HEAD_DIM))
```

```python
import functools

import numpy as np
import jax
import jax.numpy as jnp
from jax import lax
from jax.experimental import pallas as pl
from jax.experimental.pallas import tpu as pltpu

F32 = jnp.float32
BF16 = jnp.bfloat16

D_MODEL = 1024
HEADS = 8
HEAD_DIM = 128
CHUNK = 64
GRID_W = 64
N_EXPERTS = 32
TOP_K = 4
D_FF = 1024
SWIGLU_LIMIT = 7.0
SWIGLU_ALPHA = 1.702
EPS = 1e-6

MOD_PARTS = 6
COND_ROWS = 16

LANES = 128
SUBLANES = 8
SCAN_ROWS = 2048
SCAN_PIECE = 256
TOKEN_TILE = 512
MOVE_TILE = 1024
EXPERT_TILE = 1024
EXPERT_CHUNK = 1024
EXPERT_SUB = 256
ROUTE_TILE = 256
SEG_ALIGN = 8
SLOTS = ROUTE_TILE * TOP_K + N_EXPERTS * SEG_ALIGN
VMEM_LIMIT = 56 * 1024 * 1024


def _cparams(*sem):
    return pltpu.CompilerParams(dimension_semantics=sem, vmem_limit_bytes=VMEM_LIMIT)


def _sigmoid(x):
    return 1.0 / (1.0 + jnp.exp(-x))


def _rms(x, w):
    return x * lax.rsqrt(jnp.mean(x * x, axis=-1, keepdims=True) + EPS) * w


def _dot(a, b):
    return jnp.dot(a, b, preferred_element_type=F32)


def _dot_nt(a, b):
    return lax.dot_general(a, b, (((1,), (1,)), ((), ())), preferred_element_type=F32)


PACKED = D_MODEL // 2
_HIGH_HALF = 0xFFFF0000


def _pack_pairs(x):
    c = x.shape[1] // 2
    lo = lax.bitcast_convert_type(x[:, :c], jnp.uint32) >> 16
    hi = lax.bitcast_convert_type(x[:, c:], jnp.uint32) & jnp.uint32(_HIGH_HALF)
    return hi | lo


def _unpack_pairs(u):
    lo = lax.bitcast_convert_type(u << 16, F32)
    hi = lax.bitcast_convert_type(u & jnp.uint32(_HIGH_HALF), F32)
    return jnp.concatenate([lo, hi], axis=1).astype(BF16)


def _mod_kernel(cond_ref, w_ref, b_ref, o_ref):
    c = cond_ref[...]
    s = (c * _sigmoid(c)).astype(BF16)
    o_ref[...] = _dot(s, w_ref[...].astype(BF16)) + b_ref[...]


def _modulation(cond16, w_mod, b_mod):
    depth = w_mod.shape[0]
    n_out = w_mod.shape[2]
    tn = 1024
    return pl.pallas_call(
        _mod_kernel,
        grid=(depth, n_out // tn),
        in_specs=[
            pl.BlockSpec((COND_ROWS, D_MODEL), lambda l, j: (0, 0)),
            pl.BlockSpec((None, D_MODEL, tn), lambda l, j: (l, 0, j)),
            pl.BlockSpec((None, 1, tn), lambda l, j: (l, 0, j)),
        ],
        out_specs=pl.BlockSpec((None, COND_ROWS, tn), lambda l, j: (l, 0, j)),
        out_shape=jax.ShapeDtypeStruct((depth, COND_ROWS, n_out), F32),
        compiler_params=_cparams("arbitrary", "arbitrary"),
        name="modulation",
    )(cond16, w_mod, b_mod.reshape(depth, 1, n_out))


def _two_group_specs(tm, prompt_tiles):
    return [
        pl.BlockSpec((tm, D_MODEL), lambda i, *_: (jnp.minimum(i, prompt_tiles - 1), 0)),
        pl.BlockSpec((tm, D_MODEL), lambda i, *_: (jnp.maximum(i - prompt_tiles, 0), 0)),
    ]


def _pick_group(a_ref, b_ref, prompt_tiles):
    return jnp.where(pl.program_id(0) < prompt_tiles, a_ref[...], b_ref[...])


def _hgrn_kernel(x_ref, mod_ref, nw_ref, wq_ref, wff_ref, wfb_ref, wv_ref, wg_ref, lbf_ref, lbb_ref,
                 gn_ref, s0_ref, o_ref, so_ref, hm_s, q_s, vt_s, kf_s, lf_s, kb_s, lb_s, of_s, ob_s, st_s,
                 *, seq_len, zero_init):
    rows = x_ref.shape[0]
    piece = SCAN_PIECE
    n_pieces = rows // piece
    cpp = piece // CHUNK
    per_piece_seq = seq_len == piece
    assert per_piece_seq or seq_len == rows

    @pl.when(pl.program_id(1) == 0)
    def _():
        y = _rms(x_ref[...], nw_ref[...])
        hm_s[...] = (y * (1.0 + mod_ref[1:2, :]) + mod_ref[0:1, :]).astype(BF16)

    hm = hm_s[...]

    zq = _dot(hm, wq_ref[...])
    q_s[...] = zq * _sigmoid(zq)
    v = _dot(hm, wv_ref[...])
    for p in range(n_pieces):
        vt_s[p] = v[p * piece:(p + 1) * piece, :].T.astype(BF16)
    for w_ref, lbr, k_s, l_s in ((wff_ref, lbf_ref, kf_s, lf_s), (wfb_ref, lbb_ref, kb_s, lb_s)):
        lb = lbr[...]
        f = lb + (1.0 - lb) * _sigmoid(_dot(hm, w_ref[...]))
        k_s[...] = 1.0 - f
        l_s[...] = jnp.log(f)

    def load_state(j, d):
        for h in range(2):
            if zero_init:
                st_s[d, h] = jnp.zeros((HEAD_DIM, HEAD_DIM), F32)
            else:
                st_s[d, h] = s0_ref[j, d, h].T

    def store_state(j, d):
        for h in range(2):
            so_ref[j, d, h] = st_s[d, h].T

    def piece_body(i, carry):
        r = lax.broadcasted_iota(jnp.int32, (piece, piece), 0)
        c = lax.broadcasted_iota(jnp.int32, (piece, piece), 1)
        same = (r // CHUNK) == (c // CHUNK)
        row_chunk = lax.broadcasted_iota(jnp.int32, (piece, 1), 0) // CHUNK
        col_chunk = lax.broadcasted_iota(jnp.int32, (1, piece), 1) // CHUNK
        for d in range(2):
            k_s, l_s, o_s = ((kf_s, lf_s, of_s), (kb_s, lb_s, ob_s))[d]
            p = i if d == 0 else n_pieces - 1 - i
            if per_piece_seq:
                load_state(p, d)
            sl = pl.ds(pl.multiple_of(p * piece, piece), piece)
            keep = same & ((c <= r) if d == 0 else (c >= r))
            tri = jnp.where(keep, 1.0, 0.0).astype(BF16)
            lf = l_s[sl, :]
            hi = lf.astype(BF16)
            mid = (lf - hi.astype(F32)).astype(BF16)
            b = _dot(tri, hi) + _dot(tri, mid)
            b3 = b.reshape(cpp, CHUNK, 2 * HEAD_DIM)
            edge = CHUNK - 1 if d == 0 else 0
            bl3 = b3[:, edge:edge + 1, :]
            centre = CHUNK // 2 - 1 if d == 0 else CHUNK // 2
            bm3 = b3[:, centre:centre + 1, :]
            q = q_s[sl, :]
            q3 = q.reshape(cpp, CHUNK, 2 * HEAD_DIM)
            k3 = k_s[sl, :].reshape(cpp, CHUNK, 2 * HEAD_DIM)
            flat = lambda a: a.reshape(piece, 2 * HEAD_DIM).astype(BF16)
            qd = (q * jnp.exp(b)).astype(BF16)
            qi = flat(q3 * jnp.exp(b3 - bm3))
            ki = flat(k3 * jnp.exp(bm3 - b3))
            ks = flat(k3 * jnp.exp(bl3 - b3))
            dec = jnp.exp(bl3)
            vt = vt_s[p]
            for h in range(2):
                hs = slice(h * HEAD_DIM, (h + 1) * HEAD_DIM)
                att = jnp.where(keep, _dot_nt(qi[:, hs], ki[:, hs]), 0.0).astype(BF16)
                vt_h = vt[hs, :]
                v_exp = jnp.concatenate(
                    [jnp.where(col_chunk == ci, vt_h, jnp.zeros_like(vt_h)) for ci in range(cpp)], axis=0)
                incr = _dot(v_exp, ks[:, hs])
                st = st_s[d, h]
                before = [None] * cpp
                for ci in (range(cpp) if d == 0 else range(cpp - 1, -1, -1)):
                    before[ci] = st.astype(BF16)
                    st = st * dec[ci, :, hs] + incr[ci * HEAD_DIM:(ci + 1) * HEAD_DIM, :]
                st_s[d, h] = st
                qd_h = qd[:, hs]
                q_exp = [jnp.where(row_chunk == ci, qd_h, jnp.zeros_like(qd_h)) for ci in range(cpp)]
                lhs = jnp.concatenate([att] + q_exp, axis=1)
                rhs_t = jnp.concatenate([vt_h] + before, axis=1)
                o_s[sl, hs] = _dot_nt(lhs, rhs_t)
            if per_piece_seq:
                store_state(p, d)
        return carry

    if not per_piece_seq:
        load_state(0, 0)
        load_state(0, 1)
    lax.fori_loop(0, n_pieces, piece_body, 0, unroll=4)
    if not per_piece_seq:
        store_state(0, 0)
        store_state(0, 1)

    o = of_s[...] + ob_s[...]
    gn = gn_ref[...]
    o = jnp.concatenate(
        [_rms(o[:, h * HEAD_DIM:(h + 1) * HEAD_DIM], gn[:, h * HEAD_DIM:(h + 1) * HEAD_DIM])
         for h in range(2)], axis=1)
    o_ref[...] = (o * _sigmoid(_dot(hm, wg_ref[...]))).astype(o_ref.dtype)


def _hgrn_scan(x, mod_l, mod_row0, mod_row_step, norm_w, w_in_bf, lb_f, lb_b, gnorm, s0, seq_len,
               zero_init):
    n_rows = x.shape[0]
    n_seq_total = n_rows // seq_len
    rows = SCAN_ROWS
    seq_per_step = rows // seq_len
    pair = 2 * HEAD_DIM
    n_pairs = HEADS // 2

    def w_spec(seg):
        return pl.BlockSpec((D_MODEL, pair), lambda sb, hp: (0, seg * n_pairs + hp))

    vec_spec = pl.BlockSpec((1, pair), lambda sb, hp: (0, hp))
    st_block = (seq_per_step, 2, 2, HEAD_DIM, HEAD_DIM)
    st_spec = pl.BlockSpec(st_block, lambda sb, hp: (sb, 0, hp, 0, 0))
    s0_spec = pl.BlockSpec(st_block, (lambda sb, hp: (0, 0, hp, 0, 0)) if zero_init
                           else (lambda sb, hp: (sb, 0, hp, 0, 0)))
    scratch = [
        pltpu.VMEM((rows, D_MODEL), BF16),
        pltpu.VMEM((rows, pair), F32),
        pltpu.VMEM((rows // SCAN_PIECE, pair, SCAN_PIECE), BF16),
        pltpu.VMEM((rows, pair), F32),
        pltpu.VMEM((rows, pair), F32),
        pltpu.VMEM((rows, pair), F32),
        pltpu.VMEM((rows, pair), F32),
        pltpu.VMEM((rows, pair), F32),
        pltpu.VMEM((rows, pair), F32),
        pltpu.VMEM((2, 2, HEAD_DIM, HEAD_DIM), F32),
    ]
    in_specs = [
        pl.BlockSpec((rows, D_MODEL), lambda sb, hp: (sb, 0), pipeline_mode=pl.Buffered(1)),
        pl.BlockSpec((None, MOD_PARTS, D_MODEL), lambda sb, hp: (mod_row0 + sb * mod_row_step, 0, 0)),
        pl.BlockSpec((1, D_MODEL), lambda sb, hp: (0, 0)),
        w_spec(0), w_spec(1), w_spec(2), w_spec(3), w_spec(4),
        vec_spec, vec_spec, vec_spec,
        s0_spec,
    ]
    args = [x, mod_l, norm_w.reshape(1, D_MODEL), w_in_bf, w_in_bf, w_in_bf, w_in_bf, w_in_bf,
            lb_f.reshape(1, D_MODEL), lb_b.reshape(1, D_MODEL), gnorm.reshape(1, D_MODEL), s0]
    return pl.pallas_call(
        functools.partial(_hgrn_kernel, seq_len=seq_len, zero_init=zero_init),
        grid=(n_rows // rows, n_pairs),
        in_specs=in_specs,
        out_specs=[
            pl.BlockSpec((rows, pair), lambda sb, hp: (sb, hp)),
            st_spec,
        ],
        out_shape=[
            jax.ShapeDtypeStruct((n_rows, D_MODEL), BF16),
            jax.ShapeDtypeStruct((n_seq_total, 2, HEADS, HEAD_DIM, HEAD_DIM), F32),
        ],
        scratch_shapes=scratch,
        compiler_params=_cparams("arbitrary", "arbitrary"),
        name="hgrn_scan",
    )(*args)


def _route_rows(x1, half, mod_ref, nw_ref, rwh_ref, rwl_ref, rb_ref, hf_ref, pg_ref, pgt_ref, cnt_ref):
    tm = ROUTE_TILE
    rows = slice(half * tm, (half + 1) * tm)
    hf = _rms(x1, nw_ref[...]) * (1.0 + mod_ref[4:5, :]) + mod_ref[3:4, :]
    hf_hi = hf.astype(BF16)
    hf_ref[rows, :] = hf_hi
    hf_lo = (hf - hf_hi.astype(F32)).astype(BF16)
    logits = (_dot(hf_hi, rwh_ref[...]) + (_dot(hf_hi, rwl_ref[...]) + _dot(hf_lo, rwh_ref[...]))
              + rb_ref[...])
    lane = lax.broadcasted_iota(jnp.int32, (tm, LANES), 1)
    lane_f = lane.astype(F32)
    work = logits
    vals, hots = [], []
    for _ in range(TOP_K):
        m = jnp.max(work, axis=-1, keepdims=True)
        idx_f = jnp.min(jnp.where(work == m, lane_f, float(LANES)), axis=-1, keepdims=True)
        hot = lane_f == idx_f
        vals.append(m)
        hots.append(hot)
        work = jnp.where(hot, -jnp.inf, work)
    ex = [jnp.exp(v - vals[0]) for v in vals]
    inv = 1.0 / (ex[0] + ex[1] + ex[2] + ex[3])

    multi = jnp.where(hots[0] | hots[1] | hots[2] | hots[3], 1.0, 0.0)
    tr = lax.broadcasted_iota(jnp.int32, (tm, tm), 0)
    tc = lax.broadcasted_iota(jnp.int32, (tm, tm), 1)
    earlier = jnp.where(tc < tr, 1.0, 0.0).astype(BF16)
    before = _dot(earlier, multi.astype(BF16))
    cnt = jnp.sum(multi, axis=0, keepdims=True)
    er = lax.broadcasted_iota(jnp.int32, (LANES, LANES), 0)
    ec = lax.broadcasted_iota(jnp.int32, (LANES, LANES), 1)
    lower = jnp.where(er < ec, 1.0, 0.0).astype(BF16)
    cnt_al = jnp.floor((cnt + (SEG_ALIGN - 1)) * (1.0 / SEG_ALIGN)) * SEG_ALIGN
    estart = _dot(jnp.broadcast_to(cnt_al, (SUBLANES, LANES)).astype(BF16), lower)[0:1, :]
    pos_all = before + estart

    pg = jnp.zeros((tm, LANES), F32)
    for k in range(TOP_K):
        pos = jnp.sum(jnp.where(hots[k], pos_all, 0.0), axis=-1, keepdims=True)
        pg = jnp.where(lane == k, pos, pg)
        pg = jnp.where(lane == TOP_K + k, ex[k] * inv, pg)
    pg_ref[rows, :] = pg
    pgt_ref[half] = pg.T[0:8, :]
    cnt_ref[half] = cnt.astype(jnp.int32)


def _route_tile(x1, route_refs):
    x1_ref = route_refs[-5]
    x1_ref[...] = x1
    for half in range(TOKEN_TILE // ROUTE_TILE):
        _route_rows(x1[half * ROUTE_TILE:(half + 1) * ROUTE_TILE, :], half,
                    *route_refs[:5], *route_refs[-4:])


def _proj_router_kernel(row_ref, xp_ref, xs_ref, op_ref, os_ref, w_ref, mod_ref, *route_refs,
                        prompt_tiles):
    del row_ref
    x = _pick_group(xp_ref, xs_ref, prompt_tiles)
    o = _pick_group(op_ref, os_ref, prompt_tiles)
    _route_tile(x + mod_ref[2:3, :] * _dot(o, w_ref[...]), (mod_ref,) + route_refs)


def _conv_router_kernel(row_ref, width_ref, x_ref, nwm_ref, win_ref, cw_ref, wout_ref, mod_ref,
                        *route_refs):
    del row_ref
    i = pl.program_id(0)
    x = x_ref[...]
    hm = (_rms(x, nwm_ref[...]) * (1.0 + mod_ref[1:2, :]) + mod_ref[0:1, :]).astype(BF16)
    z = _dot(hm, win_ref[...])
    bg = z[:, :D_MODEL]
    u = z[:, D_MODEL:2 * D_MODEL] * z[:, 2 * D_MODEL:]
    tm = x.shape[0]
    pos = lax.broadcasted_iota(jnp.int32, (tm, 1), 0) & (width_ref[i] - 1)
    prev = jnp.where(pos == 0, 0.0, pltpu.roll(u, 1, axis=0))
    nxt = jnp.where(pos == width_ref[i] - 1, 0.0, pltpu.roll(u, tm - 1, axis=0))
    v = cw_ref[0:1, :] * prev + cw_ref[1:2, :] * u + cw_ref[2:3, :] * nxt
    y = _dot((bg * v).astype(BF16), wout_ref[...])
    _route_tile(x + mod_ref[2:3, :] * y, (mod_ref,) + route_refs)


def _mixer_router_call(kernel_fn, n, n_prefetch, prefetch, in_specs, args, mod_l, norm_ffn,
                       router_w, router_b, name):
    tm = TOKEN_TILE
    sub = tm // ROUTE_TILE
    rw = jnp.zeros((D_MODEL, LANES), F32).at[:, :N_EXPERTS].set(router_w)
    rb = jnp.full((1, LANES), -jnp.inf, F32).at[0, :N_EXPERTS].set(router_b)
    rw_hi = rw.astype(BF16)
    rw_lo = (rw - rw_hi.astype(F32)).astype(BF16)
    const = lambda shape: pl.BlockSpec(shape, lambda i, *_: (0,) * len(shape))
    row_blk = lambda width: pl.BlockSpec((tm, width), lambda i, *_: (i, 0))
    return pl.pallas_call(
        kernel_fn,
        grid_spec=pltpu.PrefetchScalarGridSpec(
            num_scalar_prefetch=n_prefetch,
            grid=(n // tm,),
            in_specs=in_specs + [
                pl.BlockSpec((None, MOD_PARTS, D_MODEL), lambda i, row, *_: (row[i], 0, 0)),
                const((1, D_MODEL)), const((D_MODEL, LANES)), const((D_MODEL, LANES)), const((1, LANES)),
            ],
            out_specs=[
                row_blk(D_MODEL), row_blk(D_MODEL), row_blk(LANES),
                pl.BlockSpec((sub, 8, ROUTE_TILE), lambda i, *_: (i, 0, 0)),
                pl.BlockSpec((sub, 1, LANES), lambda i, *_: (i, 0, 0)),
            ],
        ),
        out_shape=[
            jax.ShapeDtypeStruct((n, D_MODEL), F32),
            jax.ShapeDtypeStruct((n, D_MODEL), BF16),
            jax.ShapeDtypeStruct((n, LANES), F32),
            jax.ShapeDtypeStruct((n // ROUTE_TILE, 8, ROUTE_TILE), F32),
            jax.ShapeDtypeStruct((n // ROUTE_TILE, 1, LANES), jnp.int32),
        ],
        compiler_params=_cparams("arbitrary"),
        name=name,
    )(*prefetch, *args, mod_l, norm_ffn.reshape(1, D_MODEL), rw_hi, rw_lo, rb)


def _proj_router(x_p, x_s, o_prompt, o_sample, w_out_bf, tile_row, mod_l, norm_ffn, router_w, router_b):
    n = x_p.shape[0] + x_s.shape[0]
    pt = x_p.shape[0] // TOKEN_TILE
    in_specs = _two_group_specs(TOKEN_TILE, pt) + _two_group_specs(TOKEN_TILE, pt) + [
        pl.BlockSpec((D_MODEL, D_MODEL), lambda i, *_: (0, 0))]
    return _mixer_router_call(functools.partial(_proj_router_kernel, prompt_tiles=pt), n, 1,
                              (tile_row,), in_specs, (x_p, x_s, o_prompt, o_sample, w_out_bf),
                              mod_l, norm_ffn, router_w, router_b, "proj_router")


def _conv_router(x, tile_row, tile_width, mod_l, norm_mix, w_in_bf, conv_w, w_out_bf, norm_ffn,
                 router_w, router_b):
    n = x.shape[0]
    in_specs = [
        pl.BlockSpec((TOKEN_TILE, D_MODEL), lambda i, *_: (i, 0)),
        pl.BlockSpec((1, D_MODEL), lambda i, *_: (0, 0)),
        pl.BlockSpec((D_MODEL, 3 * D_MODEL), lambda i, *_: (0, 0)),
        pl.BlockSpec((3, D_MODEL), lambda i, *_: (0, 0)),
        pl.BlockSpec((D_MODEL, D_MODEL), lambda i, *_: (0, 0)),
    ]
    return _mixer_router_call(_conv_router_kernel, n, 2, (tile_row, tile_width), in_specs,
                              (x, norm_mix.reshape(1, D_MODEL), w_in_bf, conv_w, w_out_bf),
                              mod_l, norm_ffn, router_w, router_b, "conv_router")


SEG_STRIDE = 4 * N_EXPERTS


def _segment_table(tbl_ref, t, e):
    base = t * SEG_STRIDE
    al = lambda v: pl.multiple_of(v, SEG_ALIGN)
    return (al(tbl_ref[base + e]), al(tbl_ref[base + N_EXPERTS + e]),
            al(tbl_ref[base + 2 * N_EXPERTS + e]))


def _tile_rows_total(tbl_ref, t):
    return pl.multiple_of(tbl_ref[t * SEG_STRIDE + 3 * N_EXPERTS], SEG_ALIGN)


def _dispatch_kernel(tbl_ref, pad_ref, nu_ref, hf_ref, pgt_ref, xb_ref, stage_s, zero_s, sem, zsem,
                     *, n_blocks):
    step = pl.program_id(0)
    n_steps = pl.num_programs(0)
    sub = MOVE_TILE // ROUTE_TILE
    half = zero_s.shape[0]

    def zero_rows(start, rows):
        start = pl.multiple_of(start, SEG_ALIGN)
        rows = pl.multiple_of(rows, SEG_ALIGN)

        @pl.when(rows > 0)
        def _():
            pltpu.make_async_copy(zero_s.at[pl.ds(0, rows), :], xb_ref.at[pl.ds(start, rows), :],
                                  zsem).start()

    @pl.when(step == 0)
    def _():
        zero_s[...] = jnp.zeros_like(zero_s)

        def pad_body(e, carry):
            start = pad_ref[e]
            rows = pad_ref[N_EXPERTS + e]
            first = jnp.minimum(rows, half)
            zero_rows(start, first)
            zero_rows(start + half, rows - first)
            return carry

        lax.fori_loop(0, N_EXPERTS, pad_body, 0)

        def tail_body(b, carry):
            for part in range(EXPERT_TILE // half):
                zero_rows(b * EXPERT_TILE + part * half, half)
            return carry

        lax.fori_loop(nu_ref[0], n_blocks, tail_body, 0)

    def wait_tile(tile, sl):
        rows = _tile_rows_total(tbl_ref, tile)
        pltpu.make_async_copy(stage_s.at[sl, pl.ds(0, rows), :], xb_ref.at[pl.ds(0, rows), :],
                              sem.at[sl]).wait()

    group = (step % 2) * sub

    @pl.when(step >= 2)
    def _():
        for j in range(sub):
            wait_tile((step - 2) * sub + j, group + j)

    s_iota = lax.broadcasted_iota(jnp.int32, (SLOTS, ROUTE_TILE), 0).astype(F32)
    for j in range(sub):
        pgt = pgt_ref[j]
        hit = s_iota == pgt[0:1, :]
        for k in range(1, TOP_K):
            hit = hit | (s_iota == pgt[k:k + 1, :])
        perm = jnp.where(hit, 1.0, 0.0).astype(BF16)
        hf = hf_ref[j * ROUTE_TILE:(j + 1) * ROUTE_TILE, :]
        stage_s[group + j] = _pack_pairs(_dot(perm, hf))

    for j in range(sub):
        def seg_body(e, carry, j=j):
            rows, src, dst = _segment_table(tbl_ref, step * sub + j, e)

            @pl.when(rows > 0)
            def _():
                pltpu.make_async_copy(stage_s.at[group + j, pl.ds(src, rows), :],
                                      xb_ref.at[pl.ds(dst, rows), :], sem.at[group + j]).start()
            return carry

        lax.fori_loop(0, N_EXPERTS, seg_body, 0)

    @pl.when(step == n_steps - 1)
    def _():
        for j in range(sub):
            wait_tile(step * sub + j, group + j)

        @pl.when(n_steps > 1)
        def _():
            for j in range(sub):
                wait_tile((step - 1) * sub + j, sub - group + j)

        zeroed = pl.multiple_of(pad_ref[2 * N_EXPERTS], SEG_ALIGN)

        @pl.when(zeroed > 0)
        def _():
            pltpu.make_async_copy(xb_ref.at[pl.ds(0, zeroed), :], xb_ref.at[pl.ds(0, zeroed), :],
                                  zsem).wait()


def _dispatch(hf, pgt, seg_tbl, pad_tbl, n_used, n_blocks):
    n = hf.shape[0]
    sub = MOVE_TILE // ROUTE_TILE
    return pl.pallas_call(
        functools.partial(_dispatch_kernel, n_blocks=n_blocks),
        grid_spec=pltpu.PrefetchScalarGridSpec(
            num_scalar_prefetch=3,
            grid=(n // MOVE_TILE,),
            in_specs=[
                pl.BlockSpec((MOVE_TILE, D_MODEL), lambda i, a, b, c: (i, 0)),
                pl.BlockSpec((sub, 8, ROUTE_TILE), lambda i, a, b, c: (i, 0, 0)),
            ],
            out_specs=pl.BlockSpec(memory_space=pl.ANY),
            scratch_shapes=[
                pltpu.VMEM((2 * sub, SLOTS, PACKED), jnp.uint32),
                pltpu.VMEM((EXPERT_TILE // 2, PACKED), jnp.uint32),
                pltpu.SemaphoreType.DMA((2 * sub,)),
                pltpu.SemaphoreType.DMA,
            ],
        ),
        out_shape=jax.ShapeDtypeStruct((n_blocks * EXPERT_TILE, PACKED), jnp.uint32),
        compiler_params=_cparams("arbitrary"),
        name="dispatch",
    )(seg_tbl, pad_tbl, n_used, hf, pgt)


def _expert_kernel(be_ref, nu_ref, nxt_ref, slot_ref, rows_ref, xb_ref, wgu_hbm, bgu_ref, wd_hbm, bd_ref,
                   yb_ref, wgu_f, wd_f, wgu_s, wd_s, sem, *, layer):
    i = pl.program_id(0)

    def fetch(e, slot):
        return (pltpu.make_async_copy(wgu_hbm.at[layer, e], wgu_f.at[slot], sem.at[slot]),
                pltpu.make_async_copy(wd_hbm.at[layer, e], wd_f.at[slot], sem.at[slot]))

    @pl.when(i < nu_ref[0])
    def _():
        e = be_ref[i]
        slot = slot_ref[i]
        first_block = jnp.logical_or(i == 0, e != be_ref[jnp.maximum(i - 1, 0)])

        @pl.when(first_block)
        def _():
            @pl.when(i == 0)
            def _():
                for cp in fetch(e, slot):
                    cp.start()

            for cp in fetch(e, slot):
                cp.wait()
            wgu_s[...] = wgu_f[slot].astype(BF16)
            wd_s[...] = wd_f[slot].astype(BF16)

            @pl.when(nxt_ref[i] >= 0)
            def _():
                for cp in fetch(nxt_ref[i], 1 - slot):
                    cp.start()

        def ffn(sl):
            gu = _dot(_unpack_pairs(xb_ref[sl, :]), wgu_s[...]) + bgu_ref[...]
            gate = jnp.minimum(gu[:, :D_FF], SWIGLU_LIMIT)
            up = jnp.clip(gu[:, D_FF:], -SWIGLU_LIMIT, SWIGLU_LIMIT)
            act = (up + 1.0) * gate * _sigmoid(SWIGLU_ALPHA * gate)
            y = _dot(act.astype(BF16), wd_s[...]) + bd_ref[...]
            yb_ref[sl, :] = _pack_pairs(y.astype(BF16).astype(F32))

        n_sub = EXPERT_TILE // EXPERT_SUB
        for k in range(1, n_sub + 1):
            used = k * EXPERT_SUB
            holds_k_parts = rows_ref[i] > used - EXPERT_SUB
            if k < n_sub:
                holds_k_parts = jnp.logical_and(holds_k_parts, rows_ref[i] <= used)

            @pl.when(holds_k_parts)
            def _(used=used):
                for start in range(0, used, EXPERT_CHUNK):
                    ffn(slice(start, min(start + EXPERT_CHUNK, used)))
                if used < EXPERT_TILE:
                    yb_ref[used:, :] = jnp.zeros((EXPERT_TILE - used, PACKED), jnp.uint32)

    @pl.when(i >= nu_ref[0])
    def _():
        yb_ref[...] = jnp.zeros_like(yb_ref)


def _experts(xb, block_e, n_used, next_e, slot, block_rows, layer, w_gu, b_gu, w_down, b_down):
    te = EXPERT_TILE
    n_blocks = xb.shape[0] // te
    depth = w_gu.shape[0]

    def x_map(i, be, nu, *_):
        return (jnp.maximum(jnp.minimum(i, nu[0] - 1), 0), 0)

    def bias_spec(width):
        return pl.BlockSpec((None, None, 1, width), lambda i, be, *_: (layer, be[i], 0, 0))

    return pl.pallas_call(
        functools.partial(_expert_kernel, layer=layer),
        grid_spec=pltpu.PrefetchScalarGridSpec(
            num_scalar_prefetch=5,
            grid=(n_blocks,),
            in_specs=[
                pl.BlockSpec((te, PACKED), x_map),
                pl.BlockSpec(memory_space=pl.ANY),
                bias_spec(2 * D_FF),
                pl.BlockSpec(memory_space=pl.ANY),
                bias_spec(D_MODEL),
            ],
            out_specs=pl.BlockSpec((te, PACKED), lambda i, *_: (i, 0)),
            scratch_shapes=[
                pltpu.VMEM((2, D_MODEL, 2 * D_FF), F32),
                pltpu.VMEM((2, D_FF, D_MODEL), F32),
                pltpu.VMEM((D_MODEL, 2 * D_FF), BF16),
                pltpu.VMEM((D_FF, D_MODEL), BF16),
                pltpu.SemaphoreType.DMA((2,)),
            ],
        ),
        out_shape=jax.ShapeDtypeStruct(xb.shape, jnp.uint32),
        compiler_params=_cparams("arbitrary"),
        name="experts",
    )(block_e, n_used, next_e, slot, block_rows, xb, w_gu,
      b_gu.reshape(depth, N_EXPERTS, 1, 2 * D_FF), w_down, b_down.reshape(depth, N_EXPERTS, 1, D_MODEL))


def _combine_kernel(row_ref, tbl_ref, x_ref, pg_ref, mod_ref, fw_ref, yb_ref, *rest, split_tiles):
    del row_ref
    stage_s, sem = rest[-2:]
    step = pl.program_id(0)
    n_steps = pl.num_programs(0)
    sub = MOVE_TILE // ROUTE_TILE
    group = (step % 2) * sub

    def fetch(tile, into):
        def seg_body(e, carry):
            rows, dst, src = _segment_table(tbl_ref, tile, e)

            @pl.when(rows > 0)
            def _():
                pltpu.make_async_copy(yb_ref.at[pl.ds(src, rows), :],
                                      stage_s.at[into, pl.ds(dst, rows), :], sem.at[into]).start()
            return carry

        lax.fori_loop(0, N_EXPERTS, seg_body, 0)

    @pl.when(step == 0)
    def _():
        stage_s[...] = jnp.zeros_like(stage_s)
        for j in range(sub):
            fetch(j, j)

    @pl.when(step + 1 < n_steps)
    def _():
        for j in range(sub):
            fetch((step + 1) * sub + j, sub - group + j)

    s_iota = lax.broadcasted_iota(jnp.int32, (ROUTE_TILE, SLOTS), 1).astype(F32)
    ys = []
    for j in range(sub):
        fetched = _tile_rows_total(tbl_ref, step * sub + j)
        pltpu.make_async_copy(yb_ref.at[pl.ds(0, fetched), :],
                              stage_s.at[group + j, pl.ds(0, fetched), :], sem.at[group + j]).wait()
        rows_sorted = _unpack_pairs(stage_s[group + j])
        pg = pg_ref[j * ROUTE_TILE:(j + 1) * ROUTE_TILE, :]
        sel = jnp.zeros((ROUTE_TILE, SLOTS), F32)
        for k in range(TOP_K):
            sel = jnp.where(s_iota == pg[:, k:k + 1], pg[:, TOP_K + k:TOP_K + k + 1], sel)
        ys.append(_dot(sel.astype(BF16), rows_sorted))
    x = x_ref[...] + mod_ref[5:6, :] * jnp.concatenate(ys, axis=0)
    if split_tiles is None:
        rest[0][...] = x
    else:
        x = _rms(x, fw_ref[...])

        @pl.when(step < split_tiles)
        def _():
            rest[0][...] = x

        @pl.when(step >= split_tiles)
        def _():
            rest[1][...] = x


def _combine(x, pg, seg_tbl, yb, tile_row, mod_l, final_w, n_prompt=None):
    n = x.shape[0]
    tm = MOVE_TILE
    sub = tm // ROUTE_TILE
    if n_prompt is None:
        split = None
        out_specs = pl.BlockSpec((tm, D_MODEL), lambda i, row, tbl: (i, 0))
        out_shape = jax.ShapeDtypeStruct((n, D_MODEL), F32)
    else:
        split = n_prompt // tm
        out_specs = [
            pl.BlockSpec((tm, D_MODEL), lambda i, row, tbl: (jnp.minimum(i, split - 1), 0)),
            pl.BlockSpec((tm, D_MODEL), lambda i, row, tbl: (jnp.maximum(i - split, 0), 0)),
        ]
        out_shape = [jax.ShapeDtypeStruct((n_prompt, D_MODEL), F32),
                     jax.ShapeDtypeStruct((n - n_prompt, D_MODEL), F32)]
    return pl.pallas_call(
        functools.partial(_combine_kernel, split_tiles=split),
        grid_spec=pltpu.PrefetchScalarGridSpec(
            num_scalar_prefetch=2,
            grid=(n // tm,),
            in_specs=[
                pl.BlockSpec((tm, D_MODEL), lambda i, row, tbl: (i, 0)),
                pl.BlockSpec((tm, LANES), lambda i, row, tbl: (i, 0)),
                pl.BlockSpec((None, MOD_PARTS, D_MODEL), lambda i, row, tbl: (row[i], 0, 0)),
                pl.BlockSpec((1, D_MODEL), lambda i, row, tbl: (0, 0)),
                pl.BlockSpec(memory_space=pl.ANY),
            ],
            out_specs=out_specs,
            scratch_shapes=[pltpu.VMEM((2 * sub, SLOTS, PACKED), jnp.uint32),
                            pltpu.SemaphoreType.DMA((2 * sub,))],
        ),
        out_shape=out_shape,
        compiler_params=_cparams("arbitrary"),
        name="combine",
    )(tile_row, seg_tbl, x, pg, mod_l, final_w.reshape(1, D_MODEL), yb)


def _moe(routed, tile_row, mod_l, layer, w_gu, b_gu, w_down, b_down, final_w, n_prompt=None):
    x, hf, pg, pgt, cnt = routed
    n = x.shape[0]

    te = EXPERT_TILE
    nt = n // ROUTE_TILE
    n_blocks = (n * TOP_K + nt * N_EXPERTS * (SEG_ALIGN - 1)) // te + 1 + N_EXPERTS
    cnt = cnt[:, 0, :N_EXPERTS]
    cnt = (cnt + SEG_ALIGN - 1) // SEG_ALIGN * SEG_ALIGN
    total = jnp.sum(cnt, axis=0)
    blocks_e = (total + te - 1) // te
    block_end = jnp.cumsum(blocks_e)
    pstart = (block_end - blocks_e) * te
    n_used = block_end[-1]
    first_row = pstart[None, :] + jnp.cumsum(cnt, axis=0) - cnt
    tile_off = jnp.cumsum(cnt, axis=1) - cnt
    tile_total = jnp.broadcast_to(jnp.sum(cnt, axis=1, keepdims=True), cnt.shape)
    seg_tbl = jnp.concatenate([cnt, tile_off, first_row, tile_total], axis=1)
    seg_tbl = seg_tbl.reshape(-1).astype(jnp.int32)
    pad_rows = blocks_e * te - total
    zeroed = jnp.sum(pad_rows) + (n_blocks - n_used) * te
    pad_tbl = jnp.concatenate([pstart + total, pad_rows, zeroed[None]]).astype(jnp.int32)
    block_ids = jnp.arange(n_blocks, dtype=jnp.int32)
    clamped = jnp.minimum(block_ids, n_used - 1)
    block_e = jnp.sum((clamped[:, None] >= block_end[None, :]).astype(jnp.int32), axis=1)
    block_e = jnp.minimum(block_e, N_EXPERTS - 1).astype(jnp.int32)
    n_used = n_used.astype(jnp.int32).reshape(1)
    owns = (blocks_e > 0)[None, :]
    ids = jnp.arange(N_EXPERTS, dtype=jnp.int32)[None, :]
    mine = block_e[:, None]
    next_e = jnp.min(jnp.where((ids > mine) & owns, ids, N_EXPERTS), axis=1)
    next_e = jnp.where(next_e < N_EXPERTS, next_e, -1).astype(jnp.int32)
    slot = ((jnp.sum(((ids <= mine) & owns).astype(jnp.int32), axis=1) + 1) % 2).astype(jnp.int32)

    row_end = jnp.sum(jnp.where(ids == mine, (pstart + total)[None, :], 0), axis=1)
    block_rows = jnp.clip(row_end - block_ids * te, 0, te).astype(jnp.int32)

    xb = _dispatch(hf, pgt, seg_tbl, pad_tbl, n_used, n_blocks)
    yb = _experts(xb, block_e, n_used, next_e, slot, block_rows, layer, w_gu, b_gu, w_down, b_down)
    return _combine(x, pg, seg_tbl, yb, tile_row, mod_l, final_w, n_prompt)


def _tile_rows(n_prompt_tok, n_sample_seq, sample_len, tile):
    starts = np.arange(0, n_prompt_tok + n_sample_seq * sample_len, tile)
    row = np.where(starts < n_prompt_tok, 0, 1 + (starts - n_prompt_tok) // sample_len)
    return jnp.asarray(row, dtype=jnp.int32)


def kernel(x_prompt, x_sample, state_hgrn, c, c_ctx, w_mod, b_mod, norm_mix, norm_ffn, hg_w_in,
           hg_lb_logits, hg_gnorm, hg_w_out, cv_w_in, cv_w, cv_w_out, router_w, router_b,
           moe_w_gu, moe_b_gu, moe_w_down, moe_b_down, final_norm):
    bp, tp, d = x_prompt.shape
    bs, ts, _ = x_sample.shape
    depth = w_mod.shape[0]
    n_prompt = bp * tp
    n = n_prompt + bs * ts
    assert d == D_MODEL and depth == 2 and 1 + bs <= COND_ROWS
    assert n_prompt % SCAN_ROWS == 0 and ts == SCAN_ROWS and tp == SCAN_PIECE

    x_p = x_prompt.reshape(n_prompt, d)
    x_s = x_sample.reshape(bs * ts, d)
    cond = jnp.zeros((COND_ROWS, d), F32).at[0].set(c_ctx).at[1:1 + bs].set(c)
    mod = _modulation(cond, w_mod, b_mod).reshape(depth, COND_ROWS, MOD_PARTS, d)

    tile_row = _tile_rows(n_prompt, bs, ts, TOKEN_TILE)
    move_row = _tile_rows(n_prompt, bs, ts, MOVE_TILE)
    starts = np.arange(0, n, TOKEN_TILE)
    tile_width = jnp.asarray(np.where(starts < n_prompt, tp, GRID_W), dtype=jnp.int32)

    lb_all = jnp.cumsum(jax.nn.softmax(hg_lb_logits.astype(F32), axis=1), axis=1)

    w_in_bf = hg_w_in[0].astype(BF16)
    gn = hg_gnorm[0].reshape(-1)
    zero_state = jnp.zeros((SCAN_ROWS // tp, 2, HEADS, HEAD_DIM, HEAD_DIM), F32)
    o_p, s_new = _hgrn_scan(x_p, mod[0], 0, 0, norm_mix[0], w_in_bf, lb_all[0, 0], lb_all[1, 0], gn,
                            zero_state, tp, True)
    o_s, _ = _hgrn_scan(x_s, mod[0], 1, 1, norm_mix[0], w_in_bf, lb_all[0, 0], lb_all[1, 0], gn,
                        state_hgrn[:, 0], ts, False)
    routed = _proj_router(x_p, x_s, o_p, o_s, hg_w_out[0].astype(BF16), tile_row, mod[0], norm_ffn[0],
                          router_w[0], router_b[0])
    x = _moe(routed, move_row, mod[0], 0, moe_w_gu, moe_b_gu, moe_w_down, moe_b_down, final_norm)

    routed = _conv_router(x, tile_row, tile_width, mod[1], norm_mix[1], cv_w_in[0].astype(BF16),
                          cv_w[0], cv_w_out[0].astype(BF16), norm_ffn[1], router_w[1], router_b[1])
    y_p, y_s = _moe(routed, move_row, mod[1], 1, moe_w_gu, moe_b_gu, moe_w_down, moe_b_down,
                    final_norm, n_prompt)

    y_prompt = y_p.reshape(bp, tp, d)
    y_sample = y_s.reshape(bs, ts, d)
    return (y_prompt, y_sample, s_new.reshape(bp, 1, 2, HEADS, HEAD_DIM, HEAD_DIM))
```

```python
import functools

import numpy as np
import jax
import jax.numpy as jnp
from jax import lax
from jax.experimental import pallas as pl
from jax.experimental.pallas import tpu as pltpu

F32 = jnp.float32
BF16 = jnp.bfloat16

D_MODEL = 1024
HEADS = 8
HEAD_DIM = 128
CHUNK = 64
GRID_W = 64
N_EXPERTS = 32
TOP_K = 4
D_FF = 1024
SWIGLU_LIMIT = 7.0
SWIGLU_ALPHA = 1.702
EPS = 1e-6

MOD_PARTS = 6
COND_ROWS = 16

LANES = 128
SUBLANES = 8
SCAN_ROWS = 2048
SCAN_PIECE = 256
TOKEN_TILE = 512
MOVE_TILE = 1024
EXPERT_TILE = 1024
EXPERT_CHUNK = 1024
EXPERT_SUB = 128
ROUTE_TILE = 256
SEG_ALIGN = 8
SLOTS = ROUTE_TILE * TOP_K + N_EXPERTS * SEG_ALIGN
VMEM_LIMIT = 56 * 1024 * 1024


def _cparams(*sem):
    return pltpu.CompilerParams(dimension_semantics=sem, vmem_limit_bytes=VMEM_LIMIT)


def _sigmoid(x):
    return 1.0 / (1.0 + jnp.exp(-x))


def _rms(x, w):
    return x * lax.rsqrt(jnp.mean(x * x, axis=-1, keepdims=True) + EPS) * w


def _dot(a, b):
    return jnp.dot(a, b, preferred_element_type=F32)


def _dot_nt(a, b):
    return lax.dot_general(a, b, (((1,), (1,)), ((), ())), preferred_element_type=F32)


PACKED = D_MODEL // 2
_HIGH_HALF = 0xFFFF0000


def _pack_pairs(x):
    c = x.shape[1] // 2
    lo = lax.bitcast_convert_type(x[:, :c], jnp.uint32) >> 16
    hi = lax.bitcast_convert_type(x[:, c:], jnp.uint32) & jnp.uint32(_HIGH_HALF)
    return hi | lo


def _unpack_pairs(u):
    lo = lax.bitcast_convert_type(u << 16, F32)
    hi = lax.bitcast_convert_type(u & jnp.uint32(_HIGH_HALF), F32)
    return jnp.concatenate([lo, hi], axis=1).astype(BF16)


def _mod_kernel(cond_ref, w_ref, b_ref, o_ref):
    c = cond_ref[...]
    s = (c * _sigmoid(c)).astype(BF16)
    o_ref[...] = _dot(s, w_ref[...].astype(BF16)) + b_ref[...]


def _modulation(cond16, w_mod, b_mod):
    depth = w_mod.shape[0]
    n_out = w_mod.shape[2]
    tn = 1024
    return pl.pallas_call(
        _mod_kernel,
        grid=(depth, n_out // tn),
        in_specs=[
            pl.BlockSpec((COND_ROWS, D_MODEL), lambda l, j: (0, 0)),
            pl.BlockSpec((None, D_MODEL, tn), lambda l, j: (l, 0, j)),
            pl.BlockSpec((None, 1, tn), lambda l, j: (l, 0, j)),
        ],
        out_specs=pl.BlockSpec((None, COND_ROWS, tn), lambda l, j: (l, 0, j)),
        out_shape=jax.ShapeDtypeStruct((depth, COND_ROWS, n_out), F32),
        compiler_params=_cparams("arbitrary", "arbitrary"),
        name="modulation",
    )(cond16, w_mod, b_mod.reshape(depth, 1, n_out))


def _two_group_specs(tm, prompt_tiles):
    return [
        pl.BlockSpec((tm, D_MODEL), lambda i, *_: (jnp.minimum(i, prompt_tiles - 1), 0)),
        pl.BlockSpec((tm, D_MODEL), lambda i, *_: (jnp.maximum(i - prompt_tiles, 0), 0)),
    ]


def _pick_group(a_ref, b_ref, prompt_tiles):
    return jnp.where(pl.program_id(0) < prompt_tiles, a_ref[...], b_ref[...])


def _hgrn_kernel(x_ref, mod_ref, nw_ref, wq_ref, wff_ref, wfb_ref, wv_ref, wg_ref, lbf_ref, lbb_ref,
                 gn_ref, s0_ref, o_ref, so_ref, hm_s, q_s, vt_s, kf_s, lf_s, kb_s, lb_s, of_s, ob_s, st_s,
                 *, seq_len, zero_init):
    rows = x_ref.shape[0]
    piece = SCAN_PIECE
    n_pieces = rows // piece
    cpp = piece // CHUNK
    per_piece_seq = seq_len == piece
    assert per_piece_seq or seq_len == rows

    @pl.when(pl.program_id(1) == 0)
    def _():
        y = _rms(x_ref[...], nw_ref[...])
        hm_s[...] = (y * (1.0 + mod_ref[1:2, :]) + mod_ref[0:1, :]).astype(BF16)

    hm = hm_s[...]

    zq = _dot(hm, wq_ref[...])
    q_s[...] = zq * _sigmoid(zq)
    v = _dot(hm, wv_ref[...])
    for p in range(n_pieces):
        vt_s[p] = v[p * piece:(p + 1) * piece, :].T.astype(BF16)
    for w_ref, lbr, k_s, l_s in ((wff_ref, lbf_ref, kf_s, lf_s), (wfb_ref, lbb_ref, kb_s, lb_s)):
        lb = lbr[...]
        f = lb + (1.0 - lb) * _sigmoid(_dot(hm, w_ref[...]))
        k_s[...] = 1.0 - f
        l_s[...] = jnp.log(f)

    def load_state(j, d):
        for h in range(2):
            if zero_init:
                st_s[d, h] = jnp.zeros((HEAD_DIM, HEAD_DIM), F32)
            else:
                st_s[d, h] = s0_ref[j, d, h].T

    def store_state(j, d):
        for h in range(2):
            so_ref[j, d, h] = st_s[d, h].T

    def piece_body(i, carry):
        r = lax.broadcasted_iota(jnp.int32, (piece, piece), 0)
        c = lax.broadcasted_iota(jnp.int32, (piece, piece), 1)
        same = (r // CHUNK) == (c // CHUNK)
        row_chunk = lax.broadcasted_iota(jnp.int32, (piece, 1), 0) // CHUNK
        col_chunk = lax.broadcasted_iota(jnp.int32, (1, piece), 1) // CHUNK
        for d in range(2):
            k_s, l_s, o_s = ((kf_s, lf_s, of_s), (kb_s, lb_s, ob_s))[d]
            p = i if d == 0 else n_pieces - 1 - i
            if per_piece_seq:
                load_state(p, d)
            sl = pl.ds(pl.multiple_of(p * piece, piece), piece)
            keep = same & ((c <= r) if d == 0 else (c >= r))
            tri = jnp.where(keep, 1.0, 0.0).astype(BF16)
            lf = l_s[sl, :]
            hi = lf.astype(BF16)
            mid = (lf - hi.astype(F32)).astype(BF16)
            b = _dot(tri, hi) + _dot(tri, mid)
            b3 = b.reshape(cpp, CHUNK, 2 * HEAD_DIM)
            edge = CHUNK - 1 if d == 0 else 0
            bl3 = b3[:, edge:edge + 1, :]
            centre = CHUNK // 2 - 1 if d == 0 else CHUNK // 2
            bm3 = b3[:, centre:centre + 1, :]
            q = q_s[sl, :]
            q3 = q.reshape(cpp, CHUNK, 2 * HEAD_DIM)
            k3 = k_s[sl, :].reshape(cpp, CHUNK, 2 * HEAD_DIM)
            flat = lambda a: a.reshape(piece, 2 * HEAD_DIM).astype(BF16)
            qd = (q * jnp.exp(b)).astype(BF16)
            qi = flat(q3 * jnp.exp(b3 - bm3))
            ki = flat(k3 * jnp.exp(bm3 - b3))
            ks = flat(k3 * jnp.exp(bl3 - b3))
            dec = jnp.exp(bl3)
            vt = vt_s[p]
            for h in range(2):
                hs = slice(h * HEAD_DIM, (h + 1) * HEAD_DIM)
                att = jnp.where(keep, _dot_nt(qi[:, hs], ki[:, hs]), 0.0).astype(BF16)
                vt_h = vt[hs, :]
                v_exp = jnp.concatenate(
                    [jnp.where(col_chunk == ci, vt_h, jnp.zeros_like(vt_h)) for ci in range(cpp)], axis=0)
                incr = _dot(v_exp, ks[:, hs])
                st = st_s[d, h]
                before = [None] * cpp
                for ci in (range(cpp) if d == 0 else range(cpp - 1, -1, -1)):
                    before[ci] = st.astype(BF16)
                    st = st * dec[ci, :, hs] + incr[ci * HEAD_DIM:(ci + 1) * HEAD_DIM, :]
                st_s[d, h] = st
                qd_h = qd[:, hs]
                q_exp = [jnp.where(row_chunk == ci, qd_h, jnp.zeros_like(qd_h)) for ci in range(cpp)]
                lhs = jnp.concatenate([att] + q_exp, axis=1)
                rhs_t = jnp.concatenate([vt_h] + before, axis=1)
                o_s[sl, hs] = _dot_nt(lhs, rhs_t)
            if per_piece_seq:
                store_state(p, d)
        return carry

    if not per_piece_seq:
        load_state(0, 0)
        load_state(0, 1)
    lax.fori_loop(0, n_pieces, piece_body, 0, unroll=4)
    if not per_piece_seq:
        store_state(0, 0)
        store_state(0, 1)

    o = of_s[...] + ob_s[...]
    gn = gn_ref[...]
    o = jnp.concatenate(
        [_rms(o[:, h * HEAD_DIM:(h + 1) * HEAD_DIM], gn[:, h * HEAD_DIM:(h + 1) * HEAD_DIM])
         for h in range(2)], axis=1)
    o_ref[...] = (o * _sigmoid(_dot(hm, wg_ref[...]))).astype(o_ref.dtype)


def _hgrn_scan(x, mod_l, mod_row0, mod_row_step, norm_w, w_in_bf, lb_f, lb_b, gnorm, s0, seq_len,
               zero_init):
    n_rows = x.shape[0]
    n_seq_total = n_rows // seq_len
    rows = SCAN_ROWS
    seq_per_step = rows // seq_len
    pair = 2 * HEAD_DIM
    n_pairs = HEADS // 2

    def w_spec(seg):
        return pl.BlockSpec((D_MODEL, pair), lambda sb, hp: (0, seg * n_pairs + hp))

    vec_spec = pl.BlockSpec((1, pair), lambda sb, hp: (0, hp))
    st_block = (seq_per_step, 2, 2, HEAD_DIM, HEAD_DIM)
    st_spec = pl.BlockSpec(st_block, lambda sb, hp: (sb, 0, hp, 0, 0))
    s0_spec = pl.BlockSpec(st_block, (lambda sb, hp: (0, 0, hp, 0, 0)) if zero_init
                           else (lambda sb, hp: (sb, 0, hp, 0, 0)))
    scratch = [
        pltpu.VMEM((rows, D_MODEL), BF16),
        pltpu.VMEM((rows, pair), F32),
        pltpu.VMEM((rows // SCAN_PIECE, pair, SCAN_PIECE), BF16),
        pltpu.VMEM((rows, pair), F32),
        pltpu.VMEM((rows, pair), F32),
        pltpu.VMEM((rows, pair), F32),
        pltpu.VMEM((rows, pair), F32),
        pltpu.VMEM((rows, pair), F32),
        pltpu.VMEM((rows, pair), F32),
        pltpu.VMEM((2, 2, HEAD_DIM, HEAD_DIM), F32),
    ]
    in_specs = [
        pl.BlockSpec((rows, D_MODEL), lambda sb, hp: (sb, 0), pipeline_mode=pl.Buffered(1)),
        pl.BlockSpec((None, MOD_PARTS, D_MODEL), lambda sb, hp: (mod_row0 + sb * mod_row_step, 0, 0)),
        pl.BlockSpec((1, D_MODEL), lambda sb, hp: (0, 0)),
        w_spec(0), w_spec(1), w_spec(2), w_spec(3), w_spec(4),
        vec_spec, vec_spec, vec_spec,
        s0_spec,
    ]
    args = [x, mod_l, norm_w.reshape(1, D_MODEL), w_in_bf, w_in_bf, w_in_bf, w_in_bf, w_in_bf,
            lb_f.reshape(1, D_MODEL), lb_b.reshape(1, D_MODEL), gnorm.reshape(1, D_MODEL), s0]
    return pl.pallas_call(
        functools.partial(_hgrn_kernel, seq_len=seq_len, zero_init=zero_init),
        grid=(n_rows // rows, n_pairs),
        in_specs=in_specs,
        out_specs=[
            pl.BlockSpec((rows, pair), lambda sb, hp: (sb, hp)),
            st_spec,
        ],
        out_shape=[
            jax.ShapeDtypeStruct((n_rows, D_MODEL), BF16),
            jax.ShapeDtypeStruct((n_seq_total, 2, HEADS, HEAD_DIM, HEAD_DIM), F32),
        ],
        scratch_shapes=scratch,
        compiler_params=_cparams("arbitrary", "arbitrary"),
        name="hgrn_scan",
    )(*args)


def _route_rows(x1, half, mod_ref, nw_ref, rwh_ref, rwl_ref, rb_ref, hf_ref, pg_ref, pgt_ref, cnt_ref):
    tm = ROUTE_TILE
    rows = slice(half * tm, (half + 1) * tm)
    hf = _rms(x1, nw_ref[...]) * (1.0 + mod_ref[4:5, :]) + mod_ref[3:4, :]
    hf_hi = hf.astype(BF16)
    hf_ref[rows, :] = hf_hi
    hf_lo = (hf - hf_hi.astype(F32)).astype(BF16)
    logits = (_dot(hf_hi, rwh_ref[...]) + (_dot(hf_hi, rwl_ref[...]) + _dot(hf_lo, rwh_ref[...]))
              + rb_ref[...])
    lane = lax.broadcasted_iota(jnp.int32, (tm, LANES), 1)
    lane_f = lane.astype(F32)
    work = logits
    vals, hots = [], []
    for _ in range(TOP_K):
        m = jnp.max(work, axis=-1, keepdims=True)
        idx_f = jnp.min(jnp.where(work == m, lane_f, float(LANES)), axis=-1, keepdims=True)
        hot = lane_f == idx_f
        vals.append(m)
        hots.append(hot)
        work = jnp.where(hot, -jnp.inf, work)
    ex = [jnp.exp(v - vals[0]) for v in vals]
    inv = 1.0 / (ex[0] + ex[1] + ex[2] + ex[3])

    multi = jnp.where(hots[0] | hots[1] | hots[2] | hots[3], 1.0, 0.0)
    tr = lax.broadcasted_iota(jnp.int32, (tm, tm), 0)
    tc = lax.broadcasted_iota(jnp.int32, (tm, tm), 1)
    earlier = jnp.where(tc < tr, 1.0, 0.0).astype(BF16)
    before = _dot(earlier, multi.astype(BF16))
    cnt = jnp.sum(multi, axis=0, keepdims=True)
    er = lax.broadcasted_iota(jnp.int32, (LANES, LANES), 0)
    ec = lax.broadcasted_iota(jnp.int32, (LANES, LANES), 1)
    lower = jnp.where(er < ec, 1.0, 0.0).astype(BF16)
    cnt_al = jnp.floor((cnt + (SEG_ALIGN - 1)) * (1.0 / SEG_ALIGN)) * SEG_ALIGN
    estart = _dot(jnp.broadcast_to(cnt_al, (SUBLANES, LANES)).astype(BF16), lower)[0:1, :]
    pos_all = before + estart

    pg = jnp.zeros((tm, LANES), F32)
    for k in range(TOP_K):
        pos = jnp.sum(jnp.where(hots[k], pos_all, 0.0), axis=-1, keepdims=True)
        pg = jnp.where(lane == k, pos, pg)
        pg = jnp.where(lane == TOP_K + k, ex[k] * inv, pg)
    pg_ref[rows, :] = pg
    pgt_ref[half] = pg.T[0:8, :]
    cnt_ref[half] = cnt.astype(jnp.int32)


def _route_tile(x1, route_refs):
    x1_ref = route_refs[-5]
    x1_ref[...] = x1
    for half in range(TOKEN_TILE // ROUTE_TILE):
        _route_rows(x1[half * ROUTE_TILE:(half + 1) * ROUTE_TILE, :], half,
                    *route_refs[:5], *route_refs[-4:])


def _proj_router_kernel(row_ref, xp_ref, xs_ref, op_ref, os_ref, w_ref, mod_ref, *route_refs,
                        prompt_tiles):
    del row_ref
    x = _pick_group(xp_ref, xs_ref, prompt_tiles)
    o = _pick_group(op_ref, os_ref, prompt_tiles)
    _route_tile(x + mod_ref[2:3, :] * _dot(o, w_ref[...]), (mod_ref,) + route_refs)


def _conv_router_kernel(row_ref, width_ref, x_ref, nwm_ref, win_ref, cw_ref, wout_ref, mod_ref,
                        *route_refs):
    del row_ref
    i = pl.program_id(0)
    x = x_ref[...]
    hm = (_rms(x, nwm_ref[...]) * (1.0 + mod_ref[1:2, :]) + mod_ref[0:1, :]).astype(BF16)
    z = _dot(hm, win_ref[...])
    bg = z[:, :D_MODEL]
    u = z[:, D_MODEL:2 * D_MODEL] * z[:, 2 * D_MODEL:]
    tm = x.shape[0]
    pos = lax.broadcasted_iota(jnp.int32, (tm, 1), 0) & (width_ref[i] - 1)
    prev = jnp.where(pos == 0, 0.0, pltpu.roll(u, 1, axis=0))
    nxt = jnp.where(pos == width_ref[i] - 1, 0.0, pltpu.roll(u, tm - 1, axis=0))
    v = cw_ref[0:1, :] * prev + cw_ref[1:2, :] * u + cw_ref[2:3, :] * nxt
    y = _dot((bg * v).astype(BF16), wout_ref[...])
    _route_tile(x + mod_ref[2:3, :] * y, (mod_ref,) + route_refs)


def _mixer_router_call(kernel_fn, n, n_prefetch, prefetch, in_specs, args, mod_l, norm_ffn,
                       router_w, router_b, name):
    tm = TOKEN_TILE
    sub = tm // ROUTE_TILE
    rw = jnp.zeros((D_MODEL, LANES), F32).at[:, :N_EXPERTS].set(router_w)
    rb = jnp.full((1, LANES), -jnp.inf, F32).at[0, :N_EXPERTS].set(router_b)
    rw_hi = rw.astype(BF16)
    rw_lo = (rw - rw_hi.astype(F32)).astype(BF16)
    const = lambda shape: pl.BlockSpec(shape, lambda i, *_: (0,) * len(shape))
    row_blk = lambda width: pl.BlockSpec((tm, width), lambda i, *_: (i, 0))
    return pl.pallas_call(
        kernel_fn,
        grid_spec=pltpu.PrefetchScalarGridSpec(
            num_scalar_prefetch=n_prefetch,
            grid=(n // tm,),
            in_specs=in_specs + [
                pl.BlockSpec((None, MOD_PARTS, D_MODEL), lambda i, row, *_: (row[i], 0, 0)),
                const((1, D_MODEL)), const((D_MODEL, LANES)), const((D_MODEL, LANES)), const((1, LANES)),
            ],
            out_specs=[
                row_blk(D_MODEL), row_blk(D_MODEL), row_blk(LANES),
                pl.BlockSpec((sub, 8, ROUTE_TILE), lambda i, *_: (i, 0, 0)),
                pl.BlockSpec((sub, 1, LANES), lambda i, *_: (i, 0, 0)),
            ],
        ),
        out_shape=[
            jax.ShapeDtypeStruct((n, D_MODEL), F32),
            jax.ShapeDtypeStruct((n, D_MODEL), BF16),
            jax.ShapeDtypeStruct((n, LANES), F32),
            jax.ShapeDtypeStruct((n // ROUTE_TILE, 8, ROUTE_TILE), F32),
            jax.ShapeDtypeStruct((n // ROUTE_TILE, 1, LANES), jnp.int32),
        ],
        compiler_params=_cparams("arbitrary"),
        name=name,
    )(*prefetch, *args, mod_l, norm_ffn.reshape(1, D_MODEL), rw_hi, rw_lo, rb)


def _proj_router(x_p, x_s, o_prompt, o_sample, w_out_bf, tile_row, mod_l, norm_ffn, router_w, router_b):
    n = x_p.shape[0] + x_s.shape[0]
    pt = x_p.shape[0] // TOKEN_TILE
    in_specs = _two_group_specs(TOKEN_TILE, pt) + _two_group_specs(TOKEN_TILE, pt) + [
        pl.BlockSpec((D_MODEL, D_MODEL), lambda i, *_: (0, 0))]
    return _mixer_router_call(functools.partial(_proj_router_kernel, prompt_tiles=pt), n, 1,
                              (tile_row,), in_specs, (x_p, x_s, o_prompt, o_sample, w_out_bf),
                              mod_l, norm_ffn, router_w, router_b, "proj_router")


def _conv_router(x, tile_row, tile_width, mod_l, norm_mix, w_in_bf, conv_w, w_out_bf, norm_ffn,
                 router_w, router_b):
    n = x.shape[0]
    in_specs = [
        pl.BlockSpec((TOKEN_TILE, D_MODEL), lambda i, *_: (i, 0)),
        pl.BlockSpec((1, D_MODEL), lambda i, *_: (0, 0)),
        pl.BlockSpec((D_MODEL, 3 * D_MODEL), lambda i, *_: (0, 0)),
        pl.BlockSpec((3, D_MODEL), lambda i, *_: (0, 0)),
        pl.BlockSpec((D_MODEL, D_MODEL), lambda i, *_: (0, 0)),
    ]
    return _mixer_router_call(_conv_router_kernel, n, 2, (tile_row, tile_width), in_specs,
                              (x, norm_mix.reshape(1, D_MODEL), w_in_bf, conv_w, w_out_bf),
                              mod_l, norm_ffn, router_w, router_b, "conv_router")


SEG_STRIDE = 4 * N_EXPERTS


def _segment_table(tbl_ref, t, e):
    base = t * SEG_STRIDE
    al = lambda v: pl.multiple_of(v, SEG_ALIGN)
    return (al(tbl_ref[base + e]), al(tbl_ref[base + N_EXPERTS + e]),
            al(tbl_ref[base + 2 * N_EXPERTS + e]))


def _tile_rows_total(tbl_ref, t):
    return pl.multiple_of(tbl_ref[t * SEG_STRIDE + 3 * N_EXPERTS], SEG_ALIGN)


def _dispatch_kernel(tbl_ref, pad_ref, nu_ref, hf_ref, pgt_ref, xb_ref, stage_s, zero_s, sem, zsem,
                     *, n_blocks):
    step = pl.program_id(0)
    n_steps = pl.num_programs(0)
    sub = MOVE_TILE // ROUTE_TILE
    half = zero_s.shape[0]

    def zero_rows(start, rows):
        start = pl.multiple_of(start, SEG_ALIGN)
        rows = pl.multiple_of(rows, SEG_ALIGN)

        @pl.when(rows > 0)
        def _():
            pltpu.make_async_copy(zero_s.at[pl.ds(0, rows), :], xb_ref.at[pl.ds(start, rows), :],
                                  zsem).start()

    @pl.when(step == 0)
    def _():
        zero_s[...] = jnp.zeros_like(zero_s)

        def pad_body(e, carry):
            start = pad_ref[e]
            rows = pad_ref[N_EXPERTS + e]
            first = jnp.minimum(rows, half)
            zero_rows(start, first)
            zero_rows(start + half, rows - first)
            return carry

        lax.fori_loop(0, N_EXPERTS, pad_body, 0)

        def tail_body(b, carry):
            for part in range(EXPERT_TILE // half):
                zero_rows(b * EXPERT_TILE + part * half, half)
            return carry

        lax.fori_loop(nu_ref[0], n_blocks, tail_body, 0)

    def wait_tile(tile, sl):
        rows = _tile_rows_total(tbl_ref, tile)
        pltpu.make_async_copy(stage_s.at[sl, pl.ds(0, rows), :], xb_ref.at[pl.ds(0, rows), :],
                              sem.at[sl]).wait()

    group = (step % 2) * sub

    @pl.when(step >= 2)
    def _():
        for j in range(sub):
            wait_tile((step - 2) * sub + j, group + j)

    s_iota = lax.broadcasted_iota(jnp.int32, (SLOTS, ROUTE_TILE), 0).astype(F32)
    for j in range(sub):
        pgt = pgt_ref[j]
        hit = s_iota == pgt[0:1, :]
        for k in range(1, TOP_K):
            hit = hit | (s_iota == pgt[k:k + 1, :])
        perm = jnp.where(hit, 1.0, 0.0).astype(BF16)
        hf = hf_ref[j * ROUTE_TILE:(j + 1) * ROUTE_TILE, :]
        stage_s[group + j] = _pack_pairs(_dot(perm, hf))

    for j in range(sub):
        def seg_body(e, carry, j=j):
            rows, src, dst = _segment_table(tbl_ref, step * sub + j, e)

            @pl.when(rows > 0)
            def _():
                pltpu.make_async_copy(stage_s.at[group + j, pl.ds(src, rows), :],
                                      xb_ref.at[pl.ds(dst, rows), :], sem.at[group + j]).start()
            return carry

        lax.fori_loop(0, N_EXPERTS, seg_body, 0)

    @pl.when(step == n_steps - 1)
    def _():
        for j in range(sub):
            wait_tile(step * sub + j, group + j)

        @pl.when(n_steps > 1)
        def _():
            for j in range(sub):
                wait_tile((step - 1) * sub + j, sub - group + j)

        zeroed = pl.multiple_of(pad_ref[2 * N_EXPERTS], SEG_ALIGN)

        @pl.when(zeroed > 0)
        def _():
            pltpu.make_async_copy(xb_ref.at[pl.ds(0, zeroed), :], xb_ref.at[pl.ds(0, zeroed), :],
                                  zsem).wait()


def _dispatch(hf, pgt, seg_tbl, pad_tbl, n_used, n_blocks):
    n = hf.shape[0]
    sub = MOVE_TILE // ROUTE_TILE
    return pl.pallas_call(
        functools.partial(_dispatch_kernel, n_blocks=n_blocks),
        grid_spec=pltpu.PrefetchScalarGridSpec(
            num_scalar_prefetch=3,
            grid=(n // MOVE_TILE,),
            in_specs=[
                pl.BlockSpec((MOVE_TILE, D_MODEL), lambda i, a, b, c: (i, 0)),
                pl.BlockSpec((sub, 8, ROUTE_TILE), lambda i, a, b, c: (i, 0, 0)),
            ],
            out_specs=pl.BlockSpec(memory_space=pl.ANY),
            scratch_shapes=[
                pltpu.VMEM((2 * sub, SLOTS, PACKED), jnp.uint32),
                pltpu.VMEM((EXPERT_TILE // 2, PACKED), jnp.uint32),
                pltpu.SemaphoreType.DMA((2 * sub,)),
                pltpu.SemaphoreType.DMA,
            ],
        ),
        out_shape=jax.ShapeDtypeStruct((n_blocks * EXPERT_TILE, PACKED), jnp.uint32),
        compiler_params=_cparams("arbitrary"),
        name="dispatch",
    )(seg_tbl, pad_tbl, n_used, hf, pgt)


def _expert_kernel(be_ref, nu_ref, nxt_ref, slot_ref, rows_ref, xb_ref, wgu_hbm, bgu_ref, wd_hbm, bd_ref,
                   yb_ref, wgu_f, wd_f, wgu_s, wd_s, sem, *, layer):
    i = pl.program_id(0)

    def fetch(e, slot):
        return (pltpu.make_async_copy(wgu_hbm.at[layer, e], wgu_f.at[slot], sem.at[slot]),
                pltpu.make_async_copy(wd_hbm.at[layer, e], wd_f.at[slot], sem.at[slot]))

    @pl.when(i < nu_ref[0])
    def _():
        e = be_ref[i]
        slot = slot_ref[i]
        first_block = jnp.logical_or(i == 0, e != be_ref[jnp.maximum(i - 1, 0)])

        @pl.when(first_block)
        def _():
            @pl.when(i == 0)
            def _():
                for cp in fetch(e, slot):
                    cp.start()

            for cp in fetch(e, slot):
                cp.wait()
            wgu_s[...] = wgu_f[slot].astype(BF16)
            wd_s[...] = wd_f[slot].astype(BF16)

            @pl.when(nxt_ref[i] >= 0)
            def _():
                for cp in fetch(nxt_ref[i], 1 - slot):
                    cp.start()

        def ffn(sl):
            gu = _dot(_unpack_pairs(xb_ref[sl, :]), wgu_s[...]) + bgu_ref[...]
            gate = jnp.minimum(gu[:, :D_FF], SWIGLU_LIMIT)
            up = jnp.clip(gu[:, D_FF:], -SWIGLU_LIMIT, SWIGLU_LIMIT)
            act = (up + 1.0) * gate * _sigmoid(SWIGLU_ALPHA * gate)
            y = _dot(act.astype(BF16), wd_s[...]) + bd_ref[...]
            yb_ref[sl, :] = _pack_pairs(y.astype(BF16).astype(F32))

        n_sub = EXPERT_TILE // EXPERT_SUB
        for k in range(1, n_sub + 1):
            used = k * EXPERT_SUB
            holds_k_parts = rows_ref[i] > used - EXPERT_SUB
            if k < n_sub:
                holds_k_parts = jnp.logical_and(holds_k_parts, rows_ref[i] <= used)

            @pl.when(holds_k_parts)
            def _(used=used):
                for start in range(0, used, EXPERT_CHUNK):
                    ffn(slice(start, min(start + EXPERT_CHUNK, used)))
                if used < EXPERT_TILE:
                    yb_ref[used:, :] = jnp.zeros((EXPERT_TILE - used, PACKED), jnp.uint32)

    @pl.when(i >= nu_ref[0])
    def _():
        yb_ref[...] = jnp.zeros_like(yb_ref)


def _experts(xb, block_e, n_used, next_e, slot, block_rows, layer, w_gu, b_gu, w_down, b_down):
    te = EXPERT_TILE
    n_blocks = xb.shape[0] // te
    depth = w_gu.shape[0]

    def x_map(i, be, nu, *_):
        return (jnp.maximum(jnp.minimum(i, nu[0] - 1), 0), 0)

    def bias_spec(width):
        return pl.BlockSpec((None, None, 1, width), lambda i, be, *_: (layer, be[i], 0, 0))

    return pl.pallas_call(
        functools.partial(_expert_kernel, layer=layer),
        grid_spec=pltpu.PrefetchScalarGridSpec(
            num_scalar_prefetch=5,
            grid=(n_blocks,),
            in_specs=[
                pl.BlockSpec((te, PACKED), x_map),
                pl.BlockSpec(memory_space=pl.ANY),
                bias_spec(2 * D_FF),
                pl.BlockSpec(memory_space=pl.ANY),
                bias_spec(D_MODEL),
            ],
            out_specs=pl.BlockSpec((te, PACKED), lambda i, *_: (i, 0)),
            scratch_shapes=[
                pltpu.VMEM((2, D_MODEL, 2 * D_FF), F32),
                pltpu.VMEM((2, D_FF, D_MODEL), F32),
                pltpu.VMEM((D_MODEL, 2 * D_FF), BF16),
                pltpu.VMEM((D_FF, D_MODEL), BF16),
                pltpu.SemaphoreType.DMA((2,)),
            ],
        ),
        out_shape=jax.ShapeDtypeStruct(xb.shape, jnp.uint32),
        compiler_params=_cparams("arbitrary"),
        name="experts",
    )(block_e, n_used, next_e, slot, block_rows, xb, w_gu,
      b_gu.reshape(depth, N_EXPERTS, 1, 2 * D_FF), w_down, b_down.reshape(depth, N_EXPERTS, 1, D_MODEL))


def _combine_kernel(row_ref, tbl_ref, x_ref, pg_ref, mod_ref, fw_ref, yb_ref, *rest, split_tiles):
    del row_ref
    stage_s, sem = rest[-2:]
    step = pl.program_id(0)
    n_steps = pl.num_programs(0)
    sub = MOVE_TILE // ROUTE_TILE
    group = (step % 2) * sub

    def fetch(tile, into):
        def seg_body(e, carry):
            rows, dst, src = _segment_table(tbl_ref, tile, e)

            @pl.when(rows > 0)
            def _():
                pltpu.make_async_copy(yb_ref.at[pl.ds(src, rows), :],
                                      stage_s.at[into, pl.ds(dst, rows), :], sem.at[into]).start()
            return carry

        lax.fori_loop(0, N_EXPERTS, seg_body, 0)

    @pl.when(step == 0)
    def _():
        stage_s[...] = jnp.zeros_like(stage_s)
        for j in range(sub):
            fetch(j, j)

    @pl.when(step + 1 < n_steps)
    def _():
        for j in range(sub):
            fetch((step + 1) * sub + j, sub - group + j)

    s_iota = lax.broadcasted_iota(jnp.int32, (ROUTE_TILE, SLOTS), 1).astype(F32)
    ys = []
    for j in range(sub):
        fetched = _tile_rows_total(tbl_ref, step * sub + j)
        pltpu.make_async_copy(yb_ref.at[pl.ds(0, fetched), :],
                              stage_s.at[group + j, pl.ds(0, fetched), :], sem.at[group + j]).wait()
        rows_sorted = _unpack_pairs(stage_s[group + j])
        pg = pg_ref[j * ROUTE_TILE:(j + 1) * ROUTE_TILE, :]
        sel = jnp.zeros((ROUTE_TILE, SLOTS), F32)
        for k in range(TOP_K):
            sel = jnp.where(s_iota == pg[:, k:k + 1], pg[:, TOP_K + k:TOP_K + k + 1], sel)
        ys.append(_dot(sel.astype(BF16), rows_sorted))
    x = x_ref[...] + mod_ref[5:6, :] * jnp.concatenate(ys, axis=0)
    if split_tiles is None:
        rest[0][...] = x
    else:
        x = _rms(x, fw_ref[...])

        @pl.when(step < split_tiles)
        def _():
            rest[0][...] = x

        @pl.when(step >= split_tiles)
        def _():
            rest[1][...] = x


def _combine(x, pg, seg_tbl, yb, tile_row, mod_l, final_w, n_prompt=None):
    n = x.shape[0]
    tm = MOVE_TILE
    sub = tm // ROUTE_TILE
    if n_prompt is None:
        split = None
        out_specs = pl.BlockSpec((tm, D_MODEL), lambda i, row, tbl: (i, 0))
        out_shape = jax.ShapeDtypeStruct((n, D_MODEL), F32)
    else:
        split = n_prompt // tm
        out_specs = [
            pl.BlockSpec((tm, D_MODEL), lambda i, row, tbl: (jnp.minimum(i, split - 1), 0)),
            pl.BlockSpec((tm, D_MODEL), lambda i, row, tbl: (jnp.maximum(i - split, 0), 0)),
        ]
        out_shape = [jax.ShapeDtypeStruct((n_prompt, D_MODEL), F32),
                     jax.ShapeDtypeStruct((n - n_prompt, D_MODEL), F32)]
    return pl.pallas_call(
        functools.partial(_combine_kernel, split_tiles=split),
        grid_spec=pltpu.PrefetchScalarGridSpec(
            num_scalar_prefetch=2,
            grid=(n // tm,),
            in_specs=[
                pl.BlockSpec((tm, D_MODEL), lambda i, row, tbl: (i, 0)),
                pl.BlockSpec((tm, LANES), lambda i, row, tbl: (i, 0)),
                pl.BlockSpec((None, MOD_PARTS, D_MODEL), lambda i, row, tbl: (row[i], 0, 0)),
                pl.BlockSpec((1, D_MODEL), lambda i, row, tbl: (0, 0)),
                pl.BlockSpec(memory_space=pl.ANY),
            ],
            out_specs=out_specs,
            scratch_shapes=[pltpu.VMEM((2 * sub, SLOTS, PACKED), jnp.uint32),
                            pltpu.SemaphoreType.DMA((2 * sub,))],
        ),
        out_shape=out_shape,
        compiler_params=_cparams("arbitrary"),
        name="combine",
    )(tile_row, seg_tbl, x, pg, mod_l, final_w.reshape(1, D_MODEL), yb)


def _moe(routed, tile_row, mod_l, layer, w_gu, b_gu, w_down, b_down, final_w, n_prompt=None):
    x, hf, pg, pgt, cnt = routed
    n = x.shape[0]

    te = EXPERT_TILE
    nt = n // ROUTE_TILE
    n_blocks = (n * TOP_K + nt * N_EXPERTS * (SEG_ALIGN - 1)) // te + 1 + N_EXPERTS
    cnt = cnt[:, 0, :N_EXPERTS]
    cnt = (cnt + SEG_ALIGN - 1) // SEG_ALIGN * SEG_ALIGN
    total = jnp.sum(cnt, axis=0)
    blocks_e = (total + te - 1) // te
    block_end = jnp.cumsum(blocks_e)
    pstart = (block_end - blocks_e) * te
    n_used = block_end[-1]
    first_row = pstart[None, :] + jnp.cumsum(cnt, axis=0) - cnt
    tile_off = jnp.cumsum(cnt, axis=1) - cnt
    tile_total = jnp.broadcast_to(jnp.sum(cnt, axis=1, keepdims=True), cnt.shape)
    seg_tbl = jnp.concatenate([cnt, tile_off, first_row, tile_total], axis=1)
    seg_tbl = seg_tbl.reshape(-1).astype(jnp.int32)
    pad_rows = blocks_e * te - total
    zeroed = jnp.sum(pad_rows) + (n_blocks - n_used) * te
    pad_tbl = jnp.concatenate([pstart + total, pad_rows, zeroed[None]]).astype(jnp.int32)
    block_ids = jnp.arange(n_blocks, dtype=jnp.int32)
    clamped = jnp.minimum(block_ids, n_used - 1)
    block_e = jnp.sum((clamped[:, None] >= block_end[None, :]).astype(jnp.int32), axis=1)
    block_e = jnp.minimum(block_e, N_EXPERTS - 1).astype(jnp.int32)
    n_used = n_used.astype(jnp.int32).reshape(1)
    owns = (blocks_e > 0)[None, :]
    ids = jnp.arange(N_EXPERTS, dtype=jnp.int32)[None, :]
    mine = block_e[:, None]
    next_e = jnp.min(jnp.where((ids > mine) & owns, ids, N_EXPERTS), axis=1)
    next_e = jnp.where(next_e < N_EXPERTS, next_e, -1).astype(jnp.int32)
    slot = ((jnp.sum(((ids <= mine) & owns).astype(jnp.int32), axis=1) + 1) % 2).astype(jnp.int32)

    row_end = jnp.sum(jnp.where(ids == mine, (pstart + total)[None, :], 0), axis=1)
    block_rows = jnp.clip(row_end - block_ids * te, 0, te).astype(jnp.int32)

    xb = _dispatch(hf, pgt, seg_tbl, pad_tbl, n_used, n_blocks)
    yb = _experts(xb, block_e, n_used, next_e, slot, block_rows, layer, w_gu, b_gu, w_down, b_down)
    return _combine(x, pg, seg_tbl, yb, tile_row, mod_l, final_w, n_prompt)


def _tile_rows(n_prompt_tok, n_sample_seq, sample_len, tile):
    starts = np.arange(0, n_prompt_tok + n_sample_seq * sample_len, tile)
    row = np.where(starts < n_prompt_tok, 0, 1 + (starts - n_prompt_tok) // sample_len)
    return jnp.asarray(row, dtype=jnp.int32)


def kernel(x_prompt, x_sample, state_hgrn, c, c_ctx, w_mod, b_mod, norm_mix, norm_ffn, hg_w_in,
           hg_lb_logits, hg_gnorm, hg_w_out, cv_w_in, cv_w, cv_w_out, router_w, router_b,
           moe_w_gu, moe_b_gu, moe_w_down, moe_b_down, final_norm):
    bp, tp, d = x_prompt.shape
    bs, ts, _ = x_sample.shape
    depth = w_mod.shape[0]
    n_prompt = bp * tp
    n = n_prompt + bs * ts
    assert d == D_MODEL and depth == 2 and 1 + bs <= COND_ROWS
    assert n_prompt % SCAN_ROWS == 0 and ts == SCAN_ROWS and tp == SCAN_PIECE

    x_p = x_prompt.reshape(n_prompt, d)
    x_s = x_sample.reshape(bs * ts, d)
    cond = jnp.zeros((COND_ROWS, d), F32).at[0].set(c_ctx).at[1:1 + bs].set(c)
    mod = _modulation(cond, w_mod, b_mod).reshape(depth, COND_ROWS, MOD_PARTS, d)

    tile_row = _tile_rows(n_prompt, bs, ts, TOKEN_TILE)
    move_row = _tile_rows(n_prompt, bs, ts, MOVE_TILE)
    starts = np.arange(0, n, TOKEN_TILE)
    tile_width = jnp.asarray(np.where(starts < n_prompt, tp, GRID_W), dtype=jnp.int32)

    lb_all = jnp.cumsum(jax.nn.softmax(hg_lb_logits.astype(F32), axis=1), axis=1)

    w_in_bf = hg_w_in[0].astype(BF16)
    gn = hg_gnorm[0].reshape(-1)
    zero_state = jnp.zeros((SCAN_ROWS // tp, 2, HEADS, HEAD_DIM, HEAD_DIM), F32)
    o_p, s_new = _hgrn_scan(x_p, mod[0], 0, 0, norm_mix[0], w_in_bf, lb_all[0, 0], lb_all[1, 0], gn,
                            zero_state, tp, True)
    o_s, _ = _hgrn_scan(x_s, mod[0], 1, 1, norm_mix[0], w_in_bf, lb_all[0, 0], lb_all[1, 0], gn,
                        state_hgrn[:, 0], ts, False)
    routed = _proj_router(x_p, x_s, o_p, o_s, hg_w_out[0].astype(BF16), tile_row, mod[0], norm_ffn[0],
                          router_w[0], router_b[0])
    x = _moe(routed, move_row, mod[0], 0, moe_w_gu, moe_b_gu, moe_w_down, moe_b_down, final_norm)

    routed = _conv_router(x, tile_row, tile_width, mod[1], norm_mix[1], cv_w_in[0].astype(BF16),
                          cv_w[0], cv_w_out[0].astype(BF16), norm_ffn[1], router_w[1], router_b[1])
    y_p, y_s = _moe(routed, move_row, mod[1], 1, moe_w_gu, moe_b_gu, moe_w_down, moe_b_down,
                    final_norm, n_prompt)

    y_prompt = y_p.reshape(bp, tp, d)
    y_sample = y_s.reshape(bs, ts, d)
    return (y_prompt, y_sample, s_new.reshape(bp, 1, 2, HEADS, HEAD_DIM, HEAD_DIM))
```
